```python
import jax, jax.numpy as jnp
from jax import lax
import numpy as np

D_MODEL = 2048
BATCH = 8
SEQ = 8192
DEPTH = 4

HEAD_DIM = 64
D_MIX = D_MODEL
D_ATTN = D_MIX // 2
D_GMLP = D_MIX - D_ATTN
N_Q_HEADS = D_ATTN // HEAD_DIM
N_KV_HEADS = 4
N_GMLP_HEADS = D_GMLP // HEAD_DIM
WINDOW = 128
CHUNK = 128
RMS_EPS = 1e-6
D_KV = N_KV_HEADS * HEAD_DIM
D_IN = D_ATTN + 2 * D_KV + D_ATTN + 3 * D_GMLP

kernel_name = "hybrid_swa_sink_gmlp_parallel_heads"


def _rmsnorm(x, g):
    xf = x.astype(jnp.float32)
    y = xf * lax.rsqrt(jnp.mean(xf * xf, axis=-1, keepdims=True) + RMS_EPS)
    return (y * g.astype(jnp.float32)).astype(x.dtype)


def _alibi_slopes(n):
    return jnp.asarray(2.0 ** (-8.0 * np.arange(1, n + 1) / n), dtype=jnp.float32)


def _band(t, nb):
    B, S, H, D = t.shape
    tb = t.reshape(B, nb, WINDOW, H, D)
    prev = jnp.pad(tb, ((0, 0), (1, 0), (0, 0), (0, 0), (0, 0)))[:, :-1]
    return jnp.concatenate([prev, tb], axis=2)


def _swa_gqa_sinks(q, k, v, sinks, slopes):
    B, S, Hq, Dh = q.shape
    Hkv = k.shape[2]
    G = Hq // Hkv
    nb = S // WINDOW
    qb = q.reshape(B, nb, WINDOW, Hkv, G, Dh)
    kb = _band(k, nb)
    vb = _band(v, nb)
    scores = jnp.einsum('bnqhgd,bnkhd->bnhgqk', qb, kb).astype(jnp.float32) * (Dh ** -0.5)
    qpos = jnp.arange(WINDOW)[:, None] + WINDOW
    kpos = jnp.arange(2 * WINDOW)[None, :]
    dist = qpos - kpos
    in_window = (dist >= 0) & (dist < WINDOW)
    not_pad = (jnp.arange(nb)[:, None] > 0) | (kpos >= WINDOW)
    mask = in_window[None] & not_pad[:, None, :]
    sl = slopes.reshape(Hkv, G)
    alibi = -sl[:, :, None, None] * dist.astype(jnp.float32)[None, None]
    scores = jnp.where(mask[None, :, None, None], scores + alibi[None, None], -jnp.inf)
    sink = sinks.astype(jnp.float32).reshape(Hkv, G)[None, None, :, :, None, None]
    m = jnp.maximum(jnp.max(scores, axis=-1, keepdims=True), sink)
    p = jnp.exp(scores - m)
    p = p / (jnp.sum(p, axis=-1, keepdims=True) + jnp.exp(sink - m))
    out = jnp.einsum('bnhgqk,bnkhd->bnqhgd', p.astype(v.dtype), vb)
    return out.reshape(B, S, Hq * Dh)


def _chunked_sgu(u, v, w_s, b_s):
    B, S, H, C = v.shape
    nc = S // CHUNK
    vc = v.reshape(B, nc, CHUNK, H, C)
    w = jnp.tril(w_s)
    mixed = jnp.einsum('hts,bnshc->bnthc', w, vc) + jnp.transpose(b_s)[None, None, :, :, None]
    return u * mixed.reshape(B, S, H, C)


def _fwd_setup_inputs(seed: int = 0) -> dict:
    key = jax.random.key(seed)
    ks = jax.random.split(key, 9)
    f32 = jnp.float32
    x = jax.random.normal(ks[0], (BATCH, SEQ, D_MODEL), f32)
    norm_g = 1.0 + 0.02 * jax.random.normal(ks[1], (DEPTH, D_MODEL), f32)
    w_in = jax.random.normal(ks[2], (DEPTH, D_MODEL, D_IN), f32) * (D_MODEL ** -0.5)
    q_norm = 1.0 + 0.02 * jax.random.normal(ks[3], (DEPTH, HEAD_DIM), f32)
    k_norm = 1.0 + 0.02 * jax.random.normal(ks[4], (DEPTH, HEAD_DIM), f32)
    sinks = 0.5 * jax.random.normal(ks[5], (DEPTH, N_Q_HEADS), f32)
    w_s = jax.random.normal(ks[6], (DEPTH, N_GMLP_HEADS, CHUNK, CHUNK), f32) * (0.5 * CHUNK ** -0.5)
    b_s = 1.0 + 0.02 * jax.random.normal(ks[7], (DEPTH, N_GMLP_HEADS, CHUNK), f32)
    w_out = jax.random.normal(ks[8], (DEPTH, D_MIX, D_MODEL), f32) * (0.5 * D_MIX ** -0.5)
    return {"x": x, "norm_g": norm_g, "w_in": w_in, "q_norm": q_norm, "k_norm": k_norm,
            "sinks": sinks, "w_s": w_s, "b_s": b_s, "w_out": w_out}


def _fwd_reference(x, norm_g, w_in, q_norm, k_norm, sinks, w_s, b_s, w_out):
    B, S, _ = x.shape
    slopes = _alibi_slopes(N_Q_HEADS)
    sizes = [D_ATTN, D_KV, D_KV, D_ATTN, D_GMLP, D_GMLP, D_GMLP]
    cuts = [int(c) for c in np.cumsum(sizes)[:-1]]
    for l in range(DEPTH):
        h = _rmsnorm(x, norm_g[l])
        proj = jnp.einsum('bsd,de->bse', h, w_in[l])
        q, k, v, g_a, z_u, z_v, g_b = jnp.split(proj, cuts, axis=-1)
        q = _rmsnorm(q.reshape(B, S, N_Q_HEADS, HEAD_DIM), q_norm[l])
        k = _rmsnorm(k.reshape(B, S, N_KV_HEADS, HEAD_DIM), k_norm[l])
        v = v.reshape(B, S, N_KV_HEADS, HEAD_DIM)
        attn = _swa_gqa_sinks(q, k, v, sinks[l], slopes) * jax.nn.silu(g_a)
        z_u = jax.nn.gelu(z_u, approximate=False).reshape(B, S, N_GMLP_HEADS, HEAD_DIM)
        z_v = jax.nn.gelu(z_v, approximate=False).reshape(B, S, N_GMLP_HEADS, HEAD_DIM)
        sgu = _chunked_sgu(z_u, z_v, w_s[l], b_s[l]).reshape(B, S, D_GMLP) * jax.nn.silu(g_b)
        mix = jnp.concatenate([attn, sgu], axis=-1)
        x = x + jnp.einsum('bse,ed->bsd', mix, w_out[l])
    return x


import jax as _jax
import jax.numpy as _jnp

TWIN_FORMAT = 'train_step'
FWD_PARAMS = ['x', 'norm_g', 'w_in', 'q_norm', 'k_norm', 'sinks', 'w_s', 'b_s', 'w_out']
TWIN_WEIGHTS = ['norm_g', 'w_in', 'q_norm', 'k_norm', 'sinks', 'w_s', 'b_s', 'w_out']
TWIN_DIFF_INPUT = 'x'
TWIN_INPUTS = ['x', 'norm_g', 'w_in', 'q_norm', 'k_norm', 'sinks', 'w_s', 'b_s', 'w_out', 'loss_target', 'm_norm_g', 'm_w_in', 'm_q_norm', 'm_k_norm', 'm_sinks', 'm_w_s', 'm_b_s', 'm_w_out', 'v_norm_g', 'v_w_in', 'v_q_norm', 'v_k_norm', 'v_sinks', 'v_w_s', 'v_b_s', 'v_w_out']
TWIN_OUTPUTS = ['loss', 'grad_x', 'grad_norm_g', 'grad_w_in', 'grad_q_norm', 'grad_k_norm', 'grad_sinks', 'grad_w_s', 'grad_b_s', 'grad_w_out', 'delta_norm_g', 'delta_w_in', 'delta_q_norm', 'delta_k_norm', 'delta_sinks', 'delta_w_s', 'delta_b_s', 'delta_w_out', 'new_m_norm_g', 'new_m_w_in', 'new_m_q_norm', 'new_m_k_norm', 'new_m_sinks', 'new_m_w_s', 'new_m_b_s', 'new_m_w_out', 'new_v_norm_g', 'new_v_w_in', 'new_v_q_norm', 'new_v_k_norm', 'new_v_sinks', 'new_v_w_s', 'new_v_b_s', 'new_v_w_out']
TWIN_LEAF_KINDS = {'loss': 'loss', 'grad_x': 'grad_x', 'grad_norm_g': 'grad_w', 'grad_w_in': 'grad_w', 'grad_q_norm': 'grad_w', 'grad_k_norm': 'grad_w', 'grad_sinks': 'grad_w', 'grad_w_s': 'grad_w', 'grad_b_s': 'grad_w', 'grad_w_out': 'grad_w', 'delta_norm_g': 'delta_w', 'delta_w_in': 'delta_w', 'delta_q_norm': 'delta_w', 'delta_k_norm': 'delta_w', 'delta_sinks': 'delta_w', 'delta_w_s': 'delta_w', 'delta_b_s': 'delta_w', 'delta_w_out': 'delta_w', 'new_m_norm_g': 'new_m', 'new_m_w_in': 'new_m', 'new_m_q_norm': 'new_m', 'new_m_k_norm': 'new_m', 'new_m_sinks': 'new_m', 'new_m_w_s': 'new_m', 'new_m_b_s': 'new_m', 'new_m_w_out': 'new_m', 'new_v_norm_g': 'new_v', 'new_v_w_in': 'new_v', 'new_v_q_norm': 'new_v', 'new_v_k_norm': 'new_v', 'new_v_sinks': 'new_v', 'new_v_w_s': 'new_v', 'new_v_b_s': 'new_v', 'new_v_w_out': 'new_v'}


def _forward(args):
    return _fwd_reference(*[args[k] for k in FWD_PARAMS])


def _output_shape():
    def fwd():
        inp = _fwd_setup_inputs(0)
        return _fwd_reference(*[inp[k] for k in FWD_PARAMS])
    out = _jax.eval_shape(fwd)
    return out.shape, out.dtype

N_MICROBATCH = 1
ADAM_LR = 0.001
ADAM_B1 = 0.9
ADAM_B2 = 0.999
ADAM_EPS = 1e-08
ADAM_WD = 0.01
ADAM_STEP = 10
PER_EXAMPLE_BATCH_AXIS = {'x': 0, 'loss_target': 0}
SHARED_INPUTS = []
_WEIGHT_DTYPES = {'norm_g': _jnp.float32, 'w_in': _jnp.float32, 'q_norm': _jnp.float32, 'k_norm': _jnp.float32, 'sinks': _jnp.float32, 'w_s': _jnp.float32, 'b_s': _jnp.float32, 'w_out': _jnp.float32}
MOMENT_SCALE = {'norm_g': 1.713607e+00, 'w_in': 3.924897e-02, 'q_norm': 1.675648e+00, 'k_norm': 1.674551e+00, 'sinks': 3.493009e+00, 'w_s': 1.023826e-01, 'b_s': 5.813650e-01, 'w_out': 9.329204e-02}


def _to_microbatches(a, axis):
    t = _jnp.moveaxis(a, axis, 0)
    t = t.reshape((N_MICROBATCH, t.shape[0] // N_MICROBATCH) + t.shape[1:])
    return _jnp.moveaxis(t, 1, axis + 1)


def setup_inputs(seed: int = 0) -> dict:
    inp = _fwd_setup_inputs(seed)
    key = _jax.random.fold_in(_jax.random.key(seed), 7919)
    shape, _ = _output_shape()
    out = dict(inp)
    out["loss_target"] = _jax.random.normal(_jax.random.fold_in(key, 0), shape, _jnp.float32)
    for i, name in enumerate(TWIN_WEIGHTS):
        w = inp[name].astype(_jnp.float32)
        if MOMENT_SCALE is None:
            s = _jnp.sqrt(_jnp.mean(_jnp.square(w)) + 1e-30)
        else:
            s = MOMENT_SCALE[name]
        km, kv = _jax.random.split(_jax.random.fold_in(key, i + 1))
        out[name] = w
        out["m_" + name] = s * _jax.random.normal(km, w.shape, _jnp.float32)
        out["v_" + name] = (s * s) * _jax.random.uniform(kv, w.shape, _jnp.float32, 0.5, 1.5)
    if N_MICROBATCH > 1:
        for name, axis in PER_EXAMPLE_BATCH_AXIS.items():
            out[name] = _to_microbatches(out[name], axis)
    return {'x': out['x'], 'norm_g': out['norm_g'], 'w_in': out['w_in'], 'q_norm': out['q_norm'], 'k_norm': out['k_norm'], 'sinks': out['sinks'], 'w_s': out['w_s'], 'b_s': out['b_s'], 'w_out': out['w_out'], 'loss_target': out['loss_target'], 'm_norm_g': out['m_norm_g'], 'm_w_in': out['m_w_in'], 'm_q_norm': out['m_q_norm'], 'm_k_norm': out['m_k_norm'], 'm_sinks': out['m_sinks'], 'm_w_s': out['m_w_s'], 'm_b_s': out['m_b_s'], 'm_w_out': out['m_w_out'], 'v_norm_g': out['v_norm_g'], 'v_w_in': out['v_w_in'], 'v_q_norm': out['v_q_norm'], 'v_k_norm': out['v_k_norm'], 'v_sinks': out['v_sinks'], 'v_w_s': out['v_w_s'], 'v_b_s': out['v_b_s'], 'v_w_out': out['v_w_out']}


def _loss(weights, diff, rest, loss_target):
    with _jax.named_scope("forward"):
        args = {**rest, TWIN_DIFF_INPUT: diff, **{k: w.astype(_WEIGHT_DTYPES[k]) for k, w in weights.items()}}
        y = _forward(args)
    with _jax.named_scope("loss_head"):
        err = _jnp.square(y.astype(_jnp.float32) - loss_target)
        return 0.5 * _jnp.sum(_jnp.mean(err, axis=-1)) if err.ndim else 0.5 * err


def _adamw(w, g, m, v):
    m = ADAM_B1 * m + (1.0 - ADAM_B1) * g
    v = ADAM_B2 * v + (1.0 - ADAM_B2) * _jnp.square(g)
    m_hat = m / (1.0 - ADAM_B1 ** ADAM_STEP)
    v_hat = v / (1.0 - ADAM_B2 ** ADAM_STEP)
    delta = -ADAM_LR * (m_hat / (_jnp.sqrt(v_hat) + ADAM_EPS) + ADAM_WD * w)
    return delta, m, v


def reference(x, norm_g, w_in, q_norm, k_norm, sinks, w_s, b_s, w_out, loss_target, m_norm_g, m_w_in, m_q_norm, m_k_norm, m_sinks, m_w_s, m_b_s, m_w_out, v_norm_g, v_w_in, v_q_norm, v_k_norm, v_sinks, v_w_s, v_b_s, v_w_out):
    given = dict(x=x, norm_g=norm_g, w_in=w_in, q_norm=q_norm, k_norm=k_norm, sinks=sinks, w_s=w_s, b_s=b_s, w_out=w_out, loss_target=loss_target, m_norm_g=m_norm_g, m_w_in=m_w_in, m_q_norm=m_q_norm, m_k_norm=m_k_norm, m_sinks=m_sinks, m_w_s=m_w_s, m_b_s=m_b_s, m_w_out=m_w_out, v_norm_g=v_norm_g, v_w_in=v_w_in, v_q_norm=v_q_norm, v_k_norm=v_k_norm, v_sinks=v_sinks, v_w_s=v_w_s, v_b_s=v_b_s, v_w_out=v_w_out)
    weights = {n: given[n] for n in TWIN_WEIGHTS}
    shared = {n: given[n] for n in SHARED_INPUTS}
    per_example = {n: given[n] for n in ['x']}
    grad_fn = _jax.value_and_grad(_loss, argnums=(0, 1))

    def one_microbatch(ex, loss_target):
        ex = dict(ex)
        diff = ex.pop(TWIN_DIFF_INPUT)
        return grad_fn(weights, diff, {**shared, **ex}, loss_target)

    if N_MICROBATCH == 1:
        loss, (grad_w, grad_x) = one_microbatch(per_example, given["loss_target"])
    else:
        def body(carry, xs):
            loss_sum, grad_sum = carry
            l_k, (gw_k, gx_k) = one_microbatch(xs[0], xs[1])
            with _jax.named_scope("update"):
                return (loss_sum + l_k, _jax.tree.map(_jnp.add, grad_sum, gw_k)), gx_k

        init = (_jnp.zeros((), _jnp.float32), _jax.tree.map(_jnp.zeros_like, weights))
        (loss, grad_w), grad_x = _jax.lax.scan(body, init, (per_example, given["loss_target"]))
    with _jax.named_scope("update"):
        delta_w, new_m, new_v = {}, {}, {}
        for n in TWIN_WEIGHTS:
            delta_w[n], new_m[n], new_v[n] = _adamw(weights[n], grad_w[n], given["m_" + n], given["v_" + n])
    return (loss, grad_x, *[grad_w[n] for n in TWIN_WEIGHTS], *[delta_w[n] for n in TWIN_WEIGHTS],
            *[new_m[n] for n in TWIN_WEIGHTS], *[new_v[n] for n in TWIN_WEIGHTS])
```

```python
import functools
import math

import numpy as np
import jax
import jax.numpy as jnp
from jax import lax
from jax.experimental import pallas as pl
from jax.experimental.pallas import tpu as pltpu

F32 = jnp.float32
MM = jnp.bfloat16

D = 2048
HD = 64
DA = 1024
DKV = 256
DG = 1024
NQ, NKV, GRP, NG = 16, 4, 4, 16
WIN = 128
DIN = 5632
C_Q, C_K, C_V, C_GA, C_U, C_VS, C_GB = 0, 1024, 1280, 1536, 2560, 3584, 4608
NCHIP = 4
SHW = DIN // NCHIP
SHR = D // NCHIP
EPS = 1e-6
NEG = -1e30
SCALE = HD ** -0.5
INV_SQRT2 = 1.0 / math.sqrt(2.0)
INV_SQRT_2PI = 1.0 / math.sqrt(2.0 * math.pi)
LR, B1, B2, ADAM_EPS, WD, STEP = 0.001, 0.9, 0.999, 1e-08, 0.01, 10
VMEM_LIMIT = 56 * 1024 * 1024

SDS = jax.ShapeDtypeStruct
NT = (((1,), (1,)), ((), ()))
TN = (((0,), (0,)), ((), ()))


def _cp(sem=None):
    return pltpu.CompilerParams(dimension_semantics=sem, vmem_limit_bytes=VMEM_LIMIT)


def _sigmoid(x):
    return 1.0 / (1.0 + jnp.exp(-x))


def _gelu(x):
    return 0.5 * x * (1.0 + lax.erf(x * INV_SQRT2))


def _dgelu(x):
    return 0.5 * (1.0 + lax.erf(x * INV_SQRT2)) + x * jnp.exp(-0.5 * x * x) * INV_SQRT_2PI


def _alibi_bias():
    slopes = 2.0 ** (-8.0 * np.arange(1, NQ + 1) / NQ)
    dist = (np.arange(WIN)[:, None] + WIN) - np.arange(2 * WIN)[None, :]
    ok = (dist >= 0) & (dist < WIN)
    b = np.where(ok[None], -slopes[:, None, None] * dist[None].astype(np.float64), NEG)
    return jnp.asarray(b.reshape(NKV, GRP * WIN, 2 * WIN), dtype=F32)


def _attn_group(pm_ref, kband, vband, hk, wq, wk, bias, padmask, sink_ref):
    kh = kband[:, hk * HD:(hk + 1) * HD]
    rk = lax.rsqrt(jnp.mean(kh * kh, axis=-1, keepdims=True) + EPS)
    kn = (kh * rk * wk).astype(MM)
    vh = vband[:, hk * HD:(hk + 1) * HD].astype(MM)
    q_raw = jnp.concatenate(
        [pm_ref[:, C_Q + (hk * GRP + g) * HD:C_Q + (hk * GRP + g + 1) * HD] for g in range(GRP)], axis=0)
    rq = lax.rsqrt(jnp.mean(q_raw * q_raw, axis=-1, keepdims=True) + EPS)
    qn = (q_raw * rq * wq).astype(MM)
    s = lax.dot_general(qn, kn, NT, preferred_element_type=F32) * SCALE + bias
    s = jnp.where(padmask, s, NEG)
    sink = jnp.concatenate([jnp.full((WIN, 1), sink_ref[hk * GRP + g], F32) for g in range(GRP)], axis=0)
    m = jnp.maximum(jnp.max(s, axis=-1, keepdims=True), sink)
    p = jnp.exp(s - m)
    esink = jnp.exp(sink - m)
    inv_l = 1.0 / (jnp.sum(p, axis=-1, keepdims=True) + esink)
    o = jnp.dot(p.astype(MM), vh, preferred_element_type=F32) * inv_l
    return dict(kh=kh, rk=rk, kn=kn, vh=vh, q_raw=q_raw, rq=rq, qn=qn, p=p, inv_l=inv_l, esink=esink, o=o)


def _unstack_heads(x):
    return [x[g * WIN:(g + 1) * WIN] for g in range(GRP)]


def _fwd_mix_call(proj, bias, q_norm, k_norm, sinks, ws_tril, b_exp):
    T = proj.shape[0]
    nb = T // WIN

    def body(sink_ref, pm_ref, kvp_ref, bias_ref, wq_ref, wk_ref, ws_ref, be_ref, mix_ref):
        n = pl.program_id(0)
        wq, wk = wq_ref[...], wk_ref[...]
        kband = jnp.concatenate([kvp_ref[:, 0:DKV], pm_ref[:, C_K:C_K + DKV]], axis=0)
        vband = jnp.concatenate([kvp_ref[:, DKV:2 * DKV], pm_ref[:, C_V:C_V + DKV]], axis=0)
        col = lax.broadcasted_iota(jnp.int32, (1, 2 * WIN), 1)
        padmask = (col >= WIN) | (n > 0)
        outs = []
        for hk in range(NKV):
            a = _attn_group(pm_ref, kband, vband, hk, wq, wk, bias_ref[hk], padmask, sink_ref)
            outs += _unstack_heads(a["o"])
        attn = jnp.concatenate(outs, axis=1)
        ga = pm_ref[:, C_GA:C_GA + DA]
        mix_ref[:, 0:DA] = (attn * (ga * _sigmoid(ga))).astype(MM)
        zu = _gelu(pm_ref[:, C_U:C_U + DG])
        zv = _gelu(pm_ref[:, C_VS:C_VS + DG]).astype(MM)
        mixed = jnp.concatenate(
            [jnp.dot(ws_ref[h], zv[:, h * HD:(h + 1) * HD], preferred_element_type=F32) for h in range(NG)], axis=1)
        mixed = mixed + be_ref[...]
        gb = pm_ref[:, C_GB:C_GB + DG]
        mix_ref[:, DA:DA + DG] = (zu * mixed * (gb * _sigmoid(gb))).astype(MM)

    return pl.pallas_call(
        body,
        grid_spec=pltpu.PrefetchScalarGridSpec(
            num_scalar_prefetch=0,
            grid=(nb,),
            in_specs=[
                pl.BlockSpec(memory_space=pltpu.SMEM),
                pl.BlockSpec((WIN, DIN), lambda n: (n, 0)),
                pl.BlockSpec((WIN, 2 * DKV), lambda n: (jnp.maximum(n - 1, 0), C_K // (2 * DKV))),
                pl.BlockSpec((NKV, GRP * WIN, 2 * WIN), lambda n: (0, 0, 0)),
                pl.BlockSpec((1, HD), lambda n: (0, 0)),
                pl.BlockSpec((1, HD), lambda n: (0, 0)),
                pl.BlockSpec((NG, WIN, WIN), lambda n: (0, 0, 0)),
                pl.BlockSpec((WIN, DG), lambda n: (0, 0)),
            ],
            out_specs=pl.BlockSpec((WIN, D), lambda n: (n, 0)),
        ),
        out_shape=SDS((T, D), MM),
        name="fwd_mix",
        compiler_params=_cp(("arbitrary",)),
    )(sinks, proj, proj, bias, q_norm, k_norm, ws_tril, b_exp)


def _bwd_mix_call(proj, dmix, bias, q_norm, k_norm, sinks, ws_tril, ws_tril_t, b_exp):
    T = proj.shape[0]
    nb = T // WIN

    def body(sink_ref, pm_ref, kvp_ref, dm_ref, bias_ref, wq_ref, wk_ref, ws_ref, wst_ref, be_ref,
             dp_ref, dwq_ref, dwk_ref, dsk_ref, dws_ref, dbs_ref, carry_ref, dbacc_ref):
        n = pl.program_id(0)

        @pl.when(n == 0)
        def _():
            carry_ref[...] = jnp.zeros_like(carry_ref)
            dbacc_ref[...] = jnp.zeros_like(dbacc_ref)
            dwq_ref[...] = jnp.zeros_like(dwq_ref)
            dwk_ref[...] = jnp.zeros_like(dwk_ref)
            dsk_ref[...] = jnp.zeros_like(dsk_ref)
            dws_ref[...] = jnp.zeros_like(dws_ref)
            dbs_ref[...] = jnp.zeros_like(dbs_ref)

        @pl.when(n < nb)
        def _():
            wq, wk = wq_ref[...], wk_ref[...]
            kband = jnp.concatenate([kvp_ref[:, 0:DKV], pm_ref[:, C_K:C_K + DKV]], axis=0)
            vband = jnp.concatenate([kvp_ref[:, DKV:2 * DKV], pm_ref[:, C_V:C_V + DKV]], axis=0)
            col = lax.broadcasted_iota(jnp.int32, (1, 2 * WIN), 1)
            padmask = (col >= WIN) | (n > 0)
            ga = pm_ref[:, C_GA:C_GA + DA]
            sga = _sigmoid(ga)
            d_attn_gated = dm_ref[:, 0:DA]
            d_attn = d_attn_gated * (ga * sga)
            dq_pieces, o_pieces, dk_pieces, dv_pieces = [], [], [], []
            dwq = jnp.zeros((1, HD), F32)
            dwk = jnp.zeros((1, HD), F32)
            for hk in range(NKV):
                a = _attn_group(pm_ref, kband, vband, hk, wq, wk, bias_ref[hk], padmask, sink_ref)
                d_o = jnp.concatenate(
                    [d_attn[:, (hk * GRP + g) * HD:(hk * GRP + g + 1) * HD] for g in range(GRP)], axis=0)
                delta = jnp.sum(d_o * a["o"], axis=-1, keepdims=True)
                pn = a["p"] * a["inv_l"]
                d_p = lax.dot_general(d_o.astype(MM), a["vh"], NT, preferred_element_type=F32)
                d_s = (pn * (d_p - delta)).astype(MM)
                d_sink = -(a["esink"] * a["inv_l"]) * delta
                for g in range(GRP):
                    h = hk * GRP + g
                    val = jnp.sum(d_sink[g * WIN:(g + 1) * WIN], axis=0, keepdims=True)
                    dsk_ref[h:h + 1, :] += jnp.broadcast_to(val, (1, WIN))
                d_qn = jnp.dot(d_s, a["kn"], preferred_element_type=F32) * SCALE
                d_kn = lax.dot_general(d_s, a["qn"], TN, preferred_element_type=F32) * SCALE
                d_v = lax.dot_general(pn.astype(MM), d_o.astype(MM), TN, preferred_element_type=F32)
                q_raw, rq = a["q_raw"], a["rq"]
                gq = d_qn * wq
                d_q = rq * gq - q_raw * (rq * rq * rq) * jnp.mean(gq * q_raw, axis=-1, keepdims=True)
                dwq = dwq + jnp.sum(d_qn * q_raw * rq, axis=0, keepdims=True)
                kh, rk = a["kh"], a["rk"]
                gk = d_kn * wk
                d_k = rk * gk - kh * (rk * rk * rk) * jnp.mean(gk * kh, axis=-1, keepdims=True)
                dwk = dwk + jnp.sum(d_kn * kh * rk, axis=0, keepdims=True)
                dq_pieces += _unstack_heads(d_q)
                o_pieces += _unstack_heads(a["o"])
                dk_pieces.append(d_k)
                dv_pieces.append(d_v)
            dwq_ref[...] += dwq
            dwk_ref[...] += dwk
            attn = jnp.concatenate(o_pieces, axis=1)
            d_ga = d_attn_gated * attn * (sga * (1.0 + ga * (1.0 - sga)))
            d_kband = jnp.concatenate(dk_pieces, axis=1)
            d_vband = jnp.concatenate(dv_pieces, axis=1)

            u = pm_ref[:, C_U:C_U + DG]
            vs = pm_ref[:, C_VS:C_VS + DG]
            gb = pm_ref[:, C_GB:C_GB + DG]
            zu = _gelu(u)
            zv = _gelu(vs)
            zvb = zv.astype(MM)
            mixed = jnp.concatenate(
                [jnp.dot(ws_ref[h], zvb[:, h * HD:(h + 1) * HD], preferred_element_type=F32) for h in range(NG)], axis=1)
            mixed = mixed + be_ref[...]
            sgb = _sigmoid(gb)
            d_sgu = dm_ref[:, DA:DA + DG]
            d_gb = d_sgu * zu * mixed * (sgb * (1.0 + gb * (1.0 - sgb)))
            d_mixed = d_sgu * zu * (gb * sgb)
            d_u = d_sgu * mixed * (gb * sgb) * _dgelu(u)
            dbacc_ref[...] += d_mixed
            d_mixed_b = d_mixed.astype(MM)
            dzv_pieces = []
            for h in range(NG):
                dm_h = d_mixed_b[:, h * HD:(h + 1) * HD]
                dzv_pieces.append(jnp.dot(wst_ref[h], dm_h, preferred_element_type=F32))
                dws_ref[h] += lax.dot_general(dm_h, zvb[:, h * HD:(h + 1) * HD], NT, preferred_element_type=F32)
            d_vs = jnp.concatenate(dzv_pieces, axis=1) * _dgelu(vs)

            dp_ref[:, C_Q:C_K] = carry_ref[:, C_Q:C_K].astype(MM)
            dp_ref[:, C_K:C_V] = (carry_ref[:, C_K:C_V] + d_kband[0:WIN]).astype(MM)
            dp_ref[:, C_V:C_GA] = (carry_ref[:, C_V:C_GA] + d_vband[0:WIN]).astype(MM)
            dp_ref[:, C_GA:DIN] = carry_ref[:, C_GA:DIN].astype(MM)
            carry_ref[:, C_Q:C_K] = jnp.concatenate(dq_pieces, axis=1)
            carry_ref[:, C_K:C_V] = d_kband[WIN:2 * WIN]
            carry_ref[:, C_V:C_GA] = d_vband[WIN:2 * WIN]
            carry_ref[:, C_GA:C_U] = d_ga
            carry_ref[:, C_U:C_VS] = d_u
            carry_ref[:, C_VS:C_GB] = d_vs
            carry_ref[:, C_GB:DIN] = d_gb

        @pl.when(n == nb)
        def _():
            dp_ref[...] = carry_ref[...].astype(MM)
            ones = jnp.ones((8, HD), MM)
            acc = dbacc_ref[...]
            hi = acc.astype(MM)
            lo = (acc - hi.astype(F32)).astype(MM)
            for h in range(NG):
                sl = slice(h * HD, (h + 1) * HD)
                r = (lax.dot_general(ones, hi[:, sl], NT, preferred_element_type=F32)
                     + lax.dot_general(ones, lo[:, sl], NT, preferred_element_type=F32))
                dbs_ref[h:h + 1, :] = r[0:1, :]
            row = lax.broadcasted_iota(jnp.int32, (WIN, WIN), 0)
            cl = lax.broadcasted_iota(jnp.int32, (WIN, WIN), 1)
            for h in range(NG):
                dws_ref[h] = jnp.where(row >= cl, dws_ref[h], 0.0)

    last = nb - 1
    return pl.pallas_call(
        body,
        grid_spec=pltpu.PrefetchScalarGridSpec(
            num_scalar_prefetch=0,
            grid=(nb + 1,),
            in_specs=[
                pl.BlockSpec(memory_space=pltpu.SMEM),
                pl.BlockSpec((WIN, DIN), lambda n: (jnp.minimum(n, last), 0)),
                pl.BlockSpec((WIN, 2 * DKV), lambda n: (jnp.maximum(jnp.minimum(n, last) - 1, 0), C_K // (2 * DKV))),
                pl.BlockSpec((WIN, D), lambda n: (jnp.minimum(n, last), 0)),
                pl.BlockSpec((NKV, GRP * WIN, 2 * WIN), lambda n: (0, 0, 0)),
                pl.BlockSpec((1, HD), lambda n: (0, 0)),
                pl.BlockSpec((1, HD), lambda n: (0, 0)),
                pl.BlockSpec((NG, WIN, WIN), lambda n: (0, 0, 0)),
                pl.BlockSpec((NG, WIN, WIN), lambda n: (0, 0, 0)),
                pl.BlockSpec((WIN, DG), lambda n: (0, 0)),
            ],
            out_specs=[
                pl.BlockSpec((WIN, DIN), lambda n: (jnp.maximum(n - 1, 0), 0)),
                pl.BlockSpec((1, HD), lambda n: (0, 0)),
                pl.BlockSpec((1, HD), lambda n: (0, 0)),
                pl.BlockSpec((NQ, WIN), lambda n: (0, 0)),
                pl.BlockSpec((NG, WIN, WIN), lambda n: (0, 0, 0)),
                pl.BlockSpec((NG, WIN), lambda n: (0, 0)),
            ],
            scratch_shapes=[pltpu.VMEM((WIN, DIN), F32), pltpu.VMEM((WIN, DG), F32)],
        ),
        out_shape=[SDS((T, DIN), MM), SDS((1, HD), F32), SDS((1, HD), F32), SDS((NQ, WIN), F32),
                   SDS((NG, WIN, WIN), F32), SDS((NG, WIN), F32)],
        name="bwd_mix",
        compiler_params=_cp(("arbitrary",)),
    )(sinks, proj, proj, dmix, bias, q_norm, k_norm, ws_tril, ws_tril_t, b_exp)


def _row_tile(T):
    return min(512, T)


def _fwd_in_call(x, g_row, w_sh):
    T = x.shape[0]
    tm = _row_tile(T)

    def body(x_ref, g_ref, w_ref, proj_ref, h_ref):
        @pl.when(pl.program_id(1) == 0)
        def _():
            xv = x_ref[...]
            r = lax.rsqrt(jnp.mean(xv * xv, axis=-1, keepdims=True) + EPS)
            h_ref[...] = (xv * r * g_ref[...]).astype(MM)

        proj_ref[...] = jnp.dot(h_ref[...], w_ref[0], preferred_element_type=F32)

    return pl.pallas_call(
        body,
        grid=(T // tm, NCHIP),
        in_specs=[pl.BlockSpec((tm, D), lambda i, j: (i, 0)),
                  pl.BlockSpec((1, D), lambda i, j: (0, 0)),
                  pl.BlockSpec((1, D, SHW), lambda i, j: (j, 0, 0))],
        out_specs=[pl.BlockSpec((tm, SHW), lambda i, j: (i, j)),
                   pl.BlockSpec((tm, D), lambda i, j: (i, 0))],
        out_shape=[SDS((T, DIN), F32), SDS((T, D), MM)],
        name="fwd_in",
        compiler_params=_cp(("arbitrary", "arbitrary")),
    )(x, g_row, w_sh)


def _fwd_out_call(x, mix, w_out):
    T = x.shape[0]
    tm = _row_tile(T)

    def body(x_ref, mix_ref, w_ref, y_ref):
        y_ref[...] = x_ref[...] + jnp.dot(mix_ref[...], w_ref[...], preferred_element_type=F32)

    return pl.pallas_call(
        body,
        grid=(T // tm,),
        in_specs=[pl.BlockSpec((tm, D), lambda i: (i, 0)),
                  pl.BlockSpec((tm, D), lambda i: (i, 0)),
                  pl.BlockSpec((D, D), lambda i: (0, 0))],
        out_specs=pl.BlockSpec((tm, D), lambda i: (i, 0)),
        out_shape=SDS((T, D), F32),
        name="fwd_out",
        compiler_params=_cp(("arbitrary",)),
    )(x, mix, w_out)


def _fwd_out_loss_call(x, mix, w_out, target):
    T = x.shape[0]
    tm = _row_tile(T)

    def body(x_ref, mix_ref, w_ref, t_ref, dy_ref, loss_ref):
        @pl.when(pl.program_id(0) == 0)
        def _():
            loss_ref[...] = jnp.zeros_like(loss_ref)

        e = x_ref[...] + jnp.dot(mix_ref[...], w_ref[...], preferred_element_type=F32) - t_ref[...]
        dy_ref[...] = e * (1.0 / D)
        loss_ref[...] += (0.5 / D) * jnp.sum(jnp.sum(e * e, axis=1, keepdims=True), axis=0, keepdims=True)

    return pl.pallas_call(
        body,
        grid=(T // tm,),
        in_specs=[pl.BlockSpec((tm, D), lambda i: (i, 0)),
                  pl.BlockSpec((tm, D), lambda i: (i, 0)),
                  pl.BlockSpec((D, D), lambda i: (0, 0)),
                  pl.BlockSpec((tm, D), lambda i: (i, 0))],
        out_specs=[pl.BlockSpec((tm, D), lambda i: (i, 0)),
                   pl.BlockSpec((1, 1), lambda i: (0, 0))],
        out_shape=[SDS((T, D), F32), SDS((1, 1), F32)],
        name="fwd_out_loss",
        compiler_params=_cp(("arbitrary",)),
    )(x, mix, w_out, target)


def _bwd_out_call(dy, w_out_t):
    T = dy.shape[0]
    tm = _row_tile(T)

    def body(dy_ref, w_ref, o_ref):
        o_ref[...] = jnp.dot(dy_ref[...].astype(MM), w_ref[...], preferred_element_type=F32)

    return pl.pallas_call(
        body,
        grid=(T // tm,),
        in_specs=[pl.BlockSpec((tm, D), lambda i: (i, 0)),
                  pl.BlockSpec((D, D), lambda i: (0, 0))],
        out_specs=pl.BlockSpec((tm, D), lambda i: (i, 0)),
        out_shape=SDS((T, D), F32),
        name="bwd_out",
        compiler_params=_cp(("arbitrary",)),
    )(dy, w_out_t)


def _bwd_in_call(dproj, w_in_t, x, dy, g_row):
    T = x.shape[0]
    tm = _row_tile(T)
    nk = NCHIP

    def body(dp_ref, w_ref, x_ref, dy_ref, g_ref, dx_ref, dg_ref, acc_ref):
        i, k = pl.program_id(0), pl.program_id(1)

        @pl.when((i == 0) & (k == 0))
        def _():
            dg_ref[...] = jnp.zeros_like(dg_ref)

        @pl.when(k == 0)
        def _():
            acc_ref[...] = jnp.zeros_like(acc_ref)

        acc_ref[...] += jnp.dot(dp_ref[...], w_ref[...], preferred_element_type=F32)

        @pl.when(k == nk - 1)
        def _():
            dh = acc_ref[...]
            xv = x_ref[...]
            r = lax.rsqrt(jnp.mean(xv * xv, axis=-1, keepdims=True) + EPS)
            gd = dh * g_ref[...]
            dx_ref[...] = dy_ref[...] + r * gd - xv * ((r * r * r) * jnp.mean(gd * xv, axis=-1, keepdims=True))
            dg_ref[...] += jnp.sum(dh * xv * r, axis=0, keepdims=True)

    return pl.pallas_call(
        body,
        grid=(T // tm, nk),
        in_specs=[pl.BlockSpec((tm, SHW), lambda i, k: (i, k)),
                  pl.BlockSpec((SHW, D), lambda i, k: (k, 0)),
                  pl.BlockSpec((tm, D), lambda i, k: (i, 0)),
                  pl.BlockSpec((tm, D), lambda i, k: (i, 0)),
                  pl.BlockSpec((1, D), lambda i, k: (0, 0))],
        out_specs=[pl.BlockSpec((tm, D), lambda i, k: (i, 0)),
                   pl.BlockSpec((1, D), lambda i, k: (0, 0))],
        out_shape=[SDS((T, D), F32), SDS((1, D), F32)],
        scratch_shapes=[pltpu.VMEM((tm, D), F32)],
        name="bwd_in",
        compiler_params=_cp(("arbitrary", "arbitrary")),
    )(dproj, w_in_t, x, dy, g_row)


def _grad_w_in_call(h, dproj):
    T = h.shape[0]
    tt = _row_tile(T)

    def body(h_ref, dp_ref, o_ref):
        @pl.when(pl.program_id(1) == 0)
        def _():
            o_ref[...] = jnp.zeros_like(o_ref)

        o_ref[0] += lax.dot_general(h_ref[...], dp_ref[...], TN, preferred_element_type=F32)

    return pl.pallas_call(
        body,
        grid=(NCHIP, T // tt),
        in_specs=[pl.BlockSpec((tt, D), lambda j, t: (t, 0)),
                  pl.BlockSpec((tt, SHW), lambda j, t: (t, j))],
        out_specs=pl.BlockSpec((1, D, SHW), lambda j, t: (j, 0, 0)),
        out_shape=SDS((NCHIP, D, SHW), F32),
        name="grad_w_in",
        compiler_params=_cp(("arbitrary", "arbitrary")),
    )(h, dproj)


def _grad_w_out_call(mix, dy):
    T = mix.shape[0]
    tt = _row_tile(T)
    tn = 1024

    def body(m_ref, dy_ref, o_ref):
        @pl.when(pl.program_id(1) == 0)
        def _():
            o_ref[...] = jnp.zeros_like(o_ref)

        o_ref[...] += lax.dot_general(m_ref[...], dy_ref[...].astype(MM), TN, preferred_element_type=F32)

    return pl.pallas_call(
        body,
        grid=(D // tn, T // tt),
        in_specs=[pl.BlockSpec((tt, D), lambda j, t: (t, 0)),
                  pl.BlockSpec((tt, tn), lambda j, t: (t, j))],
        out_specs=pl.BlockSpec((D, tn), lambda j, t: (0, j)),
        out_shape=SDS((D, D), F32),
        name="grad_w_out",
        compiler_params=_cp(("arbitrary", "arbitrary")),
    )(mix, dy)


def _transpose_call(w, name):
    nbk, R, C = w.shape
    tr = 512

    def body(w_ref, o_ref):
        o_ref[...] = w_ref[0].T

    return pl.pallas_call(
        body,
        grid=(nbk, R // tr),
        in_specs=[pl.BlockSpec((1, tr, C), lambda j, i: (j, i, 0))],
        out_specs=pl.BlockSpec((C, tr), lambda j, i: (j, i)),
        out_shape=SDS((nbk * C, R), w.dtype),
        name=name,
        compiler_params=_cp(("arbitrary", "arbitrary")),
    )(w)


def _cast_call(w, name):
    shp = w.shape
    w3 = w.reshape((-1,) + shp[-2:])
    n, R, C = w3.shape
    tr = 512

    def body(w_ref, o_ref):
        o_ref[...] = w_ref[...].astype(MM)

    out = pl.pallas_call(
        body,
        grid=(n, R // tr),
        in_specs=[pl.BlockSpec((1, tr, C), lambda j, i: (j, i, 0))],
        out_specs=pl.BlockSpec((1, tr, C), lambda j, i: (j, i, 0)),
        out_shape=SDS(w3.shape, MM),
        name=name,
        compiler_params=_cp(("arbitrary", "arbitrary")),
    )(w3)
    return out.reshape(shp)


def _adam_call(w, g, m, v, name):
    R, C = w.shape
    tr = R
    for cand in (512, 256, 128, 64, 32, 16, 8):
        if R % cand == 0 and cand * C * 4 <= 1024 * 1024:
            tr = cand
            break
    c1 = 1.0 - B1 ** STEP
    c2 = 1.0 - B2 ** STEP

    def body(w_ref, g_ref, m_ref, v_ref, d_ref, nm_ref, nv_ref):
        gv = g_ref[...]
        nm = B1 * m_ref[...] + (1.0 - B1) * gv
        nv = B2 * v_ref[...] + (1.0 - B2) * (gv * gv)
        nm_ref[...] = nm
        nv_ref[...] = nv
        d_ref[...] = -LR * ((nm / c1) / (jnp.sqrt(nv / c2) + ADAM_EPS) + WD * w_ref[...])

    spec = pl.BlockSpec((tr, C), lambda i: (i, 0))
    return pl.pallas_call(
        body,
        grid=(R // tr,),
        in_specs=[spec] * 4,
        out_specs=[spec] * 3,
        out_shape=[SDS((R, C), F32)] * 3,
        name=name,
        compiler_params=_cp(("arbitrary",)),
    )(w, g, m, v)


def _local_fwd_bwd(x, target, norm_g, w_in_sh, w_out_b, q_norm, k_norm, sinks, w_s, b_s):
    L = norm_g.shape[0]
    bias = _alibi_bias()
    tri = jnp.tril(jnp.ones((WIN, WIN), F32))
    saved = []
    xs = x
    dy = loss = None
    for l in range(L):
        proj, h = _fwd_in_call(xs, norm_g[l:l + 1], w_in_sh[l])
        ws_tril = (w_s[l] * tri).astype(MM)
        b_exp = jnp.repeat(b_s[l].T, HD, axis=1)
        mix = _fwd_mix_call(proj, bias, q_norm[l:l + 1], k_norm[l:l + 1], sinks[l], ws_tril, b_exp)
        saved.append((xs, proj, h, mix, ws_tril, b_exp))
        if l < L - 1:
            xs = _fwd_out_call(xs, mix, w_out_b[l])
        else:
            dy, loss = _fwd_out_loss_call(xs, mix, w_out_b[l], target)
    grads = [None] * L
    for l in reversed(range(L)):
        xs, proj, h, mix, ws_tril, b_exp = saved[l]
        w_out_t = _transpose_call(w_out_b[l][None], "transpose_w_out")
        w_in_t = _transpose_call(w_in_sh[l], "transpose_w_in")
        g_w_out = _grad_w_out_call(mix, dy)
        dmix = _bwd_out_call(dy, w_out_t)
        ws_tril_t = jnp.swapaxes(ws_tril, 1, 2)
        dproj, dwq, dwk, dsk, dws, dbs = _bwd_mix_call(
            proj, dmix, bias, q_norm[l:l + 1], k_norm[l:l + 1], sinks[l], ws_tril, ws_tril_t, b_exp)
        g_w_in = _grad_w_in_call(h, dproj)
        dy, dng = _bwd_in_call(dproj, w_in_t, xs, dy, norm_g[l:l + 1])
        grads[l] = (g_w_in, g_w_out, dng, dwq, dwk, dsk[:, 0], dws, dbs)
    return loss, dy, grads


MESH = pl.DeviceIdType.MESH
ANY = pl.BlockSpec(memory_space=pl.ANY)


def _place():
    x, y, c = lax.axis_index("x"), lax.axis_index("y"), lax.axis_index("c")
    others = [(1 - x, y), (x, 1 - y), (1 - x, 1 - y)]
    return x, y, c, 2 * x + y, others


def _rcopy(src, dst, ssem, rsem, dev):
    return pltpu.make_async_remote_copy(src_ref=src, dst_ref=dst, send_sem=ssem, recv_sem=rsem,
                                        device_id=dev, device_id_type=MESH)


def _gather_weights_call(w_in_b, w_out_b):
    L = w_in_b.shape[0]
    hi, ho = D // 2, SHR // 2
    ncp = 2 * L * 3

    def body(win_ref, wout_ref, *rest):
        fin, fout = rest[:L], rest[L:2 * L]
        ici_s, ici_r, d2d_s, d2d_r, loc = rest[2 * L:]
        x, y, c, me, others = _place()
        sib = (x, y, 1 - c)

        def region(kind, l, shard, half):
            if kind == 0:
                return fin[l].at[shard, pl.ds(half * hi, hi), :]
            return fout[l].at[pl.ds(shard * SHR + half * ho, ho), :]

        def own(kind, l):
            if kind == 0:
                return win_ref.at[l, pl.ds(c * hi, hi), :]
            return wout_ref.at[l, pl.ds(c * ho, ho), :]

        local = []
        for l in range(L):
            local.append(pltpu.make_async_copy(win_ref.at[l], fin[l].at[me], loc.at[2 * l]))
            local.append(pltpu.make_async_copy(wout_ref.at[l], fout[l].at[pl.ds(me * SHR, SHR), :], loc.at[2 * l + 1]))
        for cp in local:
            cp.start()
        sends = []
        for l in range(L):
            for kind in range(2):
                for j, (px, py) in enumerate(others):
                    k = (l * 2 + kind) * 3 + j
                    cp = _rcopy(own(kind, l), region(kind, l, me, c), ici_s.at[k], ici_r.at[k], (px, py, c))
                    cp.start()
                    sends.append(cp)
        for l in range(L):
            for kind in range(2):
                for j, (px, py) in enumerate(others):
                    k = (l * 2 + kind) * 3 + j
                    got = region(kind, l, 2 * px + py, c)
                    _rcopy(got, got, ici_s.at[k], ici_r.at[k], (px, py, c)).wait_recv()
                    cp = _rcopy(got, got, d2d_s.at[k], d2d_r.at[k], sib)
                    cp.start()
                    sends.append(cp)
        for l in range(L):
            for kind in range(2):
                for j, (px, py) in enumerate(others):
                    k = (l * 2 + kind) * 3 + j
                    got = region(kind, l, 2 * px + py, 1 - c)
                    _rcopy(got, got, d2d_s.at[k], d2d_r.at[k], sib).wait_recv()
        for cp in sends:
            cp.wait_send()
        for cp in local:
            cp.wait()

    outs = pl.pallas_call(
        body,
        in_specs=[ANY, ANY],
        out_specs=[ANY] * (2 * L),
        out_shape=[SDS((NCHIP, D, SHW), MM)] * L + [SDS((D, D), MM)] * L,
        scratch_shapes=[pltpu.SemaphoreType.DMA((ncp,)), pltpu.SemaphoreType.DMA((ncp,)),
                        pltpu.SemaphoreType.DMA((ncp,)), pltpu.SemaphoreType.DMA((ncp,)),
                        pltpu.SemaphoreType.DMA((2 * L,))],
        name="gather_weights",
    )(w_in_b, w_out_b)
    return list(outs[:L]), list(outs[L:])


def _sibling_presum_exchange_call(gs):
    K = len(gs)

    def body(*rest):
        g_refs, a_refs = rest[:K], rest[K:2 * K]
        ssem, rsem = rest[2 * K:]
        x, y, c, me, others = _place()
        sib = (x, y, 1 - c)
        sends = []
        for k in range(K):
            h = gs[k].shape[1] // 2
            cp = _rcopy(g_refs[k].at[:, pl.ds((1 - c) * h, h), :], a_refs[k], ssem.at[k], rsem.at[k], sib)
            cp.start()
            sends.append(cp)
        for cp in sends:
            cp.wait()

    return pl.pallas_call(
        body,
        in_specs=[ANY] * K,
        out_specs=[ANY] * K,
        out_shape=[SDS((g.shape[0], g.shape[1] // 2, g.shape[2]), F32) for g in gs],
        scratch_shapes=[pltpu.SemaphoreType.DMA((K,)), pltpu.SemaphoreType.DMA((K,))],
        name="grad_sibling_exchange",
    )(*gs)


def _chip_exchange_call(ps):
    K = len(ps)

    def body(*rest):
        p_refs, r_refs = rest[:K], rest[K:2 * K]
        ssem, rsem, loc = rest[2 * K:]
        x, y, c, me, others = _place()
        cps = []
        for k in range(K):
            scatter = ps[k].shape[0] == NCHIP
            cp = pltpu.make_async_copy(p_refs[k].at[me if scatter else 0], r_refs[k].at[me], loc.at[k])
            cp.start()
            cps.append(cp)
            for j, (px, py) in enumerate(others):
                src = p_refs[k].at[2 * px + py if scatter else 0]
                cp = _rcopy(src, r_refs[k].at[me], ssem.at[3 * k + j], rsem.at[3 * k + j], (px, py, c))
                cp.start()
                cps.append(cp)
        for cp in cps:
            cp.wait()

    return pl.pallas_call(
        body,
        in_specs=[ANY] * K,
        out_specs=[ANY] * K,
        out_shape=[SDS((NCHIP, p.shape[1], p.shape[2]), F32) for p in ps],
        scratch_shapes=[pltpu.SemaphoreType.DMA((3 * K,)), pltpu.SemaphoreType.DMA((3 * K,)),
                        pltpu.SemaphoreType.DMA((K,))],
        name="grad_chip_exchange",
    )(*ps)


def _sibling_halves_call(groups):
    flat = [(gi, li, a) for gi, grp in enumerate(groups) for li, a in enumerate(grp)]
    K = len(flat)
    G = len(groups)

    def body(*rest):
        h_refs, f_refs = rest[:K], rest[K:K + G]
        ssem, rsem, loc = rest[K + G:]
        x, y, c, me, others = _place()
        sib = (x, y, 1 - c)
        cps = []
        for k, (gi, li, a) in enumerate(flat):
            h = a.shape[0]
            dst = f_refs[gi].at[li, pl.ds(c * h, h), :]
            lc = pltpu.make_async_copy(h_refs[k], dst, loc.at[k])
            lc.start()
            rc = _rcopy(h_refs[k], dst, ssem.at[k], rsem.at[k], sib)
            rc.start()
            cps += [lc, rc]
        for cp in cps:
            cp.wait()

    return pl.pallas_call(
        body,
        in_specs=[ANY] * K,
        out_specs=[ANY] * G,
        out_shape=[SDS((len(grp), 2 * grp[0].shape[0], grp[0].shape[1]), F32) for grp in groups],
        scratch_shapes=[pltpu.SemaphoreType.DMA((K,)), pltpu.SemaphoreType.DMA((K,)), pltpu.SemaphoreType.DMA((K,))],
        name="grad_sibling_halves",
    )(*[a for _, _, a in flat])


def _rows_tile(H, C):
    for cand in (512, 256, 128, 64, 32, 16, 8):
        if H % cand == 0 and cand * C * 4 <= 2 * 1024 * 1024:
            return cand
    raise ValueError((H, C))


def _add_my_half_call(g, a, c_idx):
    N, R, C = g.shape
    H = R // 2
    tr = _rows_tile(H, C)
    nt = H // tr

    def body(c_ref, g_ref, a_ref, o_ref):
        o_ref[...] = g_ref[...] + a_ref[...]

    return pl.pallas_call(
        body,
        grid_spec=pltpu.PrefetchScalarGridSpec(
            num_scalar_prefetch=1,
            grid=(N, nt),
            in_specs=[pl.BlockSpec((1, tr, C), lambda n, i, c_ref: (n, c_ref[0] * nt + i, 0)),
                      pl.BlockSpec((1, tr, C), lambda n, i, c_ref: (n, i, 0))],
            out_specs=pl.BlockSpec((1, tr, C), lambda n, i, c_ref: (n, i, 0)),
        ),
        out_shape=SDS((N, H, C), F32),
        name="grad_add_half",
        compiler_params=_cp(("arbitrary", "arbitrary")),
    )(c_idx, g, a)


def _sum_chips_call(r):
    _, H, C = r.shape
    tr = _rows_tile(H, C)

    def body(r0, r1, r2, r3, o_ref):
        o_ref[...] = ((r0[0] + r1[0]) + r2[0]) + r3[0]

    return pl.pallas_call(
        body,
        grid=(H // tr,),
        in_specs=[pl.BlockSpec((1, tr, C), functools.partial(lambda i, s: (s, i, 0), s=s)) for s in range(NCHIP)],
        out_specs=pl.BlockSpec((tr, C), lambda i: (i, 0)),
        out_shape=SDS((H, C), F32),
        name="grad_sum_chips",
        compiler_params=_cp(("arbitrary",)),
    )(r, r, r, r)


SMALL_ROWS_ALIGN = 1024


def _pack_small(parts):
    flat = jnp.concatenate([p.reshape(-1) for p in parts])
    rows = -(-flat.shape[0] // (128 * SMALL_ROWS_ALIGN)) * SMALL_ROWS_ALIGN
    flat = jnp.pad(flat, (0, rows * 128 - flat.shape[0]))
    return flat.reshape(rows, 128)


def _unpack_small(packed, like):
    flat = packed.reshape(-1)
    out, off = [], 0
    for p in like:
        n = int(np.prod(p.shape))
        out.append(flat[off:off + n].reshape(p.shape))
        off += n
    return out


def kernel(x, norm_g, w_in, q_norm, k_norm, sinks, w_s, b_s, w_out, loss_target, m_norm_g, m_w_in, m_q_norm, m_k_norm, m_sinks, m_w_s, m_b_s, m_w_out, v_norm_g, v_w_in, v_q_norm, v_k_norm, v_sinks, v_w_s, v_b_s, v_w_out):
    L = norm_g.shape[0]
    c_idx = lax.axis_index("c").astype(jnp.int32).reshape(1)

    w_in_b = _cast_call(w_in, "cast_w_in")
    w_out_b = _cast_call(w_out, "cast_w_out")
    w_in_full, w_out_full = _gather_weights_call(w_in_b, w_out_b)

    loss, grad_x, grads = _local_fwd_bwd(x[0], loss_target[0], norm_g, w_in_full, w_out_full,
                                         q_norm, k_norm, sinks, w_s, b_s)

    small_like = [norm_g, q_norm, k_norm, sinks, w_s, b_s]
    g_small = _pack_small([jnp.stack([g[i] for g in grads]).reshape(small_like[i - 2].shape) for i in range(2, 8)])
    gs = ([g[0] for g in grads] + [g[1].reshape(NCHIP, SHR, D) for g in grads] + [g_small[None]])
    sib = _sibling_presum_exchange_call(gs)
    ps = [_add_my_half_call(g, a, c_idx) for g, a in zip(gs, sib)]
    rs = _chip_exchange_call(ps)
    hs = [_sum_chips_call(r) for r in rs]
    g_w_in, g_w_out, g_small = _sibling_halves_call([hs[:L], hs[L:2 * L], hs[2 * L:]])
    g_small = g_small[0]

    d_in, nm_in, nv_in = _adam_call(w_in.reshape(L * D, SHW), g_w_in.reshape(L * D, SHW),
                                    m_w_in.reshape(L * D, SHW), v_w_in.reshape(L * D, SHW), "adam_w_in")
    d_out, nm_out, nv_out = _adam_call(w_out.reshape(L * SHR, D), g_w_out.reshape(L * SHR, D),
                                       m_w_out.reshape(L * SHR, D), v_w_out.reshape(L * SHR, D), "adam_w_out")
    d_s, nm_s, nv_s = _adam_call(_pack_small(small_like), g_small,
                                 _pack_small([m_norm_g, m_q_norm, m_k_norm, m_sinks, m_w_s, m_b_s]),
                                 _pack_small([v_norm_g, v_q_norm, v_k_norm, v_sinks, v_w_s, v_b_s]), "adam_small")

    def full(small, win, wout):
        ng, qn, kn, sk, ws, bs = _unpack_small(small, small_like)
        return [ng, win.reshape(w_in.shape), qn, kn, sk, ws, bs, wout.reshape(w_out.shape)]

    loss_all = lax.psum(loss[0, 0], ("x", "y", "c"))
    return (loss_all, grad_x[None], *full(g_small, g_w_in, g_w_out), *full(d_s, d_in, d_out),
            *full(nm_s, nm_in, nm_out), *full(nv_s, nv_in, nv_out))
```

```python
import functools
import math

import numpy as np
import jax
import jax.numpy as jnp
from jax import lax
from jax.experimental import pallas as pl
from jax.experimental.pallas import tpu as pltpu

F32 = jnp.float32
MM = jnp.bfloat16

D = 2048
HD = 64
DA = 1024
DKV = 256
DG = 1024
NQ, NKV, GRP, NG = 16, 4, 4, 16
WIN = 128
DIN = 5632
C_Q, C_K, C_V, C_GA, C_U, C_VS, C_GB = 0, 1024, 1280, 1536, 2560, 3584, 4608
NCHIP = 4
SHW = DIN // NCHIP
SHR = D // NCHIP
EPS = 1e-6
NEG = -1e30
SCALE = HD ** -0.5
INV_SQRT2 = 1.0 / math.sqrt(2.0)
INV_SQRT_2PI = 1.0 / math.sqrt(2.0 * math.pi)
LR, B1, B2, ADAM_EPS, WD, STEP = 0.001, 0.9, 0.999, 1e-08, 0.01, 10
VMEM_LIMIT = 56 * 1024 * 1024

SDS = jax.ShapeDtypeStruct
NT = (((1,), (1,)), ((), ()))
TN = (((0,), (0,)), ((), ()))


def _cp(sem=None):
    return pltpu.CompilerParams(dimension_semantics=sem, vmem_limit_bytes=VMEM_LIMIT)


def _sigmoid(x):
    return 1.0 / (1.0 + jnp.exp(-x))


def _gelu(x):
    return 0.5 * x * (1.0 + lax.erf(x * INV_SQRT2))


def _dgelu(x):
    return 0.5 * (1.0 + lax.erf(x * INV_SQRT2)) + x * jnp.exp(-0.5 * x * x) * INV_SQRT_2PI


def _alibi_bias():
    slopes = 2.0 ** (-8.0 * np.arange(1, NQ + 1) / NQ)
    dist = (np.arange(WIN)[:, None] + WIN) - np.arange(2 * WIN)[None, :]
    ok = (dist >= 0) & (dist < WIN)
    b = np.where(ok[None], -slopes[:, None, None] * dist[None].astype(np.float64), NEG)
    return jnp.asarray(b.reshape(NKV, GRP * WIN, 2 * WIN), dtype=F32)


def _attn_group(pm_ref, kband, vband, hk, wq, wk, bias, padmask, sink_ref):
    kh = kband[:, hk * HD:(hk + 1) * HD]
    rk = lax.rsqrt(jnp.mean(kh * kh, axis=-1, keepdims=True) + EPS)
    kn = (kh * rk * wk).astype(MM)
    vh = vband[:, hk * HD:(hk + 1) * HD].astype(MM)
    q_raw = jnp.concatenate(
        [pm_ref[:, C_Q + (hk * GRP + g) * HD:C_Q + (hk * GRP + g + 1) * HD] for g in range(GRP)], axis=0)
    rq = lax.rsqrt(jnp.mean(q_raw * q_raw, axis=-1, keepdims=True) + EPS)
    qn = (q_raw * rq * wq).astype(MM)
    s = lax.dot_general(qn, kn, NT, preferred_element_type=F32) * SCALE + bias
    s = jnp.where(padmask, s, NEG)
    sink = jnp.concatenate([jnp.full((WIN, 1), sink_ref[hk * GRP + g], F32) for g in range(GRP)], axis=0)
    m = jnp.maximum(jnp.max(s, axis=-1, keepdims=True), sink)
    p = jnp.exp(s - m)
    esink = jnp.exp(sink - m)
    inv_l = 1.0 / (jnp.sum(p, axis=-1, keepdims=True) + esink)
    o = jnp.dot(p.astype(MM), vh, preferred_element_type=F32) * inv_l
    return dict(kh=kh, rk=rk, kn=kn, vh=vh, q_raw=q_raw, rq=rq, qn=qn, p=p, inv_l=inv_l, esink=esink, o=o)


def _unstack_heads(x):
    return [x[g * WIN:(g + 1) * WIN] for g in range(GRP)]


def _fwd_mix_call(proj, bias, q_norm, k_norm, sinks, ws_tril, b_exp):
    T = proj.shape[0]
    nb = T // WIN

    def body(sink_ref, pm_ref, kvp_ref, bias_ref, wq_ref, wk_ref, ws_ref, be_ref, mix_ref):
        n = pl.program_id(0)
        wq, wk = wq_ref[...], wk_ref[...]
        kband = jnp.concatenate([kvp_ref[:, 0:DKV], pm_ref[:, C_K:C_K + DKV]], axis=0)
        vband = jnp.concatenate([kvp_ref[:, DKV:2 * DKV], pm_ref[:, C_V:C_V + DKV]], axis=0)
        col = lax.broadcasted_iota(jnp.int32, (1, 2 * WIN), 1)
        padmask = (col >= WIN) | (n > 0)
        outs = []
        for hk in range(NKV):
            a = _attn_group(pm_ref, kband, vband, hk, wq, wk, bias_ref[hk], padmask, sink_ref)
            outs += _unstack_heads(a["o"])
        attn = jnp.concatenate(outs, axis=1)
        ga = pm_ref[:, C_GA:C_GA + DA]
        mix_ref[:, 0:DA] = (attn * (ga * _sigmoid(ga))).astype(MM)
        zu = _gelu(pm_ref[:, C_U:C_U + DG])
        zv = _gelu(pm_ref[:, C_VS:C_VS + DG]).astype(MM)
        mixed = jnp.concatenate(
            [jnp.dot(ws_ref[h], zv[:, h * HD:(h + 1) * HD], preferred_element_type=F32) for h in range(NG)], axis=1)
        mixed = mixed + be_ref[...]
        gb = pm_ref[:, C_GB:C_GB + DG]
        mix_ref[:, DA:DA + DG] = (zu * mixed * (gb * _sigmoid(gb))).astype(MM)

    return pl.pallas_call(
        body,
        grid_spec=pltpu.PrefetchScalarGridSpec(
            num_scalar_prefetch=0,
            grid=(nb,),
            in_specs=[
                pl.BlockSpec(memory_space=pltpu.SMEM),
                pl.BlockSpec((WIN, DIN), lambda n: (n, 0)),
                pl.BlockSpec((WIN, 2 * DKV), lambda n: (jnp.maximum(n - 1, 0), C_K // (2 * DKV))),
                pl.BlockSpec((NKV, GRP * WIN, 2 * WIN), lambda n: (0, 0, 0)),
                pl.BlockSpec((1, HD), lambda n: (0, 0)),
                pl.BlockSpec((1, HD), lambda n: (0, 0)),
                pl.BlockSpec((NG, WIN, WIN), lambda n: (0, 0, 0)),
                pl.BlockSpec((WIN, DG), lambda n: (0, 0)),
            ],
            out_specs=pl.BlockSpec((WIN, D), lambda n: (n, 0)),
        ),
        out_shape=SDS((T, D), MM),
        name="fwd_mix",
        compiler_params=_cp(("arbitrary",)),
    )(sinks, proj, proj, bias, q_norm, k_norm, ws_tril, b_exp)


def _bwd_mix_call(proj, dmix, bias, q_norm, k_norm, sinks, ws_tril, ws_tril_t, b_exp):
    T = proj.shape[0]
    nb = T // WIN

    def body(sink_ref, pm_ref, kvp_ref, dm_ref, bias_ref, wq_ref, wk_ref, ws_ref, wst_ref, be_ref,
             dp_ref, dwq_ref, dwk_ref, dsk_ref, dws_ref, dbs_ref, carry_ref, dbacc_ref):
        n = pl.program_id(0)

        @pl.when(n == 0)
        def _():
            carry_ref[...] = jnp.zeros_like(carry_ref)
            dbacc_ref[...] = jnp.zeros_like(dbacc_ref)
            dwq_ref[...] = jnp.zeros_like(dwq_ref)
            dwk_ref[...] = jnp.zeros_like(dwk_ref)
            dsk_ref[...] = jnp.zeros_like(dsk_ref)
            dws_ref[...] = jnp.zeros_like(dws_ref)
            dbs_ref[...] = jnp.zeros_like(dbs_ref)

        @pl.when(n < nb)
        def _():
            wq, wk = wq_ref[...], wk_ref[...]
            kband = jnp.concatenate([kvp_ref[:, 0:DKV], pm_ref[:, C_K:C_K + DKV]], axis=0)
            vband = jnp.concatenate([kvp_ref[:, DKV:2 * DKV], pm_ref[:, C_V:C_V + DKV]], axis=0)
            col = lax.broadcasted_iota(jnp.int32, (1, 2 * WIN), 1)
            padmask = (col >= WIN) | (n > 0)
            ga = pm_ref[:, C_GA:C_GA + DA]
            sga = _sigmoid(ga)
            d_attn_gated = dm_ref[:, 0:DA]
            d_attn = d_attn_gated * (ga * sga)
            dq_pieces, o_pieces, dk_pieces, dv_pieces = [], [], [], []
            dwq = jnp.zeros((1, HD), F32)
            dwk = jnp.zeros((1, HD), F32)
            for hk in range(NKV):
                a = _attn_group(pm_ref, kband, vband, hk, wq, wk, bias_ref[hk], padmask, sink_ref)
                d_o = jnp.concatenate(
                    [d_attn[:, (hk * GRP + g) * HD:(hk * GRP + g + 1) * HD] for g in range(GRP)], axis=0)
                delta = jnp.sum(d_o * a["o"], axis=-1, keepdims=True)
                pn = a["p"] * a["inv_l"]
                d_p = lax.dot_general(d_o.astype(MM), a["vh"], NT, preferred_element_type=F32)
                d_s = (pn * (d_p - delta)).astype(MM)
                d_sink = -(a["esink"] * a["inv_l"]) * delta
                for g in range(GRP):
                    h = hk * GRP + g
                    val = jnp.sum(d_sink[g * WIN:(g + 1) * WIN], axis=0, keepdims=True)
                    dsk_ref[h:h + 1, :] += jnp.broadcast_to(val, (1, WIN))
                d_qn = jnp.dot(d_s, a["kn"], preferred_element_type=F32) * SCALE
                d_kn = lax.dot_general(d_s, a["qn"], TN, preferred_element_type=F32) * SCALE
                d_v = lax.dot_general(pn.astype(MM), d_o.astype(MM), TN, preferred_element_type=F32)
                q_raw, rq = a["q_raw"], a["rq"]
                gq = d_qn * wq
                d_q = rq * gq - q_raw * (rq * rq * rq) * jnp.mean(gq * q_raw, axis=-1, keepdims=True)
                dwq = dwq + jnp.sum(d_qn * q_raw * rq, axis=0, keepdims=True)
                kh, rk = a["kh"], a["rk"]
                gk = d_kn * wk
                d_k = rk * gk - kh * (rk * rk * rk) * jnp.mean(gk * kh, axis=-1, keepdims=True)
                dwk = dwk + jnp.sum(d_kn * kh * rk, axis=0, keepdims=True)
                dq_pieces += _unstack_heads(d_q)
                o_pieces += _unstack_heads(a["o"])
                dk_pieces.append(d_k)
                dv_pieces.append(d_v)
            dwq_ref[...] += dwq
            dwk_ref[...] += dwk
            attn = jnp.concatenate(o_pieces, axis=1)
            d_ga = d_attn_gated * attn * (sga * (1.0 + ga * (1.0 - sga)))
            d_kband = jnp.concatenate(dk_pieces, axis=1)
            d_vband = jnp.concatenate(dv_pieces, axis=1)

            u = pm_ref[:, C_U:C_U + DG]
            vs = pm_ref[:, C_VS:C_VS + DG]
            gb = pm_ref[:, C_GB:C_GB + DG]
            zu = _gelu(u)
            zv = _gelu(vs)
            zvb = zv.astype(MM)
            mixed = jnp.concatenate(
                [jnp.dot(ws_ref[h], zvb[:, h * HD:(h + 1) * HD], preferred_element_type=F32) for h in range(NG)], axis=1)
            mixed = mixed + be_ref[...]
            sgb = _sigmoid(gb)
            d_sgu = dm_ref[:, DA:DA + DG]
            d_gb = d_sgu * zu * mixed * (sgb * (1.0 + gb * (1.0 - sgb)))
            d_mixed = d_sgu * zu * (gb * sgb)
            d_u = d_sgu * mixed * (gb * sgb) * _dgelu(u)
            dbacc_ref[...] += d_mixed
            d_mixed_b = d_mixed.astype(MM)
            dzv_pieces = []
            for h in range(NG):
                dm_h = d_mixed_b[:, h * HD:(h + 1) * HD]
                dzv_pieces.append(jnp.dot(wst_ref[h], dm_h, preferred_element_type=F32))
                dws_ref[h] += lax.dot_general(dm_h, zvb[:, h * HD:(h + 1) * HD], NT, preferred_element_type=F32)
            d_vs = jnp.concatenate(dzv_pieces, axis=1) * _dgelu(vs)

            dp_ref[:, C_Q:C_K] = carry_ref[:, C_Q:C_K].astype(MM)
            dp_ref[:, C_K:C_V] = (carry_ref[:, C_K:C_V] + d_kband[0:WIN]).astype(MM)
            dp_ref[:, C_V:C_GA] = (carry_ref[:, C_V:C_GA] + d_vband[0:WIN]).astype(MM)
            dp_ref[:, C_GA:DIN] = carry_ref[:, C_GA:DIN].astype(MM)
            carry_ref[:, C_Q:C_K] = jnp.concatenate(dq_pieces, axis=1)
            carry_ref[:, C_K:C_V] = d_kband[WIN:2 * WIN]
            carry_ref[:, C_V:C_GA] = d_vband[WIN:2 * WIN]
            carry_ref[:, C_GA:C_U] = d_ga
            carry_ref[:, C_U:C_VS] = d_u
            carry_ref[:, C_VS:C_GB] = d_vs
            carry_ref[:, C_GB:DIN] = d_gb

        @pl.when(n == nb)
        def _():
            dp_ref[...] = carry_ref[...].astype(MM)
            ones = jnp.ones((8, HD), MM)
            acc = dbacc_ref[...]
            hi = acc.astype(MM)
            lo = (acc - hi.astype(F32)).astype(MM)
            for h in range(NG):
                sl = slice(h * HD, (h + 1) * HD)
                r = (lax.dot_general(ones, hi[:, sl], NT, preferred_element_type=F32)
                     + lax.dot_general(ones, lo[:, sl], NT, preferred_element_type=F32))
                dbs_ref[h:h + 1, :] = r[0:1, :]
            row = lax.broadcasted_iota(jnp.int32, (WIN, WIN), 0)
            cl = lax.broadcasted_iota(jnp.int32, (WIN, WIN), 1)
            for h in range(NG):
                dws_ref[h] = jnp.where(row >= cl, dws_ref[h], 0.0)

    last = nb - 1
    return pl.pallas_call(
        body,
        grid_spec=pltpu.PrefetchScalarGridSpec(
            num_scalar_prefetch=0,
            grid=(nb + 1,),
            in_specs=[
                pl.BlockSpec(memory_space=pltpu.SMEM),
                pl.BlockSpec((WIN, DIN), lambda n: (jnp.minimum(n, last), 0)),
                pl.BlockSpec((WIN, 2 * DKV), lambda n: (jnp.maximum(jnp.minimum(n, last) - 1, 0), C_K // (2 * DKV))),
                pl.BlockSpec((WIN, D), lambda n: (jnp.minimum(n, last), 0)),
                pl.BlockSpec((NKV, GRP * WIN, 2 * WIN), lambda n: (0, 0, 0)),
                pl.BlockSpec((1, HD), lambda n: (0, 0)),
                pl.BlockSpec((1, HD), lambda n: (0, 0)),
                pl.BlockSpec((NG, WIN, WIN), lambda n: (0, 0, 0)),
                pl.BlockSpec((NG, WIN, WIN), lambda n: (0, 0, 0)),
                pl.BlockSpec((WIN, DG), lambda n: (0, 0)),
            ],
            out_specs=[
                pl.BlockSpec((WIN, DIN), lambda n: (jnp.maximum(n - 1, 0), 0)),
                pl.BlockSpec((1, HD), lambda n: (0, 0)),
                pl.BlockSpec((1, HD), lambda n: (0, 0)),
                pl.BlockSpec((NQ, WIN), lambda n: (0, 0)),
                pl.BlockSpec((NG, WIN, WIN), lambda n: (0, 0, 0)),
                pl.BlockSpec((NG, WIN), lambda n: (0, 0)),
            ],
            scratch_shapes=[pltpu.VMEM((WIN, DIN), F32), pltpu.VMEM((WIN, DG), F32)],
        ),
        out_shape=[SDS((T, DIN), MM), SDS((1, HD), F32), SDS((1, HD), F32), SDS((NQ, WIN), F32),
                   SDS((NG, WIN, WIN), F32), SDS((NG, WIN), F32)],
        name="bwd_mix",
        compiler_params=_cp(("arbitrary",)),
    )(sinks, proj, proj, dmix, bias, q_norm, k_norm, ws_tril, ws_tril_t, b_exp)


def _row_tile(T):
    return min(512, T)


def _fwd_in_call(x, g_row, w_sh):
    T = x.shape[0]
    tm = _row_tile(T)

    def body(x_ref, g_ref, w_ref, proj_ref, h_ref):
        @pl.when(pl.program_id(1) == 0)
        def _():
            xv = x_ref[...]
            r = lax.rsqrt(jnp.mean(xv * xv, axis=-1, keepdims=True) + EPS)
            h_ref[...] = (xv * r * g_ref[...]).astype(MM)

        proj_ref[...] = jnp.dot(h_ref[...], w_ref[0], preferred_element_type=F32)

    return pl.pallas_call(
        body,
        grid=(T // tm, NCHIP),
        in_specs=[pl.BlockSpec((tm, D), lambda i, j: (i, 0)),
                  pl.BlockSpec((1, D), lambda i, j: (0, 0)),
                  pl.BlockSpec((1, D, SHW), lambda i, j: (j, 0, 0))],
        out_specs=[pl.BlockSpec((tm, SHW), lambda i, j: (i, j)),
                   pl.BlockSpec((tm, D), lambda i, j: (i, 0))],
        out_shape=[SDS((T, DIN), F32), SDS((T, D), MM)],
        name="fwd_in",
        compiler_params=_cp(("arbitrary", "arbitrary")),
    )(x, g_row, w_sh)


def _fwd_out_call(x, mix, w_out):
    T = x.shape[0]
    tm = _row_tile(T)

    def body(x_ref, mix_ref, w_ref, y_ref):
        y_ref[...] = x_ref[...] + jnp.dot(mix_ref[...], w_ref[...], preferred_element_type=F32)

    return pl.pallas_call(
        body,
        grid=(T // tm,),
        in_specs=[pl.BlockSpec((tm, D), lambda i: (i, 0)),
                  pl.BlockSpec((tm, D), lambda i: (i, 0)),
                  pl.BlockSpec((D, D), lambda i: (0, 0))],
        out_specs=pl.BlockSpec((tm, D), lambda i: (i, 0)),
        out_shape=SDS((T, D), F32),
        name="fwd_out",
        compiler_params=_cp(("arbitrary",)),
    )(x, mix, w_out)


def _fwd_out_loss_call(x, mix, w_out, target):
    T = x.shape[0]
    tm = _row_tile(T)

    def body(x_ref, mix_ref, w_ref, t_ref, dy_ref, loss_ref):
        @pl.when(pl.program_id(0) == 0)
        def _():
            loss_ref[...] = jnp.zeros_like(loss_ref)

        e = x_ref[...] + jnp.dot(mix_ref[...], w_ref[...], preferred_element_type=F32) - t_ref[...]
        dy_ref[...] = e * (1.0 / D)
        loss_ref[...] += (0.5 / D) * jnp.sum(jnp.sum(e * e, axis=1, keepdims=True), axis=0, keepdims=True)

    return pl.pallas_call(
        body,
        grid=(T // tm,),
        in_specs=[pl.BlockSpec((tm, D), lambda i: (i, 0)),
                  pl.BlockSpec((tm, D), lambda i: (i, 0)),
                  pl.BlockSpec((D, D), lambda i: (0, 0)),
                  pl.BlockSpec((tm, D), lambda i: (i, 0))],
        out_specs=[pl.BlockSpec((tm, D), lambda i: (i, 0)),
                   pl.BlockSpec((1, 1), lambda i: (0, 0))],
        out_shape=[SDS((T, D), F32), SDS((1, 1), F32)],
        name="fwd_out_loss",
        compiler_params=_cp(("arbitrary",)),
    )(x, mix, w_out, target)


def _bwd_out_call(dy, w_out_t, token):
    T = dy.shape[0]
    tm = _row_tile(T)

    def body(dy_ref, w_ref, token_ref, o_ref):
        o_ref[...] = jnp.dot(dy_ref[...].astype(MM), w_ref[...], preferred_element_type=F32)

    return pl.pallas_call(
        body,
        grid=(T // tm,),
        in_specs=[pl.BlockSpec((tm, D), lambda i: (i, 0)),
                  pl.BlockSpec((D, D), lambda i: (0, 0)),
                  pl.BlockSpec(memory_space=pl.ANY)],
        out_specs=pl.BlockSpec((tm, D), lambda i: (i, 0)),
        out_shape=SDS((T, D), F32),
        name="bwd_out",
        compiler_params=_cp(("arbitrary",)),
    )(dy, w_out_t, token)


def _bwd_in_call(dproj, w_in_t, x, dy, g_row):
    T = x.shape[0]
    tm = _row_tile(T)
    nk = NCHIP

    def body(dp_ref, w_ref, x_ref, dy_ref, g_ref, dx_ref, dg_ref, acc_ref):
        i, k = pl.program_id(0), pl.program_id(1)

        @pl.when((i == 0) & (k == 0))
        def _():
            dg_ref[...] = jnp.zeros_like(dg_ref)

        @pl.when(k == 0)
        def _():
            acc_ref[...] = jnp.zeros_like(acc_ref)

        acc_ref[...] += jnp.dot(dp_ref[...], w_ref[...], preferred_element_type=F32)

        @pl.when(k == nk - 1)
        def _():
            dh = acc_ref[...]
            xv = x_ref[...]
            r = lax.rsqrt(jnp.mean(xv * xv, axis=-1, keepdims=True) + EPS)
            gd = dh * g_ref[...]
            dx_ref[...] = dy_ref[...] + r * gd - xv * ((r * r * r) * jnp.mean(gd * xv, axis=-1, keepdims=True))
            dg_ref[...] += jnp.sum(dh * xv * r, axis=0, keepdims=True)

    return pl.pallas_call(
        body,
        grid=(T // tm, nk),
        in_specs=[pl.BlockSpec((tm, SHW), lambda i, k: (i, k)),
                  pl.BlockSpec((SHW, D), lambda i, k: (k, 0)),
                  pl.BlockSpec((tm, D), lambda i, k: (i, 0)),
                  pl.BlockSpec((tm, D), lambda i, k: (i, 0)),
                  pl.BlockSpec((1, D), lambda i, k: (0, 0))],
        out_specs=[pl.BlockSpec((tm, D), lambda i, k: (i, 0)),
                   pl.BlockSpec((1, D), lambda i, k: (0, 0))],
        out_shape=[SDS((T, D), F32), SDS((1, D), F32)],
        scratch_shapes=[pltpu.VMEM((tm, D), F32)],
        name="bwd_in",
        compiler_params=_cp(("arbitrary", "arbitrary")),
    )(dproj, w_in_t, x, dy, g_row)


def _grad_w_in_call(h, dproj):
    T = h.shape[0]
    tt = _row_tile(T)
    nt = T // tt

    def body(h_ref, dp_ref, o_ref, acc_ref):
        t = pl.program_id(1)

        @pl.when(t == 0)
        def _():
            acc_ref[...] = jnp.zeros_like(acc_ref)

        acc_ref[...] += lax.dot_general(h_ref[...], dp_ref[...], TN, preferred_element_type=F32)

        @pl.when(t == nt - 1)
        def _():
            o_ref[0] = acc_ref[...].astype(MM)

    return pl.pallas_call(
        body,
        grid=(NCHIP, nt),
        in_specs=[pl.BlockSpec((tt, D), lambda j, t: (t, 0)),
                  pl.BlockSpec((tt, SHW), lambda j, t: (t, j))],
        out_specs=pl.BlockSpec((1, D, SHW), lambda j, t: (j, 0, 0)),
        out_shape=SDS((NCHIP, D, SHW), MM),
        scratch_shapes=[pltpu.VMEM((D, SHW), F32)],
        name="grad_w_in",
        compiler_params=_cp(("arbitrary", "arbitrary")),
    )(h, dproj)


def _grad_w_out_call(mix, dy):
    T = mix.shape[0]
    tt = _row_tile(T)
    nt = T // tt
    tn = 1024

    def body(m_ref, dy_ref, o_ref, acc_ref):
        t = pl.program_id(1)

        @pl.when(t == 0)
        def _():
            acc_ref[...] = jnp.zeros_like(acc_ref)

        acc_ref[...] += lax.dot_general(m_ref[...], dy_ref[...].astype(MM), TN, preferred_element_type=F32)

        @pl.when(t == nt - 1)
        def _():
            o_ref[...] = acc_ref[...].astype(MM)

    return pl.pallas_call(
        body,
        grid=(D // tn, nt),
        in_specs=[pl.BlockSpec((tt, D), lambda j, t: (t, 0)),
                  pl.BlockSpec((tt, tn), lambda j, t: (t, j))],
        out_specs=pl.BlockSpec((D, tn), lambda j, t: (0, j)),
        out_shape=SDS((D, D), MM),
        scratch_shapes=[pltpu.VMEM((D, tn), F32)],
        name="grad_w_out",
        compiler_params=_cp(("arbitrary", "arbitrary")),
    )(mix, dy)


def _transpose_call(w, name):
    nbk, R, C = w.shape
    tr = 512

    def body(w_ref, o_ref):
        o_ref[...] = w_ref[0].T

    return pl.pallas_call(
        body,
        grid=(nbk, R // tr),
        in_specs=[pl.BlockSpec((1, tr, C), lambda j, i: (j, i, 0))],
        out_specs=pl.BlockSpec((C, tr), lambda j, i: (j, i)),
        out_shape=SDS((nbk * C, R), w.dtype),
        name=name,
        compiler_params=_cp(("arbitrary", "arbitrary")),
    )(w)


def _cast_to_slab_call(w, chip_idx, name):
    L, R, C = w.shape
    tr = 256

    def body(chip_ref, *refs):
        for l in range(L):
            refs[L + l][...] = refs[l][...].astype(MM)

    return pl.pallas_call(
        body,
        grid_spec=pltpu.PrefetchScalarGridSpec(
            num_scalar_prefetch=1,
            grid=(R // tr,),
            in_specs=[pl.BlockSpec((1, tr, C), functools.partial(lambda i, chip_ref, l: (l, i, 0), l=l))
                      for l in range(L)],
            out_specs=[pl.BlockSpec((1, tr, C), lambda i, chip_ref: (chip_ref[0], i, 0))] * L,
        ),
        out_shape=[SDS((NCHIP, R, C), MM)] * L,
        name=name,
        compiler_params=_cp(("arbitrary",)),
    )(chip_idx, *([w] * L))


def _adam_call(w, g_parts, m, v, name):
    R, C = w.shape
    tr = R
    for cand in (512, 256, 128, 64, 32, 16, 8):
        if R % cand == 0 and cand * C * 4 <= 1024 * 1024:
            tr = cand
            break
    c1 = 1.0 - B1 ** STEP
    c2 = 1.0 - B2 ** STEP
    ng = len(g_parts)

    def body(*refs):
        w_ref, m_ref, v_ref = refs[0], refs[1 + ng], refs[2 + ng]
        g_ref, d_ref, nm_ref, nv_ref = refs[3 + ng:]
        gv = refs[1][...]
        for k in range(1, ng):
            gv = gv + refs[1 + k][...]
        nm = B1 * m_ref[...] + (1.0 - B1) * gv
        nv = B2 * v_ref[...] + (1.0 - B2) * (gv * gv)
        g_ref[...] = gv
        nm_ref[...] = nm
        nv_ref[...] = nv
        d_ref[...] = -LR * ((nm / c1) / (jnp.sqrt(nv / c2) + ADAM_EPS) + WD * w_ref[...])

    spec = pl.BlockSpec((tr, C), lambda i: (i, 0))
    return pl.pallas_call(
        body,
        grid=(R // tr,),
        in_specs=[spec] * (3 + ng),
        out_specs=[spec] * 4,
        out_shape=[SDS((R, C), F32)] * 4,
        name=name,
        compiler_params=_cp(("arbitrary",)),
    )(w, *g_parts, m, v)


MESH = pl.DeviceIdType.MESH
ANY = pl.BlockSpec(memory_space=pl.ANY)
HBM = pl.BlockSpec(memory_space=pltpu.HBM)
SEMS = pl.BlockSpec(memory_space=pltpu.SEMAPHORE)
EFFECT = pltpu.SideEffectType.DATAFLOW_SIDE_EFFECTING
NDEV = 8


def _hbm(a):
    return pltpu.with_memory_space_constraint(a, pltpu.HBM)


def _place():
    x, y, c = lax.axis_index("x"), lax.axis_index("y"), lax.axis_index("c")
    others = [(1 - x, y), (x, 1 - y), (1 - x, 1 - y)]
    return x, y, c, 2 * x + y, others


def _flipped(x, y, c, r):
    return (1 - x if r & 4 else x, 1 - y if r & 2 else y, 1 - c if r & 1 else c)


def _rcopy(src, dst, ssem, rsem, dev):
    return pltpu.make_async_remote_copy(src_ref=src, dst_ref=dst, send_sem=ssem, recv_sem=rsem,
                                        device_id=dev, device_id_type=MESH)


def _gather_start_call(fulls):
    K = len(fulls)

    def body(*refs):
        full, ssem, rsem = refs[:K], refs[K:2 * K], refs[2 * K:3 * K]
        x, y, c, me, others = _place()
        for k in range(K):
            for j, (px, py) in enumerate(others):
                _rcopy(full[k].at[me], full[k].at[me], ssem[k].at[j], rsem[k].at[j], (px, py, c)).start()

    outs = pl.pallas_call(
        body,
        in_specs=[HBM] * K,
        out_specs=[SEMS] * (2 * K) + [HBM] * K,
        out_shape=[pltpu.SemaphoreType.DMA((3,))] * (2 * K) + [pltpu.HBM(f.shape, f.dtype) for f in fulls],
        input_output_aliases={k: 2 * K + k for k in range(K)},
        name="gather_start",
        compiler_params=pltpu.CompilerParams(has_side_effects=EFFECT),
    )(*[_hbm(f) for f in fulls])
    return list(outs[:K]), list(outs[K:2 * K]), list(outs[2 * K:])


def _gather_wait_call(fulls, ssems, rsems, after, name):
    K = len(fulls)

    def body(*refs):
        full, ssem, rsem = refs[:K], refs[K:2 * K], refs[2 * K:3 * K]
        x, y, c, me, others = _place()
        for k in range(K):
            for j, (px, py) in enumerate(others):
                cp = _rcopy(full[k].at[me], full[k].at[2 * px + py], ssem[k].at[j], rsem[k].at[j], (px, py, c))
                cp.wait_send()
                cp.wait_recv()

    outs = pl.pallas_call(
        body,
        in_specs=[HBM] * K + [SEMS] * (2 * K) + [ANY],
        out_specs=[HBM] * K,
        out_shape=[pltpu.HBM(f.shape, f.dtype) for f in fulls],
        input_output_aliases={k: k for k in range(K)},
        name=name,
        compiler_params=pltpu.CompilerParams(has_side_effects=EFFECT),
    )(*fulls, *ssems, *rsems, after)
    return list(outs)


def _grad_start_call(g_in, g_out, g_small, name):
    srcs = [g_in, g_out, g_small]
    lands = [lax.empty((3,) + g_in.shape[1:], g_in.dtype), lax.empty((3,) + g_out.shape[1:], g_out.dtype),
             lax.empty((NDEV,) + g_small.shape, g_small.dtype)]

    def body(gi, go, gs, ri, ro, rs, ssem, rsem, *outs):
        token = outs[-1]
        x, y, c, me, others = _place()
        for j, (px, py) in enumerate(others):
            _rcopy(gi.at[2 * px + py], ri.at[j], ssem.at[j], rsem.at[j], (px, py, c)).start()
            _rcopy(go.at[2 * px + py], ro.at[j], ssem.at[3 + j], rsem.at[3 + j], (px, py, c)).start()
        for r in range(1, NDEV):
            _rcopy(gs, rs.at[4 * x + 2 * y + c], ssem.at[5 + r], rsem.at[5 + r], _flipped(x, y, c, r)).start()
        token[...] = jnp.zeros_like(token)

    outs = pl.pallas_call(
        body,
        in_specs=[HBM] * 6,
        out_specs=[SEMS, SEMS] + [HBM] * 6 + [pl.BlockSpec(memory_space=pltpu.VMEM)],
        out_shape=[pltpu.SemaphoreType.DMA((13,)), pltpu.SemaphoreType.DMA((13,))]
        + [pltpu.HBM(a.shape, a.dtype) for a in srcs + lands] + [SDS((8, 128), F32)],
        input_output_aliases={k: 2 + k for k in range(6)},
        name=name,
        compiler_params=pltpu.CompilerParams(has_side_effects=EFFECT),
    )(*[_hbm(a) for a in srcs + lands])
    return list(outs[2:5]), list(outs[5:8]), outs[0], outs[1], outs[8]


def _grad_wait_call(srcs, lands, ssem, rsem, after, name):
    def body(gi, go, gs, ri, ro, rs, ssem, rsem, after_ref, *outs):
        x, y, c, me, others = _place()
        for j, (px, py) in enumerate(others):
            for src, land, k in ((gi, ri, j), (go, ro, 3 + j)):
                cp = _rcopy(src.at[2 * px + py], land.at[j], ssem.at[k], rsem.at[k], (px, py, c))
                cp.wait_send()
                cp.wait_recv()
        for r in range(1, NDEV):
            cp = _rcopy(gs, rs.at[0], ssem.at[5 + r], rsem.at[5 + r], _flipped(x, y, c, r))
            cp.wait_send()
            cp.wait_recv()

    arrs = list(srcs) + list(lands)
    outs = pl.pallas_call(
        body,
        in_specs=[HBM] * 6 + [SEMS, SEMS, ANY],
        out_specs=[HBM] * 6,
        out_shape=[pltpu.HBM(a.shape, a.dtype) for a in arrs],
        input_output_aliases={k: k for k in range(6)},
        name=name,
        compiler_params=pltpu.CompilerParams(has_side_effects=EFFECT),
    )(*arrs, ssem, rsem, after)
    return list(outs[:3]), list(outs[3:6])


def _sibling_swap_call(arrs):
    K = len(arrs)

    def body(*refs):
        a_refs, t_refs = refs[:K], refs[K:2 * K]
        ssem, rsem = refs[2 * K:]
        x, y, c, me, others = _place()
        cps = [_rcopy(a_refs[k], t_refs[k], ssem.at[k], rsem.at[k], (x, y, 1 - c)) for k in range(K)]
        for cp in cps:
            cp.start()
        for cp in cps:
            cp.wait()

    return pl.pallas_call(
        body,
        in_specs=[ANY] * K,
        out_specs=[ANY] * K,
        out_shape=[SDS(a.shape, a.dtype) for a in arrs],
        scratch_shapes=[pltpu.SemaphoreType.DMA((K,)), pltpu.SemaphoreType.DMA((K,))],
        name="grad_sibling_swap",
    )(*arrs)


def _rows_tile(H, C):
    for cand in (512, 256, 128, 64, 32, 16, 8):
        if H % cand == 0 and cand * C * 4 <= 2 * 1024 * 1024:
            return cand
    raise ValueError((H, C))


def _sum_recv_call(own, recv, chip_idx, stack, l):
    _, R, C = own.shape
    tr = _rows_tile(R, C)

    def body(chip_ref, own_ref, r0, r1, r2, stack_ref, o_ref):
        o_ref[...] = ((own_ref[...].astype(F32) + r0[...].astype(F32)) + r1[...].astype(F32)) + r2[...].astype(F32)

    return pl.pallas_call(
        body,
        grid_spec=pltpu.PrefetchScalarGridSpec(
            num_scalar_prefetch=1,
            grid=(R // tr,),
            in_specs=[pl.BlockSpec((1, tr, C), lambda i, chip_ref: (chip_ref[0], i, 0))]
            + [pl.BlockSpec((1, tr, C), functools.partial(lambda i, chip_ref, s: (s, i, 0), s=s)) for s in range(3)]
            + [ANY],
            out_specs=pl.BlockSpec((1, tr, C), lambda i, chip_ref: (l, i, 0)),
        ),
        out_shape=SDS(stack.shape, F32),
        input_output_aliases={5: 0},
        name="grad_sum_recv",
        compiler_params=_cp(("arbitrary",)),
    )(chip_idx, own, recv, recv, recv, stack)


def _sum_small_call(own, recv, dev_idx):
    RS, C = own.shape
    tr = _rows_tile(RS, C)

    def body(dev_ref, own_ref, *refs):
        o_ref = refs[NDEV]
        dev = dev_ref[0]
        acc = jnp.where(dev == 0, own_ref[...], refs[0][0])
        for s in range(1, NDEV):
            acc = acc + jnp.where(dev == s, own_ref[...], refs[s][0])
        o_ref[...] = acc

    return pl.pallas_call(
        body,
        grid_spec=pltpu.PrefetchScalarGridSpec(
            num_scalar_prefetch=1,
            grid=(RS // tr,),
            in_specs=[pl.BlockSpec((tr, C), lambda i, dev_ref: (i, 0))]
            + [pl.BlockSpec((1, tr, C), functools.partial(
                lambda i, dev_ref, s: (jnp.where(dev_ref[0] == s, (s + 1) % NDEV, s), i, 0), s=s)) for s in range(NDEV)],
            out_specs=pl.BlockSpec((tr, C), lambda i, dev_ref: (i, 0)),
        ),
        out_shape=SDS((RS, C), F32),
        name="grad_sum_small",
        compiler_params=_cp(("arbitrary",)),
    )(dev_idx, own, *([recv] * NDEV))


SMALL_ROWS_ALIGN = 128


def _pack_small(parts):
    flat = jnp.concatenate([p.reshape(-1) for p in parts])
    rows = -(-flat.shape[0] // (128 * SMALL_ROWS_ALIGN)) * SMALL_ROWS_ALIGN
    flat = jnp.pad(flat, (0, rows * 128 - flat.shape[0]))
    return flat.reshape(rows, 128)


def _unpack_small(packed, like):
    flat = packed.reshape(-1)
    out, off = [], 0
    for p in like:
        n = int(np.prod(p.shape))
        out.append(flat[off:off + n].reshape(p.shape))
        off += n
    return out


def kernel(x, norm_g, w_in, q_norm, k_norm, sinks, w_s, b_s, w_out, loss_target, m_norm_g, m_w_in, m_q_norm, m_k_norm, m_sinks, m_w_s, m_b_s, m_w_out, v_norm_g, v_w_in, v_q_norm, v_k_norm, v_sinks, v_w_s, v_b_s, v_w_out):
    L = norm_g.shape[0]
    xi, yi, ci = lax.axis_index("x"), lax.axis_index("y"), lax.axis_index("c")
    chip_idx = (2 * xi + yi).astype(jnp.int32).reshape(1)
    dev_idx = (4 * xi + 2 * yi + ci).astype(jnp.int32).reshape(1)
    bias = _alibi_bias()
    tri = jnp.tril(jnp.ones((WIN, WIN), F32))

    fin = _cast_to_slab_call(w_in, chip_idx, "cast_w_in")
    fout = _cast_to_slab_call(w_out, chip_idx, "cast_w_out")
    fulls = [a for l in range(L) for a in (fin[l], fout[l])]
    g_ssems, g_rsems, fulls = _gather_start_call(fulls)

    saved = []
    xs = x[0]
    dy = loss = None
    for l in range(L):
        sl = slice(2 * l, 2 * l + 2)
        w_in_l, w_out_l = _gather_wait_call(fulls[sl], g_ssems[sl], g_rsems[sl], xs, f"gather_wait_{l}")
        w_out_l = w_out_l.reshape(D, D)
        proj, h = _fwd_in_call(xs, norm_g[l:l + 1], w_in_l)
        ws_tril = (w_s[l] * tri).astype(MM)
        b_exp = jnp.repeat(b_s[l].T, HD, axis=1)
        mix = _fwd_mix_call(proj, bias, q_norm[l:l + 1], k_norm[l:l + 1], sinks[l], ws_tril, b_exp)
        saved.append((xs, proj, h, mix, ws_tril, b_exp, w_in_l, w_out_l))
        if l < L - 1:
            xs = _fwd_out_call(xs, mix, w_out_l)
        else:
            dy, loss = _fwd_out_loss_call(xs, mix, w_out_l, loss_target[0])

    s_in = lax.empty((L, D, SHW), F32)
    s_out = lax.empty((L, SHR, D), F32)
    small_sums = [None] * L

    def finish(pending, after):
        nonlocal s_in, s_out
        l, srcs, lands, ssem, rsem = pending
        srcs, lands = _grad_wait_call(srcs, lands, ssem, rsem, after, f"grad_wait_{l}")
        s_in = _sum_recv_call(srcs[0], lands[0], chip_idx, s_in, l)
        s_out = _sum_recv_call(srcs[1], lands[1], chip_idx, s_out, l)
        small_sums[l] = _sum_small_call(srcs[2], lands[2], dev_idx)

    token = jnp.zeros((8, 128), F32)
    pending = None
    for l in reversed(range(L)):
        xs, proj, h, mix, ws_tril, b_exp, w_in_l, w_out_l = saved[l]
        w_out_t = _transpose_call(w_out_l[None], "transpose_w_out")
        w_in_t = _transpose_call(w_in_l, "transpose_w_in")
        g_w_out = _grad_w_out_call(mix, dy).reshape(NCHIP, SHR, D)
        dmix = _bwd_out_call(dy, w_out_t, token)
        ws_tril_t = jnp.swapaxes(ws_tril, 1, 2)
        dproj, dwq, dwk, dsk, dws, dbs = _bwd_mix_call(
            proj, dmix, bias, q_norm[l:l + 1], k_norm[l:l + 1], sinks[l], ws_tril, ws_tril_t, b_exp)
        g_w_in = _grad_w_in_call(h, dproj)
        dy, dng = _bwd_in_call(dproj, w_in_t, xs, dy, norm_g[l:l + 1])
        if pending is not None:
            finish(pending, dy)
        g_small = _pack_small([dng, dwq, dwk, dsk[:, 0], dws, dbs])
        srcs, lands, ssem, rsem, token = _grad_start_call(g_w_in, g_w_out, g_small, f"grad_start_{l}")
        pending = (l, srcs, lands, ssem, rsem)
    finish(pending, token)
    grad_x = dy

    t_in, t_out = _sibling_swap_call([s_in, s_out])
    g_w_in, d_in, nm_in, nv_in = _adam_call(
        w_in.reshape(L * D, SHW), [s_in.reshape(L * D, SHW), t_in.reshape(L * D, SHW)],
        m_w_in.reshape(L * D, SHW), v_w_in.reshape(L * D, SHW), "adam_w_in")
    g_w_out, d_out, nm_out, nv_out = _adam_call(
        w_out.reshape(L * SHR, D), [s_out.reshape(L * SHR, D), t_out.reshape(L * SHR, D)],
        m_w_out.reshape(L * SHR, D), v_w_out.reshape(L * SHR, D), "adam_w_out")

    def pack_layers(parts):
        return jnp.concatenate([_pack_small([p[l] for p in parts]) for l in range(L)], axis=0)

    small_like = [norm_g, q_norm, k_norm, sinks, w_s, b_s]
    g_small, d_s, nm_s, nv_s = _adam_call(
        pack_layers(small_like), [jnp.concatenate(small_sums, axis=0)],
        pack_layers([m_norm_g, m_q_norm, m_k_norm, m_sinks, m_w_s, m_b_s]),
        pack_layers([v_norm_g, v_q_norm, v_k_norm, v_sinks, v_w_s, v_b_s]), "adam_small")

    def full(small, win, wout):
        rows = small.shape[0] // L
        per_layer = [_unpack_small(small[l * rows:(l + 1) * rows], [p[l] for p in small_like]) for l in range(L)]
        ng, qn, kn, sk, ws, bs = [jnp.stack([per_layer[l][i] for l in range(L)]) for i in range(6)]
        return [ng, win.reshape(w_in.shape), qn, kn, sk, ws, bs, wout.reshape(w_out.shape)]

    loss_all = lax.psum(loss[0, 0], ("x", "y", "c"))
    return (loss_all, grad_x[None], *full(g_small, g_w_in, g_w_out), *full(d_s, d_in, d_out),
            *full(nm_s, nm_in, nm_out), *full(nv_s, nv_in, nv_out))
```

```python
import functools
import math

import numpy as np
import jax
import jax.numpy as jnp
from jax import lax
from jax.experimental import pallas as pl
from jax.experimental.pallas import tpu as pltpu

F32 = jnp.float32
MM = jnp.bfloat16

D = 2048
HD = 64
DA = 1024
DKV = 256
DG = 1024
NQ, NKV, GRP, NG = 16, 4, 4, 16
WIN = 128
DIN = 5632
C_Q, C_K, C_V, C_GA, C_U, C_VS, C_GB = 0, 1024, 1280, 1536, 2560, 3584, 4608
NCHIP = 4
SHW = DIN // NCHIP
SHR = D // NCHIP
EPS = 1e-6
NEG = -1e30
SCALE = HD ** -0.5
INV_SQRT2 = 1.0 / math.sqrt(2.0)
INV_SQRT_2PI = 1.0 / math.sqrt(2.0 * math.pi)
LR, B1, B2, ADAM_EPS, WD, STEP = 0.001, 0.9, 0.999, 1e-08, 0.01, 10
VMEM_LIMIT = 56 * 1024 * 1024

SDS = jax.ShapeDtypeStruct
NT = (((1,), (1,)), ((), ()))
TN = (((0,), (0,)), ((), ()))


def _cp(sem=None):
    return pltpu.CompilerParams(dimension_semantics=sem, vmem_limit_bytes=VMEM_LIMIT)


def _sigmoid(x):
    return 1.0 / (1.0 + jnp.exp(-x))


def _gelu(x):
    return 0.5 * x * (1.0 + lax.erf(x * INV_SQRT2))


def _dgelu(x):
    return 0.5 * (1.0 + lax.erf(x * INV_SQRT2)) + x * jnp.exp(-0.5 * x * x) * INV_SQRT_2PI


def _alibi_bias():
    slopes = 2.0 ** (-8.0 * np.arange(1, NQ + 1) / NQ)
    dist = (np.arange(WIN)[:, None] + WIN) - np.arange(2 * WIN)[None, :]
    ok = (dist >= 0) & (dist < WIN)
    first = ok & (np.arange(2 * WIN)[None, :] >= WIN)
    val = -slopes[:, None, None] * dist[None].astype(np.float64)
    return jnp.asarray(np.stack([np.where(first[None], val, NEG), np.where(ok[None], val, NEG)]), dtype=F32)


def _half_sum_matrix():
    half = np.arange(LANE) // HD
    return jnp.asarray(half[:, None] == half[None, :], dtype=MM)


LANE = 128
NQT = DA // LANE
NKT = DKV // LANE


def _tiles(ref, c0, n):
    return jnp.concatenate([ref[:, c0 + j * LANE:c0 + (j + 1) * LANE] for j in range(n)], axis=0)


def _split(x):
    hi = x.astype(MM)
    return hi, (x - hi.astype(F32)).astype(MM)


def _half_sums(x, b2):
    hi, lo = _split(x)
    return jnp.dot(hi, b2, preferred_element_type=F32) + jnp.dot(lo, b2, preferred_element_type=F32)


def _attn_fwd(pm_ref, kvp_ref, bias_ref, wq2, wk2, b2, sink_ref):
    lo_half = lax.broadcasted_iota(jnp.int32, (1, LANE), 1) < HD
    q_ts = _tiles(pm_ref, C_Q, NQT)
    rq = lax.rsqrt(_half_sums(q_ts * q_ts, b2) * (1.0 / HD) + EPS)
    qs = (q_ts * rq * wq2).astype(MM)
    k_ts = jnp.concatenate([a[:, c0 + t * LANE:c0 + (t + 1) * LANE] for t in range(NKT)
                            for a, c0 in ((kvp_ref, 0), (pm_ref, C_K))], axis=0)
    rk = lax.rsqrt(_half_sums(k_ts * k_ts, b2) * (1.0 / HD) + EPS)
    kn = (k_ts * rk * wk2).astype(MM)
    v_ts = jnp.concatenate([a[:, c0 + t * LANE:c0 + (t + 1) * LANE] for t in range(NKT)
                            for a, c0 in ((kvp_ref, DKV), (pm_ref, C_V))], axis=0).astype(MM)
    ones = jnp.ones((2 * WIN, LANE), MM)
    km, vm = {}, {}
    for hk in range(NKV):
        t, eh = hk // 2, hk % 2
        sel = lo_half if eh == 0 else jnp.logical_not(lo_half)
        rows = slice(t * 2 * WIN, (t + 1) * 2 * WIN)
        k_same = jnp.where(sel, kn[rows], jnp.zeros_like(kn[rows]))
        v_same = jnp.where(sel, v_ts[rows], jnp.zeros_like(v_ts[rows]))
        km[hk, eh], km[hk, 1 - eh] = k_same, pltpu.roll(k_same, HD, axis=1)
        vm[hk, eh], vm[hk, 1 - eh] = v_same, pltpu.roll(v_same, HD, axis=1)
    heads = []
    for h in range(NQ):
        j, e, hk = h // 2, h % 2, h // GRP
        s = lax.dot_general(qs[j * WIN:(j + 1) * WIN], km[hk, e], NT, preferred_element_type=F32) + bias_ref[0, h]
        sink = sink_ref[h]
        m = jnp.maximum(jnp.max(s, axis=-1, keepdims=True), sink)
        p = jnp.exp(s - m)
        res = jnp.dot(p.astype(MM), jnp.concatenate([vm[hk, e], ones], axis=1), preferred_element_type=F32)
        esink = jnp.exp(sink - m)
        inv = 1.0 / (res[:, LANE:] + esink)
        heads.append(dict(p=p, inv=inv, esink=esink, o=res[:, :LANE] * inv))
    return dict(lo_half=lo_half, q_ts=q_ts, rq=rq, qs=qs, k_ts=k_ts, rk=rk, km=km, vm=vm, heads=heads)


def _sgu_mix(w_ref, zt, lo_half, j):
    zero = jnp.zeros_like(zt)
    return (jnp.dot(w_ref[2 * j], jnp.where(lo_half, zt, zero), preferred_element_type=F32)
            + jnp.dot(w_ref[2 * j + 1], jnp.where(lo_half, zero, zt), preferred_element_type=F32))


def _fwd_mix_call(proj, bias, wq2, wk2, b2, sinks, ws_tril, b_exp):
    T = proj.shape[0]
    nb = T // WIN

    def body(sink_ref, pm_ref, kvp_ref, bias_ref, wq_ref, wk_ref, b2_ref, ws_ref, be_ref, mix_ref):
        a = _attn_fwd(pm_ref, kvp_ref, bias_ref, wq_ref[...], wk_ref[...], b2_ref[...], sink_ref)
        for j in range(NQT):
            cols = slice(j * LANE, (j + 1) * LANE)
            ga = pm_ref[:, C_GA + j * LANE:C_GA + (j + 1) * LANE]
            attn = a["heads"][2 * j]["o"] + a["heads"][2 * j + 1]["o"]
            mix_ref[:, cols] = (attn * (ga * _sigmoid(ga))).astype(MM)
        zu = _gelu(pm_ref[:, C_U:C_U + DG])
        zv = _gelu(pm_ref[:, C_VS:C_VS + DG]).astype(MM)
        mixed = jnp.concatenate(
            [_sgu_mix(ws_ref, zv[:, j * LANE:(j + 1) * LANE], a["lo_half"], j) for j in range(NG // 2)], axis=1)
        mixed = mixed + be_ref[...]
        gb = pm_ref[:, C_GB:C_GB + DG]
        mix_ref[:, DA:DA + DG] = (zu * mixed * (gb * _sigmoid(gb))).astype(MM)

    return pl.pallas_call(
        body,
        grid=(nb,),
        in_specs=[
            pl.BlockSpec(memory_space=pltpu.SMEM),
            pl.BlockSpec((WIN, DIN), lambda n: (n, 0)),
            pl.BlockSpec((WIN, 2 * DKV), lambda n: (jnp.maximum(n - 1, 0), C_K // (2 * DKV))),
            pl.BlockSpec((1, NQ, WIN, 2 * WIN), lambda n: (jnp.minimum(n, 1), 0, 0, 0)),
            pl.BlockSpec((1, LANE), lambda n: (0, 0)),
            pl.BlockSpec((1, LANE), lambda n: (0, 0)),
            pl.BlockSpec((LANE, LANE), lambda n: (0, 0)),
            pl.BlockSpec((NG, WIN, WIN), lambda n: (0, 0, 0)),
            pl.BlockSpec((WIN, DG), lambda n: (0, 0)),
        ],
        out_specs=pl.BlockSpec((WIN, D), lambda n: (n, 0)),
        out_shape=SDS((T, D), MM),
        name="fwd_mix",
        compiler_params=_cp(("arbitrary",)),
    )(sinks, proj, proj, bias, wq2, wk2, b2, ws_tril, b_exp)


def _bwd_mix_call(proj, dmix, bias, wq2, wk2, b2, sinks, ws_tril, ws_tril_t, b_exp):
    T = proj.shape[0]
    nb = T // WIN

    def body(sink_ref, pm_ref, kvp_ref, dm_ref, bias_ref, wq_ref, wk_ref, b2_ref, ws_ref, wst_ref, be_ref,
             dp_ref, dwq_ref, dwk_ref, dsk_ref, dws_ref, dbs_ref, carry_ref, dbacc_ref):
        n = pl.program_id(0)

        @pl.when(n == 0)
        def _():
            carry_ref[...] = jnp.zeros_like(carry_ref)
            dbacc_ref[...] = jnp.zeros_like(dbacc_ref)
            dwq_ref[...] = jnp.zeros_like(dwq_ref)
            dwk_ref[...] = jnp.zeros_like(dwk_ref)
            dsk_ref[...] = jnp.zeros_like(dsk_ref)
            dws_ref[...] = jnp.zeros_like(dws_ref)
            dbs_ref[...] = jnp.zeros_like(dbs_ref)

        @pl.when(n < nb)
        def _():
            wq2, wk2, b2 = wq_ref[...], wk_ref[...], b2_ref[...]
            a = _attn_fwd(pm_ref, kvp_ref, bias_ref, wq2, wk2, b2, sink_ref)
            lo_half, heads, km, vm, qs = a["lo_half"], a["heads"], a["km"], a["vm"], a["qs"]
            hi_half = jnp.logical_not(lo_half)

            dp_ref[:, C_Q:C_K] = carry_ref[:, C_Q:C_K].astype(MM)
            dp_ref[:, C_GA:DIN] = carry_ref[:, C_GA:DIN].astype(MM)

            row_lo = lax.broadcasted_iota(jnp.int32, (LANE, 2 * WIN), 0) < HD
            pick = [jnp.where(row_lo, 1.0, 0.0).astype(MM), jnp.where(row_lo, 0.0, 1.0).astype(MM)]
            dqs_tiles, dk_acc, dv_acc = [], {}, {}
            for j in range(NQT):
                cols = slice(C_GA + j * LANE, C_GA + (j + 1) * LANE)
                ga = pm_ref[:, cols]
                sga = _sigmoid(ga)
                d_gated = dm_ref[:, j * LANE:(j + 1) * LANE]
                attn = heads[2 * j]["o"] + heads[2 * j + 1]["o"]
                carry_ref[:, cols] = d_gated * attn * (sga * (1.0 + ga * (1.0 - sga)))
                d_o = d_gated * (ga * sga)
                d_ob = d_o.astype(MM)
                zero = jnp.zeros_like(d_ob)
                dlt_hi, dlt_lo = _split(d_o * attn)
                qs_j = qs[j * WIN:(j + 1) * WIN]
                dqs = None
                for e in range(2):
                    h = 2 * j + e
                    hk, hd = h // GRP, heads[h]
                    sel = lo_half if e == 0 else hi_half
                    d_p = lax.dot_general(d_ob, vm[hk, e], NT, preferred_element_type=F32)
                    delta = (jnp.dot(dlt_hi, pick[e], preferred_element_type=F32)
                             + jnp.dot(dlt_lo, pick[e], preferred_element_type=F32))
                    pn = hd["p"] * jnp.concatenate([hd["inv"], hd["inv"]], axis=1)
                    d_s = (pn * (d_p - delta)).astype(MM)
                    dsk_ref[h:h + 1, :] -= jnp.sum(hd["esink"] * hd["inv"] * delta[:, :LANE], axis=0, keepdims=True)
                    t = jnp.dot(d_s, km[hk, e], preferred_element_type=F32)
                    dqs = t if dqs is None else dqs + t
                    dk_h = lax.dot_general(d_s, jnp.where(sel, qs_j, zero), TN, preferred_element_type=F32)
                    dv_h = lax.dot_general(pn.astype(MM), jnp.where(sel, d_ob, zero), TN, preferred_element_type=F32)
                    key = (hk, e == hk % 2)
                    dk_acc[key] = dk_h if key not in dk_acc else dk_acc[key] + dk_h
                    dv_acc[key] = dv_h if key not in dv_acc else dv_acc[key] + dv_h
                dqs_tiles.append(dqs)

            dqs_ts = jnp.concatenate(dqs_tiles, axis=0)
            q_ts, rq = a["q_ts"], a["rq"]
            gq = dqs_ts * wq2
            d_q = rq * gq - q_ts * (rq * rq * rq) * (_half_sums(gq * q_ts, b2) * (1.0 / HD))
            dwq_ref[...] += SCALE * jnp.sum(dqs_ts * q_ts * rq, axis=0, keepdims=True)
            for j in range(NQT):
                carry_ref[:, C_Q + j * LANE:C_Q + (j + 1) * LANE] = d_q[j * WIN:(j + 1) * WIN]

            dkn_tiles, dv_tiles = [], []
            for t in range(NKT):
                for acc, out in ((dk_acc, dkn_tiles), (dv_acc, dv_tiles)):
                    parts = [acc[hk, True] + pltpu.roll(acc[hk, False], HD, axis=1) for hk in (2 * t, 2 * t + 1)]
                    out.append(parts[0] + parts[1])
            dkn_ts = jnp.concatenate(dkn_tiles, axis=0)
            dv_ts = jnp.concatenate(dv_tiles, axis=0)
            k_ts, rk = a["k_ts"], a["rk"]
            gk = dkn_ts * wk2
            d_k = rk * gk - k_ts * (rk * rk * rk) * (_half_sums(gk * k_ts, b2) * (1.0 / HD))
            dwk_ref[...] += jnp.sum(dkn_ts * k_ts * rk, axis=0, keepdims=True)
            for t in range(NKT):
                for base, val in ((C_K, d_k), (C_V, dv_ts)):
                    cols = slice(base + t * LANE, base + (t + 1) * LANE)
                    r0 = t * 2 * WIN
                    dp_ref[:, cols] = (carry_ref[:, cols] + val[r0:r0 + WIN]).astype(MM)
                    carry_ref[:, cols] = val[r0 + WIN:r0 + 2 * WIN]

            u = pm_ref[:, C_U:C_U + DG]
            vs = pm_ref[:, C_VS:C_VS + DG]
            gb = pm_ref[:, C_GB:C_GB + DG]
            zu = _gelu(u)
            zvb = _gelu(vs).astype(MM)
            mixed = jnp.concatenate(
                [_sgu_mix(ws_ref, zvb[:, j * LANE:(j + 1) * LANE], lo_half, j) for j in range(NG // 2)], axis=1)
            mixed = mixed + be_ref[...]
            sgb = _sigmoid(gb)
            d_sgu = dm_ref[:, DA:DA + DG]
            carry_ref[:, C_GB:DIN] = d_sgu * zu * mixed * (sgb * (1.0 + gb * (1.0 - sgb)))
            d_mixed = d_sgu * zu * (gb * sgb)
            carry_ref[:, C_U:C_VS] = d_sgu * mixed * (gb * sgb) * _dgelu(u)
            dbacc_ref[...] += d_mixed
            dmb = d_mixed.astype(MM)
            dzv_tiles = []
            for j in range(NG // 2):
                dt = dmb[:, j * LANE:(j + 1) * LANE]
                zt = zvb[:, j * LANE:(j + 1) * LANE]
                zero = jnp.zeros_like(dt)
                dzv_tiles.append(_sgu_mix(wst_ref, dt, lo_half, j))
                dws_ref[2 * j] += lax.dot_general(jnp.where(lo_half, dt, zero), zt, NT, preferred_element_type=F32)
                dws_ref[2 * j + 1] += lax.dot_general(jnp.where(lo_half, zero, dt), zt, NT, preferred_element_type=F32)
            carry_ref[:, C_VS:C_GB] = jnp.concatenate(dzv_tiles, axis=1) * _dgelu(vs)

        @pl.when(n == nb)
        def _():
            dp_ref[...] = carry_ref[...].astype(MM)
            lo_half = lax.broadcasted_iota(jnp.int32, (8, LANE), 1) < HD
            ones = [jnp.where(lo_half, 1.0, 0.0).astype(MM), jnp.where(lo_half, 0.0, 1.0).astype(MM)]
            hi, lo = _split(dbacc_ref[...])
            for h in range(NG):
                sl = slice((h // 2) * LANE, (h // 2 + 1) * LANE)
                r = (lax.dot_general(ones[h % 2], hi[:, sl], NT, preferred_element_type=F32)
                     + lax.dot_general(ones[h % 2], lo[:, sl], NT, preferred_element_type=F32))
                dbs_ref[h:h + 1, :] = r[0:1, :]
            row = lax.broadcasted_iota(jnp.int32, (WIN, WIN), 0)
            cl = lax.broadcasted_iota(jnp.int32, (WIN, WIN), 1)
            for h in range(NG):
                dws_ref[h] = jnp.where(row >= cl, dws_ref[h], 0.0)

    last = nb - 1
    return pl.pallas_call(
        body,
        grid_spec=pltpu.PrefetchScalarGridSpec(
            num_scalar_prefetch=0,
            grid=(nb + 1,),
            in_specs=[
                pl.BlockSpec(memory_space=pltpu.SMEM),
                pl.BlockSpec((WIN, DIN), lambda n: (jnp.minimum(n, last), 0)),
                pl.BlockSpec((WIN, 2 * DKV), lambda n: (jnp.maximum(jnp.minimum(n, last) - 1, 0), C_K // (2 * DKV))),
                pl.BlockSpec((WIN, D), lambda n: (jnp.minimum(n, last), 0)),
                pl.BlockSpec((1, NQ, WIN, 2 * WIN), lambda n: (jnp.minimum(n, 1), 0, 0, 0)),
                pl.BlockSpec((1, LANE), lambda n: (0, 0)),
                pl.BlockSpec((1, LANE), lambda n: (0, 0)),
                pl.BlockSpec((LANE, LANE), lambda n: (0, 0)),
                pl.BlockSpec((NG, WIN, WIN), lambda n: (0, 0, 0)),
                pl.BlockSpec((NG, WIN, WIN), lambda n: (0, 0, 0)),
                pl.BlockSpec((WIN, DG), lambda n: (0, 0)),
            ],
            out_specs=[
                pl.BlockSpec((WIN, DIN), lambda n: (jnp.maximum(n - 1, 0), 0)),
                pl.BlockSpec((1, LANE), lambda n: (0, 0)),
                pl.BlockSpec((1, LANE), lambda n: (0, 0)),
                pl.BlockSpec((NQ, WIN), lambda n: (0, 0)),
                pl.BlockSpec((NG, WIN, WIN), lambda n: (0, 0, 0)),
                pl.BlockSpec((NG, WIN), lambda n: (0, 0)),
            ],
            scratch_shapes=[pltpu.VMEM((WIN, DIN), F32), pltpu.VMEM((WIN, DG), F32)],
        ),
        out_shape=[SDS((T, DIN), MM), SDS((1, LANE), F32), SDS((1, LANE), F32), SDS((NQ, WIN), F32),
                   SDS((NG, WIN, WIN), F32), SDS((NG, WIN), F32)],
        name="bwd_mix",
        compiler_params=_cp(("arbitrary",)),
    )(sinks, proj, proj, dmix, bias, wq2, wk2, b2, ws_tril, ws_tril_t, b_exp)


def _row_tile(T):
    return min(512, T)


def _fwd_in_call(x, g_row, w_sh):
    T = x.shape[0]
    tm = _row_tile(T)

    def body(x_ref, g_ref, w_ref, proj_ref, h_ref):
        @pl.when(pl.program_id(1) == 0)
        def _():
            xv = x_ref[...]
            r = lax.rsqrt(jnp.mean(xv * xv, axis=-1, keepdims=True) + EPS)
            h_ref[...] = (xv * r * g_ref[...]).astype(MM)

        proj_ref[...] = jnp.dot(h_ref[...], w_ref[0], preferred_element_type=F32)

    return pl.pallas_call(
        body,
        grid=(T // tm, NCHIP),
        in_specs=[pl.BlockSpec((tm, D), lambda i, j: (i, 0)),
                  pl.BlockSpec((1, D), lambda i, j: (0, 0)),
                  pl.BlockSpec((1, D, SHW), lambda i, j: (j, 0, 0))],
        out_specs=[pl.BlockSpec((tm, SHW), lambda i, j: (i, j)),
                   pl.BlockSpec((tm, D), lambda i, j: (i, 0))],
        out_shape=[SDS((T, DIN), F32), SDS((T, D), MM)],
        name="fwd_in",
        compiler_params=_cp(("arbitrary", "arbitrary")),
    )(x, g_row, w_sh)


def _fwd_out_call(x, mix, w_out):
    T = x.shape[0]
    tm = _row_tile(T)

    def body(x_ref, mix_ref, w_ref, y_ref):
        y_ref[...] = x_ref[...] + jnp.dot(mix_ref[...], w_ref[...], preferred_element_type=F32)

    return pl.pallas_call(
        body,
        grid=(T // tm,),
        in_specs=[pl.BlockSpec((tm, D), lambda i: (i, 0)),
                  pl.BlockSpec((tm, D), lambda i: (i, 0)),
                  pl.BlockSpec((D, D), lambda i: (0, 0))],
        out_specs=pl.BlockSpec((tm, D), lambda i: (i, 0)),
        out_shape=SDS((T, D), F32),
        name="fwd_out",
        compiler_params=_cp(("arbitrary",)),
    )(x, mix, w_out)


def _fwd_out_loss_call(x, mix, w_out, target):
    T = x.shape[0]
    tm = _row_tile(T)

    def body(x_ref, mix_ref, w_ref, t_ref, dy_ref, loss_ref):
        @pl.when(pl.program_id(0) == 0)
        def _():
            loss_ref[...] = jnp.zeros_like(loss_ref)

        e = x_ref[...] + jnp.dot(mix_ref[...], w_ref[...], preferred_element_type=F32) - t_ref[...]
        dy_ref[...] = e * (1.0 / D)
        loss_ref[...] += (0.5 / D) * jnp.sum(jnp.sum(e * e, axis=1, keepdims=True), axis=0, keepdims=True)

    return pl.pallas_call(
        body,
        grid=(T // tm,),
        in_specs=[pl.BlockSpec((tm, D), lambda i: (i, 0)),
                  pl.BlockSpec((tm, D), lambda i: (i, 0)),
                  pl.BlockSpec((D, D), lambda i: (0, 0)),
                  pl.BlockSpec((tm, D), lambda i: (i, 0))],
        out_specs=[pl.BlockSpec((tm, D), lambda i: (i, 0)),
                   pl.BlockSpec((1, 1), lambda i: (0, 0))],
        out_shape=[SDS((T, D), F32), SDS((1, 1), F32)],
        name="fwd_out_loss",
        compiler_params=_cp(("arbitrary",)),
    )(x, mix, w_out, target)


def _bwd_out_call(dy, w_out_t, token):
    T = dy.shape[0]
    tm = _row_tile(T)

    def body(dy_ref, w_ref, token_ref, o_ref):
        o_ref[...] = jnp.dot(dy_ref[...].astype(MM), w_ref[...], preferred_element_type=F32)

    return pl.pallas_call(
        body,
        grid=(T // tm,),
        in_specs=[pl.BlockSpec((tm, D), lambda i: (i, 0)),
                  pl.BlockSpec((D, D), lambda i: (0, 0)),
                  pl.BlockSpec(memory_space=pl.ANY)],
        out_specs=pl.BlockSpec((tm, D), lambda i: (i, 0)),
        out_shape=SDS((T, D), F32),
        name="bwd_out",
        compiler_params=_cp(("arbitrary",)),
    )(dy, w_out_t, token)


def _bwd_in_call(dproj, w_in_t, x, dy, g_row):
    T = x.shape[0]
    tm = _row_tile(T)
    nk = NCHIP

    def body(dp_ref, w_ref, x_ref, dy_ref, g_ref, dx_ref, dg_ref, acc_ref):
        i, k = pl.program_id(0), pl.program_id(1)

        @pl.when((i == 0) & (k == 0))
        def _():
            dg_ref[...] = jnp.zeros_like(dg_ref)

        @pl.when(k == 0)
        def _():
            acc_ref[...] = jnp.zeros_like(acc_ref)

        acc_ref[...] += jnp.dot(dp_ref[...], w_ref[...], preferred_element_type=F32)

        @pl.when(k == nk - 1)
        def _():
            dh = acc_ref[...]
            xv = x_ref[...]
            r = lax.rsqrt(jnp.mean(xv * xv, axis=-1, keepdims=True) + EPS)
            gd = dh * g_ref[...]
            dx_ref[...] = dy_ref[...] + r * gd - xv * ((r * r * r) * jnp.mean(gd * xv, axis=-1, keepdims=True))
            dg_ref[...] += jnp.sum(dh * xv * r, axis=0, keepdims=True)

    return pl.pallas_call(
        body,
        grid=(T // tm, nk),
        in_specs=[pl.BlockSpec((tm, SHW), lambda i, k: (i, k)),
                  pl.BlockSpec((SHW, D), lambda i, k: (k, 0)),
                  pl.BlockSpec((tm, D), lambda i, k: (i, 0)),
                  pl.BlockSpec((tm, D), lambda i, k: (i, 0)),
                  pl.BlockSpec((1, D), lambda i, k: (0, 0))],
        out_specs=[pl.BlockSpec((tm, D), lambda i, k: (i, 0)),
                   pl.BlockSpec((1, D), lambda i, k: (0, 0))],
        out_shape=[SDS((T, D), F32), SDS((1, D), F32)],
        scratch_shapes=[pltpu.VMEM((tm, D), F32)],
        name="bwd_in",
        compiler_params=_cp(("arbitrary", "arbitrary")),
    )(dproj, w_in_t, x, dy, g_row)


def _grad_w_in_call(h, dproj):
    T = h.shape[0]
    tt = _row_tile(T)
    nt = T // tt

    def body(h_ref, dp_ref, o_ref, acc_ref):
        t = pl.program_id(1)

        @pl.when(t == 0)
        def _():
            acc_ref[...] = jnp.zeros_like(acc_ref)

        acc_ref[...] += lax.dot_general(h_ref[...], dp_ref[...], TN, preferred_element_type=F32)

        @pl.when(t == nt - 1)
        def _():
            o_ref[0] = acc_ref[...].astype(MM)

    return pl.pallas_call(
        body,
        grid=(NCHIP, nt),
        in_specs=[pl.BlockSpec((tt, D), lambda j, t: (t, 0)),
                  pl.BlockSpec((tt, SHW), lambda j, t: (t, j))],
        out_specs=pl.BlockSpec((1, D, SHW), lambda j, t: (j, 0, 0)),
        out_shape=SDS((NCHIP, D, SHW), MM),
        scratch_shapes=[pltpu.VMEM((D, SHW), F32)],
        name="grad_w_in",
        compiler_params=_cp(("arbitrary", "arbitrary")),
    )(h, dproj)


def _grad_w_out_call(mix, dy):
    T = mix.shape[0]
    tt = _row_tile(T)
    nt = T // tt
    tn = 1024

    def body(m_ref, dy_ref, o_ref, acc_ref):
        t = pl.program_id(1)

        @pl.when(t == 0)
        def _():
            acc_ref[...] = jnp.zeros_like(acc_ref)

        acc_ref[...] += lax.dot_general(m_ref[...], dy_ref[...].astype(MM), TN, preferred_element_type=F32)

        @pl.when(t == nt - 1)
        def _():
            o_ref[...] = acc_ref[...].astype(MM)

    return pl.pallas_call(
        body,
        grid=(D // tn, nt),
        in_specs=[pl.BlockSpec((tt, D), lambda j, t: (t, 0)),
                  pl.BlockSpec((tt, tn), lambda j, t: (t, j))],
        out_specs=pl.BlockSpec((D, tn), lambda j, t: (0, j)),
        out_shape=SDS((D, D), MM),
        scratch_shapes=[pltpu.VMEM((D, tn), F32)],
        name="grad_w_out",
        compiler_params=_cp(("arbitrary", "arbitrary")),
    )(mix, dy)


def _transpose_call(w, name):
    nbk, R, C = w.shape
    tr = 512

    def body(w_ref, o_ref):
        o_ref[...] = w_ref[0].T

    return pl.pallas_call(
        body,
        grid=(nbk, R // tr),
        in_specs=[pl.BlockSpec((1, tr, C), lambda j, i: (j, i, 0))],
        out_specs=pl.BlockSpec((C, tr), lambda j, i: (j, i)),
        out_shape=SDS((nbk * C, R), w.dtype),
        name=name,
        compiler_params=_cp(("arbitrary", "arbitrary")),
    )(w)


def _cast_to_slab_call(w, chip_idx, name):
    L, R, C = w.shape
    tr = 256

    def body(chip_ref, *refs):
        for l in range(L):
            refs[L + l][...] = refs[l][...].astype(MM)

    return pl.pallas_call(
        body,
        grid_spec=pltpu.PrefetchScalarGridSpec(
            num_scalar_prefetch=1,
            grid=(R // tr,),
            in_specs=[pl.BlockSpec((1, tr, C), functools.partial(lambda i, chip_ref, l: (l, i, 0), l=l))
                      for l in range(L)],
            out_specs=[pl.BlockSpec((1, tr, C), lambda i, chip_ref: (chip_ref[0], i, 0))] * L,
        ),
        out_shape=[SDS((NCHIP, R, C), MM)] * L,
        name=name,
        compiler_params=_cp(("arbitrary",)),
    )(chip_idx, *([w] * L))


def _adam_call(w, g_parts, m, v, name):
    R, C = w.shape
    tr = R
    for cand in (512, 256, 128, 64, 32, 16, 8):
        if R % cand == 0 and cand * C * 4 <= 1024 * 1024:
            tr = cand
            break
    c1 = 1.0 - B1 ** STEP
    c2 = 1.0 - B2 ** STEP
    ng = len(g_parts)

    def body(*refs):
        w_ref, m_ref, v_ref = refs[0], refs[1 + ng], refs[2 + ng]
        g_ref, d_ref, nm_ref, nv_ref = refs[3 + ng:]
        gv = refs[1][...]
        for k in range(1, ng):
            gv = gv + refs[1 + k][...]
        nm = B1 * m_ref[...] + (1.0 - B1) * gv
        nv = B2 * v_ref[...] + (1.0 - B2) * (gv * gv)
        g_ref[...] = gv
        nm_ref[...] = nm
        nv_ref[...] = nv
        d_ref[...] = -LR * ((nm / c1) / (jnp.sqrt(nv / c2) + ADAM_EPS) + WD * w_ref[...])

    spec = pl.BlockSpec((tr, C), lambda i: (i, 0))
    return pl.pallas_call(
        body,
        grid=(R // tr,),
        in_specs=[spec] * (3 + ng),
        out_specs=[spec] * 4,
        out_shape=[SDS((R, C), F32)] * 4,
        name=name,
        compiler_params=_cp(("arbitrary",)),
    )(w, *g_parts, m, v)


MESH = pl.DeviceIdType.MESH
ANY = pl.BlockSpec(memory_space=pl.ANY)
HBM = pl.BlockSpec(memory_space=pltpu.HBM)
SEMS = pl.BlockSpec(memory_space=pltpu.SEMAPHORE)
EFFECT = pltpu.SideEffectType.DATAFLOW_SIDE_EFFECTING
NDEV = 8


def _hbm(a):
    return pltpu.with_memory_space_constraint(a, pltpu.HBM)


def _place():
    x, y, c = lax.axis_index("x"), lax.axis_index("y"), lax.axis_index("c")
    others = [(1 - x, y), (x, 1 - y), (1 - x, 1 - y)]
    return x, y, c, 2 * x + y, others


def _flipped(x, y, c, r):
    return (1 - x if r & 4 else x, 1 - y if r & 2 else y, 1 - c if r & 1 else c)


def _rcopy(src, dst, ssem, rsem, dev):
    return pltpu.make_async_remote_copy(src_ref=src, dst_ref=dst, send_sem=ssem, recv_sem=rsem,
                                        device_id=dev, device_id_type=MESH)


def _gather_start_call(fulls):
    K = len(fulls)

    def body(*refs):
        full, ssem, rsem = refs[:K], refs[K:2 * K], refs[2 * K:3 * K]
        x, y, c, me, others = _place()
        for k in range(K):
            for j, (px, py) in enumerate(others):
                _rcopy(full[k].at[me], full[k].at[me], ssem[k].at[j], rsem[k].at[j], (px, py, c)).start()

    outs = pl.pallas_call(
        body,
        in_specs=[HBM] * K,
        out_specs=[SEMS] * (2 * K) + [HBM] * K,
        out_shape=[pltpu.SemaphoreType.DMA((3,))] * (2 * K) + [pltpu.HBM(f.shape, f.dtype) for f in fulls],
        input_output_aliases={k: 2 * K + k for k in range(K)},
        name="gather_start",
        compiler_params=pltpu.CompilerParams(has_side_effects=EFFECT),
    )(*[_hbm(f) for f in fulls])
    return list(outs[:K]), list(outs[K:2 * K]), list(outs[2 * K:])


def _gather_wait_call(fulls, ssems, rsems, after, name):
    K = len(fulls)

    def body(*refs):
        full, ssem, rsem = refs[:K], refs[K:2 * K], refs[2 * K:3 * K]
        x, y, c, me, others = _place()
        for k in range(K):
            for j, (px, py) in enumerate(others):
                cp = _rcopy(full[k].at[me], full[k].at[2 * px + py], ssem[k].at[j], rsem[k].at[j], (px, py, c))
                cp.wait_send()
                cp.wait_recv()

    outs = pl.pallas_call(
        body,
        in_specs=[HBM] * K + [SEMS] * (2 * K) + [ANY],
        out_specs=[HBM] * K,
        out_shape=[pltpu.HBM(f.shape, f.dtype) for f in fulls],
        input_output_aliases={k: k for k in range(K)},
        name=name,
        compiler_params=pltpu.CompilerParams(has_side_effects=EFFECT),
    )(*fulls, *ssems, *rsems, after)
    return list(outs)


def _grad_start_call(g_in, g_out, g_small, name):
    srcs = [g_in, g_out, g_small]
    lands = [lax.empty((3,) + g_in.shape[1:], g_in.dtype), lax.empty((3,) + g_out.shape[1:], g_out.dtype),
             lax.empty((NDEV,) + g_small.shape, g_small.dtype)]

    def body(gi, go, gs, ri, ro, rs, ssem, rsem, *outs):
        token = outs[-1]
        x, y, c, me, others = _place()
        for j, (px, py) in enumerate(others):
            _rcopy(gi.at[2 * px + py], ri.at[j], ssem.at[j], rsem.at[j], (px, py, c)).start()
            _rcopy(go.at[2 * px + py], ro.at[j], ssem.at[3 + j], rsem.at[3 + j], (px, py, c)).start()
        for r in range(1, NDEV):
            _rcopy(gs, rs.at[4 * x + 2 * y + c], ssem.at[5 + r], rsem.at[5 + r], _flipped(x, y, c, r)).start()
        token[...] = jnp.zeros_like(token)

    outs = pl.pallas_call(
        body,
        in_specs=[HBM] * 6,
        out_specs=[SEMS, SEMS] + [HBM] * 6 + [pl.BlockSpec(memory_space=pltpu.VMEM)],
        out_shape=[pltpu.SemaphoreType.DMA((13,)), pltpu.SemaphoreType.DMA((13,))]
        + [pltpu.HBM(a.shape, a.dtype) for a in srcs + lands] + [SDS((8, 128), F32)],
        input_output_aliases={k: 2 + k for k in range(6)},
        name=name,
        compiler_params=pltpu.CompilerParams(has_side_effects=EFFECT),
    )(*[_hbm(a) for a in srcs + lands])
    return list(outs[2:5]), list(outs[5:8]), outs[0], outs[1], outs[8]


def _grad_wait_call(srcs, lands, ssem, rsem, after, name):
    def body(gi, go, gs, ri, ro, rs, ssem, rsem, after_ref, *outs):
        x, y, c, me, others = _place()
        for j, (px, py) in enumerate(others):
            for src, land, k in ((gi, ri, j), (go, ro, 3 + j)):
                cp = _rcopy(src.at[2 * px + py], land.at[j], ssem.at[k], rsem.at[k], (px, py, c))
                cp.wait_send()
                cp.wait_recv()
        for r in range(1, NDEV):
            cp = _rcopy(gs, rs.at[0], ssem.at[5 + r], rsem.at[5 + r], _flipped(x, y, c, r))
            cp.wait_send()
            cp.wait_recv()

    arrs = list(srcs) + list(lands)
    outs = pl.pallas_call(
        body,
        in_specs=[HBM] * 6 + [SEMS, SEMS, ANY],
        out_specs=[HBM] * 6,
        out_shape=[pltpu.HBM(a.shape, a.dtype) for a in arrs],
        input_output_aliases={k: k for k in range(6)},
        name=name,
        compiler_params=pltpu.CompilerParams(has_side_effects=EFFECT),
    )(*arrs, ssem, rsem, after)
    return list(outs[:3]), list(outs[3:6])


def _sibling_swap_call(arrs):
    K = len(arrs)

    def body(*refs):
        a_refs, t_refs = refs[:K], refs[K:2 * K]
        ssem, rsem = refs[2 * K:]
        x, y, c, me, others = _place()
        cps = [_rcopy(a_refs[k], t_refs[k], ssem.at[k], rsem.at[k], (x, y, 1 - c)) for k in range(K)]
        for cp in cps:
            cp.start()
        for cp in cps:
            cp.wait()

    return pl.pallas_call(
        body,
        in_specs=[ANY] * K,
        out_specs=[ANY] * K,
        out_shape=[SDS(a.shape, a.dtype) for a in arrs],
        scratch_shapes=[pltpu.SemaphoreType.DMA((K,)), pltpu.SemaphoreType.DMA((K,))],
        name="grad_sibling_swap",
    )(*arrs)


def _rows_tile(H, C):
    for cand in (512, 256, 128, 64, 32, 16, 8):
        if H % cand == 0 and cand * C * 4 <= 2 * 1024 * 1024:
            return cand
    raise ValueError((H, C))


def _sum_recv_call(own, recv, chip_idx, stack, l):
    _, R, C = own.shape
    tr = _rows_tile(R, C)

    def body(chip_ref, own_ref, r0, r1, r2, stack_ref, o_ref):
        o_ref[...] = ((own_ref[...].astype(F32) + r0[...].astype(F32)) + r1[...].astype(F32)) + r2[...].astype(F32)

    return pl.pallas_call(
        body,
        grid_spec=pltpu.PrefetchScalarGridSpec(
            num_scalar_prefetch=1,
            grid=(R // tr,),
            in_specs=[pl.BlockSpec((1, tr, C), lambda i, chip_ref: (chip_ref[0], i, 0))]
            + [pl.BlockSpec((1, tr, C), functools.partial(lambda i, chip_ref, s: (s, i, 0), s=s)) for s in range(3)]
            + [ANY],
            out_specs=pl.BlockSpec((1, tr, C), lambda i, chip_ref: (l, i, 0)),
        ),
        out_shape=SDS(stack.shape, F32),
        input_output_aliases={5: 0},
        name="grad_sum_recv",
        compiler_params=_cp(("arbitrary",)),
    )(chip_idx, own, recv, recv, recv, stack)


def _sum_small_call(own, recv, dev_idx):
    RS, C = own.shape
    tr = _rows_tile(RS, C)

    def body(dev_ref, own_ref, *refs):
        o_ref = refs[NDEV]
        dev = dev_ref[0]
        acc = jnp.where(dev == 0, own_ref[...], refs[0][0])
        for s in range(1, NDEV):
            acc = acc + jnp.where(dev == s, own_ref[...], refs[s][0])
        o_ref[...] = acc

    return pl.pallas_call(
        body,
        grid_spec=pltpu.PrefetchScalarGridSpec(
            num_scalar_prefetch=1,
            grid=(RS // tr,),
            in_specs=[pl.BlockSpec((tr, C), lambda i, dev_ref: (i, 0))]
            + [pl.BlockSpec((1, tr, C), functools.partial(
                lambda i, dev_ref, s: (jnp.where(dev_ref[0] == s, (s + 1) % NDEV, s), i, 0), s=s)) for s in range(NDEV)],
            out_specs=pl.BlockSpec((tr, C), lambda i, dev_ref: (i, 0)),
        ),
        out_shape=SDS((RS, C), F32),
        name="grad_sum_small",
        compiler_params=_cp(("arbitrary",)),
    )(dev_idx, own, *([recv] * NDEV))


SMALL_ROWS_ALIGN = 128


def _pack_small(parts):
    flat = jnp.concatenate([p.reshape(-1) for p in parts])
    rows = -(-flat.shape[0] // (128 * SMALL_ROWS_ALIGN)) * SMALL_ROWS_ALIGN
    flat = jnp.pad(flat, (0, rows * 128 - flat.shape[0]))
    return flat.reshape(rows, 128)


def _unpack_small(packed, like):
    flat = packed.reshape(-1)
    out, off = [], 0
    for p in like:
        n = int(np.prod(p.shape))
        out.append(flat[off:off + n].reshape(p.shape))
        off += n
    return out


def kernel(x, norm_g, w_in, q_norm, k_norm, sinks, w_s, b_s, w_out, loss_target, m_norm_g, m_w_in, m_q_norm, m_k_norm, m_sinks, m_w_s, m_b_s, m_w_out, v_norm_g, v_w_in, v_q_norm, v_k_norm, v_sinks, v_w_s, v_b_s, v_w_out):
    L = norm_g.shape[0]
    xi, yi, ci = lax.axis_index("x"), lax.axis_index("y"), lax.axis_index("c")
    chip_idx = (2 * xi + yi).astype(jnp.int32).reshape(1)
    dev_idx = (4 * xi + 2 * yi + ci).astype(jnp.int32).reshape(1)
    bias = _alibi_bias()
    b2 = _half_sum_matrix()
    tri =jnp.tril(jnp.ones((WIN, WIN), F32))

    fin = _cast_to_slab_call(w_in, chip_idx, "cast_w_in")
    fout = _cast_to_slab_call(w_out, chip_idx, "cast_w_out")
    fulls = [a for l in range(L) for a in (fin[l], fout[l])]
    g_ssems, g_rsems, fulls = _gather_start_call(fulls)

    saved = []
    xs = x[0]
    dy = loss = None
    for l in range(L):
        sl = slice(2 * l, 2 * l + 2)
        w_in_l, w_out_l = _gather_wait_call(fulls[sl], g_ssems[sl], g_rsems[sl], xs, f"gather_wait_{l}")
        w_out_l = w_out_l.reshape(D, D)
        proj, h = _fwd_in_call(xs, norm_g[l:l + 1], w_in_l)
        ws_tril = (w_s[l] * tri).astype(MM)
        b_exp = jnp.repeat(b_s[l].T, HD, axis=1)
        wq2 = jnp.tile(q_norm[l:l + 1], (1, 2)) * SCALE
        wk2 = jnp.tile(k_norm[l:l + 1], (1, 2))
        mix = _fwd_mix_call(proj, bias, wq2, wk2, b2, sinks[l], ws_tril, b_exp)
        saved.append((xs, proj, h, mix, ws_tril, b_exp, w_in_l, w_out_l, wq2, wk2))
        if l < L - 1:
            xs = _fwd_out_call(xs, mix, w_out_l)
        else:
            dy, loss = _fwd_out_loss_call(xs, mix, w_out_l, loss_target[0])

    s_in = lax.empty((L, D, SHW), F32)
    s_out = lax.empty((L, SHR, D), F32)
    small_sums = [None] * L

    def finish(pending, after):
        nonlocal s_in, s_out
        l, srcs, lands, ssem, rsem = pending
        srcs, lands = _grad_wait_call(srcs, lands, ssem, rsem, after, f"grad_wait_{l}")
        s_in = _sum_recv_call(srcs[0], lands[0], chip_idx, s_in, l)
        s_out = _sum_recv_call(srcs[1], lands[1], chip_idx, s_out, l)
        small_sums[l] = _sum_small_call(srcs[2], lands[2], dev_idx)

    token = jnp.zeros((8, 128), F32)
    pending = None
    for l in reversed(range(L)):
        xs, proj, h, mix, ws_tril, b_exp, w_in_l, w_out_l, wq2, wk2 = saved[l]
        w_out_t = _transpose_call(w_out_l[None], "transpose_w_out")
        w_in_t = _transpose_call(w_in_l, "transpose_w_in")
        g_w_out = _grad_w_out_call(mix, dy).reshape(NCHIP, SHR, D)
        dmix = _bwd_out_call(dy, w_out_t, token)
        ws_tril_t = jnp.swapaxes(ws_tril, 1, 2)
        dproj, dwq, dwk, dsk, dws, dbs = _bwd_mix_call(
            proj, dmix, bias, wq2, wk2, b2, sinks[l], ws_tril, ws_tril_t, b_exp)
        dwq, dwk = dwq[:, :HD] + dwq[:, HD:], dwk[:, :HD] + dwk[:, HD:]
        g_w_in = _grad_w_in_call(h, dproj)
        dy, dng = _bwd_in_call(dproj, w_in_t, xs, dy, norm_g[l:l + 1])
        if pending is not None:
            finish(pending, dy)
        g_small = _pack_small([dng, dwq, dwk, dsk[:, 0], dws, dbs])
        srcs, lands, ssem, rsem, token = _grad_start_call(g_w_in, g_w_out, g_small, f"grad_start_{l}")
        pending = (l, srcs, lands, ssem, rsem)
    finish(pending, token)
    grad_x = dy

    t_in, t_out = _sibling_swap_call([s_in, s_out])
    g_w_in, d_in, nm_in, nv_in = _adam_call(
        w_in.reshape(L * D, SHW), [s_in.reshape(L * D, SHW), t_in.reshape(L * D, SHW)],
        m_w_in.reshape(L * D, SHW), v_w_in.reshape(L * D, SHW), "adam_w_in")
    g_w_out, d_out, nm_out, nv_out = _adam_call(
        w_out.reshape(L * SHR, D), [s_out.reshape(L * SHR, D), t_out.reshape(L * SHR, D)],
        m_w_out.reshape(L * SHR, D), v_w_out.reshape(L * SHR, D), "adam_w_out")

    def pack_layers(parts):
        return jnp.concatenate([_pack_small([p[l] for p in parts]) for l in range(L)], axis=0)

    small_like = [norm_g, q_norm, k_norm, sinks, w_s, b_s]
    g_small, d_s, nm_s, nv_s = _adam_call(
        pack_layers(small_like), [jnp.concatenate(small_sums, axis=0)],
        pack_layers([m_norm_g, m_q_norm, m_k_norm, m_sinks, m_w_s, m_b_s]),
        pack_layers([v_norm_g, v_q_norm, v_k_norm, v_sinks, v_w_s, v_b_s]), "adam_small")

    def full(small, win, wout):
        rows = small.shape[0] // L
        per_layer = [_unpack_small(small[l * rows:(l + 1) * rows], [p[l] for p in small_like]) for l in range(L)]
        ng, qn, kn, sk, ws, bs = [jnp.stack([per_layer[l][i] for l in range(L)]) for i in range(6)]
        return [ng, win.reshape(w_in.shape), qn, kn, sk, ws, bs, wout.reshape(w_out.shape)]

    loss_all = lax.psum(loss[0, 0], ("x", "y", "c"))
    return (loss_all, grad_x[None], *full(g_small, g_w_in, g_w_out), *full(d_s, d_in, d_out),
            *full(nm_s, nm_in, nm_out), *full(nv_s, nv_in, nv_out))
```

```python
import functools
import math

import numpy as np
import jax
import jax.numpy as jnp
from jax import lax
from jax.experimental import pallas as pl
from jax.experimental.pallas import tpu as pltpu

F32 = jnp.float32
MM = jnp.bfloat16

D = 2048
HD = 64
DA = 1024
DKV = 256
DG = 1024
NQ, NKV, GRP, NG = 16, 4, 4, 16
WIN = 128
DIN = 5632
C_Q, C_K, C_V, C_GA, C_U, C_VS, C_GB = 0, 1024, 1280, 1536, 2560, 3584, 4608
NCHIP = 4
SHW = DIN // NCHIP
SHR = D // NCHIP
EPS = 1e-6
NEG = -1e30
SCALE = HD ** -0.5
INV_SQRT2 = 1.0 / math.sqrt(2.0)
INV_SQRT_2PI = 1.0 / math.sqrt(2.0 * math.pi)
LR, B1, B2, ADAM_EPS, WD, STEP = 0.001, 0.9, 0.999, 1e-08, 0.01, 10
VMEM_LIMIT = 56 * 1024 * 1024

SDS = jax.ShapeDtypeStruct
NT = (((1,), (1,)), ((), ()))
TN = (((0,), (0,)), ((), ()))


def _cp(sem=None):
    return pltpu.CompilerParams(dimension_semantics=sem, vmem_limit_bytes=VMEM_LIMIT)


def _sigmoid(x):
    return 1.0 / (1.0 + jnp.exp(-x))


def _gelu(x):
    return 0.5 * x * (1.0 + lax.erf(x * INV_SQRT2))


def _dgelu(x):
    return 0.5 * (1.0 + lax.erf(x * INV_SQRT2)) + x * jnp.exp(-0.5 * x * x) * INV_SQRT_2PI


def _alibi_bias():
    slopes = 2.0 ** (-8.0 * np.arange(1, NQ + 1) / NQ)
    dist = (np.arange(WIN)[:, None] + WIN) - np.arange(2 * WIN)[None, :]
    ok = (dist >= 0) & (dist < WIN)
    first = ok & (np.arange(2 * WIN)[None, :] >= WIN)
    val = -slopes[:, None, None] * dist[None].astype(np.float64)
    return jnp.asarray(np.stack([np.where(first[None], val, NEG), np.where(ok[None], val, NEG)]), dtype=F32)


def _half_sum_matrix():
    half = np.arange(LANE) // HD
    return jnp.asarray(half[:, None] == half[None, :], dtype=MM)


LANE = 128
NQT = DA // LANE
NKT = DKV // LANE


def _tiles(ref, c0, n):
    return jnp.concatenate([ref[:, c0 + j * LANE:c0 + (j + 1) * LANE] for j in range(n)], axis=0)


def _split(x):
    hi = x.astype(MM)
    return hi, (x - hi.astype(F32)).astype(MM)


def _half_sums(x, b2):
    hi, lo = _split(x)
    return jnp.dot(hi, b2, preferred_element_type=F32) + jnp.dot(lo, b2, preferred_element_type=F32)


def _attn_fwd(pm_ref, kvp_ref, bias_ref, wq2, wk2, b2, sink_ref):
    lo_half = lax.broadcasted_iota(jnp.int32, (1, LANE), 1) < HD
    q_ts = _tiles(pm_ref, C_Q, NQT)
    rq = lax.rsqrt(_half_sums(q_ts * q_ts, b2) * (1.0 / HD) + EPS)
    qs = (q_ts * rq * wq2).astype(MM)
    k_ts = jnp.concatenate([a[:, c0 + t * LANE:c0 + (t + 1) * LANE] for t in range(NKT)
                            for a, c0 in ((kvp_ref, 0), (pm_ref, C_K))], axis=0)
    rk = lax.rsqrt(_half_sums(k_ts * k_ts, b2) * (1.0 / HD) + EPS)
    kn = (k_ts * rk * wk2).astype(MM)
    v_ts = jnp.concatenate([a[:, c0 + t * LANE:c0 + (t + 1) * LANE] for t in range(NKT)
                            for a, c0 in ((kvp_ref, DKV), (pm_ref, C_V))], axis=0).astype(MM)
    ones = jnp.ones((2 * WIN, LANE), MM)
    km, vm = {}, {}
    for hk in range(NKV):
        t, eh = hk // 2, hk % 2
        sel = lo_half if eh == 0 else jnp.logical_not(lo_half)
        rows = slice(t * 2 * WIN, (t + 1) * 2 * WIN)
        k_same = jnp.where(sel, kn[rows], jnp.zeros_like(kn[rows]))
        v_same = jnp.where(sel, v_ts[rows], jnp.zeros_like(v_ts[rows]))
        km[hk, eh], km[hk, 1 - eh] = k_same, pltpu.roll(k_same, HD, axis=1)
        vm[hk, eh], vm[hk, 1 - eh] = v_same, pltpu.roll(v_same, HD, axis=1)
    heads = []
    for h in range(NQ):
        j, e, hk = h // 2, h % 2, h // GRP
        s = lax.dot_general(qs[j * WIN:(j + 1) * WIN], km[hk, e], NT, preferred_element_type=F32) + bias_ref[0, h]
        sink = sink_ref[h]
        m = jnp.maximum(jnp.max(s, axis=-1, keepdims=True), sink)
        p = jnp.exp(s - m)
        res = jnp.dot(p.astype(MM), jnp.concatenate([vm[hk, e], ones], axis=1), preferred_element_type=F32)
        esink = jnp.exp(sink - m)
        inv = 1.0 / (res[:, LANE:] + esink)
        heads.append(dict(p=p, inv=inv, esink=esink, o=res[:, :LANE] * inv))
    return dict(lo_half=lo_half, q_ts=q_ts, rq=rq, qs=qs, k_ts=k_ts, rk=rk, km=km, vm=vm, heads=heads)


def _sgu_mix(w_ref, zt, lo_half, j):
    zero = jnp.zeros_like(zt)
    return (jnp.dot(w_ref[2 * j], jnp.where(lo_half, zt, zero), preferred_element_type=F32)
            + jnp.dot(w_ref[2 * j + 1], jnp.where(lo_half, zero, zt), preferred_element_type=F32))


def _fwd_mix_call(proj, bias, wq2, wk2, b2, sinks, ws_tril, b_exp):
    T = proj.shape[0]
    nb = T // WIN

    def body(sink_ref, pm_ref, kvp_ref, bias_ref, wq_ref, wk_ref, b2_ref, ws_ref, be_ref, mix_ref):
        a = _attn_fwd(pm_ref, kvp_ref, bias_ref, wq_ref[...], wk_ref[...], b2_ref[...], sink_ref)
        for j in range(NQT):
            cols = slice(j * LANE, (j + 1) * LANE)
            ga = pm_ref[:, C_GA + j * LANE:C_GA + (j + 1) * LANE]
            attn = a["heads"][2 * j]["o"] + a["heads"][2 * j + 1]["o"]
            mix_ref[:, cols] = (attn * (ga * _sigmoid(ga))).astype(MM)
        zu = _gelu(pm_ref[:, C_U:C_U + DG])
        zv = _gelu(pm_ref[:, C_VS:C_VS + DG]).astype(MM)
        mixed = jnp.concatenate(
            [_sgu_mix(ws_ref, zv[:, j * LANE:(j + 1) * LANE], a["lo_half"], j) for j in range(NG // 2)], axis=1)
        mixed = mixed + be_ref[...]
        gb = pm_ref[:, C_GB:C_GB + DG]
        mix_ref[:, DA:DA + DG] = (zu * mixed * (gb * _sigmoid(gb))).astype(MM)

    return pl.pallas_call(
        body,
        grid=(nb,),
        in_specs=[
            pl.BlockSpec(memory_space=pltpu.SMEM),
            pl.BlockSpec((WIN, DIN), lambda n: (n, 0)),
            pl.BlockSpec((WIN, 2 * DKV), lambda n: (jnp.maximum(n - 1, 0), C_K // (2 * DKV))),
            pl.BlockSpec((1, NQ, WIN, 2 * WIN), lambda n: (jnp.minimum(n, 1), 0, 0, 0)),
            pl.BlockSpec((1, LANE), lambda n: (0, 0)),
            pl.BlockSpec((1, LANE), lambda n: (0, 0)),
            pl.BlockSpec((LANE, LANE), lambda n: (0, 0)),
            pl.BlockSpec((NG, WIN, WIN), lambda n: (0, 0, 0)),
            pl.BlockSpec((WIN, DG), lambda n: (0, 0)),
        ],
        out_specs=pl.BlockSpec((WIN, D), lambda n: (n, 0)),
        out_shape=SDS((T, D), MM),
        name="fwd_mix",
        compiler_params=_cp(("arbitrary",)),
    )(sinks, proj, proj, bias, wq2, wk2, b2, ws_tril, b_exp)


def _bwd_mix_call(proj, dmix, bias, wq2, wk2, b2, sinks, ws_tril, ws_tril_t, b_exp):
    T = proj.shape[0]
    nb = T // WIN

    def body(sink_ref, pm_ref, kvp_ref, dm_ref, bias_ref, wq_ref, wk_ref, b2_ref, ws_ref, wst_ref, be_ref,
             dp_ref, dwq_ref, dwk_ref, dsk_ref, dws_ref, dbs_ref, carry_ref, dbacc_ref):
        n = pl.program_id(0)

        @pl.when(n == 0)
        def _():
            carry_ref[...] = jnp.zeros_like(carry_ref)
            dbacc_ref[...] = jnp.zeros_like(dbacc_ref)
            dwq_ref[...] = jnp.zeros_like(dwq_ref)
            dwk_ref[...] = jnp.zeros_like(dwk_ref)
            dsk_ref[...] = jnp.zeros_like(dsk_ref)
            dws_ref[...] = jnp.zeros_like(dws_ref)
            dbs_ref[...] = jnp.zeros_like(dbs_ref)

        @pl.when(n < nb)
        def _():
            wq2, wk2, b2 = wq_ref[...], wk_ref[...], b2_ref[...]
            a = _attn_fwd(pm_ref, kvp_ref, bias_ref, wq2, wk2, b2, sink_ref)
            lo_half, heads, km, vm, qs = a["lo_half"], a["heads"], a["km"], a["vm"], a["qs"]
            hi_half = jnp.logical_not(lo_half)

            dp_ref[:, C_Q:C_K] = carry_ref[:, C_Q:C_K].astype(MM)
            dp_ref[:, C_GA:DIN] = carry_ref[:, C_GA:DIN].astype(MM)

            row_lo = lax.broadcasted_iota(jnp.int32, (LANE, 2 * WIN), 0) < HD
            pick = [jnp.where(row_lo, 1.0, 0.0).astype(MM), jnp.where(row_lo, 0.0, 1.0).astype(MM)]
            dqs_tiles, dk_acc, dv_acc = [], {}, {}
            for j in range(NQT):
                cols = slice(C_GA + j * LANE, C_GA + (j + 1) * LANE)
                ga = pm_ref[:, cols]
                sga = _sigmoid(ga)
                d_gated = dm_ref[:, j * LANE:(j + 1) * LANE]
                attn = heads[2 * j]["o"] + heads[2 * j + 1]["o"]
                carry_ref[:, cols] = d_gated * attn * (sga * (1.0 + ga * (1.0 - sga)))
                d_o = d_gated * (ga * sga)
                d_ob = d_o.astype(MM)
                zero = jnp.zeros_like(d_ob)
                dlt_hi, dlt_lo = _split(d_o * attn)
                qs_j = qs[j * WIN:(j + 1) * WIN]
                dqs = None
                for e in range(2):
                    h = 2 * j + e
                    hk, hd = h // GRP, heads[h]
                    sel = lo_half if e == 0 else hi_half
                    d_p = lax.dot_general(d_ob, vm[hk, e], NT, preferred_element_type=F32)
                    delta = (jnp.dot(dlt_hi, pick[e], preferred_element_type=F32)
                             + jnp.dot(dlt_lo, pick[e], preferred_element_type=F32))
                    pn = hd["p"] * jnp.concatenate([hd["inv"], hd["inv"]], axis=1)
                    d_s = (pn * (d_p - delta)).astype(MM)
                    dsk_ref[h:h + 1, :] -= jnp.sum(hd["esink"] * hd["inv"] * delta[:, :LANE], axis=0, keepdims=True)
                    t = jnp.dot(d_s, km[hk, e], preferred_element_type=F32)
                    dqs = t if dqs is None else dqs + t
                    dk_h = lax.dot_general(d_s, jnp.where(sel, qs_j, zero), TN, preferred_element_type=F32)
                    dv_h = lax.dot_general(pn.astype(MM), jnp.where(sel, d_ob, zero), TN, preferred_element_type=F32)
                    key = (hk, e == hk % 2)
                    dk_acc[key] = dk_h if key not in dk_acc else dk_acc[key] + dk_h
                    dv_acc[key] = dv_h if key not in dv_acc else dv_acc[key] + dv_h
                dqs_tiles.append(dqs)

            dqs_ts = jnp.concatenate(dqs_tiles, axis=0)
            q_ts, rq = a["q_ts"], a["rq"]
            gq = dqs_ts * wq2
            d_q = rq * gq - q_ts * (rq * rq * rq) * (_half_sums(gq * q_ts, b2) * (1.0 / HD))
            dwq_ref[...] += SCALE * jnp.sum(dqs_ts * q_ts * rq, axis=0, keepdims=True)
            for j in range(NQT):
                carry_ref[:, C_Q + j * LANE:C_Q + (j + 1) * LANE] = d_q[j * WIN:(j + 1) * WIN]

            dkn_tiles, dv_tiles = [], []
            for t in range(NKT):
                for acc, out in ((dk_acc, dkn_tiles), (dv_acc, dv_tiles)):
                    parts = [acc[hk, True] + pltpu.roll(acc[hk, False], HD, axis=1) for hk in (2 * t, 2 * t + 1)]
                    out.append(parts[0] + parts[1])
            dkn_ts = jnp.concatenate(dkn_tiles, axis=0)
            dv_ts = jnp.concatenate(dv_tiles, axis=0)
            k_ts, rk = a["k_ts"], a["rk"]
            gk = dkn_ts * wk2
            d_k = rk * gk - k_ts * (rk * rk * rk) * (_half_sums(gk * k_ts, b2) * (1.0 / HD))
            dwk_ref[...] += jnp.sum(dkn_ts * k_ts * rk, axis=0, keepdims=True)
            for t in range(NKT):
                for base, val in ((C_K, d_k), (C_V, dv_ts)):
                    cols = slice(base + t * LANE, base + (t + 1) * LANE)
                    r0 = t * 2 * WIN
                    dp_ref[:, cols] = (carry_ref[:, cols] + val[r0:r0 + WIN]).astype(MM)
                    carry_ref[:, cols] = val[r0 + WIN:r0 + 2 * WIN]

            u = pm_ref[:, C_U:C_U + DG]
            vs = pm_ref[:, C_VS:C_VS + DG]
            gb = pm_ref[:, C_GB:C_GB + DG]
            zu = _gelu(u)
            zvb = _gelu(vs).astype(MM)
            mixed = jnp.concatenate(
                [_sgu_mix(ws_ref, zvb[:, j * LANE:(j + 1) * LANE], lo_half, j) for j in range(NG // 2)], axis=1)
            mixed = mixed + be_ref[...]
            sgb = _sigmoid(gb)
            d_sgu = dm_ref[:, DA:DA + DG]
            carry_ref[:, C_GB:DIN] = d_sgu * zu * mixed * (sgb * (1.0 + gb * (1.0 - sgb)))
            d_mixed = d_sgu * zu * (gb * sgb)
            carry_ref[:, C_U:C_VS] = d_sgu * mixed * (gb * sgb) * _dgelu(u)
            dbacc_ref[...] += d_mixed
            dmb = d_mixed.astype(MM)
            dzv_tiles = []
            for j in range(NG // 2):
                dt = dmb[:, j * LANE:(j + 1) * LANE]
                zt = zvb[:, j * LANE:(j + 1) * LANE]
                zero = jnp.zeros_like(dt)
                dzv_tiles.append(_sgu_mix(wst_ref, dt, lo_half, j))
                dws_ref[2 * j] += lax.dot_general(jnp.where(lo_half, dt, zero), zt, NT, preferred_element_type=F32)
                dws_ref[2 * j + 1] += lax.dot_general(jnp.where(lo_half, zero, dt), zt, NT, preferred_element_type=F32)
            carry_ref[:, C_VS:C_GB] = jnp.concatenate(dzv_tiles, axis=1) * _dgelu(vs)

        @pl.when(n == nb)
        def _():
            dp_ref[...] = carry_ref[...].astype(MM)
            lo_half = lax.broadcasted_iota(jnp.int32, (8, LANE), 1) < HD
            ones = [jnp.where(lo_half, 1.0, 0.0).astype(MM), jnp.where(lo_half, 0.0, 1.0).astype(MM)]
            hi, lo = _split(dbacc_ref[...])
            for h in range(NG):
                sl = slice((h // 2) * LANE, (h // 2 + 1) * LANE)
                r = (lax.dot_general(ones[h % 2], hi[:, sl], NT, preferred_element_type=F32)
                     + lax.dot_general(ones[h % 2], lo[:, sl], NT, preferred_element_type=F32))
                dbs_ref[h:h + 1, :] = r[0:1, :]
            row = lax.broadcasted_iota(jnp.int32, (WIN, WIN), 0)
            cl = lax.broadcasted_iota(jnp.int32, (WIN, WIN), 1)
            for h in range(NG):
                dws_ref[h] = jnp.where(row >= cl, dws_ref[h], 0.0)

    last = nb - 1
    return pl.pallas_call(
        body,
        grid_spec=pltpu.PrefetchScalarGridSpec(
            num_scalar_prefetch=0,
            grid=(nb + 1,),
            in_specs=[
                pl.BlockSpec(memory_space=pltpu.SMEM),
                pl.BlockSpec((WIN, DIN), lambda n: (jnp.minimum(n, last), 0)),
                pl.BlockSpec((WIN, 2 * DKV), lambda n: (jnp.maximum(jnp.minimum(n, last) - 1, 0), C_K // (2 * DKV))),
                pl.BlockSpec((WIN, D), lambda n: (jnp.minimum(n, last), 0)),
                pl.BlockSpec((1, NQ, WIN, 2 * WIN), lambda n: (jnp.minimum(n, 1), 0, 0, 0)),
                pl.BlockSpec((1, LANE), lambda n: (0, 0)),
                pl.BlockSpec((1, LANE), lambda n: (0, 0)),
                pl.BlockSpec((LANE, LANE), lambda n: (0, 0)),
                pl.BlockSpec((NG, WIN, WIN), lambda n: (0, 0, 0)),
                pl.BlockSpec((NG, WIN, WIN), lambda n: (0, 0, 0)),
                pl.BlockSpec((WIN, DG), lambda n: (0, 0)),
            ],
            out_specs=[
                pl.BlockSpec((WIN, DIN), lambda n: (jnp.maximum(n - 1, 0), 0)),
                pl.BlockSpec((1, LANE), lambda n: (0, 0)),
                pl.BlockSpec((1, LANE), lambda n: (0, 0)),
                pl.BlockSpec((NQ, WIN), lambda n: (0, 0)),
                pl.BlockSpec((NG, WIN, WIN), lambda n: (0, 0, 0)),
                pl.BlockSpec((NG, WIN), lambda n: (0, 0)),
            ],
            scratch_shapes=[pltpu.VMEM((WIN, DIN), F32), pltpu.VMEM((WIN, DG), F32)],
        ),
        out_shape=[SDS((T, DIN), MM), SDS((1, LANE), F32), SDS((1, LANE), F32), SDS((NQ, WIN), F32),
                   SDS((NG, WIN, WIN), F32), SDS((NG, WIN), F32)],
        name="bwd_mix",
        compiler_params=_cp(("arbitrary",)),
    )(sinks, proj, proj, dmix, bias, wq2, wk2, b2, ws_tril, ws_tril_t, b_exp)


WEIGHT_RESIDENT_ROWS = 256


def _row_tile(T):
    return min(512, T)


def _fwd_in_call(x, g_row, w_sh):
    T = x.shape[0]
    tm = min(WEIGHT_RESIDENT_ROWS, T)

    def body(x_ref, g_ref, w_hbm, proj_ref, h_ref, w_vmem, sem):
        @pl.when(pl.program_id(0) == 0)
        def _():
            cp = pltpu.make_async_copy(w_hbm, w_vmem, sem)
            cp.start()
            cp.wait()

        xv = x_ref[...]
        r = lax.rsqrt(jnp.mean(xv * xv, axis=-1, keepdims=True) + EPS)
        h = (xv * r * g_ref[...]).astype(MM)
        h_ref[...] = h
        for j in range(NCHIP):
            proj_ref[:, j * SHW:(j + 1) * SHW] = jnp.dot(h, w_vmem[j], preferred_element_type=F32)

    return pl.pallas_call(
        body,
        grid=(T // tm,),
        in_specs=[pl.BlockSpec((tm, D), lambda i: (i, 0)),
                  pl.BlockSpec((1, D), lambda i: (0, 0)),
                  pl.BlockSpec(memory_space=pl.ANY)],
        out_specs=[pl.BlockSpec((tm, DIN), lambda i: (i, 0)),
                   pl.BlockSpec((tm, D), lambda i: (i, 0))],
        out_shape=[SDS((T, DIN), F32), SDS((T, D), MM)],
        scratch_shapes=[pltpu.VMEM((NCHIP, D, SHW), MM), pltpu.SemaphoreType.DMA],
        name="fwd_in",
        compiler_params=_cp(("arbitrary",)),
    )(x, g_row, w_sh)


def _fwd_out_call(x, mix, w_out):
    T = x.shape[0]
    tm = _row_tile(T)

    def body(x_ref, mix_ref, w_ref, y_ref):
        y_ref[...] = x_ref[...] + jnp.dot(mix_ref[...], w_ref[...], preferred_element_type=F32)

    return pl.pallas_call(
        body,
        grid=(T // tm,),
        in_specs=[pl.BlockSpec((tm, D), lambda i: (i, 0)),
                  pl.BlockSpec((tm, D), lambda i: (i, 0)),
                  pl.BlockSpec((D, D), lambda i: (0, 0))],
        out_specs=pl.BlockSpec((tm, D), lambda i: (i, 0)),
        out_shape=SDS((T, D), F32),
        name="fwd_out",
        compiler_params=_cp(("arbitrary",)),
    )(x, mix, w_out)


def _fwd_out_loss_call(x, mix, w_out, target):
    T = x.shape[0]
    tm = _row_tile(T)

    def body(x_ref, mix_ref, w_ref, t_ref, dy_ref, loss_ref):
        @pl.when(pl.program_id(0) == 0)
        def _():
            loss_ref[...] = jnp.zeros_like(loss_ref)

        e = x_ref[...] + jnp.dot(mix_ref[...], w_ref[...], preferred_element_type=F32) - t_ref[...]
        dy_ref[...] = e * (1.0 / D)
        loss_ref[...] += (0.5 / D) * jnp.sum(jnp.sum(e * e, axis=1, keepdims=True), axis=0, keepdims=True)

    return pl.pallas_call(
        body,
        grid=(T // tm,),
        in_specs=[pl.BlockSpec((tm, D), lambda i: (i, 0)),
                  pl.BlockSpec((tm, D), lambda i: (i, 0)),
                  pl.BlockSpec((D, D), lambda i: (0, 0)),
                  pl.BlockSpec((tm, D), lambda i: (i, 0))],
        out_specs=[pl.BlockSpec((tm, D), lambda i: (i, 0)),
                   pl.BlockSpec((1, 1), lambda i: (0, 0))],
        out_shape=[SDS((T, D), F32), SDS((1, 1), F32)],
        name="fwd_out_loss",
        compiler_params=_cp(("arbitrary",)),
    )(x, mix, w_out, target)


def _bwd_out_call(dy, w_out_t):
    T = dy.shape[0]
    tm = _row_tile(T)

    def body(dy_ref, w_ref, o_ref):
        o_ref[...] = jnp.dot(dy_ref[...].astype(MM), w_ref[...], preferred_element_type=F32)

    return pl.pallas_call(
        body,
        grid=(T // tm,),
        in_specs=[pl.BlockSpec((tm, D), lambda i: (i, 0)),
                  pl.BlockSpec((D, D), lambda i: (0, 0))],
        out_specs=pl.BlockSpec((tm, D), lambda i: (i, 0)),
        out_shape=SDS((T, D), F32),
        name="bwd_out",
        compiler_params=_cp(("arbitrary",)),
    )(dy, w_out_t)


def _bwd_in_call(dproj, w_in_t, x, dy, g_row, token):
    T = x.shape[0]
    tm = min(WEIGHT_RESIDENT_ROWS, T)

    def body(dp_ref, w_hbm, x_ref, dy_ref, g_ref, token_ref, dx_ref, dg_ref, w_vmem, sem):
        @pl.when(pl.program_id(0) == 0)
        def _():
            cp = pltpu.make_async_copy(w_hbm, w_vmem, sem)
            cp.start()
            dg_ref[...] = jnp.zeros_like(dg_ref)
            cp.wait()

        dh = jnp.dot(dp_ref[...], w_vmem[...], preferred_element_type=F32)
        xv = x_ref[...]
        r = lax.rsqrt(jnp.mean(xv * xv, axis=-1, keepdims=True) + EPS)
        gd = dh * g_ref[...]
        dx_ref[...] = dy_ref[...] + r * gd - xv * ((r * r * r) * jnp.mean(gd * xv, axis=-1, keepdims=True))
        dg_ref[...] += jnp.sum(dh * xv * r, axis=0, keepdims=True)

    return pl.pallas_call(
        body,
        grid=(T // tm,),
        in_specs=[pl.BlockSpec((tm, DIN), lambda i: (i, 0)),
                  pl.BlockSpec(memory_space=pl.ANY),
                  pl.BlockSpec((tm, D), lambda i: (i, 0)),
                  pl.BlockSpec((tm, D), lambda i: (i, 0)),
                  pl.BlockSpec((1, D), lambda i: (0, 0)),
                  pl.BlockSpec(memory_space=pl.ANY)],
        out_specs=[pl.BlockSpec((tm, D), lambda i: (i, 0)),
                   pl.BlockSpec((1, D), lambda i: (0, 0))],
        out_shape=[SDS((T, D), F32), SDS((1, D), F32)],
        scratch_shapes=[pltpu.VMEM((DIN, D), MM), pltpu.SemaphoreType.DMA],
        name="bwd_in",
        compiler_params=_cp(("arbitrary",)),
    )(dproj, w_in_t, x, dy, g_row, token)


def _grad_w_in_call(h, dproj):
    T = h.shape[0]
    tt = _row_tile(T)
    nt = T // tt

    def body(h_ref, dp_ref, o_ref, acc_ref):
        t = pl.program_id(1)

        @pl.when(t == 0)
        def _():
            acc_ref[...] = jnp.zeros_like(acc_ref)

        acc_ref[...] += lax.dot_general(h_ref[...], dp_ref[...], TN, preferred_element_type=F32)

        @pl.when(t == nt - 1)
        def _():
            o_ref[0] = acc_ref[...].astype(MM)

    return pl.pallas_call(
        body,
        grid=(NCHIP, nt),
        in_specs=[pl.BlockSpec((tt, D), lambda j, t: (t, 0)),
                  pl.BlockSpec((tt, SHW), lambda j, t: (t, j))],
        out_specs=pl.BlockSpec((1, D, SHW), lambda j, t: (j, 0, 0)),
        out_shape=SDS((NCHIP, D, SHW), MM),
        scratch_shapes=[pltpu.VMEM((D, SHW), F32)],
        name="grad_w_in",
        compiler_params=_cp(("arbitrary", "arbitrary")),
    )(h, dproj)


def _grad_w_out_call(mix, dy):
    T = mix.shape[0]
    tt = _row_tile(T)
    nt = T // tt
    tn = 1024

    def body(m_ref, dy_ref, o_ref, acc_ref):
        t = pl.program_id(1)

        @pl.when(t == 0)
        def _():
            acc_ref[...] = jnp.zeros_like(acc_ref)

        acc_ref[...] += lax.dot_general(m_ref[...], dy_ref[...].astype(MM), TN, preferred_element_type=F32)

        @pl.when(t == nt - 1)
        def _():
            o_ref[...] = acc_ref[...].astype(MM)

    return pl.pallas_call(
        body,
        grid=(D // tn, nt),
        in_specs=[pl.BlockSpec((tt, D), lambda j, t: (t, 0)),
                  pl.BlockSpec((tt, tn), lambda j, t: (t, j))],
        out_specs=pl.BlockSpec((D, tn), lambda j, t: (0, j)),
        out_shape=SDS((D, D), MM),
        scratch_shapes=[pltpu.VMEM((D, tn), F32)],
        name="grad_w_out",
        compiler_params=_cp(("arbitrary", "arbitrary")),
    )(mix, dy)


def _transpose_call(w, name):
    nbk, R, C = w.shape
    tr = 512

    def body(w_ref, o_ref):
        o_ref[...] = w_ref[0].T

    return pl.pallas_call(
        body,
        grid=(nbk, R // tr),
        in_specs=[pl.BlockSpec((1, tr, C), lambda j, i: (j, i, 0))],
        out_specs=pl.BlockSpec((C, tr), lambda j, i: (j, i)),
        out_shape=SDS((nbk * C, R), w.dtype),
        name=name,
        compiler_params=_cp(("arbitrary", "arbitrary")),
    )(w)


def _cast_to_slab_call(w, chip_idx, name):
    L, R, C = w.shape
    tr = 256

    def body(chip_ref, *refs):
        for l in range(L):
            refs[L + l][...] = refs[l][...].astype(MM)

    return pl.pallas_call(
        body,
        grid_spec=pltpu.PrefetchScalarGridSpec(
            num_scalar_prefetch=1,
            grid=(R // tr,),
            in_specs=[pl.BlockSpec((1, tr, C), functools.partial(lambda i, chip_ref, l: (l, i, 0), l=l))
                      for l in range(L)],
            out_specs=[pl.BlockSpec((1, tr, C), lambda i, chip_ref: (chip_ref[0], i, 0))] * L,
        ),
        out_shape=[SDS((NCHIP, R, C), MM)] * L,
        name=name,
        compiler_params=_cp(("arbitrary",)),
    )(chip_idx, *([w] * L))


def _adam_call(w, g_parts, m, v, name):
    R, C = w.shape
    tr = R
    for cand in (512, 256, 128, 64, 32, 16, 8):
        if R % cand == 0 and cand * C * 4 <= 1024 * 1024:
            tr = cand
            break
    c1 = 1.0 - B1 ** STEP
    c2 = 1.0 - B2 ** STEP
    ng = len(g_parts)

    def body(*refs):
        w_ref, m_ref, v_ref = refs[0], refs[1 + ng], refs[2 + ng]
        g_ref, d_ref, nm_ref, nv_ref = refs[3 + ng:]
        gv = refs[1][...]
        for k in range(1, ng):
            gv = gv + refs[1 + k][...]
        nm = B1 * m_ref[...] + (1.0 - B1) * gv
        nv = B2 * v_ref[...] + (1.0 - B2) * (gv * gv)
        g_ref[...] = gv
        nm_ref[...] = nm
        nv_ref[...] = nv
        d_ref[...] = -LR * ((nm / c1) / (jnp.sqrt(nv / c2) + ADAM_EPS) + WD * w_ref[...])

    spec = pl.BlockSpec((tr, C), lambda i: (i, 0))
    return pl.pallas_call(
        body,
        grid=(R // tr,),
        in_specs=[spec] * (3 + ng),
        out_specs=[spec] * 4,
        out_shape=[SDS((R, C), F32)] * 4,
        name=name,
        compiler_params=_cp(("arbitrary",)),
    )(w, *g_parts, m, v)


MESH = pl.DeviceIdType.MESH
ANY = pl.BlockSpec(memory_space=pl.ANY)
HBM = pl.BlockSpec(memory_space=pltpu.HBM)
SEMS = pl.BlockSpec(memory_space=pltpu.SEMAPHORE)
EFFECT = pltpu.SideEffectType.DATAFLOW_SIDE_EFFECTING
NDEV = 8


def _hbm(a):
    return pltpu.with_memory_space_constraint(a, pltpu.HBM)


def _place():
    x, y, c = lax.axis_index("x"), lax.axis_index("y"), lax.axis_index("c")
    others = [(1 - x, y), (x, 1 - y), (1 - x, 1 - y)]
    return x, y, c, 2 * x + y, others


def _flipped(x, y, c, r):
    return (1 - x if r & 4 else x, 1 - y if r & 2 else y, 1 - c if r & 1 else c)


def _rcopy(src, dst, ssem, rsem, dev):
    return pltpu.make_async_remote_copy(src_ref=src, dst_ref=dst, send_sem=ssem, recv_sem=rsem,
                                        device_id=dev, device_id_type=MESH)


def _slab(ref, chip, c, halved):
    if not halved:
        return ref.at[chip]
    h = ref.shape[1] // 2
    return ref.at[chip, pl.ds(c * h, h), :]


def _gather_start_call(fulls, n_halved):
    K = len(fulls)

    def body(*refs):
        full, ssem, rsem = refs[:K], refs[K:2 * K], refs[2 * K:3 * K]
        x, y, c, me, others = _place()
        for k in range(K):
            for j, (px, py) in enumerate(others):
                part = _slab(full[k], me, c, k < n_halved)
                _rcopy(part, part, ssem[k].at[j], rsem[k].at[j], (px, py, c)).start()

    outs = pl.pallas_call(
        body,
        in_specs=[HBM] * K,
        out_specs=[SEMS] * (2 * K) + [HBM] * K,
        out_shape=[pltpu.SemaphoreType.DMA((3,))] * (2 * K) + [pltpu.HBM(f.shape, f.dtype) for f in fulls],
        input_output_aliases={k: 2 * K + k for k in range(K)},
        name="gather_start",
        compiler_params=pltpu.CompilerParams(has_side_effects=EFFECT),
    )(*[_hbm(f) for f in fulls])
    return list(outs[:K]), list(outs[K:2 * K]), list(outs[2 * K:])


def _gather_wait_call(fulls, ssems, rsems, after, halved, name):
    K = len(fulls)

    def body(*refs):
        full, ssem, rsem = refs[:K], refs[K:2 * K], refs[2 * K:3 * K]
        x, y, c, me, others = _place()
        for k in range(K):
            for j, (px, py) in enumerate(others):
                cp = _rcopy(_slab(full[k], me, c, halved), _slab(full[k], 2 * px + py, c, halved),
                            ssem[k].at[j], rsem[k].at[j], (px, py, c))
                cp.wait_send()
                cp.wait_recv()

    outs = pl.pallas_call(
        body,
        in_specs=[HBM] * K + [SEMS] * (2 * K) + [ANY],
        out_specs=[HBM] * K,
        out_shape=[pltpu.HBM(f.shape, f.dtype) for f in fulls],
        input_output_aliases={k: k for k in range(K)},
        name=name,
        compiler_params=pltpu.CompilerParams(has_side_effects=EFFECT),
    )(*fulls, *ssems, *rsems, after)
    return list(outs)


def _sibling_forward_call(fulls):
    K = len(fulls)

    def body(*refs):
        full = refs[:K]
        ssem, rsem = refs[2 * K:]
        x, y, c, me, others = _place()
        cps = []
        for k in range(K):
            for j, (px, py) in enumerate(others):
                mine = _slab(full[k], 2 * px + py, c, True)
                cps.append(_rcopy(mine, mine, ssem.at[3 * k + j], rsem.at[3 * k + j], (x, y, 1 - c)))
        for cp in cps:
            cp.start()
        for k in range(K):
            for j, (px, py) in enumerate(others):
                theirs = _slab(full[k], 2 * px + py, 1 - c, True)
                _rcopy(theirs, theirs, ssem.at[3 * k + j], rsem.at[3 * k + j], (x, y, 1 - c)).wait_recv()
        for cp in cps:
            cp.wait_send()

    outs = pl.pallas_call(
        body,
        in_specs=[ANY] * K,
        out_specs=[ANY] * K,
        out_shape=[SDS(f.shape, f.dtype) for f in fulls],
        input_output_aliases={k: k for k in range(K)},
        scratch_shapes=[pltpu.SemaphoreType.DMA((3 * K,)), pltpu.SemaphoreType.DMA((3 * K,))],
        name="gather_sibling_forward",
    )(*fulls)
    return list(outs)


def _grad_start_call(g_in, g_out, g_small, name):
    srcs = [g_in, g_out, g_small]
    lands = [lax.empty((3,) + g_in.shape[1:], g_in.dtype), lax.empty((3,) + g_out.shape[1:], g_out.dtype),
             lax.empty((NDEV,) + g_small.shape, g_small.dtype)]

    def body(gi, go, gs, ri, ro, rs, ssem, rsem, *outs):
        token = outs[-1]
        x, y, c, me, others = _place()
        for j, (px, py) in enumerate(others):
            _rcopy(gi.at[2 * px + py], ri.at[j], ssem.at[j], rsem.at[j], (px, py, c)).start()
            _rcopy(go.at[2 * px + py], ro.at[j], ssem.at[3 + j], rsem.at[3 + j], (px, py, c)).start()
        for r in range(1, NDEV):
            _rcopy(gs, rs.at[4 * x + 2 * y + c], ssem.at[5 + r], rsem.at[5 + r], _flipped(x, y, c, r)).start()
        token[...] = jnp.zeros_like(token)

    outs = pl.pallas_call(
        body,
        in_specs=[HBM] * 6,
        out_specs=[SEMS, SEMS] + [HBM] * 6 + [pl.BlockSpec(memory_space=pltpu.VMEM)],
        out_shape=[pltpu.SemaphoreType.DMA((13,)), pltpu.SemaphoreType.DMA((13,))]
        + [pltpu.HBM(a.shape, a.dtype) for a in srcs + lands] + [SDS((8, 128), F32)],
        input_output_aliases={k: 2 + k for k in range(6)},
        name=name,
        compiler_params=pltpu.CompilerParams(has_side_effects=EFFECT),
    )(*[_hbm(a) for a in srcs + lands])
    return list(outs[2:5]), list(outs[5:8]), outs[0], outs[1], outs[8]


def _grad_wait_call(srcs, lands, ssem, rsem, after, name):
    def body(gi, go, gs, ri, ro, rs, ssem, rsem, after_ref, *outs):
        x, y, c, me, others = _place()
        for j, (px, py) in enumerate(others):
            for src, land, k in ((gi, ri, j), (go, ro, 3 + j)):
                cp = _rcopy(src.at[2 * px + py], land.at[j], ssem.at[k], rsem.at[k], (px, py, c))
                cp.wait_send()
                cp.wait_recv()
        for r in range(1, NDEV):
            cp = _rcopy(gs, rs.at[0], ssem.at[5 + r], rsem.at[5 + r], _flipped(x, y, c, r))
            cp.wait_send()
            cp.wait_recv()

    arrs = list(srcs) + list(lands)
    outs = pl.pallas_call(
        body,
        in_specs=[HBM] * 6 + [SEMS, SEMS, ANY],
        out_specs=[HBM] * 6,
        out_shape=[pltpu.HBM(a.shape, a.dtype) for a in arrs],
        input_output_aliases={k: k for k in range(6)},
        name=name,
        compiler_params=pltpu.CompilerParams(has_side_effects=EFFECT),
    )(*arrs, ssem, rsem, after)
    return list(outs[:3]), list(outs[3:6])


def _sibling_swap_call(arrs):
    K = len(arrs)

    def body(*refs):
        a_refs, t_refs = refs[:K], refs[K:2 * K]
        ssem, rsem = refs[2 * K:]
        x, y, c, me, others = _place()
        cps = [_rcopy(a_refs[k], t_refs[k], ssem.at[k], rsem.at[k], (x, y, 1 - c)) for k in range(K)]
        for cp in cps:
            cp.start()
        for cp in cps:
            cp.wait()

    return pl.pallas_call(
        body,
        in_specs=[ANY] * K,
        out_specs=[ANY] * K,
        out_shape=[SDS(a.shape, a.dtype) for a in arrs],
        scratch_shapes=[pltpu.SemaphoreType.DMA((K,)), pltpu.SemaphoreType.DMA((K,))],
        name="grad_sibling_swap",
    )(*arrs)


def _small_allreduce_call(a):
    R, C = a.shape

    def body(a_ref, o_ref, recv_ref, ssem, rsem):
        x, y, c, me, others = _place()
        dev = 4 * x + 2 * y + c
        recv_ref[pl.ds(dev, 1)] = a_ref[...][None]
        cps = [_rcopy(a_ref, recv_ref.at[dev], ssem.at[r - 1], rsem.at[r - 1], _flipped(x, y, c, r))
               for r in range(1, NDEV)]
        for cp in cps:
            cp.start()
        for cp in cps:
            cp.wait()
        acc = recv_ref[0]
        for s in range(1, NDEV):
            acc = acc + recv_ref[s]
        o_ref[...] = acc

    return pl.pallas_call(
        body,
        in_specs=[pl.BlockSpec(memory_space=pltpu.VMEM)],
        out_specs=pl.BlockSpec(memory_space=pltpu.VMEM),
        out_shape=SDS((R, C), F32),
        scratch_shapes=[pltpu.VMEM((NDEV, R, C), F32), pltpu.SemaphoreType.DMA((NDEV - 1,)),
                        pltpu.SemaphoreType.DMA((NDEV - 1,))],
        name="small_allreduce",
    )(a)


def _rows_tile(H, C):
    for cand in (512, 256, 128, 64, 32, 16, 8):
        if H % cand == 0 and cand * C * 4 <= 2 * 1024 * 1024:
            return cand
    raise ValueError((H, C))


def _sum_recv_call(own, recv, chip_idx, stack, l):
    _, R, C = own.shape
    tr = _rows_tile(R, C)

    def body(chip_ref, own_ref, r0, r1, r2, stack_ref, o_ref):
        o_ref[...] = ((own_ref[...].astype(F32) + r0[...].astype(F32)) + r1[...].astype(F32)) + r2[...].astype(F32)

    return pl.pallas_call(
        body,
        grid_spec=pltpu.PrefetchScalarGridSpec(
            num_scalar_prefetch=1,
            grid=(R // tr,),
            in_specs=[pl.BlockSpec((1, tr, C), lambda i, chip_ref: (chip_ref[0], i, 0))]
            + [pl.BlockSpec((1, tr, C), functools.partial(lambda i, chip_ref, s: (s, i, 0), s=s)) for s in range(3)]
            + [ANY],
            out_specs=pl.BlockSpec((1, tr, C), lambda i, chip_ref: (l, i, 0)),
        ),
        out_shape=SDS(stack.shape, F32),
        input_output_aliases={5: 0},
        name="grad_sum_recv",
        compiler_params=_cp(("arbitrary",)),
    )(chip_idx, own, recv, recv, recv, stack)


def _sum_small_call(own, recv, dev_idx):
    RS, C = own.shape
    tr = _rows_tile(RS, C)

    def body(dev_ref, own_ref, *refs):
        o_ref = refs[NDEV]
        dev = dev_ref[0]
        acc = jnp.where(dev == 0, own_ref[...], refs[0][0])
        for s in range(1, NDEV):
            acc = acc + jnp.where(dev == s, own_ref[...], refs[s][0])
        o_ref[...] = acc

    return pl.pallas_call(
        body,
        grid_spec=pltpu.PrefetchScalarGridSpec(
            num_scalar_prefetch=1,
            grid=(RS // tr,),
            in_specs=[pl.BlockSpec((tr, C), lambda i, dev_ref: (i, 0))]
            + [pl.BlockSpec((1, tr, C), functools.partial(
                lambda i, dev_ref, s: (jnp.where(dev_ref[0] == s, (s + 1) % NDEV, s), i, 0), s=s)) for s in range(NDEV)],
            out_specs=pl.BlockSpec((tr, C), lambda i, dev_ref: (i, 0)),
        ),
        out_shape=SDS((RS, C), F32),
        name="grad_sum_small",
        compiler_params=_cp(("arbitrary",)),
    )(dev_idx, own, *([recv] * NDEV))


SMALL_ROWS_ALIGN = 128


def _pack_small(parts):
    flat = jnp.concatenate([p.reshape(-1) for p in parts])
    rows = -(-flat.shape[0] // (128 * SMALL_ROWS_ALIGN)) * SMALL_ROWS_ALIGN
    flat = jnp.pad(flat, (0, rows * 128 - flat.shape[0]))
    return flat.reshape(rows, 128)


def _unpack_small(packed, like):
    flat = packed.reshape(-1)
    out, off = [], 0
    for p in like:
        n = int(np.prod(p.shape))
        out.append(flat[off:off + n].reshape(p.shape))
        off += n
    return out


def kernel(x, norm_g, w_in, q_norm, k_norm, sinks, w_s, b_s, w_out, loss_target, m_norm_g, m_w_in, m_q_norm, m_k_norm, m_sinks, m_w_s, m_b_s, m_w_out, v_norm_g, v_w_in, v_q_norm, v_k_norm, v_sinks, v_w_s, v_b_s, v_w_out):
    L = norm_g.shape[0]
    xi, yi, ci = lax.axis_index("x"), lax.axis_index("y"), lax.axis_index("c")
    chip_idx = (2 * xi + yi).astype(jnp.int32).reshape(1)
    dev_idx = (4 * xi + 2 * yi + ci).astype(jnp.int32).reshape(1)
    bias = _alibi_bias()
    b2 = _half_sum_matrix()
    tri =jnp.tril(jnp.ones((WIN, WIN), F32))

    fin = _cast_to_slab_call(w_in, chip_idx, "cast_w_in")
    fout = _cast_to_slab_call(w_out, chip_idx, "cast_w_out")
    fulls = [a for l in range(L) for a in (fin[l], fout[l])]
    g_ssems, g_rsems, fulls = _gather_start_call(fulls, 2)

    saved = []
    xs = x[0]
    dy = loss = None
    for l in range(L):
        sl = slice(2 * l, 2 * l + 2)
        w_in_l, w_out_l = _gather_wait_call(fulls[sl], g_ssems[sl], g_rsems[sl], xs, l == 0, f"gather_wait_{l}")
        if l == 0:
            w_in_l, w_out_l = _sibling_forward_call([w_in_l, w_out_l])
        w_out_l = w_out_l.reshape(D, D)
        proj, h = _fwd_in_call(xs, norm_g[l:l + 1], w_in_l)
        ws_tril = (w_s[l] * tri).astype(MM)
        b_exp = jnp.repeat(b_s[l].T, HD, axis=1)
        wq2 = jnp.tile(q_norm[l:l + 1], (1, 2)) * SCALE
        wk2 = jnp.tile(k_norm[l:l + 1], (1, 2))
        mix = _fwd_mix_call(proj, bias, wq2, wk2, b2, sinks[l], ws_tril, b_exp)
        saved.append((xs, proj, h, mix, ws_tril, b_exp, w_in_l, w_out_l, wq2, wk2))
        if l < L - 1:
            xs = _fwd_out_call(xs, mix, w_out_l)
        else:
            dy, loss = _fwd_out_loss_call(xs, mix, w_out_l, loss_target[0])

    s_in = lax.empty((L, D, SHW), F32)
    s_out = lax.empty((L, SHR, D), F32)
    small_sums = [None] * L

    def finish(pending, after):
        nonlocal s_in, s_out
        l, srcs, lands, ssem, rsem = pending
        srcs, lands = _grad_wait_call(srcs, lands, ssem, rsem, after, f"grad_wait_{l}")
        s_in = _sum_recv_call(srcs[0], lands[0], chip_idx, s_in, l)
        s_out = _sum_recv_call(srcs[1], lands[1], chip_idx, s_out, l)
        small_sums[l] = _sum_small_call(srcs[2], lands[2], dev_idx)

    pending = None
    d_norm_g = [None] * L
    for l in reversed(range(L)):
        xs, proj, h, mix, ws_tril, b_exp, w_in_l, w_out_l, wq2, wk2 = saved[l]
        w_out_t = _transpose_call(w_out_l[None], "transpose_w_out")
        w_in_t = _transpose_call(w_in_l, "transpose_w_in")
        g_w_out = _grad_w_out_call(mix, dy).reshape(NCHIP, SHR, D)
        dmix = _bwd_out_call(dy, w_out_t)
        ws_tril_t = jnp.swapaxes(ws_tril, 1, 2)
        dproj, dwq, dwk, dsk, dws, dbs = _bwd_mix_call(
            proj, dmix, bias, wq2, wk2, b2, sinks[l], ws_tril, ws_tril_t, b_exp)
        dwq, dwk = dwq[:, :HD] + dwq[:, HD:], dwk[:, :HD] + dwk[:, HD:]
        g_w_in = _grad_w_in_call(h, dproj)
        g_small = _pack_small([dwq, dwk, dsk[:, 0], dws, dbs])
        srcs, lands, ssem, rsem, token = _grad_start_call(g_w_in, g_w_out, g_small, f"grad_start_{l}")
        dy, d_norm_g[l] = _bwd_in_call(dproj, w_in_t, xs, dy, norm_g[l:l + 1], token)
        if pending is not None:
            finish(pending, dy)
        pending = (l, srcs, lands, ssem, rsem)
    finish(pending, dy)
    grad_x = dy
    g_norm_g = _small_allreduce_call(jnp.concatenate(d_norm_g, axis=0))

    t_in, t_out = _sibling_swap_call([s_in, s_out])
    g_w_in, d_in, nm_in, nv_in = _adam_call(
        w_in.reshape(L * D, SHW), [s_in.reshape(L * D, SHW), t_in.reshape(L * D, SHW)],
        m_w_in.reshape(L * D, SHW), v_w_in.reshape(L * D, SHW), "adam_w_in")
    g_w_out, d_out, nm_out, nv_out = _adam_call(
        w_out.reshape(L * SHR, D), [s_out.reshape(L * SHR, D), t_out.reshape(L * SHR, D)],
        m_w_out.reshape(L * SHR, D), v_w_out.reshape(L * SHR, D), "adam_w_out")

    def pack_layers(parts):
        return jnp.concatenate([_pack_small([p[l] for p in parts]) for l in range(L)], axis=0)

    small_like = [q_norm, k_norm, sinks, w_s, b_s]
    g_small, d_s, nm_s, nv_s = _adam_call(
        pack_layers(small_like), [jnp.concatenate(small_sums, axis=0)],
        pack_layers([m_q_norm, m_k_norm, m_sinks, m_w_s, m_b_s]),
        pack_layers([v_q_norm, v_k_norm, v_sinks, v_w_s, v_b_s]), "adam_small")
    norm_outs = _adam_call(norm_g, [g_norm_g], m_norm_g, v_norm_g, "adam_norm_g")

    def full(i, small, win, wout):
        rows = small.shape[0] // L
        per_layer = [_unpack_small(small[l * rows:(l + 1) * rows], [p[l] for p in small_like]) for l in range(L)]
        qn, kn, sk, ws, bs = [jnp.stack([per_layer[l][k] for l in range(L)]) for k in range(5)]
        return [norm_outs[i], win.reshape(w_in.shape), qn, kn, sk, ws, bs, wout.reshape(w_out.shape)]

    loss_all = lax.psum(loss[0, 0], ("x", "y", "c"))
    return (loss_all, grad_x[None], *full(0, g_small, g_w_in, g_w_out), *full(1, d_s, d_in, d_out),
            *full(2, nm_s, nm_in, nm_out), *full(3, nv_s, nv_in, nv_out))
```

```python
import functools
import math

import numpy as np
import jax
import jax.numpy as jnp
from jax import lax
from jax.experimental import pallas as pl
from jax.experimental.pallas import tpu as pltpu

F32 = jnp.float32
MM = jnp.bfloat16

D = 2048
HD = 64
DA = 1024
DKV = 256
DG = 1024
NQ, NKV, GRP, NG = 16, 4, 4, 16
WIN = 128
DIN = 5632
C_Q, C_K, C_V, C_GA, C_U, C_VS, C_GB = 0, 1024, 1280, 1536, 2560, 3584, 4608
NCHIP = 4
SHW = DIN // NCHIP
SHR = D // NCHIP
EPS = 1e-6
NEG = -1e30
SCALE = HD ** -0.5
INV_SQRT2 = 1.0 / math.sqrt(2.0)
INV_SQRT_2PI = 1.0 / math.sqrt(2.0 * math.pi)
LR, B1, B2, ADAM_EPS, WD, STEP = 0.001, 0.9, 0.999, 1e-08, 0.01, 10
VMEM_LIMIT = 56 * 1024 * 1024

SDS = jax.ShapeDtypeStruct
NT = (((1,), (1,)), ((), ()))
TN = (((0,), (0,)), ((), ()))


def _cp(sem=None):
    return pltpu.CompilerParams(dimension_semantics=sem, vmem_limit_bytes=VMEM_LIMIT)


def _sigmoid(x):
    return 1.0 / (1.0 + jnp.exp(-x))


def _gelu(x):
    return 0.5 * x * (1.0 + lax.erf(x * INV_SQRT2))


def _dgelu(x):
    return 0.5 * (1.0 + lax.erf(x * INV_SQRT2)) + x * jnp.exp(-0.5 * x * x) * INV_SQRT_2PI


def _alibi_bias():
    slopes = 2.0 ** (-8.0 * np.arange(1, NQ + 1) / NQ)
    dist = (np.arange(WIN)[:, None] + WIN) - np.arange(2 * WIN)[None, :]
    ok = (dist >= 0) & (dist < WIN)
    first = ok & (np.arange(2 * WIN)[None, :] >= WIN)
    val = -slopes[:, None, None] * dist[None].astype(np.float64)
    return jnp.asarray(np.stack([np.where(first[None], val, NEG), np.where(ok[None], val, NEG)]), dtype=F32)


def _half_sum_matrix():
    half = np.arange(LANE) // HD
    return jnp.asarray(half[:, None] == half[None, :], dtype=MM)


LANE = 128
NQT = DA // LANE
NKT = DKV // LANE


def _tiles(ref, c0, n):
    return jnp.concatenate([ref[:, c0 + j * LANE:c0 + (j + 1) * LANE] for j in range(n)], axis=0)


def _split(x):
    hi = x.astype(MM)
    return hi, (x - hi.astype(F32)).astype(MM)


def _half_sums(x, b2):
    hi, lo = _split(x)
    return jnp.dot(hi, b2, preferred_element_type=F32) + jnp.dot(lo, b2, preferred_element_type=F32)


def _attn_fwd(pm_ref, kvp_ref, bias_ref, wq2, wk2, b2, sink_ref):
    lo_half = lax.broadcasted_iota(jnp.int32, (1, LANE), 1) < HD
    q_ts = _tiles(pm_ref, C_Q, NQT)
    rq = lax.rsqrt(_half_sums(q_ts * q_ts, b2) * (1.0 / HD) + EPS)
    qs = (q_ts * rq * wq2).astype(MM)
    k_ts = jnp.concatenate([a[:, c0 + t * LANE:c0 + (t + 1) * LANE] for t in range(NKT)
                            for a, c0 in ((kvp_ref, 0), (pm_ref, C_K))], axis=0)
    rk = lax.rsqrt(_half_sums(k_ts * k_ts, b2) * (1.0 / HD) + EPS)
    kn = (k_ts * rk * wk2).astype(MM)
    v_ts = jnp.concatenate([a[:, c0 + t * LANE:c0 + (t + 1) * LANE] for t in range(NKT)
                            for a, c0 in ((kvp_ref, DKV), (pm_ref, C_V))], axis=0).astype(MM)
    ones = jnp.ones((2 * WIN, LANE), MM)
    km, vm = {}, {}
    for hk in range(NKV):
        t, eh = hk // 2, hk % 2
        sel = lo_half if eh == 0 else jnp.logical_not(lo_half)
        rows = slice(t * 2 * WIN, (t + 1) * 2 * WIN)
        k_same = jnp.where(sel, kn[rows], jnp.zeros_like(kn[rows]))
        v_same = jnp.where(sel, v_ts[rows], jnp.zeros_like(v_ts[rows]))
        km[hk, eh], km[hk, 1 - eh] = k_same, pltpu.roll(k_same, HD, axis=1)
        vm[hk, eh], vm[hk, 1 - eh] = v_same, pltpu.roll(v_same, HD, axis=1)
    heads = []
    for h in range(NQ):
        j, e, hk = h // 2, h % 2, h // GRP
        s = lax.dot_general(qs[j * WIN:(j + 1) * WIN], km[hk, e], NT, preferred_element_type=F32) + bias_ref[0, h]
        sink = sink_ref[h]
        m = jnp.maximum(jnp.max(s, axis=-1, keepdims=True), sink)
        p = jnp.exp(s - m)
        res = jnp.dot(p.astype(MM), jnp.concatenate([vm[hk, e], ones], axis=1), preferred_element_type=F32)
        esink = jnp.exp(sink - m)
        inv = 1.0 / (res[:, LANE:] + esink)
        heads.append(dict(p=p, inv=inv, esink=esink, o=res[:, :LANE] * inv))
    return dict(lo_half=lo_half, q_ts=q_ts, rq=rq, qs=qs, k_ts=k_ts, rk=rk, km=km, vm=vm, heads=heads)


def _sgu_mix(w_ref, zt, lo_half, j):
    zero = jnp.zeros_like(zt)
    return (jnp.dot(w_ref[2 * j], jnp.where(lo_half, zt, zero), preferred_element_type=F32)
            + jnp.dot(w_ref[2 * j + 1], jnp.where(lo_half, zero, zt), preferred_element_type=F32))


def _fwd_mix_call(proj, bias, wq2, wk2, b2, sinks, ws_tril, b_exp):
    T = proj.shape[0]
    nb = T // WIN

    def body(sink_ref, pm_ref, kvp_ref, bias_ref, wq_ref, wk_ref, b2_ref, ws_ref, be_ref, mix_ref):
        a = _attn_fwd(pm_ref, kvp_ref, bias_ref, wq_ref[...], wk_ref[...], b2_ref[...], sink_ref)
        for j in range(NQT):
            cols = slice(j * LANE, (j + 1) * LANE)
            ga = pm_ref[:, C_GA + j * LANE:C_GA + (j + 1) * LANE]
            attn = a["heads"][2 * j]["o"] + a["heads"][2 * j + 1]["o"]
            mix_ref[:, cols] = (attn * (ga * _sigmoid(ga))).astype(MM)
        zu = _gelu(pm_ref[:, C_U:C_U + DG])
        zv = _gelu(pm_ref[:, C_VS:C_VS + DG]).astype(MM)
        mixed = jnp.concatenate(
            [_sgu_mix(ws_ref, zv[:, j * LANE:(j + 1) * LANE], a["lo_half"], j) for j in range(NG // 2)], axis=1)
        mixed = mixed + be_ref[...]
        gb = pm_ref[:, C_GB:C_GB + DG]
        mix_ref[:, DA:DA + DG] = (zu * mixed * (gb * _sigmoid(gb))).astype(MM)

    return pl.pallas_call(
        body,
        grid=(nb,),
        in_specs=[
            pl.BlockSpec(memory_space=pltpu.SMEM),
            pl.BlockSpec((WIN, DIN), lambda n: (n, 0)),
            pl.BlockSpec((WIN, 2 * DKV), lambda n: (jnp.maximum(n - 1, 0), C_K // (2 * DKV))),
            pl.BlockSpec((1, NQ, WIN, 2 * WIN), lambda n: (jnp.minimum(n, 1), 0, 0, 0)),
            pl.BlockSpec((1, LANE), lambda n: (0, 0)),
            pl.BlockSpec((1, LANE), lambda n: (0, 0)),
            pl.BlockSpec((LANE, LANE), lambda n: (0, 0)),
            pl.BlockSpec((NG, WIN, WIN), lambda n: (0, 0, 0)),
            pl.BlockSpec((WIN, DG), lambda n: (0, 0)),
        ],
        out_specs=pl.BlockSpec((WIN, D), lambda n: (n, 0)),
        out_shape=SDS((T, D), MM),
        name="fwd_mix",
        compiler_params=_cp(("arbitrary",)),
    )(sinks, proj, proj, bias, wq2, wk2, b2, ws_tril, b_exp)


def _bwd_mix_call(proj, dmix, bias, wq2, wk2, b2, sinks, ws_tril, ws_tril_t, b_exp):
    T = proj.shape[0]
    nb = T // WIN

    def body(sink_ref, pm_ref, kvp_ref, dm_ref, bias_ref, wq_ref, wk_ref, b2_ref, ws_ref, wst_ref, be_ref,
             dp_ref, dwq_ref, dwk_ref, dsk_ref, dws_ref, dbs_ref, carry_ref, dbacc_ref):
        n = pl.program_id(0)

        @pl.when(n == 0)
        def _():
            carry_ref[...] = jnp.zeros_like(carry_ref)
            dbacc_ref[...] = jnp.zeros_like(dbacc_ref)
            dwq_ref[...] = jnp.zeros_like(dwq_ref)
            dwk_ref[...] = jnp.zeros_like(dwk_ref)
            dsk_ref[...] = jnp.zeros_like(dsk_ref)
            dws_ref[...] = jnp.zeros_like(dws_ref)
            dbs_ref[...] = jnp.zeros_like(dbs_ref)

        @pl.when(n < nb)
        def _():
            wq2, wk2, b2 = wq_ref[...], wk_ref[...], b2_ref[...]
            a = _attn_fwd(pm_ref, kvp_ref, bias_ref, wq2, wk2, b2, sink_ref)
            lo_half, heads, km, vm, qs = a["lo_half"], a["heads"], a["km"], a["vm"], a["qs"]
            hi_half = jnp.logical_not(lo_half)

            dp_ref[:, C_Q:C_K] = carry_ref[:, C_Q:C_K].astype(MM)
            dp_ref[:, C_GA:DIN] = carry_ref[:, C_GA:DIN].astype(MM)

            row_lo = lax.broadcasted_iota(jnp.int32, (LANE, 2 * WIN), 0) < HD
            pick = [jnp.where(row_lo, 1.0, 0.0).astype(MM), jnp.where(row_lo, 0.0, 1.0).astype(MM)]
            dqs_tiles, dk_acc, dv_acc = [], {}, {}
            for j in range(NQT):
                cols = slice(C_GA + j * LANE, C_GA + (j + 1) * LANE)
                ga = pm_ref[:, cols]
                sga = _sigmoid(ga)
                d_gated = dm_ref[:, j * LANE:(j + 1) * LANE]
                attn = heads[2 * j]["o"] + heads[2 * j + 1]["o"]
                carry_ref[:, cols] = d_gated * attn * (sga * (1.0 + ga * (1.0 - sga)))
                d_o = d_gated * (ga * sga)
                d_ob = d_o.astype(MM)
                zero = jnp.zeros_like(d_ob)
                dlt_hi, dlt_lo = _split(d_o * attn)
                qs_j = qs[j * WIN:(j + 1) * WIN]
                dqs = None
                for e in range(2):
                    h = 2 * j + e
                    hk, hd = h // GRP, heads[h]
                    sel = lo_half if e == 0 else hi_half
                    d_p = lax.dot_general(d_ob, vm[hk, e], NT, preferred_element_type=F32)
                    delta = (jnp.dot(dlt_hi, pick[e], preferred_element_type=F32)
                             + jnp.dot(dlt_lo, pick[e], preferred_element_type=F32))
                    pn = hd["p"] * jnp.concatenate([hd["inv"], hd["inv"]], axis=1)
                    d_s = (pn * (d_p - delta)).astype(MM)
                    dsk_ref[h:h + 1, :] -= jnp.sum(hd["esink"] * hd["inv"] * delta[:, :LANE], axis=0, keepdims=True)
                    t = jnp.dot(d_s, km[hk, e], preferred_element_type=F32)
                    dqs = t if dqs is None else dqs + t
                    dk_h = lax.dot_general(d_s, jnp.where(sel, qs_j, zero), TN, preferred_element_type=F32)
                    dv_h = lax.dot_general(pn.astype(MM), jnp.where(sel, d_ob, zero), TN, preferred_element_type=F32)
                    key = (hk, e == hk % 2)
                    dk_acc[key] = dk_h if key not in dk_acc else dk_acc[key] + dk_h
                    dv_acc[key] = dv_h if key not in dv_acc else dv_acc[key] + dv_h
                dqs_tiles.append(dqs)

            dqs_ts = jnp.concatenate(dqs_tiles, axis=0)
            q_ts, rq = a["q_ts"], a["rq"]
            gq = dqs_ts * wq2
            d_q = rq * gq - q_ts * (rq * rq * rq) * (_half_sums(gq * q_ts, b2) * (1.0 / HD))
            dwq_ref[...] += SCALE * jnp.sum(dqs_ts * q_ts * rq, axis=0, keepdims=True)
            for j in range(NQT):
                carry_ref[:, C_Q + j * LANE:C_Q + (j + 1) * LANE] = d_q[j * WIN:(j + 1) * WIN]

            dkn_tiles, dv_tiles = [], []
            for t in range(NKT):
                for acc, out in ((dk_acc, dkn_tiles), (dv_acc, dv_tiles)):
                    parts = [acc[hk, True] + pltpu.roll(acc[hk, False], HD, axis=1) for hk in (2 * t, 2 * t + 1)]
                    out.append(parts[0] + parts[1])
            dkn_ts = jnp.concatenate(dkn_tiles, axis=0)
            dv_ts = jnp.concatenate(dv_tiles, axis=0)
            k_ts, rk = a["k_ts"], a["rk"]
            gk = dkn_ts * wk2
            d_k = rk * gk - k_ts * (rk * rk * rk) * (_half_sums(gk * k_ts, b2) * (1.0 / HD))
            dwk_ref[...] += jnp.sum(dkn_ts * k_ts * rk, axis=0, keepdims=True)
            for t in range(NKT):
                for base, val in ((C_K, d_k), (C_V, dv_ts)):
                    cols = slice(base + t * LANE, base + (t + 1) * LANE)
                    r0 = t * 2 * WIN
                    dp_ref[:, cols] = (carry_ref[:, cols] + val[r0:r0 + WIN]).astype(MM)
                    carry_ref[:, cols] = val[r0 + WIN:r0 + 2 * WIN]

            u = pm_ref[:, C_U:C_U + DG]
            vs = pm_ref[:, C_VS:C_VS + DG]
            gb = pm_ref[:, C_GB:C_GB + DG]
            zu = _gelu(u)
            zvb = _gelu(vs).astype(MM)
            mixed = jnp.concatenate(
                [_sgu_mix(ws_ref, zvb[:, j * LANE:(j + 1) * LANE], lo_half, j) for j in range(NG // 2)], axis=1)
            mixed = mixed + be_ref[...]
            sgb = _sigmoid(gb)
            d_sgu = dm_ref[:, DA:DA + DG]
            carry_ref[:, C_GB:DIN] = d_sgu * zu * mixed * (sgb * (1.0 + gb * (1.0 - sgb)))
            d_mixed = d_sgu * zu * (gb * sgb)
            carry_ref[:, C_U:C_VS] = d_sgu * mixed * (gb * sgb) * _dgelu(u)
            dbacc_ref[...] += d_mixed
            dmb = d_mixed.astype(MM)
            dzv_tiles = []
            for j in range(NG // 2):
                dt = dmb[:, j * LANE:(j + 1) * LANE]
                zt = zvb[:, j * LANE:(j + 1) * LANE]
                zero = jnp.zeros_like(dt)
                dzv_tiles.append(_sgu_mix(wst_ref, dt, lo_half, j))
                dws_ref[2 * j] += lax.dot_general(jnp.where(lo_half, dt, zero), zt, NT, preferred_element_type=F32)
                dws_ref[2 * j + 1] += lax.dot_general(jnp.where(lo_half, zero, dt), zt, NT, preferred_element_type=F32)
            carry_ref[:, C_VS:C_GB] = jnp.concatenate(dzv_tiles, axis=1) * _dgelu(vs)

        @pl.when(n == nb)
        def _():
            dp_ref[...] = carry_ref[...].astype(MM)
            lo_half = lax.broadcasted_iota(jnp.int32, (8, LANE), 1) < HD
            ones = [jnp.where(lo_half, 1.0, 0.0).astype(MM), jnp.where(lo_half, 0.0, 1.0).astype(MM)]
            hi, lo = _split(dbacc_ref[...])
            for h in range(NG):
                sl = slice((h // 2) * LANE, (h // 2 + 1) * LANE)
                r = (lax.dot_general(ones[h % 2], hi[:, sl], NT, preferred_element_type=F32)
                     + lax.dot_general(ones[h % 2], lo[:, sl], NT, preferred_element_type=F32))
                dbs_ref[h:h + 1, :] = r[0:1, :]
            row = lax.broadcasted_iota(jnp.int32, (WIN, WIN), 0)
            cl = lax.broadcasted_iota(jnp.int32, (WIN, WIN), 1)
            for h in range(NG):
                dws_ref[h] = jnp.where(row >= cl, dws_ref[h], 0.0)

    last = nb - 1
    return pl.pallas_call(
        body,
        grid_spec=pltpu.PrefetchScalarGridSpec(
            num_scalar_prefetch=0,
            grid=(nb + 1,),
            in_specs=[
                pl.BlockSpec(memory_space=pltpu.SMEM),
                pl.BlockSpec((WIN, DIN), lambda n: (jnp.minimum(n, last), 0)),
                pl.BlockSpec((WIN, 2 * DKV), lambda n: (jnp.maximum(jnp.minimum(n, last) - 1, 0), C_K // (2 * DKV))),
                pl.BlockSpec((WIN, D), lambda n: (jnp.minimum(n, last), 0)),
                pl.BlockSpec((1, NQ, WIN, 2 * WIN), lambda n: (jnp.minimum(n, 1), 0, 0, 0)),
                pl.BlockSpec((1, LANE), lambda n: (0, 0)),
                pl.BlockSpec((1, LANE), lambda n: (0, 0)),
                pl.BlockSpec((LANE, LANE), lambda n: (0, 0)),
                pl.BlockSpec((NG, WIN, WIN), lambda n: (0, 0, 0)),
                pl.BlockSpec((NG, WIN, WIN), lambda n: (0, 0, 0)),
                pl.BlockSpec((WIN, DG), lambda n: (0, 0)),
            ],
            out_specs=[
                pl.BlockSpec((WIN, DIN), lambda n: (jnp.maximum(n - 1, 0), 0)),
                pl.BlockSpec((1, LANE), lambda n: (0, 0)),
                pl.BlockSpec((1, LANE), lambda n: (0, 0)),
                pl.BlockSpec((NQ, WIN), lambda n: (0, 0)),
                pl.BlockSpec((NG, WIN, WIN), lambda n: (0, 0, 0)),
                pl.BlockSpec((NG, WIN), lambda n: (0, 0)),
            ],
            scratch_shapes=[pltpu.VMEM((WIN, DIN), F32), pltpu.VMEM((WIN, DG), F32)],
        ),
        out_shape=[SDS((T, DIN), MM), SDS((1, LANE), F32), SDS((1, LANE), F32), SDS((NQ, WIN), F32),
                   SDS((NG, WIN, WIN), F32), SDS((NG, WIN), F32)],
        name="bwd_mix",
        compiler_params=_cp(("arbitrary",)),
    )(sinks, proj, proj, dmix, bias, wq2, wk2, b2, ws_tril, ws_tril_t, b_exp)


WEIGHT_RESIDENT_ROWS = 256
GRAD_TOKEN_TILE = 1024


def _row_tile(T):
    return min(512, T)


def _fwd_in_call(x, g_row, w_sh):
    T = x.shape[0]
    tm = min(WEIGHT_RESIDENT_ROWS, T)

    def body(x_ref, g_ref, w_hbm, proj_ref, h_ref, w_vmem, sem):
        @pl.when(pl.program_id(0) == 0)
        def _():
            cp = pltpu.make_async_copy(w_hbm, w_vmem, sem)
            cp.start()
            cp.wait()

        xv = x_ref[...]
        r = lax.rsqrt(jnp.mean(xv * xv, axis=-1, keepdims=True) + EPS)
        h = (xv * r * g_ref[...]).astype(MM)
        h_ref[...] = h
        for j in range(NCHIP):
            proj_ref[:, j * SHW:(j + 1) * SHW] = jnp.dot(h, w_vmem[j], preferred_element_type=F32)

    return pl.pallas_call(
        body,
        grid=(T // tm,),
        in_specs=[pl.BlockSpec((tm, D), lambda i: (i, 0)),
                  pl.BlockSpec((1, D), lambda i: (0, 0)),
                  pl.BlockSpec(memory_space=pl.ANY)],
        out_specs=[pl.BlockSpec((tm, DIN), lambda i: (i, 0)),
                   pl.BlockSpec((tm, D), lambda i: (i, 0))],
        out_shape=[SDS((T, DIN), F32), SDS((T, D), MM)],
        scratch_shapes=[pltpu.VMEM((NCHIP, D, SHW), MM), pltpu.SemaphoreType.DMA],
        name="fwd_in",
        compiler_params=_cp(("arbitrary",)),
    )(x, g_row, w_sh)


def _fwd_out_call(x, mix, w_out):
    T = x.shape[0]
    tm = _row_tile(T)

    def body(x_ref, mix_ref, w_ref, y_ref):
        y_ref[...] = x_ref[...] + jnp.dot(mix_ref[...], w_ref[...], preferred_element_type=F32)

    return pl.pallas_call(
        body,
        grid=(T // tm,),
        in_specs=[pl.BlockSpec((tm, D), lambda i: (i, 0)),
                  pl.BlockSpec((tm, D), lambda i: (i, 0)),
                  pl.BlockSpec((D, D), lambda i: (0, 0))],
        out_specs=pl.BlockSpec((tm, D), lambda i: (i, 0)),
        out_shape=SDS((T, D), F32),
        name="fwd_out",
        compiler_params=_cp(("arbitrary",)),
    )(x, mix, w_out)


def _fwd_out_loss_call(x, mix, w_out, target):
    T = x.shape[0]
    tm = _row_tile(T)

    def body(x_ref, mix_ref, w_ref, t_ref, dy_ref, loss_ref):
        @pl.when(pl.program_id(0) == 0)
        def _():
            loss_ref[...] = jnp.zeros_like(loss_ref)

        e = x_ref[...] + jnp.dot(mix_ref[...], w_ref[...], preferred_element_type=F32) - t_ref[...]
        dy_ref[...] = e * (1.0 / D)
        loss_ref[...] += (0.5 / D) * jnp.sum(jnp.sum(e * e, axis=1, keepdims=True), axis=0, keepdims=True)

    return pl.pallas_call(
        body,
        grid=(T // tm,),
        in_specs=[pl.BlockSpec((tm, D), lambda i: (i, 0)),
                  pl.BlockSpec((tm, D), lambda i: (i, 0)),
                  pl.BlockSpec((D, D), lambda i: (0, 0)),
                  pl.BlockSpec((tm, D), lambda i: (i, 0))],
        out_specs=[pl.BlockSpec((tm, D), lambda i: (i, 0)),
                   pl.BlockSpec((1, 1), lambda i: (0, 0))],
        out_shape=[SDS((T, D), F32), SDS((1, 1), F32)],
        name="fwd_out_loss",
        compiler_params=_cp(("arbitrary",)),
    )(x, mix, w_out, target)


def _bwd_out_call(dy, w_out, token):
    T = dy.shape[0]
    tm = _row_tile(T)

    def body(dy_ref, w_ref, token_ref, o_ref):
        o_ref[...] = lax.dot_general(dy_ref[...].astype(MM), w_ref[...], NT, preferred_element_type=F32)

    return pl.pallas_call(
        body,
        grid=(T // tm,),
        in_specs=[pl.BlockSpec((tm, D), lambda i: (i, 0)),
                  pl.BlockSpec((D, D), lambda i: (0, 0)),
                  pl.BlockSpec(memory_space=pl.ANY)],
        out_specs=pl.BlockSpec((tm, D), lambda i: (i, 0)),
        out_shape=SDS((T, D), F32),
        name="bwd_out",
        compiler_params=_cp(("arbitrary",)),
    )(dy, w_out, token)


def _bwd_in_call(dproj, w_sh, x, dy, g_row, token):
    T = x.shape[0]
    tm = min(WEIGHT_RESIDENT_ROWS, T)

    def body(dp_ref, w_hbm, x_ref, dy_ref, g_ref, token_ref, dx_ref, dg_ref, w_vmem, sem):
        @pl.when(pl.program_id(0) == 0)
        def _():
            cp = pltpu.make_async_copy(w_hbm, w_vmem, sem)
            cp.start()
            dg_ref[...] = jnp.zeros_like(dg_ref)
            cp.wait()

        dh = lax.dot_general(dp_ref[:, 0:SHW], w_vmem[0], NT, preferred_element_type=F32)
        for j in range(1, NCHIP):
            dh = dh + lax.dot_general(dp_ref[:, j * SHW:(j + 1) * SHW], w_vmem[j], NT, preferred_element_type=F32)
        xv = x_ref[...]
        r = lax.rsqrt(jnp.mean(xv * xv, axis=-1, keepdims=True) + EPS)
        gd = dh * g_ref[...]
        dx_ref[...] = dy_ref[...] + r * gd - xv * ((r * r * r) * jnp.mean(gd * xv, axis=-1, keepdims=True))
        dg_ref[...] += jnp.sum(dh * xv * r, axis=0, keepdims=True)

    return pl.pallas_call(
        body,
        grid=(T // tm,),
        in_specs=[pl.BlockSpec((tm, DIN), lambda i: (i, 0)),
                  pl.BlockSpec(memory_space=pl.ANY),
                  pl.BlockSpec((tm, D), lambda i: (i, 0)),
                  pl.BlockSpec((tm, D), lambda i: (i, 0)),
                  pl.BlockSpec((1, D), lambda i: (0, 0)),
                  pl.BlockSpec(memory_space=pl.ANY)],
        out_specs=[pl.BlockSpec((tm, D), lambda i: (i, 0)),
                   pl.BlockSpec((1, D), lambda i: (0, 0))],
        out_shape=[SDS((T, D), F32), SDS((1, D), F32)],
        scratch_shapes=[pltpu.VMEM((NCHIP, D, SHW), MM), pltpu.SemaphoreType.DMA],
        name="bwd_in",
        compiler_params=_cp(("arbitrary",)),
    )(dproj, w_sh, x, dy, g_row, token)


def _grad_w_in_call(h, dproj):
    T = h.shape[0]
    tt = min(GRAD_TOKEN_TILE, T)
    nt = T // tt

    def body(h_ref, dp_ref, o_ref, acc_ref):
        t = pl.program_id(1)

        @pl.when(t == 0)
        def _():
            acc_ref[...] = jnp.zeros_like(acc_ref)

        acc_ref[...] += lax.dot_general(h_ref[...], dp_ref[...], TN, preferred_element_type=F32)

        @pl.when(t == nt - 1)
        def _():
            o_ref[0] = acc_ref[...].astype(MM)

    return pl.pallas_call(
        body,
        grid=(NCHIP, nt),
        in_specs=[pl.BlockSpec((tt, D), lambda j, t: (t, 0)),
                  pl.BlockSpec((tt, SHW), lambda j, t: (t, j))],
        out_specs=pl.BlockSpec((1, D, SHW), lambda j, t: (j, 0, 0)),
        out_shape=SDS((NCHIP, D, SHW), MM),
        scratch_shapes=[pltpu.VMEM((D, SHW), F32)],
        name="grad_w_in",
        compiler_params=_cp(("arbitrary", "arbitrary")),
    )(h, dproj)


def _grad_w_out_call(mix, dy):
    T = mix.shape[0]
    tt = min(GRAD_TOKEN_TILE, T)
    nt = T // tt
    tn = 1024

    def body(m_ref, dy_ref, o_ref, acc_ref):
        t = pl.program_id(1)

        @pl.when(t == 0)
        def _():
            acc_ref[...] = jnp.zeros_like(acc_ref)

        acc_ref[...] += lax.dot_general(m_ref[...], dy_ref[...].astype(MM), TN, preferred_element_type=F32)

        @pl.when(t == nt - 1)
        def _():
            o_ref[...] = acc_ref[...].astype(MM)

    return pl.pallas_call(
        body,
        grid=(D // tn, nt),
        in_specs=[pl.BlockSpec((tt, D), lambda j, t: (t, 0)),
                  pl.BlockSpec((tt, tn), lambda j, t: (t, j))],
        out_specs=pl.BlockSpec((D, tn), lambda j, t: (0, j)),
        out_shape=SDS((D, D), MM),
        scratch_shapes=[pltpu.VMEM((D, tn), F32)],
        name="grad_w_out",
        compiler_params=_cp(("arbitrary", "arbitrary")),
    )(mix, dy)


def _cast_to_slab_call(w, chip_idx, name):
    L, R, C = w.shape
    tr = 256

    def body(chip_ref, *refs):
        for l in range(L):
            refs[L + l][...] = refs[l][...].astype(MM)

    return pl.pallas_call(
        body,
        grid_spec=pltpu.PrefetchScalarGridSpec(
            num_scalar_prefetch=1,
            grid=(R // tr,),
            in_specs=[pl.BlockSpec((1, tr, C), functools.partial(lambda i, chip_ref, l: (l, i, 0), l=l))
                      for l in range(L)],
            out_specs=[pl.BlockSpec((1, tr, C), lambda i, chip_ref: (chip_ref[0], i, 0))] * L,
        ),
        out_shape=[SDS((NCHIP, R, C), MM)] * L,
        name=name,
        compiler_params=_cp(("arbitrary",)),
    )(chip_idx, *([w] * L))


def _adam_call(w, g_parts, m, v, name):
    R, C = w.shape
    tr = R
    for cand in (512, 256, 128, 64, 32, 16, 8):
        if R % cand == 0 and cand * C * 4 <= 1024 * 1024:
            tr = cand
            break
    c1 = 1.0 - B1 ** STEP
    c2 = 1.0 - B2 ** STEP
    ng = len(g_parts)

    def body(*refs):
        w_ref, m_ref, v_ref = refs[0], refs[1 + ng], refs[2 + ng]
        g_ref, d_ref, nm_ref, nv_ref = refs[3 + ng:]
        gv = refs[1][...]
        for k in range(1, ng):
            gv = gv + refs[1 + k][...]
        nm = B1 * m_ref[...] + (1.0 - B1) * gv
        nv = B2 * v_ref[...] + (1.0 - B2) * (gv * gv)
        g_ref[...] = gv
        nm_ref[...] = nm
        nv_ref[...] = nv
        d_ref[...] = -LR * ((nm / c1) / (jnp.sqrt(nv / c2) + ADAM_EPS) + WD * w_ref[...])

    spec = pl.BlockSpec((tr, C), lambda i: (i, 0))
    return pl.pallas_call(
        body,
        grid=(R // tr,),
        in_specs=[spec] * (3 + ng),
        out_specs=[spec] * 4,
        out_shape=[SDS((R, C), F32)] * 4,
        name=name,
        compiler_params=_cp(("arbitrary",)),
    )(w, *g_parts, m, v)


MESH = pl.DeviceIdType.MESH
ANY = pl.BlockSpec(memory_space=pl.ANY)
HBM = pl.BlockSpec(memory_space=pltpu.HBM)
SEMS = pl.BlockSpec(memory_space=pltpu.SEMAPHORE)
EFFECT = pltpu.SideEffectType.DATAFLOW_SIDE_EFFECTING
NDEV = 8


def _hbm(a):
    return pltpu.with_memory_space_constraint(a, pltpu.HBM)


def _place():
    x, y, c = lax.axis_index("x"), lax.axis_index("y"), lax.axis_index("c")
    others = [(1 - x, y), (x, 1 - y), (1 - x, 1 - y)]
    return x, y, c, 2 * x + y, others


def _flipped(x, y, c, r):
    return (1 - x if r & 4 else x, 1 - y if r & 2 else y, 1 - c if r & 1 else c)


def _rcopy(src, dst, ssem, rsem, dev):
    return pltpu.make_async_remote_copy(src_ref=src, dst_ref=dst, send_sem=ssem, recv_sem=rsem,
                                        device_id=dev, device_id_type=MESH)


def _slab(ref, chip, c, halved):
    if not halved:
        return ref.at[chip]
    h = ref.shape[1] // 2
    return ref.at[chip, pl.ds(c * h, h), :]


def _gather_start_call(fulls, n_halved):
    K = len(fulls)

    def body(*refs):
        full, ssem, rsem = refs[:K], refs[K:2 * K], refs[2 * K:3 * K]
        x, y, c, me, others = _place()
        for k in range(K):
            for j, (px, py) in enumerate(others):
                part = _slab(full[k], me, c, k < n_halved)
                _rcopy(part, part, ssem[k].at[j], rsem[k].at[j], (px, py, c)).start()

    outs = pl.pallas_call(
        body,
        in_specs=[HBM] * K,
        out_specs=[SEMS] * (2 * K) + [HBM] * K,
        out_shape=[pltpu.SemaphoreType.DMA((3,))] * (2 * K) + [pltpu.HBM(f.shape, f.dtype) for f in fulls],
        input_output_aliases={k: 2 * K + k for k in range(K)},
        name="gather_start",
        compiler_params=pltpu.CompilerParams(has_side_effects=EFFECT),
    )(*[_hbm(f) for f in fulls])
    return list(outs[:K]), list(outs[K:2 * K]), list(outs[2 * K:])


def _gather_wait_call(fulls, ssems, rsems, after, halved, name):
    K = len(fulls)

    def body(*refs):
        full, ssem, rsem = refs[:K], refs[K:2 * K], refs[2 * K:3 * K]
        x, y, c, me, others = _place()
        for k in range(K):
            for j, (px, py) in enumerate(others):
                cp = _rcopy(_slab(full[k], me, c, halved), _slab(full[k], 2 * px + py, c, halved),
                            ssem[k].at[j], rsem[k].at[j], (px, py, c))
                cp.wait_send()
                cp.wait_recv()

    outs = pl.pallas_call(
        body,
        in_specs=[HBM] * K + [SEMS] * (2 * K) + [ANY],
        out_specs=[HBM] * K,
        out_shape=[pltpu.HBM(f.shape, f.dtype) for f in fulls],
        input_output_aliases={k: k for k in range(K)},
        name=name,
        compiler_params=pltpu.CompilerParams(has_side_effects=EFFECT),
    )(*fulls, *ssems, *rsems, after)
    return list(outs)


def _sibling_forward_call(fulls):
    K = len(fulls)

    def body(*refs):
        full = refs[:K]
        ssem, rsem = refs[2 * K:]
        x, y, c, me, others = _place()
        cps = []
        for k in range(K):
            for j, (px, py) in enumerate(others):
                mine = _slab(full[k], 2 * px + py, c, True)
                cps.append(_rcopy(mine, mine, ssem.at[3 * k + j], rsem.at[3 * k + j], (x, y, 1 - c)))
        for cp in cps:
            cp.start()
        for k in range(K):
            for j, (px, py) in enumerate(others):
                theirs = _slab(full[k], 2 * px + py, 1 - c, True)
                _rcopy(theirs, theirs, ssem.at[3 * k + j], rsem.at[3 * k + j], (x, y, 1 - c)).wait_recv()
        for cp in cps:
            cp.wait_send()

    outs = pl.pallas_call(
        body,
        in_specs=[ANY] * K,
        out_specs=[ANY] * K,
        out_shape=[SDS(f.shape, f.dtype) for f in fulls],
        input_output_aliases={k: k for k in range(K)},
        scratch_shapes=[pltpu.SemaphoreType.DMA((3 * K,)), pltpu.SemaphoreType.DMA((3 * K,))],
        name="gather_sibling_forward",
    )(*fulls)
    return list(outs)


def _grad_copies(srcs, lands, ssem, rsem):
    x, y, c, me, others = _place()
    cps, k = [], 0
    for src, land in zip(srcs, lands):
        if len(src.shape) == 3:
            for j, (px, py) in enumerate(others):
                cps.append(_rcopy(src.at[2 * px + py], land.at[j], ssem.at[k + j], rsem.at[k + j], (px, py, c)))
            k += 3
        else:
            for r in range(1, NDEV):
                cps.append(_rcopy(src, land.at[4 * x + 2 * y + c], ssem.at[k + r - 1], rsem.at[k + r - 1],
                                  _flipped(x, y, c, r)))
            k += NDEV - 1
    return cps


def _n_grad_copies(srcs):
    return sum(3 if len(s.shape) == 3 else NDEV - 1 for s in srcs)


def _grad_start_call(srcs, name):
    srcs = list(srcs)
    K = len(srcs)
    lands = [lax.empty(((3,) + s.shape[1:]) if len(s.shape) == 3 else ((NDEV,) + s.shape), s.dtype) for s in srcs]
    n = _n_grad_copies(srcs)

    def body(*refs):
        ssem, rsem, token = refs[2 * K], refs[2 * K + 1], refs[-1]
        for cp in _grad_copies(refs[:K], refs[K:2 * K], ssem, rsem):
            cp.start()
        token[...] = jnp.zeros_like(token)

    outs = pl.pallas_call(
        body,
        in_specs=[HBM] * (2 * K),
        out_specs=[SEMS, SEMS] + [HBM] * (2 * K) + [pl.BlockSpec(memory_space=pltpu.VMEM)],
        out_shape=[pltpu.SemaphoreType.DMA((n,)), pltpu.SemaphoreType.DMA((n,))]
        + [pltpu.HBM(a.shape, a.dtype) for a in srcs + lands] + [SDS((8, 128), F32)],
        input_output_aliases={k: 2 + k for k in range(2 * K)},
        name=name,
        compiler_params=pltpu.CompilerParams(has_side_effects=EFFECT),
    )(*[_hbm(a) for a in srcs + lands])
    return list(outs[2:2 + K]), list(outs[2 + K:2 + 2 * K]), outs[0], outs[1], outs[-1]


def _grad_wait_call(srcs, lands, ssem, rsem, after, name):
    K = len(srcs)

    def body(*refs):
        for cp in _grad_copies(refs[:K], refs[K:2 * K], refs[2 * K], refs[2 * K + 1]):
            cp.wait_send()
            cp.wait_recv()

    arrs = list(srcs) + list(lands)
    outs = pl.pallas_call(
        body,
        in_specs=[HBM] * (2 * K) + [SEMS, SEMS, ANY],
        out_specs=[HBM] * (2 * K),
        out_shape=[pltpu.HBM(a.shape, a.dtype) for a in arrs],
        input_output_aliases={k: k for k in range(2 * K)},
        name=name,
        compiler_params=pltpu.CompilerParams(has_side_effects=EFFECT),
    )(*arrs, ssem, rsem, after)
    return list(outs[:K]), list(outs[K:])


def _sibling_swap_call(arrs):
    K = len(arrs)

    def body(*refs):
        a_refs, t_refs = refs[:K], refs[K:2 * K]
        ssem, rsem = refs[2 * K:]
        x, y, c, me, others = _place()
        cps = [_rcopy(a_refs[k], t_refs[k], ssem.at[k], rsem.at[k], (x, y, 1 - c)) for k in range(K)]
        for cp in cps:
            cp.start()
        for cp in cps:
            cp.wait()

    return pl.pallas_call(
        body,
        in_specs=[ANY] * K,
        out_specs=[ANY] * K,
        out_shape=[SDS(a.shape, a.dtype) for a in arrs],
        scratch_shapes=[pltpu.SemaphoreType.DMA((K,)), pltpu.SemaphoreType.DMA((K,))],
        name="grad_sibling_swap",
    )(*arrs)


def _small_allreduce_call(a):
    R, C = a.shape

    def body(a_ref, o_ref, recv_ref, ssem, rsem):
        x, y, c, me, others = _place()
        dev = 4 * x + 2 * y + c
        recv_ref[pl.ds(dev, 1)] = a_ref[...][None]
        cps = [_rcopy(a_ref, recv_ref.at[dev], ssem.at[r - 1], rsem.at[r - 1], _flipped(x, y, c, r))
               for r in range(1, NDEV)]
        for cp in cps:
            cp.start()
        for cp in cps:
            cp.wait()
        acc = recv_ref[0]
        for s in range(1, NDEV):
            acc = acc + recv_ref[s]
        o_ref[...] = acc

    return pl.pallas_call(
        body,
        in_specs=[pl.BlockSpec(memory_space=pltpu.VMEM)],
        out_specs=pl.BlockSpec(memory_space=pltpu.VMEM),
        out_shape=SDS((R, C), F32),
        scratch_shapes=[pltpu.VMEM((NDEV, R, C), F32), pltpu.SemaphoreType.DMA((NDEV - 1,)),
                        pltpu.SemaphoreType.DMA((NDEV - 1,))],
        name="small_allreduce",
    )(a)


def _rows_tile(H, C):
    for cand in (512, 256, 128, 64, 32, 16, 8):
        if H % cand == 0 and cand * C * 4 <= 2 * 1024 * 1024:
            return cand
    raise ValueError((H, C))


def _sum_recv_call(own, recv, chip_idx, stack, l):
    _, R, C = own.shape
    tr = _rows_tile(R, C)

    def body(chip_ref, own_ref, r0, r1, r2, stack_ref, o_ref):
        o_ref[...] = ((own_ref[...].astype(F32) + r0[...].astype(F32)) + r1[...].astype(F32)) + r2[...].astype(F32)

    return pl.pallas_call(
        body,
        grid_spec=pltpu.PrefetchScalarGridSpec(
            num_scalar_prefetch=1,
            grid=(R // tr,),
            in_specs=[pl.BlockSpec((1, tr, C), lambda i, chip_ref: (chip_ref[0], i, 0))]
            + [pl.BlockSpec((1, tr, C), functools.partial(lambda i, chip_ref, s: (s, i, 0), s=s)) for s in range(3)]
            + [ANY],
            out_specs=pl.BlockSpec((1, tr, C), lambda i, chip_ref: (l, i, 0)),
        ),
        out_shape=SDS(stack.shape, F32),
        input_output_aliases={5: 0},
        name="grad_sum_recv",
        compiler_params=_cp(("arbitrary",)),
    )(chip_idx, own, recv, recv, recv, stack)


def _sum_small_call(own, recv, dev_idx):
    RS, C = own.shape
    tr = _rows_tile(RS, C)

    def body(dev_ref, own_ref, *refs):
        o_ref = refs[NDEV]
        dev = dev_ref[0]
        acc = jnp.where(dev == 0, own_ref[...], refs[0][0]).astype(F32)
        for s in range(1, NDEV):
            acc = acc + jnp.where(dev == s, own_ref[...], refs[s][0]).astype(F32)
        o_ref[...] = acc

    return pl.pallas_call(
        body,
        grid_spec=pltpu.PrefetchScalarGridSpec(
            num_scalar_prefetch=1,
            grid=(RS // tr,),
            in_specs=[pl.BlockSpec((tr, C), lambda i, dev_ref: (i, 0))]
            + [pl.BlockSpec((1, tr, C), functools.partial(
                lambda i, dev_ref, s: (jnp.where(dev_ref[0] == s, (s + 1) % NDEV, s), i, 0), s=s)) for s in range(NDEV)],
            out_specs=pl.BlockSpec((tr, C), lambda i, dev_ref: (i, 0)),
        ),
        out_shape=SDS((RS, C), F32),
        name="grad_sum_small",
        compiler_params=_cp(("arbitrary",)),
    )(dev_idx, own, *([recv] * NDEV))


SMALL_ROWS_ALIGN = 128


def _pack_small(parts):
    flat = jnp.concatenate([p.reshape(-1) for p in parts])
    rows = -(-flat.shape[0] // (128 * SMALL_ROWS_ALIGN)) * SMALL_ROWS_ALIGN
    flat = jnp.pad(flat, (0, rows * 128 - flat.shape[0]))
    return flat.reshape(rows, 128)


def _unpack_small(packed, like):
    flat = packed.reshape(-1)
    out, off = [], 0
    for p in like:
        n = int(np.prod(p.shape))
        out.append(flat[off:off + n].reshape(p.shape))
        off += n
    return out


def kernel(x, norm_g, w_in, q_norm, k_norm, sinks, w_s, b_s, w_out, loss_target, m_norm_g, m_w_in, m_q_norm, m_k_norm, m_sinks, m_w_s, m_b_s, m_w_out, v_norm_g, v_w_in, v_q_norm, v_k_norm, v_sinks, v_w_s, v_b_s, v_w_out):
    L = norm_g.shape[0]
    xi, yi, ci = lax.axis_index("x"), lax.axis_index("y"), lax.axis_index("c")
    chip_idx = (2 * xi + yi).astype(jnp.int32).reshape(1)
    dev_idx = (4 * xi + 2 * yi + ci).astype(jnp.int32).reshape(1)
    bias = _alibi_bias()
    b2 = _half_sum_matrix()
    tri =jnp.tril(jnp.ones((WIN, WIN), F32))

    fin = _cast_to_slab_call(w_in, chip_idx, "cast_w_in")
    fout = _cast_to_slab_call(w_out, chip_idx, "cast_w_out")
    fulls = [a for l in range(L) for a in (fin[l], fout[l])]
    g_ssems, g_rsems, fulls = _gather_start_call(fulls, 2)

    saved = []
    xs = x[0]
    dy = loss = None
    for l in range(L):
        sl = slice(2 * l, 2 * l + 2)
        w_in_l, w_out_l = _gather_wait_call(fulls[sl], g_ssems[sl], g_rsems[sl], xs, l == 0, f"gather_wait_{l}")
        if l == 0:
            w_in_l, w_out_l = _sibling_forward_call([w_in_l, w_out_l])
        w_out_l = w_out_l.reshape(D, D)
        proj, h = _fwd_in_call(xs, norm_g[l:l + 1], w_in_l)
        ws_tril = (w_s[l] * tri).astype(MM)
        b_exp = jnp.repeat(b_s[l].T, HD, axis=1)
        wq2 = jnp.tile(q_norm[l:l + 1], (1, 2)) * SCALE
        wk2 = jnp.tile(k_norm[l:l + 1], (1, 2))
        mix = _fwd_mix_call(proj, bias, wq2, wk2, b2, sinks[l], ws_tril, b_exp)
        saved.append((xs, proj, h, mix, ws_tril, b_exp, w_in_l, w_out_l, wq2, wk2))
        if l < L - 1:
            xs = _fwd_out_call(xs, mix, w_out_l)
        else:
            dy, loss = _fwd_out_loss_call(xs, mix, w_out_l, loss_target[0])

    s_in = lax.empty((L, D, SHW), F32)
    s_out = lax.empty((L, SHR, D), F32)
    small_sums = [None] * L

    def finish(pending, after):
        nonlocal s_in, s_out
        l, exchanges = pending
        for tag, kinds, srcs, lands, ssem, rsem in exchanges:
            srcs, lands = _grad_wait_call(srcs, lands, ssem, rsem, after, f"grad_wait_{l}{tag}")
            for kind, src, land in zip(kinds, srcs, lands):
                if kind == "in":
                    s_in = _sum_recv_call(src, land, chip_idx, s_in, l)
                elif kind == "out":
                    s_out = _sum_recv_call(src, land, chip_idx, s_out, l)
                else:
                    small_sums[l] = _sum_small_call(src, land, dev_idx)

    def start(l, tag, kinds, srcs):
        srcs, lands, ssem, rsem, token = _grad_start_call(srcs, f"grad_start_{l}{tag}")
        return (tag, kinds, srcs, lands, ssem, rsem), token

    pending = None
    d_norm_g = [None] * L
    for l in reversed(range(L)):
        xs, proj, h, mix, ws_tril, b_exp, w_in_l, w_out_l, wq2, wk2 = saved[l]
        g_w_out = _grad_w_out_call(mix, dy).reshape(NCHIP, SHR, D)
        exchanges, token = [], jnp.zeros((8, 128), F32)
        if l == 0:
            ex, token = start(l, "_out", ["out"], [g_w_out])
            exchanges.append(ex)
        dmix = _bwd_out_call(dy, w_out_l, token)
        ws_tril_t = jnp.swapaxes(ws_tril, 1, 2)
        dproj, dwq, dwk, dsk, dws, dbs = _bwd_mix_call(
            proj, dmix, bias, wq2, wk2, b2, sinks[l], ws_tril, ws_tril_t, b_exp)
        dwq, dwk = dwq[:, :HD] + dwq[:, HD:], dwk[:, :HD] + dwk[:, HD:]
        g_w_in = _grad_w_in_call(h, dproj)
        g_small = _pack_small([dwq, dwk, dsk[:, 0], dws, dbs]).astype(MM)
        if l == 0:
            ex, token = start(l, "", ["in", "small"], [g_w_in, g_small])
        else:
            ex, token = start(l, "", ["in", "out", "small"], [g_w_in, g_w_out, g_small])
        exchanges.append(ex)
        dy, d_norm_g[l] = _bwd_in_call(dproj, w_in_l, xs, dy, norm_g[l:l + 1], token)
        if pending is not None:
            finish(pending, dy)
        pending = (l, exchanges)
    finish(pending, dy)
    grad_x = dy
    g_norm_g = _small_allreduce_call(jnp.concatenate(d_norm_g, axis=0))

    t_in, t_out = _sibling_swap_call([s_in, s_out])
    g_w_in, d_in, nm_in, nv_in = _adam_call(
        w_in.reshape(L * D, SHW), [s_in.reshape(L * D, SHW), t_in.reshape(L * D, SHW)],
        m_w_in.reshape(L * D, SHW), v_w_in.reshape(L * D, SHW), "adam_w_in")
    g_w_out, d_out, nm_out, nv_out = _adam_call(
        w_out.reshape(L * SHR, D), [s_out.reshape(L * SHR, D), t_out.reshape(L * SHR, D)],
        m_w_out.reshape(L * SHR, D), v_w_out.reshape(L * SHR, D), "adam_w_out")

    def pack_layers(parts):
        return jnp.concatenate([_pack_small([p[l] for p in parts]) for l in range(L)], axis=0)

    small_like = [q_norm, k_norm, sinks, w_s, b_s]
    g_small, d_s, nm_s, nv_s = _adam_call(
        pack_layers(small_like), [jnp.concatenate(small_sums, axis=0)],
        pack_layers([m_q_norm, m_k_norm, m_sinks, m_w_s, m_b_s]),
        pack_layers([v_q_norm, v_k_norm, v_sinks, v_w_s, v_b_s]), "adam_small")
    norm_outs = _adam_call(norm_g, [g_norm_g], m_norm_g, v_norm_g, "adam_norm_g")

    def full(i, small, win, wout):
        rows = small.shape[0] // L
        per_layer = [_unpack_small(small[l * rows:(l + 1) * rows], [p[l] for p in small_like]) for l in range(L)]
        qn, kn, sk, ws, bs = [jnp.stack([per_layer[l][k] for l in range(L)]) for k in range(5)]
        return [norm_outs[i], win.reshape(w_in.shape), qn, kn, sk, ws, bs, wout.reshape(w_out.shape)]

    loss_all = lax.psum(loss[0, 0], ("x", "y", "c"))
    return (loss_all, grad_x[None], *full(0, g_small, g_w_in, g_w_out), *full(1, d_s, d_in, d_out),
            *full(2, nm_s, nm_in, nm_out), *full(3, nv_s, nv_in, nv_out))
```

```python
import functools
import math

import numpy as np
import jax
import jax.numpy as jnp
from jax import lax
from jax.experimental import pallas as pl
from jax.experimental.pallas import tpu as pltpu

F32 = jnp.float32
MM = jnp.bfloat16

D = 2048
HD = 64
DA = 1024
DKV = 256
DG = 1024
NQ, NKV, GRP, NG = 16, 4, 4, 16
WIN = 128
DIN = 5632
C_Q, C_K, C_V, C_GA, C_U, C_VS, C_GB = 0, 1024, 1280, 1536, 2560, 3584, 4608
NCHIP = 4
SHW = DIN // NCHIP
SHR = D // NCHIP
EPS = 1e-6
NEG = -1e30
SCALE = HD ** -0.5
INV_SQRT2 = 1.0 / math.sqrt(2.0)
INV_SQRT_2PI = 1.0 / math.sqrt(2.0 * math.pi)
LR, B1, B2, ADAM_EPS, WD, STEP = 0.001, 0.9, 0.999, 1e-08, 0.01, 10
VMEM_LIMIT = 56 * 1024 * 1024

SDS = jax.ShapeDtypeStruct
NT = (((1,), (1,)), ((), ()))
TN = (((0,), (0,)), ((), ()))


def _cp(sem=None):
    return pltpu.CompilerParams(dimension_semantics=sem, vmem_limit_bytes=VMEM_LIMIT)


def _sigmoid(x):
    return 1.0 / (1.0 + jnp.exp(-x))


def _gelu(x):
    return 0.5 * x * (1.0 + lax.erf(x * INV_SQRT2))


def _gelu_and_grad(x):
    cdf = 0.5 * (1.0 + lax.erf(x * INV_SQRT2))
    return x * cdf, cdf + x * jnp.exp(-0.5 * x * x) * INV_SQRT_2PI


def _alibi_bias():
    slopes = 2.0 ** (-8.0 * np.arange(1, NQ + 1) / NQ)
    dist = (np.arange(WIN)[:, None] + WIN) - np.arange(2 * WIN)[None, :]
    ok = (dist >= 0) & (dist < WIN)
    first = ok & (np.arange(2 * WIN)[None, :] >= WIN)
    val = -slopes[:, None, None] * dist[None].astype(np.float64)
    return jnp.asarray(np.stack([np.where(first[None], val, NEG), np.where(ok[None], val, NEG)]), dtype=F32)


def _half_sum_matrix():
    half = np.arange(LANE) // HD
    return jnp.asarray(half[:, None] == half[None, :], dtype=MM)


LANE = 128
NQT = DA // LANE
NKT = DKV // LANE


def _tiles(ref, c0, n):
    return jnp.concatenate([ref[:, c0 + j * LANE:c0 + (j + 1) * LANE] for j in range(n)], axis=0)


def _split(x):
    hi = x.astype(MM)
    return hi, (x - hi.astype(F32)).astype(MM)


def _half_sums(x, b2):
    hi, lo = _split(x)
    return jnp.dot(hi, b2, preferred_element_type=F32) + jnp.dot(lo, b2, preferred_element_type=F32)


def _attn_fwd(pm_ref, kvp_ref, bias_ref, wq2, wk2, b2, sink_ref):
    lo_half = lax.broadcasted_iota(jnp.int32, (1, LANE), 1) < HD
    q_ts = _tiles(pm_ref, C_Q, NQT)
    rq = lax.rsqrt(_half_sums(q_ts * q_ts, b2) * (1.0 / HD) + EPS)
    qs = (q_ts * rq * wq2).astype(MM)
    k_ts = jnp.concatenate([a[:, c0 + t * LANE:c0 + (t + 1) * LANE] for t in range(NKT)
                            for a, c0 in ((kvp_ref, 0), (pm_ref, C_K))], axis=0)
    rk = lax.rsqrt(_half_sums(k_ts * k_ts, b2) * (1.0 / HD) + EPS)
    kn = (k_ts * rk * wk2).astype(MM)
    v_ts = jnp.concatenate([a[:, c0 + t * LANE:c0 + (t + 1) * LANE] for t in range(NKT)
                            for a, c0 in ((kvp_ref, DKV), (pm_ref, C_V))], axis=0).astype(MM)
    ones = jnp.ones((2 * WIN, LANE), MM)
    km, vm = {}, {}
    for hk in range(NKV):
        t, eh = hk // 2, hk % 2
        sel = lo_half if eh == 0 else jnp.logical_not(lo_half)
        rows = slice(t * 2 * WIN, (t + 1) * 2 * WIN)
        k_same = jnp.where(sel, kn[rows], jnp.zeros_like(kn[rows]))
        v_same = jnp.where(sel, v_ts[rows], jnp.zeros_like(v_ts[rows]))
        km[hk, eh], km[hk, 1 - eh] = k_same, pltpu.roll(k_same, HD, axis=1)
        vm[hk, eh], vm[hk, 1 - eh] = v_same, pltpu.roll(v_same, HD, axis=1)
    heads = []
    for h in range(NQ):
        j, e, hk = h // 2, h % 2, h // GRP
        s = lax.dot_general(qs[j * WIN:(j + 1) * WIN], km[hk, e], NT, preferred_element_type=F32) + bias_ref[0, h]
        sink = sink_ref[h]
        m = jnp.maximum(jnp.max(s, axis=-1, keepdims=True), sink)
        p = jnp.exp(s - m)
        pb = p.astype(MM)
        res = jnp.dot(pb, jnp.concatenate([vm[hk, e], ones], axis=1), preferred_element_type=F32)
        esink = jnp.exp(sink - m)
        inv = 1.0 / (res[:, LANE:] + esink)
        heads.append(dict(p=p, pb=pb, inv=inv, esink=esink, o=res[:, :LANE] * inv))
    return dict(lo_half=lo_half, q_ts=q_ts, rq=rq, qs=qs, k_ts=k_ts, rk=rk, km=km, vm=vm, heads=heads)


def _sgu_mix(w_ref, zt, lo_half, j):
    zero = jnp.zeros_like(zt)
    return (jnp.dot(w_ref[2 * j], jnp.where(lo_half, zt, zero), preferred_element_type=F32)
            + jnp.dot(w_ref[2 * j + 1], jnp.where(lo_half, zero, zt), preferred_element_type=F32))


def _fwd_mix_call(proj, bias, wq2, wk2, b2, sinks, ws_tril, b_exp):
    T = proj.shape[0]
    nb = T // WIN

    def body(sink_ref, pm_ref, kvp_ref, bias_ref, wq_ref, wk_ref, b2_ref, ws_ref, be_ref, mix_ref):
        a = _attn_fwd(pm_ref, kvp_ref, bias_ref, wq_ref[...], wk_ref[...], b2_ref[...], sink_ref)
        for j in range(NQT):
            cols = slice(j * LANE, (j + 1) * LANE)
            ga = pm_ref[:, C_GA + j * LANE:C_GA + (j + 1) * LANE]
            attn = a["heads"][2 * j]["o"] + a["heads"][2 * j + 1]["o"]
            mix_ref[:, cols] = (attn * (ga * _sigmoid(ga))).astype(MM)
        zu = _gelu(pm_ref[:, C_U:C_U + DG])
        zv = _gelu(pm_ref[:, C_VS:C_VS + DG]).astype(MM)
        mixed = jnp.concatenate(
            [_sgu_mix(ws_ref, zv[:, j * LANE:(j + 1) * LANE], a["lo_half"], j) for j in range(NG // 2)], axis=1)
        mixed = mixed + be_ref[...]
        gb = pm_ref[:, C_GB:C_GB + DG]
        mix_ref[:, DA:DA + DG] = (zu * mixed * (gb * _sigmoid(gb))).astype(MM)

    return pl.pallas_call(
        body,
        grid=(nb,),
        in_specs=[
            pl.BlockSpec(memory_space=pltpu.SMEM),
            pl.BlockSpec((WIN, DIN), lambda n: (n, 0)),
            pl.BlockSpec((WIN, 2 * DKV), lambda n: (jnp.maximum(n - 1, 0), C_K // (2 * DKV))),
            pl.BlockSpec((1, NQ, WIN, 2 * WIN), lambda n: (jnp.minimum(n, 1), 0, 0, 0)),
            pl.BlockSpec((1, LANE), lambda n: (0, 0)),
            pl.BlockSpec((1, LANE), lambda n: (0, 0)),
            pl.BlockSpec((LANE, LANE), lambda n: (0, 0)),
            pl.BlockSpec((NG, WIN, WIN), lambda n: (0, 0, 0)),
            pl.BlockSpec((WIN, DG), lambda n: (0, 0)),
        ],
        out_specs=pl.BlockSpec((WIN, D), lambda n: (n, 0)),
        out_shape=SDS((T, D), MM),
        name="fwd_mix",
        compiler_params=_cp(("arbitrary",)),
    )(sinks, proj, proj, bias, wq2, wk2, b2, ws_tril, b_exp)


def _bwd_mix_call(proj, dmix, bias, wq2, wk2, b2, sinks, ws_tril, ws_tril_t, b_exp):
    T = proj.shape[0]
    nb = T // WIN

    def body(sink_ref, pm_ref, kvp_ref, dm_ref, bias_ref, wq_ref, wk_ref, b2_ref, ws_ref, wst_ref, be_ref,
             dp_ref, dwq_ref, dwk_ref, dsk_ref, dws_ref, dbs_ref, carry_ref, dbacc_ref):
        n = pl.program_id(0)

        @pl.when(n == 0)
        def _():
            carry_ref[...] = jnp.zeros_like(carry_ref)
            dbacc_ref[...] = jnp.zeros_like(dbacc_ref)
            dwq_ref[...] = jnp.zeros_like(dwq_ref)
            dwk_ref[...] = jnp.zeros_like(dwk_ref)
            dsk_ref[...] = jnp.zeros_like(dsk_ref)
            dws_ref[...] = jnp.zeros_like(dws_ref)
            dbs_ref[...] = jnp.zeros_like(dbs_ref)

        @pl.when(n < nb)
        def _():
            wq2, wk2, b2 = wq_ref[...], wk_ref[...], b2_ref[...]
            a = _attn_fwd(pm_ref, kvp_ref, bias_ref, wq2, wk2, b2, sink_ref)
            lo_half, heads, km, vm, qs = a["lo_half"], a["heads"], a["km"], a["vm"], a["qs"]

            dp_ref[:, C_Q:C_K] = carry_ref[:, C_Q:C_K].astype(MM)
            dp_ref[:, C_GA:DIN] = carry_ref[:, C_GA:DIN].astype(MM)

            row_lo = lax.broadcasted_iota(jnp.int32, (LANE, LANE), 0) < HD
            pick = [jnp.where(row_lo, 1.0, 0.0).astype(MM), jnp.where(row_lo, 0.0, 1.0).astype(MM)]
            chan_lo = lax.broadcasted_iota(jnp.int32, (LANE, 1), 0) < HD
            dqs_tiles, dk_acc, dv_acc = [], {}, {}
            for j in range(NQT):
                cols = slice(C_GA + j * LANE, C_GA + (j + 1) * LANE)
                ga = pm_ref[:, cols]
                sga = _sigmoid(ga)
                d_gated = dm_ref[:, j * LANE:(j + 1) * LANE]
                attn = heads[2 * j]["o"] + heads[2 * j + 1]["o"]
                carry_ref[:, cols] = d_gated * attn * (sga * (1.0 + ga * (1.0 - sga)))
                d_o = d_gated * (ga * sga)
                d_ob = d_o.astype(MM)
                dlt = (d_o * attn).astype(MM)
                inv_tile = jnp.where(lo_half, heads[2 * j]["inv"], heads[2 * j + 1]["inv"])
                d_os_t = (d_o * inv_tile).astype(MM).T
                qs_t = qs[j * WIN:(j + 1) * WIN].T
                zero_t = jnp.zeros_like(qs_t)
                dqs = None
                for e in range(2):
                    h = 2 * j + e
                    hk, hd = h // GRP, heads[h]
                    sel_t = chan_lo if e == 0 else jnp.logical_not(chan_lo)
                    d_p = lax.dot_general(d_ob, vm[hk, e], NT, preferred_element_type=F32)
                    delta = jnp.dot(dlt, pick[e], preferred_element_type=F32)
                    dsk_ref[h:h + 1, :] -= jnp.sum(hd["esink"] * hd["inv"] * delta, axis=0, keepdims=True)
                    g = (d_p - jnp.concatenate([delta, delta], axis=1)) * jnp.concatenate([hd["inv"], hd["inv"]], axis=1)
                    d_s = (hd["p"] * g).astype(MM)
                    t = jnp.dot(d_s, km[hk, e], preferred_element_type=F32)
                    dqs = t if dqs is None else dqs + t
                    dk_h = jnp.dot(jnp.where(sel_t, qs_t, zero_t), d_s, preferred_element_type=F32)
                    dv_h = jnp.dot(jnp.where(sel_t, d_os_t, zero_t), hd["pb"], preferred_element_type=F32)
                    key = (hk, e == hk % 2)
                    dk_acc[key] = dk_h if key not in dk_acc else dk_acc[key] + dk_h
                    dv_acc[key] = dv_h if key not in dv_acc else dv_acc[key] + dv_h
                dqs_tiles.append(dqs)

            dqs_ts = jnp.concatenate(dqs_tiles, axis=0)
            q_ts, rq = a["q_ts"], a["rq"]
            gq = dqs_ts * wq2
            d_q = rq * gq - q_ts * (rq * rq * rq) * (_half_sums(gq * q_ts, b2) * (1.0 / HD))
            dwq_ref[...] += SCALE * jnp.sum(dqs_ts * q_ts * rq, axis=0, keepdims=True)
            for j in range(NQT):
                carry_ref[:, C_Q + j * LANE:C_Q + (j + 1) * LANE] = d_q[j * WIN:(j + 1) * WIN]

            def swap_halves(xt):
                return jnp.concatenate([xt[HD:], xt[:HD]], axis=0)

            dkn_tiles, dv_tiles = [], []
            for t in range(NKT):
                for acc, out in ((dk_acc, dkn_tiles), (dv_acc, dv_tiles)):
                    parts = [acc[hk, True] + swap_halves(acc[hk, False]) for hk in (2 * t, 2 * t + 1)]
                    out.append((parts[0] + parts[1]).T)
            dkn_ts = jnp.concatenate(dkn_tiles, axis=0)
            dv_ts = jnp.concatenate(dv_tiles, axis=0)
            k_ts, rk = a["k_ts"], a["rk"]
            gk = dkn_ts * wk2
            d_k = rk * gk - k_ts * (rk * rk * rk) * (_half_sums(gk * k_ts, b2) * (1.0 / HD))
            dwk_ref[...] += jnp.sum(dkn_ts * k_ts * rk, axis=0, keepdims=True)
            for t in range(NKT):
                for base, val in ((C_K, d_k), (C_V, dv_ts)):
                    cols = slice(base + t * LANE, base + (t + 1) * LANE)
                    r0 = t * 2 * WIN
                    dp_ref[:, cols] = (carry_ref[:, cols] + val[r0:r0 + WIN]).astype(MM)
                    carry_ref[:, cols] = val[r0 + WIN:r0 + 2 * WIN]

            u = pm_ref[:, C_U:C_U + DG]
            vs = pm_ref[:, C_VS:C_VS + DG]
            gb = pm_ref[:, C_GB:C_GB + DG]
            zu, dzu = _gelu_and_grad(u)
            zv, dzv = _gelu_and_grad(vs)
            zvb = zv.astype(MM)
            mixed = jnp.concatenate(
                [_sgu_mix(ws_ref, zvb[:, j * LANE:(j + 1) * LANE], lo_half, j) for j in range(NG // 2)], axis=1)
            mixed = mixed + be_ref[...]
            sgb = _sigmoid(gb)
            d_sgu = dm_ref[:, DA:DA + DG]
            carry_ref[:, C_GB:DIN] = d_sgu * zu * mixed * (sgb * (1.0 + gb * (1.0 - sgb)))
            d_mixed = d_sgu * zu * (gb * sgb)
            carry_ref[:, C_U:C_VS] = d_sgu * mixed * (gb * sgb) * dzu
            dbacc_ref[...] += d_mixed
            dmb = d_mixed.astype(MM)
            dzv_tiles = []
            for j in range(NG // 2):
                dt = dmb[:, j * LANE:(j + 1) * LANE]
                zt = zvb[:, j * LANE:(j + 1) * LANE]
                zero = jnp.zeros_like(dt)
                dzv_tiles.append(_sgu_mix(wst_ref, dt, lo_half, j))
                dws_ref[2 * j] += lax.dot_general(jnp.where(lo_half, dt, zero), zt, NT, preferred_element_type=F32)
                dws_ref[2 * j + 1] += lax.dot_general(jnp.where(lo_half, zero, dt), zt, NT, preferred_element_type=F32)
            carry_ref[:, C_VS:C_GB] = jnp.concatenate(dzv_tiles, axis=1) * dzv

        @pl.when(n == nb)
        def _():
            dp_ref[...] = carry_ref[...].astype(MM)
            lo_half = lax.broadcasted_iota(jnp.int32, (8, LANE), 1) < HD
            ones = [jnp.where(lo_half, 1.0, 0.0).astype(MM), jnp.where(lo_half, 0.0, 1.0).astype(MM)]
            hi, lo = _split(dbacc_ref[...])
            for h in range(NG):
                sl = slice((h // 2) * LANE, (h // 2 + 1) * LANE)
                r = (lax.dot_general(ones[h % 2], hi[:, sl], NT, preferred_element_type=F32)
                     + lax.dot_general(ones[h % 2], lo[:, sl], NT, preferred_element_type=F32))
                dbs_ref[h:h + 1, :] = r[0:1, :]
            row = lax.broadcasted_iota(jnp.int32, (WIN, WIN), 0)
            cl = lax.broadcasted_iota(jnp.int32, (WIN, WIN), 1)
            for h in range(NG):
                dws_ref[h] = jnp.where(row >= cl, dws_ref[h], 0.0)

    last = nb - 1
    return pl.pallas_call(
        body,
        grid_spec=pltpu.PrefetchScalarGridSpec(
            num_scalar_prefetch=0,
            grid=(nb + 1,),
            in_specs=[
                pl.BlockSpec(memory_space=pltpu.SMEM),
                pl.BlockSpec((WIN, DIN), lambda n: (jnp.minimum(n, last), 0)),
                pl.BlockSpec((WIN, 2 * DKV), lambda n: (jnp.maximum(jnp.minimum(n, last) - 1, 0), C_K // (2 * DKV))),
                pl.BlockSpec((WIN, D), lambda n: (jnp.minimum(n, last), 0)),
                pl.BlockSpec((1, NQ, WIN, 2 * WIN), lambda n: (jnp.minimum(n, 1), 0, 0, 0)),
                pl.BlockSpec((1, LANE), lambda n: (0, 0)),
                pl.BlockSpec((1, LANE), lambda n: (0, 0)),
                pl.BlockSpec((LANE, LANE), lambda n: (0, 0)),
                pl.BlockSpec((NG, WIN, WIN), lambda n: (0, 0, 0)),
                pl.BlockSpec((NG, WIN, WIN), lambda n: (0, 0, 0)),
                pl.BlockSpec((WIN, DG), lambda n: (0, 0)),
            ],
            out_specs=[
                pl.BlockSpec((WIN, DIN), lambda n: (jnp.maximum(n - 1, 0), 0)),
                pl.BlockSpec((1, LANE), lambda n: (0, 0)),
                pl.BlockSpec((1, LANE), lambda n: (0, 0)),
                pl.BlockSpec((NQ, WIN), lambda n: (0, 0)),
                pl.BlockSpec((NG, WIN, WIN), lambda n: (0, 0, 0)),
                pl.BlockSpec((NG, WIN), lambda n: (0, 0)),
            ],
            scratch_shapes=[pltpu.VMEM((WIN, DIN), F32), pltpu.VMEM((WIN, DG), F32)],
        ),
        out_shape=[SDS((T, DIN), MM), SDS((1, LANE), F32), SDS((1, LANE), F32), SDS((NQ, WIN), F32),
                   SDS((NG, WIN, WIN), F32), SDS((NG, WIN), F32)],
        name="bwd_mix",
        compiler_params=_cp(("arbitrary",)),
    )(sinks, proj, proj, dmix, bias, wq2, wk2, b2, ws_tril, ws_tril_t, b_exp)


WEIGHT_RESIDENT_ROWS = 256
GRAD_TOKEN_TILE = 1024


def _row_tile(T):
    return min(512, T)


def _fwd_in_call(x, g_row, w_sh):
    T = x.shape[0]
    tm = min(WEIGHT_RESIDENT_ROWS, T)

    def body(x_ref, g_ref, w_hbm, proj_ref, h_ref, w_vmem, sem):
        @pl.when(pl.program_id(0) == 0)
        def _():
            cp = pltpu.make_async_copy(w_hbm, w_vmem, sem)
            cp.start()
            cp.wait()

        xv = x_ref[...]
        r = lax.rsqrt(jnp.mean(xv * xv, axis=-1, keepdims=True) + EPS)
        h = (xv * r * g_ref[...]).astype(MM)
        h_ref[...] = h
        for j in range(NCHIP):
            proj_ref[:, j * SHW:(j + 1) * SHW] = jnp.dot(h, w_vmem[j], preferred_element_type=F32)

    return pl.pallas_call(
        body,
        grid=(T // tm,),
        in_specs=[pl.BlockSpec((tm, D), lambda i: (i, 0)),
                  pl.BlockSpec((1, D), lambda i: (0, 0)),
                  pl.BlockSpec(memory_space=pl.ANY)],
        out_specs=[pl.BlockSpec((tm, DIN), lambda i: (i, 0)),
                   pl.BlockSpec((tm, D), lambda i: (i, 0))],
        out_shape=[SDS((T, DIN), F32), SDS((T, D), MM)],
        scratch_shapes=[pltpu.VMEM((NCHIP, D, SHW), MM), pltpu.SemaphoreType.DMA],
        name="fwd_in",
        compiler_params=_cp(("arbitrary",)),
    )(x, g_row, w_sh)


def _fwd_out_call(x, mix, w_out):
    T = x.shape[0]
    tm = _row_tile(T)

    def body(x_ref, mix_ref, w_ref, y_ref):
        y_ref[...] = x_ref[...] + jnp.dot(mix_ref[...], w_ref[...], preferred_element_type=F32)

    return pl.pallas_call(
        body,
        grid=(T // tm,),
        in_specs=[pl.BlockSpec((tm, D), lambda i: (i, 0)),
                  pl.BlockSpec((tm, D), lambda i: (i, 0)),
                  pl.BlockSpec((D, D), lambda i: (0, 0))],
        out_specs=pl.BlockSpec((tm, D), lambda i: (i, 0)),
        out_shape=SDS((T, D), F32),
        name="fwd_out",
        compiler_params=_cp(("arbitrary",)),
    )(x, mix, w_out)


def _fwd_out_loss_call(x, mix, w_out, target):
    T = x.shape[0]
    tm = _row_tile(T)

    def body(x_ref, mix_ref, w_ref, t_ref, dy_ref, loss_ref):
        @pl.when(pl.program_id(0) == 0)
        def _():
            loss_ref[...] = jnp.zeros_like(loss_ref)

        e = x_ref[...] + jnp.dot(mix_ref[...], w_ref[...], preferred_element_type=F32) - t_ref[...]
        dy_ref[...] = e * (1.0 / D)
        loss_ref[...] += (0.5 / D) * jnp.sum(jnp.sum(e * e, axis=1, keepdims=True), axis=0, keepdims=True)

    return pl.pallas_call(
        body,
        grid=(T // tm,),
        in_specs=[pl.BlockSpec((tm, D), lambda i: (i, 0)),
                  pl.BlockSpec((tm, D), lambda i: (i, 0)),
                  pl.BlockSpec((D, D), lambda i: (0, 0)),
                  pl.BlockSpec((tm, D), lambda i: (i, 0))],
        out_specs=[pl.BlockSpec((tm, D), lambda i: (i, 0)),
                   pl.BlockSpec((1, 1), lambda i: (0, 0))],
        out_shape=[SDS((T, D), F32), SDS((1, 1), F32)],
        name="fwd_out_loss",
        compiler_params=_cp(("arbitrary",)),
    )(x, mix, w_out, target)


def _bwd_out_call(dy, w_out, token):
    T = dy.shape[0]
    tm = _row_tile(T)

    def body(dy_ref, w_ref, token_ref, o_ref):
        o_ref[...] = lax.dot_general(dy_ref[...].astype(MM), w_ref[...], NT, preferred_element_type=F32)

    return pl.pallas_call(
        body,
        grid=(T // tm,),
        in_specs=[pl.BlockSpec((tm, D), lambda i: (i, 0)),
                  pl.BlockSpec((D, D), lambda i: (0, 0)),
                  pl.BlockSpec(memory_space=pl.ANY)],
        out_specs=pl.BlockSpec((tm, D), lambda i: (i, 0)),
        out_shape=SDS((T, D), F32),
        name="bwd_out",
        compiler_params=_cp(("arbitrary",)),
    )(dy, w_out, token)


def _bwd_in_call(dproj, w_sh, x, dy, g_row, token):
    T = x.shape[0]
    tm = min(WEIGHT_RESIDENT_ROWS, T)

    def body(dp_ref, w_hbm, x_ref, dy_ref, g_ref, token_ref, dx_ref, dg_ref, w_vmem, sem):
        @pl.when(pl.program_id(0) == 0)
        def _():
            cp = pltpu.make_async_copy(w_hbm, w_vmem, sem)
            cp.start()
            dg_ref[...] = jnp.zeros_like(dg_ref)
            cp.wait()

        dh = lax.dot_general(dp_ref[:, 0:SHW], w_vmem[0], NT, preferred_element_type=F32)
        for j in range(1, NCHIP):
            dh = dh + lax.dot_general(dp_ref[:, j * SHW:(j + 1) * SHW], w_vmem[j], NT, preferred_element_type=F32)
        xv = x_ref[...]
        r = lax.rsqrt(jnp.mean(xv * xv, axis=-1, keepdims=True) + EPS)
        gd = dh * g_ref[...]
        dx_ref[...] = dy_ref[...] + r * gd - xv * ((r * r * r) * jnp.mean(gd * xv, axis=-1, keepdims=True))
        dg_ref[...] += jnp.sum(dh * xv * r, axis=0, keepdims=True)

    return pl.pallas_call(
        body,
        grid=(T // tm,),
        in_specs=[pl.BlockSpec((tm, DIN), lambda i: (i, 0)),
                  pl.BlockSpec(memory_space=pl.ANY),
                  pl.BlockSpec((tm, D), lambda i: (i, 0)),
                  pl.BlockSpec((tm, D), lambda i: (i, 0)),
                  pl.BlockSpec((1, D), lambda i: (0, 0)),
                  pl.BlockSpec(memory_space=pl.ANY)],
        out_specs=[pl.BlockSpec((tm, D), lambda i: (i, 0)),
                   pl.BlockSpec((1, D), lambda i: (0, 0))],
        out_shape=[SDS((T, D), F32), SDS((1, D), F32)],
        scratch_shapes=[pltpu.VMEM((NCHIP, D, SHW), MM), pltpu.SemaphoreType.DMA],
        name="bwd_in",
        compiler_params=_cp(("arbitrary",)),
    )(dproj, w_sh, x, dy, g_row, token)


def _grad_w_in_call(h, dproj, token):
    T = h.shape[0]
    tt = min(GRAD_TOKEN_TILE, T)
    nt = T // tt

    def body(h_ref, dp_ref, token_ref, o_ref, acc_ref):
        t = pl.program_id(1)

        @pl.when(t == 0)
        def _():
            acc_ref[...] = jnp.zeros_like(acc_ref)

        acc_ref[...] += lax.dot_general(h_ref[...], dp_ref[...], TN, preferred_element_type=F32)

        @pl.when(t == nt - 1)
        def _():
            o_ref[0] = acc_ref[...].astype(MM)

    return pl.pallas_call(
        body,
        grid=(NCHIP, nt),
        in_specs=[pl.BlockSpec((tt, D), lambda j, t: (t, 0)),
                  pl.BlockSpec((tt, SHW), lambda j, t: (t, j)),
                  pl.BlockSpec(memory_space=pl.ANY)],
        out_specs=pl.BlockSpec((1, D, SHW), lambda j, t: (j, 0, 0)),
        out_shape=SDS((NCHIP, D, SHW), MM),
        scratch_shapes=[pltpu.VMEM((D, SHW), F32)],
        name="grad_w_in",
        compiler_params=_cp(("arbitrary", "arbitrary")),
    )(h, dproj, token)


def _grad_w_out_call(mix, dy):
    T = mix.shape[0]
    tt = min(GRAD_TOKEN_TILE, T)
    nt = T // tt
    tn = 1024

    def body(m_ref, dy_ref, o_ref, acc_ref):
        t = pl.program_id(1)

        @pl.when(t == 0)
        def _():
            acc_ref[...] = jnp.zeros_like(acc_ref)

        acc_ref[...] += lax.dot_general(m_ref[...], dy_ref[...].astype(MM), TN, preferred_element_type=F32)

        @pl.when(t == nt - 1)
        def _():
            o_ref[...] = acc_ref[...].astype(MM)

    return pl.pallas_call(
        body,
        grid=(D // tn, nt),
        in_specs=[pl.BlockSpec((tt, D), lambda j, t: (t, 0)),
                  pl.BlockSpec((tt, tn), lambda j, t: (t, j))],
        out_specs=pl.BlockSpec((D, tn), lambda j, t: (0, j)),
        out_shape=SDS((D, D), MM),
        scratch_shapes=[pltpu.VMEM((D, tn), F32)],
        name="grad_w_out",
        compiler_params=_cp(("arbitrary", "arbitrary")),
    )(mix, dy)


def _cast_to_slab_call(w, chip_idx, name):
    L, R, C = w.shape
    tr = 256

    def body(chip_ref, *refs):
        for l in range(L):
            refs[L + l][...] = refs[l][...].astype(MM)

    return pl.pallas_call(
        body,
        grid_spec=pltpu.PrefetchScalarGridSpec(
            num_scalar_prefetch=1,
            grid=(R // tr,),
            in_specs=[pl.BlockSpec((1, tr, C), functools.partial(lambda i, chip_ref, l: (l, i, 0), l=l))
                      for l in range(L)],
            out_specs=[pl.BlockSpec((1, tr, C), lambda i, chip_ref: (chip_ref[0], i, 0))] * L,
        ),
        out_shape=[SDS((NCHIP, R, C), MM)] * L,
        name=name,
        compiler_params=_cp(("arbitrary",)),
    )(chip_idx, *([w] * L))


def _adam_call(w, g_parts, m, v, name):
    R, C = w.shape
    tr = R
    for cand in (512, 256, 128, 64, 32, 16, 8):
        if R % cand == 0 and cand * C * 4 <= 1024 * 1024:
            tr = cand
            break
    c1 = 1.0 - B1 ** STEP
    c2 = 1.0 - B2 ** STEP
    ng = len(g_parts)

    def body(*refs):
        w_ref, m_ref, v_ref = refs[0], refs[1 + ng], refs[2 + ng]
        g_ref, d_ref, nm_ref, nv_ref = refs[3 + ng:]
        gv = refs[1][...]
        for k in range(1, ng):
            gv = gv + refs[1 + k][...]
        nm = B1 * m_ref[...] + (1.0 - B1) * gv
        nv = B2 * v_ref[...] + (1.0 - B2) * (gv * gv)
        g_ref[...] = gv
        nm_ref[...] = nm
        nv_ref[...] = nv
        d_ref[...] = -LR * ((nm / c1) / (jnp.sqrt(nv / c2) + ADAM_EPS) + WD * w_ref[...])

    spec = pl.BlockSpec((tr, C), lambda i: (i, 0))
    return pl.pallas_call(
        body,
        grid=(R // tr,),
        in_specs=[spec] * (3 + ng),
        out_specs=[spec] * 4,
        out_shape=[SDS((R, C), F32)] * 4,
        name=name,
        compiler_params=_cp(("arbitrary",)),
    )(w, *g_parts, m, v)


MESH = pl.DeviceIdType.MESH
ANY = pl.BlockSpec(memory_space=pl.ANY)
HBM = pl.BlockSpec(memory_space=pltpu.HBM)
SEMS = pl.BlockSpec(memory_space=pltpu.SEMAPHORE)
EFFECT = pltpu.SideEffectType.DATAFLOW_SIDE_EFFECTING
NDEV = 8


def _hbm(a):
    return pltpu.with_memory_space_constraint(a, pltpu.HBM)


def _place():
    x, y, c = lax.axis_index("x"), lax.axis_index("y"), lax.axis_index("c")
    others = [(1 - x, y), (x, 1 - y), (1 - x, 1 - y)]
    return x, y, c, 2 * x + y, others


def _flipped(x, y, c, r):
    return (1 - x if r & 4 else x, 1 - y if r & 2 else y, 1 - c if r & 1 else c)


def _rcopy(src, dst, ssem, rsem, dev):
    return pltpu.make_async_remote_copy(src_ref=src, dst_ref=dst, send_sem=ssem, recv_sem=rsem,
                                        device_id=dev, device_id_type=MESH)


def _slab(ref, chip, c, halved):
    if not halved:
        return ref.at[chip]
    h = ref.shape[1] // 2
    return ref.at[chip, pl.ds(c * h, h), :]


def _gather_start_call(fulls, n_halved):
    K = len(fulls)

    def body(*refs):
        full, ssem, rsem = refs[:K], refs[K:2 * K], refs[2 * K:3 * K]
        x, y, c, me, others = _place()
        for k in range(K):
            for j, (px, py) in enumerate(others):
                part = _slab(full[k], me, c, k < n_halved)
                _rcopy(part, part, ssem[k].at[j], rsem[k].at[j], (px, py, c)).start()

    outs = pl.pallas_call(
        body,
        in_specs=[HBM] * K,
        out_specs=[SEMS] * (2 * K) + [HBM] * K,
        out_shape=[pltpu.SemaphoreType.DMA((3,))] * (2 * K) + [pltpu.HBM(f.shape, f.dtype) for f in fulls],
        input_output_aliases={k: 2 * K + k for k in range(K)},
        name="gather_start",
        compiler_params=pltpu.CompilerParams(has_side_effects=EFFECT),
    )(*[_hbm(f) for f in fulls])
    return list(outs[:K]), list(outs[K:2 * K]), list(outs[2 * K:])


def _gather_wait_call(fulls, ssems, rsems, after, halved, name):
    K = len(fulls)

    def body(*refs):
        full, ssem, rsem = refs[:K], refs[K:2 * K], refs[2 * K:3 * K]
        x, y, c, me, others = _place()
        for k in range(K):
            for j, (px, py) in enumerate(others):
                cp = _rcopy(_slab(full[k], me, c, halved), _slab(full[k], 2 * px + py, c, halved),
                            ssem[k].at[j], rsem[k].at[j], (px, py, c))
                cp.wait_send()
                cp.wait_recv()

    outs = pl.pallas_call(
        body,
        in_specs=[HBM] * K + [SEMS] * (2 * K) + [ANY],
        out_specs=[HBM] * K,
        out_shape=[pltpu.HBM(f.shape, f.dtype) for f in fulls],
        input_output_aliases={k: k for k in range(K)},
        name=name,
        compiler_params=pltpu.CompilerParams(has_side_effects=EFFECT),
    )(*fulls, *ssems, *rsems, after)
    return list(outs)


def _sibling_forward_call(fulls):
    K = len(fulls)

    def body(*refs):
        full = refs[:K]
        ssem, rsem = refs[2 * K:]
        x, y, c, me, others = _place()
        cps = []
        for k in range(K):
            for j, (px, py) in enumerate(others):
                mine = _slab(full[k], 2 * px + py, c, True)
                cps.append(_rcopy(mine, mine, ssem.at[3 * k + j], rsem.at[3 * k + j], (x, y, 1 - c)))
        for cp in cps:
            cp.start()
        for k in range(K):
            for j, (px, py) in enumerate(others):
                theirs = _slab(full[k], 2 * px + py, 1 - c, True)
                _rcopy(theirs, theirs, ssem.at[3 * k + j], rsem.at[3 * k + j], (x, y, 1 - c)).wait_recv()
        for cp in cps:
            cp.wait_send()

    outs = pl.pallas_call(
        body,
        in_specs=[ANY] * K,
        out_specs=[ANY] * K,
        out_shape=[SDS(f.shape, f.dtype) for f in fulls],
        input_output_aliases={k: k for k in range(K)},
        scratch_shapes=[pltpu.SemaphoreType.DMA((3 * K,)), pltpu.SemaphoreType.DMA((3 * K,))],
        name="gather_sibling_forward",
    )(*fulls)
    return list(outs)


def _grad_copies(srcs, lands, ssem, rsem):
    x, y, c, me, others = _place()
    cps, k = [], 0
    for src, land in zip(srcs, lands):
        if len(src.shape) == 3:
            for j, (px, py) in enumerate(others):
                cps.append(_rcopy(src.at[2 * px + py], land.at[j], ssem.at[k + j], rsem.at[k + j], (px, py, c)))
            k += 3
        else:
            for r in range(1, NDEV):
                cps.append(_rcopy(src, land.at[4 * x + 2 * y + c], ssem.at[k + r - 1], rsem.at[k + r - 1],
                                  _flipped(x, y, c, r)))
            k += NDEV - 1
    return cps


def _n_grad_copies(srcs):
    return sum(3 if len(s.shape) == 3 else NDEV - 1 for s in srcs)


def _grad_start_call(srcs, name):
    srcs = list(srcs)
    K = len(srcs)
    lands = [lax.empty(((3,) + s.shape[1:]) if len(s.shape) == 3 else ((NDEV,) + s.shape), s.dtype) for s in srcs]
    n = _n_grad_copies(srcs)

    def body(*refs):
        ssem, rsem, token = refs[2 * K], refs[2 * K + 1], refs[-1]
        for cp in _grad_copies(refs[:K], refs[K:2 * K], ssem, rsem):
            cp.start()
        token[...] = jnp.zeros_like(token)

    outs = pl.pallas_call(
        body,
        in_specs=[HBM] * (2 * K),
        out_specs=[SEMS, SEMS] + [HBM] * (2 * K) + [pl.BlockSpec(memory_space=pltpu.VMEM)],
        out_shape=[pltpu.SemaphoreType.DMA((n,)), pltpu.SemaphoreType.DMA((n,))]
        + [pltpu.HBM(a.shape, a.dtype) for a in srcs + lands] + [SDS((8, 128), F32)],
        input_output_aliases={k: 2 + k for k in range(2 * K)},
        name=name,
        compiler_params=pltpu.CompilerParams(has_side_effects=EFFECT),
    )(*[_hbm(a) for a in srcs + lands])
    return list(outs[2:2 + K]), list(outs[2 + K:2 + 2 * K]), outs[0], outs[1], outs[-1]


def _grad_wait_call(srcs, lands, ssem, rsem, after, name):
    K = len(srcs)

    def body(*refs):
        for cp in _grad_copies(refs[:K], refs[K:2 * K], refs[2 * K], refs[2 * K + 1]):
            cp.wait_send()
            cp.wait_recv()

    arrs = list(srcs) + list(lands)
    outs = pl.pallas_call(
        body,
        in_specs=[HBM] * (2 * K) + [SEMS, SEMS, ANY],
        out_specs=[HBM] * (2 * K),
        out_shape=[pltpu.HBM(a.shape, a.dtype) for a in arrs],
        input_output_aliases={k: k for k in range(2 * K)},
        name=name,
        compiler_params=pltpu.CompilerParams(has_side_effects=EFFECT),
    )(*arrs, ssem, rsem, after)
    return list(outs[:K]), list(outs[K:])


def _sibling_swap_call(arrs):
    K = len(arrs)

    def body(*refs):
        a_refs, t_refs = refs[:K], refs[K:2 * K]
        ssem, rsem = refs[2 * K:]
        x, y, c, me, others = _place()
        cps = [_rcopy(a_refs[k], t_refs[k], ssem.at[k], rsem.at[k], (x, y, 1 - c)) for k in range(K)]
        for cp in cps:
            cp.start()
        for cp in cps:
            cp.wait()

    return pl.pallas_call(
        body,
        in_specs=[ANY] * K,
        out_specs=[ANY] * K,
        out_shape=[SDS(a.shape, a.dtype) for a in arrs],
        scratch_shapes=[pltpu.SemaphoreType.DMA((K,)), pltpu.SemaphoreType.DMA((K,))],
        name="grad_sibling_swap",
    )(*arrs)


def _small_allreduce_call(a):
    R, C = a.shape

    def body(a_ref, o_ref, recv_ref, ssem, rsem):
        x, y, c, me, others = _place()
        dev = 4 * x + 2 * y + c
        recv_ref[pl.ds(dev, 1)] = a_ref[...][None]
        cps = [_rcopy(a_ref, recv_ref.at[dev], ssem.at[r - 1], rsem.at[r - 1], _flipped(x, y, c, r))
               for r in range(1, NDEV)]
        for cp in cps:
            cp.start()
        for cp in cps:
            cp.wait()
        acc = recv_ref[0]
        for s in range(1, NDEV):
            acc = acc + recv_ref[s]
        o_ref[...] = acc

    return pl.pallas_call(
        body,
        in_specs=[pl.BlockSpec(memory_space=pltpu.VMEM)],
        out_specs=pl.BlockSpec(memory_space=pltpu.VMEM),
        out_shape=SDS((R, C), F32),
        scratch_shapes=[pltpu.VMEM((NDEV, R, C), F32), pltpu.SemaphoreType.DMA((NDEV - 1,)),
                        pltpu.SemaphoreType.DMA((NDEV - 1,))],
        name="small_allreduce",
    )(a)


def _rows_tile(H, C):
    for cand in (512, 256, 128, 64, 32, 16, 8):
        if H % cand == 0 and cand * C * 4 <= 2 * 1024 * 1024:
            return cand
    raise ValueError((H, C))


def _sum_recv_call(own, recv, chip_idx, stack, l):
    _, R, C = own.shape
    tr = _rows_tile(R, C)

    def body(chip_ref, own_ref, r0, r1, r2, stack_ref, o_ref):
        o_ref[...] = ((own_ref[...].astype(F32) + r0[...].astype(F32)) + r1[...].astype(F32)) + r2[...].astype(F32)

    return pl.pallas_call(
        body,
        grid_spec=pltpu.PrefetchScalarGridSpec(
            num_scalar_prefetch=1,
            grid=(R // tr,),
            in_specs=[pl.BlockSpec((1, tr, C), lambda i, chip_ref: (chip_ref[0], i, 0))]
            + [pl.BlockSpec((1, tr, C), functools.partial(lambda i, chip_ref, s: (s, i, 0), s=s)) for s in range(3)]
            + [ANY],
            out_specs=pl.BlockSpec((1, tr, C), lambda i, chip_ref: (l, i, 0)),
        ),
        out_shape=SDS(stack.shape, F32),
        input_output_aliases={5: 0},
        name="grad_sum_recv",
        compiler_params=_cp(("arbitrary",)),
    )(chip_idx, own, recv, recv, recv, stack)


def _sum_small_call(own, recv, dev_idx):
    RS, C = own.shape
    tr = _rows_tile(RS, C)

    def body(dev_ref, own_ref, *refs):
        o_ref = refs[NDEV]
        dev = dev_ref[0]
        acc = jnp.where(dev == 0, own_ref[...], refs[0][0]).astype(F32)
        for s in range(1, NDEV):
            acc = acc + jnp.where(dev == s, own_ref[...], refs[s][0]).astype(F32)
        o_ref[...] = acc

    return pl.pallas_call(
        body,
        grid_spec=pltpu.PrefetchScalarGridSpec(
            num_scalar_prefetch=1,
            grid=(RS // tr,),
            in_specs=[pl.BlockSpec((tr, C), lambda i, dev_ref: (i, 0))]
            + [pl.BlockSpec((1, tr, C), functools.partial(
                lambda i, dev_ref, s: (jnp.where(dev_ref[0] == s, (s + 1) % NDEV, s), i, 0), s=s)) for s in range(NDEV)],
            out_specs=pl.BlockSpec((tr, C), lambda i, dev_ref: (i, 0)),
        ),
        out_shape=SDS((RS, C), F32),
        name="grad_sum_small",
        compiler_params=_cp(("arbitrary",)),
    )(dev_idx, own, *([recv] * NDEV))


SMALL_ROWS_ALIGN = 128


def _pack_small(parts):
    flat = jnp.concatenate([p.reshape(-1) for p in parts])
    rows = -(-flat.shape[0] // (128 * SMALL_ROWS_ALIGN)) * SMALL_ROWS_ALIGN
    flat = jnp.pad(flat, (0, rows * 128 - flat.shape[0]))
    return flat.reshape(rows, 128)


def _unpack_small(packed, like):
    flat = packed.reshape(-1)
    out, off = [], 0
    for p in like:
        n = int(np.prod(p.shape))
        out.append(flat[off:off + n].reshape(p.shape))
        off += n
    return out


def kernel(x, norm_g, w_in, q_norm, k_norm, sinks, w_s, b_s, w_out, loss_target, m_norm_g, m_w_in, m_q_norm, m_k_norm, m_sinks, m_w_s, m_b_s, m_w_out, v_norm_g, v_w_in, v_q_norm, v_k_norm, v_sinks, v_w_s, v_b_s, v_w_out):
    L = norm_g.shape[0]
    xi, yi, ci = lax.axis_index("x"), lax.axis_index("y"), lax.axis_index("c")
    chip_idx = (2 * xi + yi).astype(jnp.int32).reshape(1)
    dev_idx = (4 * xi + 2 * yi + ci).astype(jnp.int32).reshape(1)
    bias = _alibi_bias()
    b2 = _half_sum_matrix()
    tri =jnp.tril(jnp.ones((WIN, WIN), F32))

    fin = _cast_to_slab_call(w_in, chip_idx, "cast_w_in")
    fout = _cast_to_slab_call(w_out, chip_idx, "cast_w_out")
    fulls = [a for l in range(L) for a in (fin[l], fout[l])]
    g_ssems, g_rsems, fulls = _gather_start_call(fulls, 2)

    saved = []
    xs = x[0]
    dy = loss = None
    for l in range(L):
        sl = slice(2 * l, 2 * l + 2)
        w_in_l, w_out_l = _gather_wait_call(fulls[sl], g_ssems[sl], g_rsems[sl], xs, l == 0, f"gather_wait_{l}")
        if l == 0:
            w_in_l, w_out_l = _sibling_forward_call([w_in_l, w_out_l])
        w_out_l = w_out_l.reshape(D, D)
        proj, h = _fwd_in_call(xs, norm_g[l:l + 1], w_in_l)
        ws_tril = (w_s[l] * tri).astype(MM)
        b_exp = jnp.repeat(b_s[l].T, HD, axis=1)
        wq2 = jnp.tile(q_norm[l:l + 1], (1, 2)) * SCALE
        wk2 = jnp.tile(k_norm[l:l + 1], (1, 2))
        mix = _fwd_mix_call(proj, bias, wq2, wk2, b2, sinks[l], ws_tril, b_exp)
        saved.append((xs, proj, h, mix, ws_tril, b_exp, w_in_l, w_out_l, wq2, wk2))
        if l < L - 1:
            xs = _fwd_out_call(xs, mix, w_out_l)
        else:
            dy, loss = _fwd_out_loss_call(xs, mix, w_out_l, loss_target[0])

    s_in = lax.empty((L, D, SHW), F32)
    s_out = lax.empty((L, SHR, D), F32)
    ws_sums, tiny_sums = [None] * L, [None] * L

    def finish(pending, after):
        nonlocal s_in, s_out
        l, exchanges = pending
        for tag, kinds, srcs, lands, ssem, rsem in exchanges:
            srcs, lands = _grad_wait_call(srcs, lands, ssem, rsem, after, f"grad_wait_{l}{tag}")
            for kind, src, land in zip(kinds, srcs, lands):
                if kind == "in":
                    s_in = _sum_recv_call(src, land, chip_idx, s_in, l)
                elif kind == "out":
                    s_out = _sum_recv_call(src, land, chip_idx, s_out, l)
                elif kind == "ws":
                    ws_sums[l] = _sum_small_call(src, land, dev_idx)
                else:
                    tiny_sums[l] = _sum_small_call(src, land, dev_idx)

    def start(l, tag, kinds, srcs):
        srcs, lands, ssem, rsem, token = _grad_start_call(srcs, f"grad_start_{l}{tag}")
        return (tag, kinds, srcs, lands, ssem, rsem), token

    pending = None
    d_norm_g = [None] * L
    for l in reversed(range(L)):
        xs, proj, h, mix, ws_tril, b_exp, w_in_l, w_out_l, wq2, wk2 = saved[l]
        g_w_out = _grad_w_out_call(mix, dy).reshape(NCHIP, SHR, D)
        exchanges, token = [], jnp.zeros((8, 128), F32)
        if l == 0:
            ex, token = start(l, "_out", ["out"], [g_w_out])
            exchanges.append(ex)
        dmix = _bwd_out_call(dy, w_out_l, token)
        ws_tril_t = jnp.swapaxes(ws_tril, 1, 2)
        dproj, dwq, dwk, dsk, dws, dbs = _bwd_mix_call(
            proj, dmix, bias, wq2, wk2, b2, sinks[l], ws_tril, ws_tril_t, b_exp)
        dwq, dwk = dwq[:, :HD] + dwq[:, HD:], dwk[:, :HD] + dwk[:, HD:]
        g_ws = dws.reshape(NG * WIN, WIN).astype(MM)
        g_tiny = _pack_small([dwq, dwk, dsk[:, 0], dbs]).astype(MM)
        token = jnp.zeros((8, 128), F32)
        if l == 0:
            ex, token = start(l, "_small", ["ws", "tiny"], [g_ws, g_tiny])
            exchanges.append(ex)
        g_w_in = _grad_w_in_call(h, dproj, token)
        if l == 0:
            ex, token = start(l, "", ["in"], [g_w_in])
        else:
            ex, token = start(l, "", ["in", "out", "ws", "tiny"], [g_w_in, g_w_out, g_ws, g_tiny])
        exchanges.append(ex)
        dy, d_norm_g[l] = _bwd_in_call(dproj, w_in_l, xs, dy, norm_g[l:l + 1], token)
        if pending is not None:
            finish(pending, dy)
        pending = (l, exchanges)
    finish(pending, dy)
    grad_x = dy
    g_norm_g = _small_allreduce_call(jnp.concatenate(d_norm_g, axis=0))

    t_in, t_out = _sibling_swap_call([s_in, s_out])
    g_w_in, d_in, nm_in, nv_in = _adam_call(
        w_in.reshape(L * D, SHW), [s_in.reshape(L * D, SHW), t_in.reshape(L * D, SHW)],
        m_w_in.reshape(L * D, SHW), v_w_in.reshape(L * D, SHW), "adam_w_in")
    g_w_out, d_out, nm_out, nv_out = _adam_call(
        w_out.reshape(L * SHR, D), [s_out.reshape(L * SHR, D), t_out.reshape(L * SHR, D)],
        m_w_out.reshape(L * SHR, D), v_w_out.reshape(L * SHR, D), "adam_w_out")

    def pack_layers(parts):
        return jnp.concatenate([_pack_small([p[l] for p in parts]) for l in range(L)], axis=0)

    ws_rows = (L * NG * WIN, WIN)
    ws_outs = _adam_call(w_s.reshape(ws_rows), [jnp.concatenate(ws_sums, axis=0)],
                         m_w_s.reshape(ws_rows), v_w_s.reshape(ws_rows), "adam_w_s")
    tiny_like = [q_norm, k_norm, sinks, b_s]
    tiny_outs = _adam_call(
        pack_layers(tiny_like), [jnp.concatenate(tiny_sums, axis=0)],
        pack_layers([m_q_norm, m_k_norm, m_sinks, m_b_s]),
        pack_layers([v_q_norm, v_k_norm, v_sinks, v_b_s]), "adam_tiny")
    norm_outs = _adam_call(norm_g, [g_norm_g], m_norm_g, v_norm_g, "adam_norm_g")

    def full(i, win, wout):
        tiny = tiny_outs[i]
        rows = tiny.shape[0] // L
        per_layer = [_unpack_small(tiny[l * rows:(l + 1) * rows], [p[l] for p in tiny_like]) for l in range(L)]
        qn, kn, sk, bs = [jnp.stack([per_layer[l][k] for l in range(L)]) for k in range(4)]
        return [norm_outs[i], win.reshape(w_in.shape), qn, kn, sk, ws_outs[i].reshape(w_s.shape), bs,
                wout.reshape(w_out.shape)]

    loss_all = lax.psum(loss[0, 0], ("x", "y", "c"))
    return (loss_all, grad_x[None], *full(0, g_w_in, g_w_out), *full(1, d_in, d_out),
            *full(2, nm_in, nm_out), *full(3, nv_in, nv_out))
```

```python
import functools
import math

import numpy as np
import jax
import jax.numpy as jnp
from jax import lax
from jax.experimental import pallas as pl
from jax.experimental.pallas import tpu as pltpu

F32 = jnp.float32
MM = jnp.bfloat16

D = 2048
HD = 64
DA = 1024
DKV = 256
DG = 1024
NQ, NKV, GRP, NG = 16, 4, 4, 16
WIN = 128
DIN = 5632
C_Q, C_K, C_V, C_GA, C_U, C_VS, C_GB = 0, 1024, 1280, 1536, 2560, 3584, 4608
NCHIP = 4
SHW = DIN // NCHIP
SHR = D // NCHIP
EPS = 1e-6
NEG = -1e30
SCALE = HD ** -0.5
INV_SQRT2 = 1.0 / math.sqrt(2.0)
INV_SQRT_2PI = 1.0 / math.sqrt(2.0 * math.pi)
LR, B1, B2, ADAM_EPS, WD, STEP = 0.001, 0.9, 0.999, 1e-08, 0.01, 10
VMEM_LIMIT = 56 * 1024 * 1024

SDS = jax.ShapeDtypeStruct
NT = (((1,), (1,)), ((), ()))
TN = (((0,), (0,)), ((), ()))


def _cp(sem=None):
    return pltpu.CompilerParams(dimension_semantics=sem, vmem_limit_bytes=VMEM_LIMIT)


def _sigmoid(x):
    return 1.0 / (1.0 + jnp.exp(-x))


def _gelu(x):
    return 0.5 * x * (1.0 + lax.erf(x * INV_SQRT2))


def _gelu_and_grad(x):
    cdf = 0.5 * (1.0 + lax.erf(x * INV_SQRT2))
    return x * cdf, cdf + x * jnp.exp(-0.5 * x * x) * INV_SQRT_2PI


def _alibi_bias():
    slopes = 2.0 ** (-8.0 * np.arange(1, NQ + 1) / NQ)
    dist = (np.arange(WIN)[:, None] + WIN) - np.arange(2 * WIN)[None, :]
    ok = (dist >= 0) & (dist < WIN)
    first = ok & (np.arange(2 * WIN)[None, :] >= WIN)
    val = -slopes[:, None, None] * dist[None].astype(np.float64)
    return jnp.asarray(np.stack([np.where(first[None], val, NEG), np.where(ok[None], val, NEG)]), dtype=F32)


def _half_sum_matrix():
    half = np.arange(LANE) // HD
    return jnp.asarray(half[:, None] == half[None, :], dtype=MM)


LANE = 128
NQT = DA // LANE
NKT = DKV // LANE


def _tiles(ref, c0, n):
    return jnp.concatenate([ref[:, c0 + j * LANE:c0 + (j + 1) * LANE] for j in range(n)], axis=0)


def _split(x):
    hi = x.astype(MM)
    return hi, (x - hi.astype(F32)).astype(MM)


def _half_sums(x, b2):
    hi, lo = _split(x)
    return jnp.dot(hi, b2, preferred_element_type=F32) + jnp.dot(lo, b2, preferred_element_type=F32)


def _attn_fwd(pm_ref, kvp_ref, bias_ref, wq2, wk2, b2, sink_ref):
    lo_half = lax.broadcasted_iota(jnp.int32, (1, LANE), 1) < HD
    q_ts = _tiles(pm_ref, C_Q, NQT)
    rq = lax.rsqrt(_half_sums(q_ts * q_ts, b2) * (1.0 / HD) + EPS)
    qs = (q_ts * rq * wq2).astype(MM)
    k_ts = jnp.concatenate([a[:, c0 + t * LANE:c0 + (t + 1) * LANE] for t in range(NKT)
                            for a, c0 in ((kvp_ref, 0), (pm_ref, C_K))], axis=0)
    rk = lax.rsqrt(_half_sums(k_ts * k_ts, b2) * (1.0 / HD) + EPS)
    kn = (k_ts * rk * wk2).astype(MM)
    v_ts = jnp.concatenate([a[:, c0 + t * LANE:c0 + (t + 1) * LANE] for t in range(NKT)
                            for a, c0 in ((kvp_ref, DKV), (pm_ref, C_V))], axis=0).astype(MM)
    ones = jnp.ones((2 * WIN, LANE), MM)
    km, vm = {}, {}
    for hk in range(NKV):
        t, eh = hk // 2, hk % 2
        sel = lo_half if eh == 0 else jnp.logical_not(lo_half)
        rows = slice(t * 2 * WIN, (t + 1) * 2 * WIN)
        k_same = jnp.where(sel, kn[rows], jnp.zeros_like(kn[rows]))
        v_same = jnp.where(sel, v_ts[rows], jnp.zeros_like(v_ts[rows]))
        km[hk, eh], km[hk, 1 - eh] = k_same, pltpu.roll(k_same, HD, axis=1)
        vm[hk, eh], vm[hk, 1 - eh] = v_same, pltpu.roll(v_same, HD, axis=1)
    hs = range(NQ)
    s = [lax.dot_general(qs[(h // 2) * WIN:(h // 2 + 1) * WIN], km[h // GRP, h % 2], NT, preferred_element_type=F32)
         + bias_ref[0, h] for h in hs]
    m = [jnp.maximum(jnp.max(s[h], axis=-1, keepdims=True), sink_ref[h]) for h in hs]
    p = [jnp.exp(s[h] - m[h]) for h in hs]
    pb = [p[h].astype(MM) for h in hs]
    res = [jnp.dot(pb[h], jnp.concatenate([vm[h // GRP, h % 2], ones], axis=1), preferred_element_type=F32) for h in hs]
    esink = [jnp.exp(sink_ref[h] - m[h]) for h in hs]
    inv = [1.0 / (res[h][:, LANE:] + esink[h]) for h in hs]
    heads = [dict(p=p[h], pb=pb[h], inv=inv[h], esink=esink[h], o=res[h][:, :LANE] * inv[h]) for h in hs]
    return dict(lo_half=lo_half, q_ts=q_ts, rq=rq, qs=qs, k_ts=k_ts, rk=rk, km=km, vm=vm, heads=heads)


def _sgu_mix(w_ref, zt, lo_half, j):
    zero = jnp.zeros_like(zt)
    return (jnp.dot(w_ref[2 * j], jnp.where(lo_half, zt, zero), preferred_element_type=F32)
            + jnp.dot(w_ref[2 * j + 1], jnp.where(lo_half, zero, zt), preferred_element_type=F32))


def _fwd_mix_call(proj, bias, wq2, wk2, b2, sinks, ws_tril, b_exp):
    T = proj.shape[0]
    nb = T // WIN

    def body(sink_ref, pm_ref, kvp_ref, bias_ref, wq_ref, wk_ref, b2_ref, ws_ref, be_ref, mix_ref):
        lo_half = lax.broadcasted_iota(jnp.int32, (1, LANE), 1) < HD
        zu = _gelu(pm_ref[:, C_U:C_U + DG])
        zv = _gelu(pm_ref[:, C_VS:C_VS + DG]).astype(MM)
        mixed = jnp.concatenate(
            [_sgu_mix(ws_ref, zv[:, j * LANE:(j + 1) * LANE], lo_half, j) for j in range(NG // 2)], axis=1)
        mixed = mixed + be_ref[...]
        gb = pm_ref[:, C_GB:C_GB + DG]
        mix_ref[:, DA:DA + DG] = (zu * mixed * (gb * _sigmoid(gb))).astype(MM)
        a = _attn_fwd(pm_ref, kvp_ref, bias_ref, wq_ref[...], wk_ref[...], b2_ref[...], sink_ref)
        for j in range(NQT):
            cols = slice(j * LANE, (j + 1) * LANE)
            ga = pm_ref[:, C_GA + j * LANE:C_GA + (j + 1) * LANE]
            attn = a["heads"][2 * j]["o"] + a["heads"][2 * j + 1]["o"]
            mix_ref[:, cols] = (attn * (ga * _sigmoid(ga))).astype(MM)

    return pl.pallas_call(
        body,
        grid=(nb,),
        in_specs=[
            pl.BlockSpec(memory_space=pltpu.SMEM),
            pl.BlockSpec((WIN, DIN), lambda n: (n, 0)),
            pl.BlockSpec((WIN, 2 * DKV), lambda n: (jnp.maximum(n - 1, 0), C_K // (2 * DKV))),
            pl.BlockSpec((1, NQ, WIN, 2 * WIN), lambda n: (jnp.minimum(n, 1), 0, 0, 0)),
            pl.BlockSpec((1, LANE), lambda n: (0, 0)),
            pl.BlockSpec((1, LANE), lambda n: (0, 0)),
            pl.BlockSpec((LANE, LANE), lambda n: (0, 0)),
            pl.BlockSpec((NG, WIN, WIN), lambda n: (0, 0, 0)),
            pl.BlockSpec((WIN, DG), lambda n: (0, 0)),
        ],
        out_specs=pl.BlockSpec((WIN, D), lambda n: (n, 0)),
        out_shape=SDS((T, D), MM),
        name="fwd_mix",
        compiler_params=_cp(("arbitrary",)),
    )(sinks, proj, proj, bias, wq2, wk2, b2, ws_tril, b_exp)


def _bwd_mix_call(proj, dmix, bias, wq2, wk2, b2, sinks, ws_tril, ws_tril_t, b_exp):
    T = proj.shape[0]
    nb = T // WIN

    def body(sink_ref, pm_ref, kvp_ref, dm_ref, bias_ref, wq_ref, wk_ref, b2_ref, ws_ref, wst_ref, be_ref,
             dp_ref, dwq_ref, dwk_ref, dsk_ref, dws_ref, dbs_ref, carry_ref, dbacc_ref):
        n = pl.program_id(0)

        @pl.when(n == 0)
        def _():
            carry_ref[...] = jnp.zeros_like(carry_ref)
            dbacc_ref[...] = jnp.zeros_like(dbacc_ref)
            dwq_ref[...] = jnp.zeros_like(dwq_ref)
            dwk_ref[...] = jnp.zeros_like(dwk_ref)
            dsk_ref[...] = jnp.zeros_like(dsk_ref)
            dws_ref[...] = jnp.zeros_like(dws_ref)
            dbs_ref[...] = jnp.zeros_like(dbs_ref)

        @pl.when(n < nb)
        def _():
            dp_ref[:, C_Q:C_K] = carry_ref[:, C_Q:C_K].astype(MM)
            dp_ref[:, C_GA:DIN] = carry_ref[:, C_GA:DIN].astype(MM)

            lo_half = lax.broadcasted_iota(jnp.int32, (1, LANE), 1) < HD
            u = pm_ref[:, C_U:C_U + DG]
            vs = pm_ref[:, C_VS:C_VS + DG]
            gb = pm_ref[:, C_GB:C_GB + DG]
            zu, dzu = _gelu_and_grad(u)
            zv, dzv = _gelu_and_grad(vs)
            zvb = zv.astype(MM)
            mixed = jnp.concatenate(
                [_sgu_mix(ws_ref, zvb[:, j * LANE:(j + 1) * LANE], lo_half, j) for j in range(NG // 2)], axis=1)
            mixed = mixed + be_ref[...]
            sgb = _sigmoid(gb)
            d_sgu = dm_ref[:, DA:DA + DG]
            carry_ref[:, C_GB:DIN] = d_sgu * zu * mixed * (sgb * (1.0 + gb * (1.0 - sgb)))
            d_mixed = d_sgu * zu * (gb * sgb)
            carry_ref[:, C_U:C_VS] = d_sgu * mixed * (gb * sgb) * dzu
            dbacc_ref[...] += d_mixed
            dmb = d_mixed.astype(MM)
            dzv_tiles = [_sgu_mix(wst_ref, dmb[:, j * LANE:(j + 1) * LANE], lo_half, j) for j in range(NG // 2)]
            carry_ref[:, C_VS:C_GB] = jnp.concatenate(dzv_tiles, axis=1) * dzv
            for j in range(NG // 2):
                dt = dmb[:, j * LANE:(j + 1) * LANE]
                zt = zvb[:, j * LANE:(j + 1) * LANE]
                zero = jnp.zeros_like(dt)
                dws_ref[2 * j] += lax.dot_general(jnp.where(lo_half, dt, zero), zt, NT, preferred_element_type=F32)
                dws_ref[2 * j + 1] += lax.dot_general(jnp.where(lo_half, zero, dt), zt, NT, preferred_element_type=F32)

            wq2, wk2, b2 = wq_ref[...], wk_ref[...], b2_ref[...]
            a = _attn_fwd(pm_ref, kvp_ref, bias_ref, wq2, wk2, b2, sink_ref)
            heads, km, vm, qs = a["heads"], a["km"], a["vm"], a["qs"]

            row_lo = lax.broadcasted_iota(jnp.int32, (LANE, LANE), 0) < HD
            pick = [jnp.where(row_lo, 1.0, 0.0).astype(MM), jnp.where(row_lo, 0.0, 1.0).astype(MM)]
            chan_lo = lax.broadcasted_iota(jnp.int32, (LANE, 1), 0) < HD
            tiles, hs = range(NQT), range(NQ)
            sel_t = [chan_lo, jnp.logical_not(chan_lo)]
            d_o, attn = [], []
            for j in tiles:
                cols = slice(C_GA + j * LANE, C_GA + (j + 1) * LANE)
                ga = pm_ref[:, cols]
                sga = _sigmoid(ga)
                d_gated = dm_ref[:, j * LANE:(j + 1) * LANE]
                attn.append(heads[2 * j]["o"] + heads[2 * j + 1]["o"])
                carry_ref[:, cols] = d_gated * attn[j] * (sga * (1.0 + ga * (1.0 - sga)))
                d_o.append(d_gated * (ga * sga))
            d_ob = [d_o[j].astype(MM) for j in tiles]
            dlt = [(d_o[j] * attn[j]).astype(MM) for j in tiles]
            d_os_t = [(d_o[j] * jnp.where(lo_half, heads[2 * j]["inv"], heads[2 * j + 1]["inv"])).astype(MM).T
                      for j in tiles]
            qs_t = [qs[j * WIN:(j + 1) * WIN].T for j in tiles]
            zero_t = jnp.zeros_like(qs_t[0])
            dv_h = [jnp.dot(jnp.where(sel_t[h % 2], d_os_t[h // 2], zero_t), heads[h]["pb"], preferred_element_type=F32)
                    for h in hs]
            d_p = [lax.dot_general(d_ob[h // 2], vm[h // GRP, h % 2], NT, preferred_element_type=F32) for h in hs]
            delta = [jnp.dot(dlt[h // 2], pick[h % 2], preferred_element_type=F32) for h in hs]
            for h in hs:
                dsk_ref[h:h + 1, :] -= jnp.sum(heads[h]["esink"] * heads[h]["inv"] * delta[h], axis=0, keepdims=True)
            d_s = [(heads[h]["p"] * ((d_p[h] - jnp.concatenate([delta[h], delta[h]], axis=1))
                                     * jnp.concatenate([heads[h]["inv"], heads[h]["inv"]], axis=1))).astype(MM)
                   for h in hs]
            dqs_h = [jnp.dot(d_s[h], km[h // GRP, h % 2], preferred_element_type=F32) for h in hs]
            dqs_tiles = [dqs_h[2 * j] + dqs_h[2 * j + 1] for j in tiles]
            dk_h = [jnp.dot(jnp.where(sel_t[h % 2], qs_t[h // 2], zero_t), d_s[h], preferred_element_type=F32)
                    for h in hs]
            dk_acc, dv_acc = {}, {}
            for h in hs:
                key = (h // GRP, h % 2 == (h // GRP) % 2)
                dk_acc[key] = dk_h[h] if key not in dk_acc else dk_acc[key] + dk_h[h]
                dv_acc[key] = dv_h[h] if key not in dv_acc else dv_acc[key] + dv_h[h]

            dqs_ts = jnp.concatenate(dqs_tiles, axis=0)
            q_ts, rq = a["q_ts"], a["rq"]
            gq = dqs_ts * wq2
            d_q = rq * gq - q_ts * (rq * rq * rq) * (_half_sums(gq * q_ts, b2) * (1.0 / HD))
            dwq_ref[...] += SCALE * jnp.sum(dqs_ts * q_ts * rq, axis=0, keepdims=True)
            for j in range(NQT):
                carry_ref[:, C_Q + j * LANE:C_Q + (j + 1) * LANE] = d_q[j * WIN:(j + 1) * WIN]

            def swap_halves(xt):
                return jnp.concatenate([xt[HD:], xt[:HD]], axis=0)

            dkn_tiles, dv_tiles = [], []
            for t in range(NKT):
                for acc, out in ((dk_acc, dkn_tiles), (dv_acc, dv_tiles)):
                    parts = [acc[hk, True] + swap_halves(acc[hk, False]) for hk in (2 * t, 2 * t + 1)]
                    out.append((parts[0] + parts[1]).T)
            dkn_ts = jnp.concatenate(dkn_tiles, axis=0)
            dv_ts = jnp.concatenate(dv_tiles, axis=0)
            k_ts, rk = a["k_ts"], a["rk"]
            gk = dkn_ts * wk2
            d_k = rk * gk - k_ts * (rk * rk * rk) * (_half_sums(gk * k_ts, b2) * (1.0 / HD))
            dwk_ref[...] += jnp.sum(dkn_ts * k_ts * rk, axis=0, keepdims=True)
            for t in range(NKT):
                for base, val in ((C_K, d_k), (C_V, dv_ts)):
                    cols = slice(base + t * LANE, base + (t + 1) * LANE)
                    r0 = t * 2 * WIN
                    dp_ref[:, cols] = (carry_ref[:, cols] + val[r0:r0 + WIN]).astype(MM)
                    carry_ref[:, cols] = val[r0 + WIN:r0 + 2 * WIN]

        @pl.when(n == nb)
        def _():
            dp_ref[...] = carry_ref[...].astype(MM)
            lo_half = lax.broadcasted_iota(jnp.int32, (8, LANE), 1) < HD
            ones = [jnp.where(lo_half, 1.0, 0.0).astype(MM), jnp.where(lo_half, 0.0, 1.0).astype(MM)]
            hi, lo = _split(dbacc_ref[...])
            for h in range(NG):
                sl = slice((h // 2) * LANE, (h // 2 + 1) * LANE)
                r = (lax.dot_general(ones[h % 2], hi[:, sl], NT, preferred_element_type=F32)
                     + lax.dot_general(ones[h % 2], lo[:, sl], NT, preferred_element_type=F32))
                dbs_ref[h:h + 1, :] = r[0:1, :]
            row = lax.broadcasted_iota(jnp.int32, (WIN, WIN), 0)
            cl = lax.broadcasted_iota(jnp.int32, (WIN, WIN), 1)
            for h in range(NG):
                dws_ref[h] = jnp.where(row >= cl, dws_ref[h], 0.0)

    last = nb - 1
    return pl.pallas_call(
        body,
        grid_spec=pltpu.PrefetchScalarGridSpec(
            num_scalar_prefetch=0,
            grid=(nb + 1,),
            in_specs=[
                pl.BlockSpec(memory_space=pltpu.SMEM),
                pl.BlockSpec((WIN, DIN), lambda n: (jnp.minimum(n, last), 0)),
                pl.BlockSpec((WIN, 2 * DKV), lambda n: (jnp.maximum(jnp.minimum(n, last) - 1, 0), C_K // (2 * DKV))),
                pl.BlockSpec((WIN, D), lambda n: (jnp.minimum(n, last), 0)),
                pl.BlockSpec((1, NQ, WIN, 2 * WIN), lambda n: (jnp.minimum(n, 1), 0, 0, 0)),
                pl.BlockSpec((1, LANE), lambda n: (0, 0)),
                pl.BlockSpec((1, LANE), lambda n: (0, 0)),
                pl.BlockSpec((LANE, LANE), lambda n: (0, 0)),
                pl.BlockSpec((NG, WIN, WIN), lambda n: (0, 0, 0)),
                pl.BlockSpec((NG, WIN, WIN), lambda n: (0, 0, 0)),
                pl.BlockSpec((WIN, DG), lambda n: (0, 0)),
            ],
            out_specs=[
                pl.BlockSpec((WIN, DIN), lambda n: (jnp.maximum(n - 1, 0), 0)),
                pl.BlockSpec((1, LANE), lambda n: (0, 0)),
                pl.BlockSpec((1, LANE), lambda n: (0, 0)),
                pl.BlockSpec((NQ, WIN), lambda n: (0, 0)),
                pl.BlockSpec((NG, WIN, WIN), lambda n: (0, 0, 0)),
                pl.BlockSpec((NG, WIN), lambda n: (0, 0)),
            ],
            scratch_shapes=[pltpu.VMEM((WIN, DIN), F32), pltpu.VMEM((WIN, DG), F32)],
        ),
        out_shape=[SDS((T, DIN), MM), SDS((1, LANE), F32), SDS((1, LANE), F32), SDS((NQ, WIN), F32),
                   SDS((NG, WIN, WIN), F32), SDS((NG, WIN), F32)],
        name="bwd_mix",
        compiler_params=_cp(("arbitrary",)),
    )(sinks, proj, proj, dmix, bias, wq2, wk2, b2, ws_tril, ws_tril_t, b_exp)


WEIGHT_RESIDENT_ROWS = 256
GRAD_TOKEN_TILE = 1024


def _row_tile(T):
    return min(512, T)


def _fwd_in_call(x, g_row, w_sh):
    T = x.shape[0]
    tm = min(WEIGHT_RESIDENT_ROWS, T)

    def body(x_ref, g_ref, w_hbm, proj_ref, h_ref, w_vmem, sem):
        @pl.when(pl.program_id(0) == 0)
        def _():
            cp = pltpu.make_async_copy(w_hbm, w_vmem, sem)
            cp.start()
            cp.wait()

        xv = x_ref[...]
        r = lax.rsqrt(jnp.mean(xv * xv, axis=-1, keepdims=True) + EPS)
        h = (xv * r * g_ref[...]).astype(MM)
        h_ref[...] = h
        for j in range(NCHIP):
            proj_ref[:, j * SHW:(j + 1) * SHW] = jnp.dot(h, w_vmem[j], preferred_element_type=F32)

    return pl.pallas_call(
        body,
        grid=(T // tm,),
        in_specs=[pl.BlockSpec((tm, D), lambda i: (i, 0)),
                  pl.BlockSpec((1, D), lambda i: (0, 0)),
                  pl.BlockSpec(memory_space=pl.ANY)],
        out_specs=[pl.BlockSpec((tm, DIN), lambda i: (i, 0)),
                   pl.BlockSpec((tm, D), lambda i: (i, 0))],
        out_shape=[SDS((T, DIN), F32), SDS((T, D), MM)],
        scratch_shapes=[pltpu.VMEM((NCHIP, D, SHW), MM), pltpu.SemaphoreType.DMA],
        name="fwd_in",
        compiler_params=_cp(("arbitrary",)),
    )(x, g_row, w_sh)


def _fwd_out_call(x, mix, w_out):
    T = x.shape[0]
    tm = _row_tile(T)

    def body(x_ref, mix_ref, w_ref, y_ref):
        y_ref[...] = x_ref[...] + jnp.dot(mix_ref[...], w_ref[...], preferred_element_type=F32)

    return pl.pallas_call(
        body,
        grid=(T // tm,),
        in_specs=[pl.BlockSpec((tm, D), lambda i: (i, 0)),
                  pl.BlockSpec((tm, D), lambda i: (i, 0)),
                  pl.BlockSpec((D, D), lambda i: (0, 0))],
        out_specs=pl.BlockSpec((tm, D), lambda i: (i, 0)),
        out_shape=SDS((T, D), F32),
        name="fwd_out",
        compiler_params=_cp(("arbitrary",)),
    )(x, mix, w_out)


def _fwd_out_loss_call(x, mix, w_out, target):
    T = x.shape[0]
    tm = _row_tile(T)

    def body(x_ref, mix_ref, w_ref, t_ref, dy_ref, loss_ref):
        @pl.when(pl.program_id(0) == 0)
        def _():
            loss_ref[...] = jnp.zeros_like(loss_ref)

        e = x_ref[...] + jnp.dot(mix_ref[...], w_ref[...], preferred_element_type=F32) - t_ref[...]
        dy_ref[...] = e * (1.0 / D)
        loss_ref[...] += (0.5 / D) * jnp.sum(jnp.sum(e * e, axis=1, keepdims=True), axis=0, keepdims=True)

    return pl.pallas_call(
        body,
        grid=(T // tm,),
        in_specs=[pl.BlockSpec((tm, D), lambda i: (i, 0)),
                  pl.BlockSpec((tm, D), lambda i: (i, 0)),
                  pl.BlockSpec((D, D), lambda i: (0, 0)),
                  pl.BlockSpec((tm, D), lambda i: (i, 0))],
        out_specs=[pl.BlockSpec((tm, D), lambda i: (i, 0)),
                   pl.BlockSpec((1, 1), lambda i: (0, 0))],
        out_shape=[SDS((T, D), F32), SDS((1, 1), F32)],
        name="fwd_out_loss",
        compiler_params=_cp(("arbitrary",)),
    )(x, mix, w_out, target)


def _bwd_out_call(dy, w_out, token):
    T = dy.shape[0]
    tm = _row_tile(T)

    def body(dy_ref, w_ref, token_ref, o_ref):
        o_ref[...] = lax.dot_general(dy_ref[...].astype(MM), w_ref[...], NT, preferred_element_type=F32)

    return pl.pallas_call(
        body,
        grid=(T // tm,),
        in_specs=[pl.BlockSpec((tm, D), lambda i: (i, 0)),
                  pl.BlockSpec((D, D), lambda i: (0, 0)),
                  pl.BlockSpec(memory_space=pl.ANY)],
        out_specs=pl.BlockSpec((tm, D), lambda i: (i, 0)),
        out_shape=SDS((T, D), F32),
        name="bwd_out",
        compiler_params=_cp(("arbitrary",)),
    )(dy, w_out, token)


def _bwd_in_call(dproj, w_sh, x, dy, g_row, token):
    T = x.shape[0]
    tm = min(WEIGHT_RESIDENT_ROWS, T)

    def body(dp_ref, w_hbm, x_ref, dy_ref, g_ref, token_ref, dx_ref, dg_ref, w_vmem, sem):
        @pl.when(pl.program_id(0) == 0)
        def _():
            cp = pltpu.make_async_copy(w_hbm, w_vmem, sem)
            cp.start()
            dg_ref[...] = jnp.zeros_like(dg_ref)
            cp.wait()

        dh = lax.dot_general(dp_ref[:, 0:SHW], w_vmem[0], NT, preferred_element_type=F32)
        for j in range(1, NCHIP):
            dh = dh + lax.dot_general(dp_ref[:, j * SHW:(j + 1) * SHW], w_vmem[j], NT, preferred_element_type=F32)
        xv = x_ref[...]
        r = lax.rsqrt(jnp.mean(xv * xv, axis=-1, keepdims=True) + EPS)
        gd = dh * g_ref[...]
        dx_ref[...] = dy_ref[...] + r * gd - xv * ((r * r * r) * jnp.mean(gd * xv, axis=-1, keepdims=True))
        dg_ref[...] += jnp.sum(dh * xv * r, axis=0, keepdims=True)

    return pl.pallas_call(
        body,
        grid=(T // tm,),
        in_specs=[pl.BlockSpec((tm, DIN), lambda i: (i, 0)),
                  pl.BlockSpec(memory_space=pl.ANY),
                  pl.BlockSpec((tm, D), lambda i: (i, 0)),
                  pl.BlockSpec((tm, D), lambda i: (i, 0)),
                  pl.BlockSpec((1, D), lambda i: (0, 0)),
                  pl.BlockSpec(memory_space=pl.ANY)],
        out_specs=[pl.BlockSpec((tm, D), lambda i: (i, 0)),
                   pl.BlockSpec((1, D), lambda i: (0, 0))],
        out_shape=[SDS((T, D), F32), SDS((1, D), F32)],
        scratch_shapes=[pltpu.VMEM((NCHIP, D, SHW), MM), pltpu.SemaphoreType.DMA],
        name="bwd_in",
        compiler_params=_cp(("arbitrary",)),
    )(dproj, w_sh, x, dy, g_row, token)


def _grad_w_in_call(h, dproj, token):
    T = h.shape[0]
    tt = min(GRAD_TOKEN_TILE, T)
    nt = T // tt

    def body(h_ref, dp_ref, token_ref, o_ref, acc_ref):
        t = pl.program_id(1)

        @pl.when(t == 0)
        def _():
            acc_ref[...] = jnp.zeros_like(acc_ref)

        acc_ref[...] += lax.dot_general(h_ref[...], dp_ref[...], TN, preferred_element_type=F32)

        @pl.when(t == nt - 1)
        def _():
            o_ref[0] = acc_ref[...].astype(MM)

    return pl.pallas_call(
        body,
        grid=(NCHIP, nt),
        in_specs=[pl.BlockSpec((tt, D), lambda j, t: (t, 0)),
                  pl.BlockSpec((tt, SHW), lambda j, t: (t, j)),
                  pl.BlockSpec(memory_space=pl.ANY)],
        out_specs=pl.BlockSpec((1, D, SHW), lambda j, t: (j, 0, 0)),
        out_shape=SDS((NCHIP, D, SHW), MM),
        scratch_shapes=[pltpu.VMEM((D, SHW), F32)],
        name="grad_w_in",
        compiler_params=_cp(("arbitrary", "arbitrary")),
    )(h, dproj, token)


def _grad_w_out_call(mix, dy):
    T = mix.shape[0]
    tt = min(GRAD_TOKEN_TILE, T)
    nt = T // tt
    tn = 1024

    def body(m_ref, dy_ref, o_ref, acc_ref):
        t = pl.program_id(1)

        @pl.when(t == 0)
        def _():
            acc_ref[...] = jnp.zeros_like(acc_ref)

        acc_ref[...] += lax.dot_general(m_ref[...], dy_ref[...].astype(MM), TN, preferred_element_type=F32)

        @pl.when(t == nt - 1)
        def _():
            o_ref[...] = acc_ref[...].astype(MM)

    return pl.pallas_call(
        body,
        grid=(D // tn, nt),
        in_specs=[pl.BlockSpec((tt, D), lambda j, t: (t, 0)),
                  pl.BlockSpec((tt, tn), lambda j, t: (t, j))],
        out_specs=pl.BlockSpec((D, tn), lambda j, t: (0, j)),
        out_shape=SDS((D, D), MM),
        scratch_shapes=[pltpu.VMEM((D, tn), F32)],
        name="grad_w_out",
        compiler_params=_cp(("arbitrary", "arbitrary")),
    )(mix, dy)


def _cast_to_slab_call(w, chip_idx, name):
    L, R, C = w.shape
    tr = 256

    def body(chip_ref, *refs):
        for l in range(L):
            refs[L + l][...] = refs[l][...].astype(MM)

    return pl.pallas_call(
        body,
        grid_spec=pltpu.PrefetchScalarGridSpec(
            num_scalar_prefetch=1,
            grid=(R // tr,),
            in_specs=[pl.BlockSpec((1, tr, C), functools.partial(lambda i, chip_ref, l: (l, i, 0), l=l))
                      for l in range(L)],
            out_specs=[pl.BlockSpec((1, tr, C), lambda i, chip_ref: (chip_ref[0], i, 0))] * L,
        ),
        out_shape=[SDS((NCHIP, R, C), MM)] * L,
        name=name,
        compiler_params=_cp(("arbitrary",)),
    )(chip_idx, *([w] * L))


def _adam_call(w, g_parts, m, v, name):
    R, C = w.shape
    tr = R
    for cand in (512, 256, 128, 64, 32, 16, 8):
        if R % cand == 0 and cand * C * 4 <= 1024 * 1024:
            tr = cand
            break
    c1 = 1.0 - B1 ** STEP
    c2 = 1.0 - B2 ** STEP
    ng = len(g_parts)

    def body(*refs):
        w_ref, m_ref, v_ref = refs[0], refs[1 + ng], refs[2 + ng]
        g_ref, d_ref, nm_ref, nv_ref = refs[3 + ng:]
        gv = refs[1][...]
        for k in range(1, ng):
            gv = gv + refs[1 + k][...]
        nm = B1 * m_ref[...] + (1.0 - B1) * gv
        nv = B2 * v_ref[...] + (1.0 - B2) * (gv * gv)
        g_ref[...] = gv
        nm_ref[...] = nm
        nv_ref[...] = nv
        d_ref[...] = -LR * ((nm / c1) / (jnp.sqrt(nv / c2) + ADAM_EPS) + WD * w_ref[...])

    spec = pl.BlockSpec((tr, C), lambda i: (i, 0))
    return pl.pallas_call(
        body,
        grid=(R // tr,),
        in_specs=[spec] * (3 + ng),
        out_specs=[spec] * 4,
        out_shape=[SDS((R, C), F32)] * 4,
        name=name,
        compiler_params=_cp(("arbitrary",)),
    )(w, *g_parts, m, v)


MESH = pl.DeviceIdType.MESH
ANY = pl.BlockSpec(memory_space=pl.ANY)
HBM = pl.BlockSpec(memory_space=pltpu.HBM)
SEMS = pl.BlockSpec(memory_space=pltpu.SEMAPHORE)
EFFECT = pltpu.SideEffectType.DATAFLOW_SIDE_EFFECTING
NDEV = 8


def _hbm(a):
    return pltpu.with_memory_space_constraint(a, pltpu.HBM)


def _place():
    x, y, c = lax.axis_index("x"), lax.axis_index("y"), lax.axis_index("c")
    others = [(1 - x, y), (x, 1 - y), (1 - x, 1 - y)]
    return x, y, c, 2 * x + y, others


def _flipped(x, y, c, r):
    return (1 - x if r & 4 else x, 1 - y if r & 2 else y, 1 - c if r & 1 else c)


def _rcopy(src, dst, ssem, rsem, dev):
    return pltpu.make_async_remote_copy(src_ref=src, dst_ref=dst, send_sem=ssem, recv_sem=rsem,
                                        device_id=dev, device_id_type=MESH)


def _slab(ref, chip, c, halved):
    if not halved:
        return ref.at[chip]
    h = ref.shape[1] // 2
    return ref.at[chip, pl.ds(c * h, h), :]


def _gather_start_call(fulls, n_halved):
    K = len(fulls)

    def body(*refs):
        full, ssem, rsem = refs[:K], refs[K:2 * K], refs[2 * K:3 * K]
        x, y, c, me, others = _place()
        for k in range(K):
            for j, (px, py) in enumerate(others):
                part = _slab(full[k], me, c, k < n_halved)
                _rcopy(part, part, ssem[k].at[j], rsem[k].at[j], (px, py, c)).start()

    outs = pl.pallas_call(
        body,
        in_specs=[HBM] * K,
        out_specs=[SEMS] * (2 * K) + [HBM] * K,
        out_shape=[pltpu.SemaphoreType.DMA((3,))] * (2 * K) + [pltpu.HBM(f.shape, f.dtype) for f in fulls],
        input_output_aliases={k: 2 * K + k for k in range(K)},
        name="gather_start",
        compiler_params=pltpu.CompilerParams(has_side_effects=EFFECT),
    )(*[_hbm(f) for f in fulls])
    return list(outs[:K]), list(outs[K:2 * K]), list(outs[2 * K:])


def _gather_wait_call(fulls, ssems, rsems, after, halved, name):
    K = len(fulls)

    def body(*refs):
        full, ssem, rsem = refs[:K], refs[K:2 * K], refs[2 * K:3 * K]
        x, y, c, me, others = _place()
        for k in range(K):
            for j, (px, py) in enumerate(others):
                cp = _rcopy(_slab(full[k], me, c, halved), _slab(full[k], 2 * px + py, c, halved),
                            ssem[k].at[j], rsem[k].at[j], (px, py, c))
                cp.wait_send()
                cp.wait_recv()

    outs = pl.pallas_call(
        body,
        in_specs=[HBM] * K + [SEMS] * (2 * K) + [ANY],
        out_specs=[HBM] * K,
        out_shape=[pltpu.HBM(f.shape, f.dtype) for f in fulls],
        input_output_aliases={k: k for k in range(K)},
        name=name,
        compiler_params=pltpu.CompilerParams(has_side_effects=EFFECT),
    )(*fulls, *ssems, *rsems, after)
    return list(outs)


def _sibling_forward_call(fulls):
    K = len(fulls)

    def body(*refs):
        full = refs[:K]
        ssem, rsem = refs[2 * K:]
        x, y, c, me, others = _place()
        cps = []
        for k in range(K):
            for j, (px, py) in enumerate(others):
                mine = _slab(full[k], 2 * px + py, c, True)
                cps.append(_rcopy(mine, mine, ssem.at[3 * k + j], rsem.at[3 * k + j], (x, y, 1 - c)))
        for cp in cps:
            cp.start()
        for k in range(K):
            for j, (px, py) in enumerate(others):
                theirs = _slab(full[k], 2 * px + py, 1 - c, True)
                _rcopy(theirs, theirs, ssem.at[3 * k + j], rsem.at[3 * k + j], (x, y, 1 - c)).wait_recv()
        for cp in cps:
            cp.wait_send()

    outs = pl.pallas_call(
        body,
        in_specs=[ANY] * K,
        out_specs=[ANY] * K,
        out_shape=[SDS(f.shape, f.dtype) for f in fulls],
        input_output_aliases={k: k for k in range(K)},
        scratch_shapes=[pltpu.SemaphoreType.DMA((3 * K,)), pltpu.SemaphoreType.DMA((3 * K,))],
        name="gather_sibling_forward",
    )(*fulls)
    return list(outs)


def _grad_copies(srcs, lands, ssem, rsem):
    x, y, c, me, others = _place()
    cps, k = [], 0
    for src, land in zip(srcs, lands):
        if len(src.shape) == 3:
            for j, (px, py) in enumerate(others):
                cps.append(_rcopy(src.at[2 * px + py], land.at[j], ssem.at[k + j], rsem.at[k + j], (px, py, c)))
            k += 3
        else:
            for r in range(1, NDEV):
                cps.append(_rcopy(src, land.at[4 * x + 2 * y + c], ssem.at[k + r - 1], rsem.at[k + r - 1],
                                  _flipped(x, y, c, r)))
            k += NDEV - 1
    return cps


def _n_grad_copies(srcs):
    return sum(3 if len(s.shape) == 3 else NDEV - 1 for s in srcs)


def _grad_start_call(srcs, name):
    srcs = list(srcs)
    K = len(srcs)
    lands = [lax.empty(((3,) + s.shape[1:]) if len(s.shape) == 3 else ((NDEV,) + s.shape), s.dtype) for s in srcs]
    n = _n_grad_copies(srcs)

    def body(*refs):
        ssem, rsem, token = refs[2 * K], refs[2 * K + 1], refs[-1]
        for cp in _grad_copies(refs[:K], refs[K:2 * K], ssem, rsem):
            cp.start()
        token[...] = jnp.zeros_like(token)

    outs = pl.pallas_call(
        body,
        in_specs=[HBM] * (2 * K),
        out_specs=[SEMS, SEMS] + [HBM] * (2 * K) + [pl.BlockSpec(memory_space=pltpu.VMEM)],
        out_shape=[pltpu.SemaphoreType.DMA((n,)), pltpu.SemaphoreType.DMA((n,))]
        + [pltpu.HBM(a.shape, a.dtype) for a in srcs + lands] + [SDS((8, 128), F32)],
        input_output_aliases={k: 2 + k for k in range(2 * K)},
        name=name,
        compiler_params=pltpu.CompilerParams(has_side_effects=EFFECT),
    )(*[_hbm(a) for a in srcs + lands])
    return list(outs[2:2 + K]), list(outs[2 + K:2 + 2 * K]), outs[0], outs[1], outs[-1]


def _grad_wait_call(srcs, lands, ssem, rsem, after, name):
    K = len(srcs)

    def body(*refs):
        for cp in _grad_copies(refs[:K], refs[K:2 * K], refs[2 * K], refs[2 * K + 1]):
            cp.wait_send()
            cp.wait_recv()

    arrs = list(srcs) + list(lands)
    outs = pl.pallas_call(
        body,
        in_specs=[HBM] * (2 * K) + [SEMS, SEMS, ANY],
        out_specs=[HBM] * (2 * K),
        out_shape=[pltpu.HBM(a.shape, a.dtype) for a in arrs],
        input_output_aliases={k: k for k in range(2 * K)},
        name=name,
        compiler_params=pltpu.CompilerParams(has_side_effects=EFFECT),
    )(*arrs, ssem, rsem, after)
    return list(outs[:K]), list(outs[K:])


def _sibling_swap_call(arrs):
    K = len(arrs)

    def body(*refs):
        a_refs, t_refs = refs[:K], refs[K:2 * K]
        ssem, rsem = refs[2 * K:]
        x, y, c, me, others = _place()
        cps = [_rcopy(a_refs[k], t_refs[k], ssem.at[k], rsem.at[k], (x, y, 1 - c)) for k in range(K)]
        for cp in cps:
            cp.start()
        for cp in cps:
            cp.wait()

    return pl.pallas_call(
        body,
        in_specs=[ANY] * K,
        out_specs=[ANY] * K,
        out_shape=[SDS(a.shape, a.dtype) for a in arrs],
        scratch_shapes=[pltpu.SemaphoreType.DMA((K,)), pltpu.SemaphoreType.DMA((K,))],
        name="grad_sibling_swap",
    )(*arrs)


def _small_allreduce_call(a):
    R, C = a.shape

    def body(a_ref, o_ref, recv_ref, ssem, rsem):
        x, y, c, me, others = _place()
        dev = 4 * x + 2 * y + c
        recv_ref[pl.ds(dev, 1)] = a_ref[...][None]
        cps = [_rcopy(a_ref, recv_ref.at[dev], ssem.at[r - 1], rsem.at[r - 1], _flipped(x, y, c, r))
               for r in range(1, NDEV)]
        for cp in cps:
            cp.start()
        for cp in cps:
            cp.wait()
        acc = recv_ref[0]
        for s in range(1, NDEV):
            acc = acc + recv_ref[s]
        o_ref[...] = acc

    return pl.pallas_call(
        body,
        in_specs=[pl.BlockSpec(memory_space=pltpu.VMEM)],
        out_specs=pl.BlockSpec(memory_space=pltpu.VMEM),
        out_shape=SDS((R, C), F32),
        scratch_shapes=[pltpu.VMEM((NDEV, R, C), F32), pltpu.SemaphoreType.DMA((NDEV - 1,)),
                        pltpu.SemaphoreType.DMA((NDEV - 1,))],
        name="small_allreduce",
    )(a)


def _rows_tile(H, C):
    for cand in (512, 256, 128, 64, 32, 16, 8):
        if H % cand == 0 and cand * C * 4 <= 2 * 1024 * 1024:
            return cand
    raise ValueError((H, C))


def _sum_recv_call(own, recv, chip_idx, stack, l):
    _, R, C = own.shape
    tr = _rows_tile(R, C)

    def body(chip_ref, own_ref, r0, r1, r2, stack_ref, o_ref):
        o_ref[...] = ((own_ref[...].astype(F32) + r0[...].astype(F32)) + r1[...].astype(F32)) + r2[...].astype(F32)

    return pl.pallas_call(
        body,
        grid_spec=pltpu.PrefetchScalarGridSpec(
            num_scalar_prefetch=1,
            grid=(R // tr,),
            in_specs=[pl.BlockSpec((1, tr, C), lambda i, chip_ref: (chip_ref[0], i, 0))]
            + [pl.BlockSpec((1, tr, C), functools.partial(lambda i, chip_ref, s: (s, i, 0), s=s)) for s in range(3)]
            + [ANY],
            out_specs=pl.BlockSpec((1, tr, C), lambda i, chip_ref: (l, i, 0)),
        ),
        out_shape=SDS(stack.shape, F32),
        input_output_aliases={5: 0},
        name="grad_sum_recv",
        compiler_params=_cp(("arbitrary",)),
    )(chip_idx, own, recv, recv, recv, stack)


def _sum_small_call(own, recv, dev_idx):
    RS, C = own.shape
    tr = _rows_tile(RS, C)

    def body(dev_ref, own_ref, *refs):
        o_ref = refs[NDEV]
        dev = dev_ref[0]
        acc = jnp.where(dev == 0, own_ref[...], refs[0][0]).astype(F32)
        for s in range(1, NDEV):
            acc = acc + jnp.where(dev == s, own_ref[...], refs[s][0]).astype(F32)
        o_ref[...] = acc

    return pl.pallas_call(
        body,
        grid_spec=pltpu.PrefetchScalarGridSpec(
            num_scalar_prefetch=1,
            grid=(RS // tr,),
            in_specs=[pl.BlockSpec((tr, C), lambda i, dev_ref: (i, 0))]
            + [pl.BlockSpec((1, tr, C), functools.partial(
                lambda i, dev_ref, s: (jnp.where(dev_ref[0] == s, (s + 1) % NDEV, s), i, 0), s=s)) for s in range(NDEV)],
            out_specs=pl.BlockSpec((tr, C), lambda i, dev_ref: (i, 0)),
        ),
        out_shape=SDS((RS, C), F32),
        name="grad_sum_small",
        compiler_params=_cp(("arbitrary",)),
    )(dev_idx, own, *([recv] * NDEV))


SMALL_ROWS_ALIGN = 128


def _pack_small(parts):
    flat = jnp.concatenate([p.reshape(-1) for p in parts])
    rows = -(-flat.shape[0] // (128 * SMALL_ROWS_ALIGN)) * SMALL_ROWS_ALIGN
    flat = jnp.pad(flat, (0, rows * 128 - flat.shape[0]))
    return flat.reshape(rows, 128)


def _unpack_small(packed, like):
    flat = packed.reshape(-1)
    out, off = [], 0
    for p in like:
        n = int(np.prod(p.shape))
        out.append(flat[off:off + n].reshape(p.shape))
        off += n
    return out


def kernel(x, norm_g, w_in, q_norm, k_norm, sinks, w_s, b_s, w_out, loss_target, m_norm_g, m_w_in, m_q_norm, m_k_norm, m_sinks, m_w_s, m_b_s, m_w_out, v_norm_g, v_w_in, v_q_norm, v_k_norm, v_sinks, v_w_s, v_b_s, v_w_out):
    L = norm_g.shape[0]
    xi, yi, ci = lax.axis_index("x"), lax.axis_index("y"), lax.axis_index("c")
    chip_idx = (2 * xi + yi).astype(jnp.int32).reshape(1)
    dev_idx = (4 * xi + 2 * yi + ci).astype(jnp.int32).reshape(1)
    bias = _alibi_bias()
    b2 = _half_sum_matrix()
    tri =jnp.tril(jnp.ones((WIN, WIN), F32))

    fin = _cast_to_slab_call(w_in, chip_idx, "cast_w_in")
    fout = _cast_to_slab_call(w_out, chip_idx, "cast_w_out")
    fulls = [a for l in range(L) for a in (fin[l], fout[l])]
    g_ssems, g_rsems, fulls = _gather_start_call(fulls, 2)

    saved = []
    xs = x[0]
    dy = loss = None
    for l in range(L):
        sl = slice(2 * l, 2 * l + 2)
        w_in_l, w_out_l = _gather_wait_call(fulls[sl], g_ssems[sl], g_rsems[sl], xs, l == 0, f"gather_wait_{l}")
        if l == 0:
            w_in_l, w_out_l = _sibling_forward_call([w_in_l, w_out_l])
        w_out_l = w_out_l.reshape(D, D)
        proj, h = _fwd_in_call(xs, norm_g[l:l + 1], w_in_l)
        ws_tril = (w_s[l] * tri).astype(MM)
        b_exp = jnp.repeat(b_s[l].T, HD, axis=1)
        wq2 = jnp.tile(q_norm[l:l + 1], (1, 2)) * SCALE
        wk2 = jnp.tile(k_norm[l:l + 1], (1, 2))
        mix = _fwd_mix_call(proj, bias, wq2, wk2, b2, sinks[l], ws_tril, b_exp)
        saved.append((xs, proj, h, mix, ws_tril, b_exp, w_in_l, w_out_l, wq2, wk2))
        if l < L - 1:
            xs = _fwd_out_call(xs, mix, w_out_l)
        else:
            dy, loss = _fwd_out_loss_call(xs, mix, w_out_l, loss_target[0])

    s_in = lax.empty((L, D, SHW), F32)
    s_out = lax.empty((L, SHR, D), F32)
    ws_sums, tiny_sums = [None] * L, [None] * L

    def finish(pending, after):
        nonlocal s_in, s_out
        l, exchanges = pending
        for tag, kinds, srcs, lands, ssem, rsem in exchanges:
            srcs, lands = _grad_wait_call(srcs, lands, ssem, rsem, after, f"grad_wait_{l}{tag}")
            for kind, src, land in zip(kinds, srcs, lands):
                if kind == "in":
                    s_in = _sum_recv_call(src, land, chip_idx, s_in, l)
                elif kind == "out":
                    s_out = _sum_recv_call(src, land, chip_idx, s_out, l)
                elif kind == "ws":
                    ws_sums[l] = _sum_small_call(src, land, dev_idx)
                else:
                    tiny_sums[l] = _sum_small_call(src, land, dev_idx)

    def start(l, tag, kinds, srcs):
        srcs, lands, ssem, rsem, token = _grad_start_call(srcs, f"grad_start_{l}{tag}")
        return (tag, kinds, srcs, lands, ssem, rsem), token

    pending = None
    d_norm_g = [None] * L
    for l in reversed(range(L)):
        xs, proj, h, mix, ws_tril, b_exp, w_in_l, w_out_l, wq2, wk2 = saved[l]
        g_w_out = _grad_w_out_call(mix, dy).reshape(NCHIP, SHR, D)
        exchanges, token = [], jnp.zeros((8, 128), F32)
        if l == 0:
            ex, token = start(l, "_out", ["out"], [g_w_out])
            exchanges.append(ex)
        dmix = _bwd_out_call(dy, w_out_l, token)
        ws_tril_t = jnp.swapaxes(ws_tril, 1, 2)
        dproj, dwq, dwk, dsk, dws, dbs = _bwd_mix_call(
            proj, dmix, bias, wq2, wk2, b2, sinks[l], ws_tril, ws_tril_t, b_exp)
        dwq, dwk = dwq[:, :HD] + dwq[:, HD:], dwk[:, :HD] + dwk[:, HD:]
        g_ws = dws.reshape(NG * WIN, WIN).astype(MM)
        g_tiny = _pack_small([dwq, dwk, dsk[:, 0], dbs]).astype(MM)
        token = jnp.zeros((8, 128), F32)
        if l == 0:
            ex, token = start(l, "_small", ["ws", "tiny"], [g_ws, g_tiny])
            exchanges.append(ex)
        g_w_in = _grad_w_in_call(h, dproj, token)
        if l == 0:
            ex, token = start(l, "", ["in"], [g_w_in])
        else:
            ex, token = start(l, "", ["in", "out", "ws", "tiny"], [g_w_in, g_w_out, g_ws, g_tiny])
        exchanges.append(ex)
        dy, d_norm_g[l] = _bwd_in_call(dproj, w_in_l, xs, dy, norm_g[l:l + 1], token)
        if pending is not None:
            finish(pending, dy)
        pending = (l, exchanges)
    finish(pending, dy)
    grad_x = dy
    g_norm_g = _small_allreduce_call(jnp.concatenate(d_norm_g, axis=0))

    t_in, t_out = _sibling_swap_call([s_in, s_out])
    g_w_in, d_in, nm_in, nv_in = _adam_call(
        w_in.reshape(L * D, SHW), [s_in.reshape(L * D, SHW), t_in.reshape(L * D, SHW)],
        m_w_in.reshape(L * D, SHW), v_w_in.reshape(L * D, SHW), "adam_w_in")
    g_w_out, d_out, nm_out, nv_out = _adam_call(
        w_out.reshape(L * SHR, D), [s_out.reshape(L * SHR, D), t_out.reshape(L * SHR, D)],
        m_w_out.reshape(L * SHR, D), v_w_out.reshape(L * SHR, D), "adam_w_out")

    def pack_layers(parts):
        return jnp.concatenate([_pack_small([p[l] for p in parts]) for l in range(L)], axis=0)

    ws_rows = (L * NG * WIN, WIN)
    ws_outs = _adam_call(w_s.reshape(ws_rows), [jnp.concatenate(ws_sums, axis=0)],
                         m_w_s.reshape(ws_rows), v_w_s.reshape(ws_rows), "adam_w_s")
    tiny_like = [q_norm, k_norm, sinks, b_s]
    tiny_outs = _adam_call(
        pack_layers(tiny_like), [jnp.concatenate(tiny_sums, axis=0)],
        pack_layers([m_q_norm, m_k_norm, m_sinks, m_b_s]),
        pack_layers([v_q_norm, v_k_norm, v_sinks, v_b_s]), "adam_tiny")
    norm_outs = _adam_call(norm_g, [g_norm_g], m_norm_g, v_norm_g, "adam_norm_g")

    def full(i, win, wout):
        tiny = tiny_outs[i]
        rows = tiny.shape[0] // L
        per_layer = [_unpack_small(tiny[l * rows:(l + 1) * rows], [p[l] for p in tiny_like]) for l in range(L)]
        qn, kn, sk, bs = [jnp.stack([per_layer[l][k] for l in range(L)]) for k in range(4)]
        return [norm_outs[i], win.reshape(w_in.shape), qn, kn, sk, ws_outs[i].reshape(w_s.shape), bs,
                wout.reshape(w_out.shape)]

    loss_all = lax.psum(loss[0, 0], ("x", "y", "c"))
    return (loss_all, grad_x[None], *full(0, g_w_in, g_w_out), *full(1, d_in, d_out),
            *full(2, nm_in, nm_out), *full(3, nv_in, nv_out))
```

```python
import functools
import math

import numpy as np
import jax
import jax.numpy as jnp
from jax import lax
from jax.experimental import pallas as pl
from jax.experimental.pallas import tpu as pltpu

F32 = jnp.float32
MM = jnp.bfloat16

D = 2048
HD = 64
DA = 1024
DKV = 256
DG = 1024
NQ, NKV, GRP, NG = 16, 4, 4, 16
WIN = 128
DIN = 5632
C_Q, C_K, C_V, C_GA, C_U, C_VS, C_GB = 0, 1024, 1280, 1536, 2560, 3584, 4608
NCHIP = 4
SHW = DIN // NCHIP
SHR = D // NCHIP
EPS = 1e-6
NEG = -1e30
SCALE = HD ** -0.5
INV_SQRT2 = 1.0 / math.sqrt(2.0)
INV_SQRT_2PI = 1.0 / math.sqrt(2.0 * math.pi)
LR, B1, B2, ADAM_EPS, WD, STEP = 0.001, 0.9, 0.999, 1e-08, 0.01, 10
VMEM_LIMIT = 56 * 1024 * 1024

SDS = jax.ShapeDtypeStruct
NT = (((1,), (1,)), ((), ()))
TN = (((0,), (0,)), ((), ()))


def _cp(sem=None):
    return pltpu.CompilerParams(dimension_semantics=sem, vmem_limit_bytes=VMEM_LIMIT)


def _sigmoid(x):
    return 1.0 / (1.0 + jnp.exp(-x))


def _gelu(x):
    return 0.5 * x * (1.0 + lax.erf(x * INV_SQRT2))


def _gelu_and_grad(x):
    cdf = 0.5 * (1.0 + lax.erf(x * INV_SQRT2))
    return x * cdf, cdf + x * jnp.exp(-0.5 * x * x) * INV_SQRT_2PI


def _alibi_bias():
    slopes = 2.0 ** (-8.0 * np.arange(1, NQ + 1) / NQ)
    dist = (np.arange(WIN)[:, None] + WIN) - np.arange(2 * WIN)[None, :]
    ok = (dist >= 0) & (dist < WIN)
    first = ok & (np.arange(2 * WIN)[None, :] >= WIN)
    val = -slopes[:, None, None] * dist[None].astype(np.float64)
    return jnp.asarray(np.stack([np.where(first[None], val, NEG), np.where(ok[None], val, NEG)]), dtype=F32)


def _half_sum_matrix():
    half = np.arange(LANE) // HD
    return jnp.asarray(half[:, None] == half[None, :], dtype=MM)


LANE = 128
NQT = DA // LANE
NKT = DKV // LANE


def _tiles(ref, c0, n):
    return jnp.concatenate([ref[:, c0 + j * LANE:c0 + (j + 1) * LANE] for j in range(n)], axis=0)


def _split(x):
    hi = x.astype(MM)
    return hi, (x - hi.astype(F32)).astype(MM)


def _half_sums(x, b2):
    hi, lo = _split(x)
    return jnp.dot(hi, b2, preferred_element_type=F32) + jnp.dot(lo, b2, preferred_element_type=F32)


def _attn_fwd(pm_ref, kvp_ref, bias_ref, wq2, wk2, b2, sink_ref):
    lo_half = lax.broadcasted_iota(jnp.int32, (1, LANE), 1) < HD
    q_ts = _tiles(pm_ref, C_Q, NQT)
    rq = lax.rsqrt(_half_sums(q_ts * q_ts, b2) * (1.0 / HD) + EPS)
    qs = (q_ts * rq * wq2).astype(MM)
    k_ts = jnp.concatenate([a[:, c0 + t * LANE:c0 + (t + 1) * LANE] for t in range(NKT)
                            for a, c0 in ((kvp_ref, 0), (pm_ref, C_K))], axis=0)
    rk = lax.rsqrt(_half_sums(k_ts * k_ts, b2) * (1.0 / HD) + EPS)
    kn = (k_ts * rk * wk2).astype(MM)
    v_ts = jnp.concatenate([a[:, c0 + t * LANE:c0 + (t + 1) * LANE] for t in range(NKT)
                            for a, c0 in ((kvp_ref, DKV), (pm_ref, C_V))], axis=0).astype(MM)
    ones = jnp.ones((2 * WIN, LANE), MM)
    km, vm = {}, {}
    for hk in range(NKV):
        t, eh = hk // 2, hk % 2
        sel = lo_half if eh == 0 else jnp.logical_not(lo_half)
        rows = slice(t * 2 * WIN, (t + 1) * 2 * WIN)
        k_same = jnp.where(sel, kn[rows], jnp.zeros_like(kn[rows]))
        v_same = jnp.where(sel, v_ts[rows], jnp.zeros_like(v_ts[rows]))
        km[hk, eh], km[hk, 1 - eh] = k_same, pltpu.roll(k_same, HD, axis=1)
        vm[hk, eh], vm[hk, 1 - eh] = v_same, pltpu.roll(v_same, HD, axis=1)
    hs = range(NQ)
    s = [lax.dot_general(qs[(h // 2) * WIN:(h // 2 + 1) * WIN], km[h // GRP, h % 2], NT, preferred_element_type=F32)
         + bias_ref[0, h] for h in hs]
    m = [jnp.maximum(jnp.max(s[h], axis=-1, keepdims=True), sink_ref[h]) for h in hs]
    p = [jnp.exp(s[h] - m[h]) for h in hs]
    pb = [p[h].astype(MM) for h in hs]
    res = [jnp.dot(pb[h], jnp.concatenate([vm[h // GRP, h % 2], ones], axis=1), preferred_element_type=F32) for h in hs]
    esink = [jnp.exp(sink_ref[h] - m[h]) for h in hs]
    inv = [1.0 / (res[h][:, LANE:] + esink[h]) for h in hs]
    heads = [dict(p=p[h], pb=pb[h], inv=inv[h], esink=esink[h], o=res[h][:, :LANE] * inv[h]) for h in hs]
    return dict(lo_half=lo_half, q_ts=q_ts, rq=rq, qs=qs, k_ts=k_ts, rk=rk, km=km, vm=vm, heads=heads)


def _sgu_mix(w_ref, zt, lo_half, j):
    zero = jnp.zeros_like(zt)
    return (jnp.dot(w_ref[2 * j], jnp.where(lo_half, zt, zero), preferred_element_type=F32)
            + jnp.dot(w_ref[2 * j + 1], jnp.where(lo_half, zero, zt), preferred_element_type=F32))


def _fwd_mix_call(proj, bias, wq2, wk2, b2, sinks, ws_tril, b_exp):
    T = proj.shape[0]
    nb = T // WIN

    def body(sink_ref, pm_ref, kvp_ref, bias_ref, wq_ref, wk_ref, b2_ref, ws_ref, be_ref, mix_ref):
        lo_half = lax.broadcasted_iota(jnp.int32, (1, LANE), 1) < HD
        zu = _gelu(pm_ref[:, C_U:C_U + DG])
        zv = _gelu(pm_ref[:, C_VS:C_VS + DG]).astype(MM)
        mixed = jnp.concatenate(
            [_sgu_mix(ws_ref, zv[:, j * LANE:(j + 1) * LANE], lo_half, j) for j in range(NG // 2)], axis=1)
        mixed = mixed + be_ref[...]
        gb = pm_ref[:, C_GB:C_GB + DG]
        mix_ref[:, DA:DA + DG] = (zu * mixed * (gb * _sigmoid(gb))).astype(MM)
        a = _attn_fwd(pm_ref, kvp_ref, bias_ref, wq_ref[...], wk_ref[...], b2_ref[...], sink_ref)
        for j in range(NQT):
            cols = slice(j * LANE, (j + 1) * LANE)
            ga = pm_ref[:, C_GA + j * LANE:C_GA + (j + 1) * LANE]
            attn = a["heads"][2 * j]["o"] + a["heads"][2 * j + 1]["o"]
            mix_ref[:, cols] = (attn * (ga * _sigmoid(ga))).astype(MM)

    return pl.pallas_call(
        body,
        grid=(nb,),
        in_specs=[
            pl.BlockSpec(memory_space=pltpu.SMEM),
            pl.BlockSpec((WIN, DIN), lambda n: (n, 0)),
            pl.BlockSpec((WIN, 2 * DKV), lambda n: (jnp.maximum(n - 1, 0), C_K // (2 * DKV))),
            pl.BlockSpec((1, NQ, WIN, 2 * WIN), lambda n: (jnp.minimum(n, 1), 0, 0, 0)),
            pl.BlockSpec((1, LANE), lambda n: (0, 0)),
            pl.BlockSpec((1, LANE), lambda n: (0, 0)),
            pl.BlockSpec((LANE, LANE), lambda n: (0, 0)),
            pl.BlockSpec((NG, WIN, WIN), lambda n: (0, 0, 0)),
            pl.BlockSpec((WIN, DG), lambda n: (0, 0)),
        ],
        out_specs=pl.BlockSpec((WIN, D), lambda n: (n, 0)),
        out_shape=SDS((T, D), MM),
        name="fwd_mix",
        compiler_params=_cp(("arbitrary",)),
    )(sinks, proj, proj, bias, wq2, wk2, b2, ws_tril, b_exp)


def _bwd_mix_call(proj, dmix, bias, wq2, wk2, b2, sinks, ws_tril, ws_tril_t, b_exp):
    T = proj.shape[0]
    nb = T // WIN

    def body(sink_ref, pm_ref, kvp_ref, dm_ref, bias_ref, wq_ref, wk_ref, b2_ref, ws_ref, wst_ref, be_ref,
             dp_ref, dwq_ref, dwk_ref, dsk_ref, dws_ref, dbs_ref, carry_ref, dbacc_ref):
        n = pl.program_id(0)

        @pl.when(n == 0)
        def _():
            carry_ref[...] = jnp.zeros_like(carry_ref)
            dbacc_ref[...] = jnp.zeros_like(dbacc_ref)
            dwq_ref[...] = jnp.zeros_like(dwq_ref)
            dwk_ref[...] = jnp.zeros_like(dwk_ref)
            dsk_ref[...] = jnp.zeros_like(dsk_ref)
            dws_ref[...] = jnp.zeros_like(dws_ref)
            dbs_ref[...] = jnp.zeros_like(dbs_ref)

        @pl.when(n < nb)
        def _():
            dp_ref[:, C_Q:C_K] = carry_ref[:, C_Q:C_K].astype(MM)
            dp_ref[:, C_GA:DIN] = carry_ref[:, C_GA:DIN].astype(MM)

            lo_half = lax.broadcasted_iota(jnp.int32, (1, LANE), 1) < HD
            u = pm_ref[:, C_U:C_U + DG]
            vs = pm_ref[:, C_VS:C_VS + DG]
            gb = pm_ref[:, C_GB:C_GB + DG]
            zu, dzu = _gelu_and_grad(u)
            zv, dzv = _gelu_and_grad(vs)
            zvb = zv.astype(MM)
            mixed = jnp.concatenate(
                [_sgu_mix(ws_ref, zvb[:, j * LANE:(j + 1) * LANE], lo_half, j) for j in range(NG // 2)], axis=1)
            mixed = mixed + be_ref[...]
            sgb = _sigmoid(gb)
            d_sgu = dm_ref[:, DA:DA + DG]
            carry_ref[:, C_GB:DIN] = d_sgu * zu * mixed * (sgb * (1.0 + gb * (1.0 - sgb)))
            d_mixed = d_sgu * zu * (gb * sgb)
            carry_ref[:, C_U:C_VS] = d_sgu * mixed * (gb * sgb) * dzu
            dbacc_ref[...] += d_mixed
            dmb = d_mixed.astype(MM)
            dzv_tiles = [_sgu_mix(wst_ref, dmb[:, j * LANE:(j + 1) * LANE], lo_half, j) for j in range(NG // 2)]
            carry_ref[:, C_VS:C_GB] = jnp.concatenate(dzv_tiles, axis=1) * dzv
            for j in range(NG // 2):
                dt = dmb[:, j * LANE:(j + 1) * LANE]
                zt = zvb[:, j * LANE:(j + 1) * LANE]
                zero = jnp.zeros_like(dt)
                dws_ref[2 * j] += lax.dot_general(jnp.where(lo_half, dt, zero), zt, NT, preferred_element_type=F32)
                dws_ref[2 * j + 1] += lax.dot_general(jnp.where(lo_half, zero, dt), zt, NT, preferred_element_type=F32)

            wq2, wk2, b2 = wq_ref[...], wk_ref[...], b2_ref[...]
            a = _attn_fwd(pm_ref, kvp_ref, bias_ref, wq2, wk2, b2, sink_ref)
            heads, km, vm, qs = a["heads"], a["km"], a["vm"], a["qs"]

            row_lo = lax.broadcasted_iota(jnp.int32, (LANE, LANE), 0) < HD
            pick = [jnp.where(row_lo, 1.0, 0.0).astype(MM), jnp.where(row_lo, 0.0, 1.0).astype(MM)]
            chan_lo = lax.broadcasted_iota(jnp.int32, (LANE, 1), 0) < HD
            tiles, hs = range(NQT), range(NQ)
            sel_t = [chan_lo, jnp.logical_not(chan_lo)]
            d_o, attn = [], []
            for j in tiles:
                cols = slice(C_GA + j * LANE, C_GA + (j + 1) * LANE)
                ga = pm_ref[:, cols]
                sga = _sigmoid(ga)
                d_gated = dm_ref[:, j * LANE:(j + 1) * LANE]
                attn.append(heads[2 * j]["o"] + heads[2 * j + 1]["o"])
                carry_ref[:, cols] = d_gated * attn[j] * (sga * (1.0 + ga * (1.0 - sga)))
                d_o.append(d_gated * (ga * sga))
            d_ob = [d_o[j].astype(MM) for j in tiles]
            dlt = [(d_o[j] * attn[j]).astype(MM) for j in tiles]
            d_os_t = [(d_o[j] * jnp.where(lo_half, heads[2 * j]["inv"], heads[2 * j + 1]["inv"])).astype(MM).T
                      for j in tiles]
            qs_t = [qs[j * WIN:(j + 1) * WIN].T for j in tiles]
            zero_t = jnp.zeros_like(qs_t[0])
            dv_h = [jnp.dot(jnp.where(sel_t[h % 2], d_os_t[h // 2], zero_t), heads[h]["pb"], preferred_element_type=F32)
                    for h in hs]
            d_p = [lax.dot_general(d_ob[h // 2], vm[h // GRP, h % 2], NT, preferred_element_type=F32) for h in hs]
            delta = [jnp.dot(dlt[h // 2], pick[h % 2], preferred_element_type=F32) for h in hs]
            for h in hs:
                dsk_ref[h:h + 1, :] -= jnp.sum(heads[h]["esink"] * heads[h]["inv"] * delta[h], axis=0, keepdims=True)
            d_s = [(heads[h]["p"] * ((d_p[h] - jnp.concatenate([delta[h], delta[h]], axis=1))
                                     * jnp.concatenate([heads[h]["inv"], heads[h]["inv"]], axis=1))).astype(MM)
                   for h in hs]
            dqs_h = [jnp.dot(d_s[h], km[h // GRP, h % 2], preferred_element_type=F32) for h in hs]
            dqs_tiles = [dqs_h[2 * j] + dqs_h[2 * j + 1] for j in tiles]
            dk_h = [jnp.dot(jnp.where(sel_t[h % 2], qs_t[h // 2], zero_t), d_s[h], preferred_element_type=F32)
                    for h in hs]
            dk_acc, dv_acc = {}, {}
            for h in hs:
                key = (h // GRP, h % 2 == (h // GRP) % 2)
                dk_acc[key] = dk_h[h] if key not in dk_acc else dk_acc[key] + dk_h[h]
                dv_acc[key] = dv_h[h] if key not in dv_acc else dv_acc[key] + dv_h[h]

            dqs_ts = jnp.concatenate(dqs_tiles, axis=0)
            q_ts, rq = a["q_ts"], a["rq"]
            gq = dqs_ts * wq2
            d_q = rq * gq - q_ts * (rq * rq * rq) * (_half_sums(gq * q_ts, b2) * (1.0 / HD))
            dwq_ref[...] += SCALE * jnp.sum(dqs_ts * q_ts * rq, axis=0, keepdims=True)
            for j in range(NQT):
                carry_ref[:, C_Q + j * LANE:C_Q + (j + 1) * LANE] = d_q[j * WIN:(j + 1) * WIN]

            def swap_halves(xt):
                return jnp.concatenate([xt[HD:], xt[:HD]], axis=0)

            dkn_tiles, dv_tiles = [], []
            for t in range(NKT):
                for acc, out in ((dk_acc, dkn_tiles), (dv_acc, dv_tiles)):
                    parts = [acc[hk, True] + swap_halves(acc[hk, False]) for hk in (2 * t, 2 * t + 1)]
                    out.append((parts[0] + parts[1]).T)
            dkn_ts = jnp.concatenate(dkn_tiles, axis=0)
            dv_ts = jnp.concatenate(dv_tiles, axis=0)
            k_ts, rk = a["k_ts"], a["rk"]
            gk = dkn_ts * wk2
            d_k = rk * gk - k_ts * (rk * rk * rk) * (_half_sums(gk * k_ts, b2) * (1.0 / HD))
            dwk_ref[...] += jnp.sum(dkn_ts * k_ts * rk, axis=0, keepdims=True)
            for t in range(NKT):
                for base, val in ((C_K, d_k), (C_V, dv_ts)):
                    cols = slice(base + t * LANE, base + (t + 1) * LANE)
                    r0 = t * 2 * WIN
                    dp_ref[:, cols] = (carry_ref[:, cols] + val[r0:r0 + WIN]).astype(MM)
                    carry_ref[:, cols] = val[r0 + WIN:r0 + 2 * WIN]

        @pl.when(n == nb)
        def _():
            dp_ref[...] = carry_ref[...].astype(MM)
            lo_half = lax.broadcasted_iota(jnp.int32, (8, LANE), 1) < HD
            ones = [jnp.where(lo_half, 1.0, 0.0).astype(MM), jnp.where(lo_half, 0.0, 1.0).astype(MM)]
            hi, lo = _split(dbacc_ref[...])
            for h in range(NG):
                sl = slice((h // 2) * LANE, (h // 2 + 1) * LANE)
                r = (lax.dot_general(ones[h % 2], hi[:, sl], NT, preferred_element_type=F32)
                     + lax.dot_general(ones[h % 2], lo[:, sl], NT, preferred_element_type=F32))
                dbs_ref[h:h + 1, :] = r[0:1, :]
            row = lax.broadcasted_iota(jnp.int32, (WIN, WIN), 0)
            cl = lax.broadcasted_iota(jnp.int32, (WIN, WIN), 1)
            for h in range(NG):
                dws_ref[h] = jnp.where(row >= cl, dws_ref[h], 0.0)

    last = nb - 1
    return pl.pallas_call(
        body,
        grid_spec=pltpu.PrefetchScalarGridSpec(
            num_scalar_prefetch=0,
            grid=(nb + 1,),
            in_specs=[
                pl.BlockSpec(memory_space=pltpu.SMEM),
                pl.BlockSpec((WIN, DIN), lambda n: (jnp.minimum(n, last), 0)),
                pl.BlockSpec((WIN, 2 * DKV), lambda n: (jnp.maximum(jnp.minimum(n, last) - 1, 0), C_K // (2 * DKV))),
                pl.BlockSpec((WIN, D), lambda n: (jnp.minimum(n, last), 0)),
                pl.BlockSpec((1, NQ, WIN, 2 * WIN), lambda n: (jnp.minimum(n, 1), 0, 0, 0)),
                pl.BlockSpec((1, LANE), lambda n: (0, 0)),
                pl.BlockSpec((1, LANE), lambda n: (0, 0)),
                pl.BlockSpec((LANE, LANE), lambda n: (0, 0)),
                pl.BlockSpec((NG, WIN, WIN), lambda n: (0, 0, 0)),
                pl.BlockSpec((NG, WIN, WIN), lambda n: (0, 0, 0)),
                pl.BlockSpec((WIN, DG), lambda n: (0, 0)),
            ],
            out_specs=[
                pl.BlockSpec((WIN, DIN), lambda n: (jnp.maximum(n - 1, 0), 0)),
                pl.BlockSpec((1, LANE), lambda n: (0, 0)),
                pl.BlockSpec((1, LANE), lambda n: (0, 0)),
                pl.BlockSpec((NQ, WIN), lambda n: (0, 0)),
                pl.BlockSpec((NG, WIN, WIN), lambda n: (0, 0, 0)),
                pl.BlockSpec((NG, WIN), lambda n: (0, 0)),
            ],
            scratch_shapes=[pltpu.VMEM((WIN, DIN), F32), pltpu.VMEM((WIN, DG), F32)],
        ),
        out_shape=[SDS((T, DIN), MM), SDS((1, LANE), F32), SDS((1, LANE), F32), SDS((NQ, WIN), F32),
                   SDS((NG, WIN, WIN), F32), SDS((NG, WIN), F32)],
        name="bwd_mix",
        compiler_params=_cp(("arbitrary",)),
    )(sinks, proj, proj, dmix, bias, wq2, wk2, b2, ws_tril, ws_tril_t, b_exp)


WEIGHT_RESIDENT_ROWS = 256
GRAD_TOKEN_TILE = 1024


def _row_tile(T):
    return min(512, T)


def _fwd_in_call(x, g_row, w_sh):
    T = x.shape[0]
    tm = min(WEIGHT_RESIDENT_ROWS, T)

    def body(x_ref, g_ref, w_hbm, proj_ref, h_ref, w_vmem, sem):
        @pl.when(pl.program_id(0) == 0)
        def _():
            cps = [pltpu.make_async_copy(w_hbm.at[j], w_vmem.at[:, pl.ds(j * SHW, SHW)], sem.at[j]) for j in range(NCHIP)]
            for cp in cps:
                cp.start()
            for cp in cps:
                cp.wait()

        xv = x_ref[...]
        r = lax.rsqrt(jnp.mean(xv * xv, axis=-1, keepdims=True) + EPS)
        h = (xv * r * g_ref[...]).astype(MM)
        h_ref[...] = h
        proj_ref[...] = jnp.dot(h, w_vmem[...], preferred_element_type=F32)

    return pl.pallas_call(
        body,
        grid=(T // tm,),
        in_specs=[pl.BlockSpec((tm, D), lambda i: (i, 0)),
                  pl.BlockSpec((1, D), lambda i: (0, 0)),
                  pl.BlockSpec(memory_space=pl.ANY)],
        out_specs=[pl.BlockSpec((tm, DIN), lambda i: (i, 0)),
                   pl.BlockSpec((tm, D), lambda i: (i, 0))],
        out_shape=[SDS((T, DIN), F32), SDS((T, D), MM)],
        scratch_shapes=[pltpu.VMEM((D, DIN), MM), pltpu.SemaphoreType.DMA((NCHIP,))],
        name="fwd_in",
        compiler_params=_cp(("arbitrary",)),
    )(x, g_row, w_sh)


def _fwd_out_call(x, mix, w_out):
    T = x.shape[0]
    tm = _row_tile(T)

    def body(x_ref, mix_ref, w_ref, y_ref):
        y_ref[...] = x_ref[...] + jnp.dot(mix_ref[...], w_ref[...], preferred_element_type=F32)

    return pl.pallas_call(
        body,
        grid=(T // tm,),
        in_specs=[pl.BlockSpec((tm, D), lambda i: (i, 0)),
                  pl.BlockSpec((tm, D), lambda i: (i, 0)),
                  pl.BlockSpec((D, D), lambda i: (0, 0))],
        out_specs=pl.BlockSpec((tm, D), lambda i: (i, 0)),
        out_shape=SDS((T, D), F32),
        name="fwd_out",
        compiler_params=_cp(("arbitrary",)),
    )(x, mix, w_out)


def _fwd_out_loss_call(x, mix, w_out, target):
    T = x.shape[0]
    tm = _row_tile(T)

    def body(x_ref, mix_ref, w_ref, t_ref, dy_ref, loss_ref):
        @pl.when(pl.program_id(0) == 0)
        def _():
            loss_ref[...] = jnp.zeros_like(loss_ref)

        e = x_ref[...] + jnp.dot(mix_ref[...], w_ref[...], preferred_element_type=F32) - t_ref[...]
        dy_ref[...] = e * (1.0 / D)
        loss_ref[...] += (0.5 / D) * jnp.sum(jnp.sum(e * e, axis=1, keepdims=True), axis=0, keepdims=True)

    return pl.pallas_call(
        body,
        grid=(T // tm,),
        in_specs=[pl.BlockSpec((tm, D), lambda i: (i, 0)),
                  pl.BlockSpec((tm, D), lambda i: (i, 0)),
                  pl.BlockSpec((D, D), lambda i: (0, 0)),
                  pl.BlockSpec((tm, D), lambda i: (i, 0))],
        out_specs=[pl.BlockSpec((tm, D), lambda i: (i, 0)),
                   pl.BlockSpec((1, 1), lambda i: (0, 0))],
        out_shape=[SDS((T, D), F32), SDS((1, 1), F32)],
        name="fwd_out_loss",
        compiler_params=_cp(("arbitrary",)),
    )(x, mix, w_out, target)


def _bwd_out_call(dy, w_out, token):
    T = dy.shape[0]
    tm = _row_tile(T)

    def body(dy_ref, w_ref, token_ref, o_ref):
        o_ref[...] = lax.dot_general(dy_ref[...].astype(MM), w_ref[...], NT, preferred_element_type=F32)

    return pl.pallas_call(
        body,
        grid=(T // tm,),
        in_specs=[pl.BlockSpec((tm, D), lambda i: (i, 0)),
                  pl.BlockSpec((D, D), lambda i: (0, 0)),
                  pl.BlockSpec(memory_space=pl.ANY)],
        out_specs=pl.BlockSpec((tm, D), lambda i: (i, 0)),
        out_shape=SDS((T, D), F32),
        name="bwd_out",
        compiler_params=_cp(("arbitrary",)),
    )(dy, w_out, token)


def _bwd_in_call(dproj, w_sh, x, dy, g_row, token):
    T = x.shape[0]
    tm = min(WEIGHT_RESIDENT_ROWS, T)

    def body(dp_ref, w_hbm, x_ref, dy_ref, g_ref, token_ref, dx_ref, dg_ref, w_vmem, sem):
        @pl.when(pl.program_id(0) == 0)
        def _():
            cps = [pltpu.make_async_copy(w_hbm.at[j], w_vmem.at[:, pl.ds(j * SHW, SHW)], sem.at[j]) for j in range(NCHIP)]
            for cp in cps:
                cp.start()
            dg_ref[...] = jnp.zeros_like(dg_ref)
            for cp in cps:
                cp.wait()

        dh = lax.dot_general(dp_ref[...], w_vmem[...], NT, preferred_element_type=F32)
        xv = x_ref[...]
        r = lax.rsqrt(jnp.mean(xv * xv, axis=-1, keepdims=True) + EPS)
        gd = dh * g_ref[...]
        dx_ref[...] = dy_ref[...] + r * gd - xv * ((r * r * r) * jnp.mean(gd * xv, axis=-1, keepdims=True))
        dg_ref[...] += jnp.sum(dh * xv * r, axis=0, keepdims=True)

    return pl.pallas_call(
        body,
        grid=(T // tm,),
        in_specs=[pl.BlockSpec((tm, DIN), lambda i: (i, 0)),
                  pl.BlockSpec(memory_space=pl.ANY),
                  pl.BlockSpec((tm, D), lambda i: (i, 0)),
                  pl.BlockSpec((tm, D), lambda i: (i, 0)),
                  pl.BlockSpec((1, D), lambda i: (0, 0)),
                  pl.BlockSpec(memory_space=pl.ANY)],
        out_specs=[pl.BlockSpec((tm, D), lambda i: (i, 0)),
                   pl.BlockSpec((1, D), lambda i: (0, 0))],
        out_shape=[SDS((T, D), F32), SDS((1, D), F32)],
        scratch_shapes=[pltpu.VMEM((D, DIN), MM), pltpu.SemaphoreType.DMA((NCHIP,))],
        name="bwd_in",
        compiler_params=_cp(("arbitrary",)),
    )(dproj, w_sh, x, dy, g_row, token)


def _grad_w_in_call(h, dproj, token):
    T = h.shape[0]
    tt = min(GRAD_TOKEN_TILE, T)
    nt = T // tt

    def body(h_ref, dp_ref, token_ref, o_ref, acc_ref):
        t = pl.program_id(1)

        @pl.when(t == 0)
        def _():
            acc_ref[...] = jnp.zeros_like(acc_ref)

        acc_ref[...] += lax.dot_general(h_ref[...], dp_ref[...], TN, preferred_element_type=F32)

        @pl.when(t == nt - 1)
        def _():
            o_ref[0] = acc_ref[...].astype(MM)

    return pl.pallas_call(
        body,
        grid=(NCHIP, nt),
        in_specs=[pl.BlockSpec((tt, D), lambda j, t: (t, 0)),
                  pl.BlockSpec((tt, SHW), lambda j, t: (t, j)),
                  pl.BlockSpec(memory_space=pl.ANY)],
        out_specs=pl.BlockSpec((1, D, SHW), lambda j, t: (j, 0, 0)),
        out_shape=SDS((NCHIP, D, SHW), MM),
        scratch_shapes=[pltpu.VMEM((D, SHW), F32)],
        name="grad_w_in",
        compiler_params=_cp(("arbitrary", "arbitrary")),
    )(h, dproj, token)


def _grad_w_out_call(mix, dy):
    T = mix.shape[0]
    tt = min(GRAD_TOKEN_TILE, T)
    nt = T // tt
    tn = 1024

    def body(m_ref, dy_ref, o_ref, acc_ref):
        t = pl.program_id(1)

        @pl.when(t == 0)
        def _():
            acc_ref[...] = jnp.zeros_like(acc_ref)

        acc_ref[...] += lax.dot_general(m_ref[...], dy_ref[...].astype(MM), TN, preferred_element_type=F32)

        @pl.when(t == nt - 1)
        def _():
            o_ref[...] = acc_ref[...].astype(MM)

    return pl.pallas_call(
        body,
        grid=(D // tn, nt),
        in_specs=[pl.BlockSpec((tt, D), lambda j, t: (t, 0)),
                  pl.BlockSpec((tt, tn), lambda j, t: (t, j))],
        out_specs=pl.BlockSpec((D, tn), lambda j, t: (0, j)),
        out_shape=SDS((D, D), MM),
        scratch_shapes=[pltpu.VMEM((D, tn), F32)],
        name="grad_w_out",
        compiler_params=_cp(("arbitrary", "arbitrary")),
    )(mix, dy)


def _cast_to_slab_call(w, chip_idx, name):
    L, R, C = w.shape
    tr = 256

    def body(chip_ref, *refs):
        for l in range(L):
            refs[L + l][...] = refs[l][...].astype(MM)

    return pl.pallas_call(
        body,
        grid_spec=pltpu.PrefetchScalarGridSpec(
            num_scalar_prefetch=1,
            grid=(R // tr,),
            in_specs=[pl.BlockSpec((1, tr, C), functools.partial(lambda i, chip_ref, l: (l, i, 0), l=l))
                      for l in range(L)],
            out_specs=[pl.BlockSpec((1, tr, C), lambda i, chip_ref: (chip_ref[0], i, 0))] * L,
        ),
        out_shape=[SDS((NCHIP, R, C), MM)] * L,
        name=name,
        compiler_params=_cp(("arbitrary",)),
    )(chip_idx, *([w] * L))


def _adam_call(w, g_parts, m, v, name):
    R, C = w.shape
    tr = R
    for cand in (512, 256, 128, 64, 32, 16, 8):
        if R % cand == 0 and cand * C * 4 <= 1024 * 1024:
            tr = cand
            break
    c1 = 1.0 - B1 ** STEP
    c2 = 1.0 - B2 ** STEP
    ng = len(g_parts)

    def body(*refs):
        w_ref, m_ref, v_ref = refs[0], refs[1 + ng], refs[2 + ng]
        g_ref, d_ref, nm_ref, nv_ref = refs[3 + ng:]
        gv = refs[1][...].astype(F32)
        for k in range(1, ng):
            gv = gv + refs[1 + k][...].astype(F32)
        nm = B1 * m_ref[...] + (1.0 - B1) * gv
        nv = B2 * v_ref[...] + (1.0 - B2) * (gv * gv)
        g_ref[...] = gv
        nm_ref[...] = nm
        nv_ref[...] = nv
        d_ref[...] = -LR * ((nm / c1) / (jnp.sqrt(nv / c2) + ADAM_EPS) + WD * w_ref[...])

    spec = pl.BlockSpec((tr, C), lambda i: (i, 0))
    return pl.pallas_call(
        body,
        grid=(R // tr,),
        in_specs=[spec] * (3 + ng),
        out_specs=[spec] * 4,
        out_shape=[SDS((R, C), F32)] * 4,
        name=name,
        compiler_params=_cp(("arbitrary",)),
    )(w, *g_parts, m, v)


MESH = pl.DeviceIdType.MESH
ANY = pl.BlockSpec(memory_space=pl.ANY)
HBM = pl.BlockSpec(memory_space=pltpu.HBM)
SEMS = pl.BlockSpec(memory_space=pltpu.SEMAPHORE)
EFFECT = pltpu.SideEffectType.DATAFLOW_SIDE_EFFECTING
NDEV = 8


def _hbm(a):
    return pltpu.with_memory_space_constraint(a, pltpu.HBM)


def _place():
    x, y, c = lax.axis_index("x"), lax.axis_index("y"), lax.axis_index("c")
    others = [(1 - x, y), (x, 1 - y), (1 - x, 1 - y)]
    return x, y, c, 2 * x + y, others


def _flipped(x, y, c, r):
    return (1 - x if r & 4 else x, 1 - y if r & 2 else y, 1 - c if r & 1 else c)


def _rcopy(src, dst, ssem, rsem, dev):
    return pltpu.make_async_remote_copy(src_ref=src, dst_ref=dst, send_sem=ssem, recv_sem=rsem,
                                        device_id=dev, device_id_type=MESH)


def _slab(ref, chip, c, halved):
    if not halved:
        return ref.at[chip]
    h = ref.shape[1] // 2
    return ref.at[chip, pl.ds(c * h, h), :]


def _gather_start_call(fulls, n_halved):
    K = len(fulls)

    def body(*refs):
        full, ssem, rsem = refs[:K], refs[K:2 * K], refs[2 * K:3 * K]
        x, y, c, me, others = _place()
        for k in range(K):
            for j, (px, py) in enumerate(others):
                part = _slab(full[k], me, c, k < n_halved)
                _rcopy(part, part, ssem[k].at[j], rsem[k].at[j], (px, py, c)).start()

    outs = pl.pallas_call(
        body,
        in_specs=[HBM] * K,
        out_specs=[SEMS] * (2 * K) + [HBM] * K,
        out_shape=[pltpu.SemaphoreType.DMA((3,))] * (2 * K) + [pltpu.HBM(f.shape, f.dtype) for f in fulls],
        input_output_aliases={k: 2 * K + k for k in range(K)},
        name="gather_start",
        compiler_params=pltpu.CompilerParams(has_side_effects=EFFECT),
    )(*[_hbm(f) for f in fulls])
    return list(outs[:K]), list(outs[K:2 * K]), list(outs[2 * K:])


def _gather_wait_call(fulls, ssems, rsems, after, halved, name):
    K = len(fulls)

    def body(*refs):
        full, ssem, rsem = refs[:K], refs[K:2 * K], refs[2 * K:3 * K]
        x, y, c, me, others = _place()
        for k in range(K):
            for j, (px, py) in enumerate(others):
                cp = _rcopy(_slab(full[k], me, c, halved), _slab(full[k], 2 * px + py, c, halved),
                            ssem[k].at[j], rsem[k].at[j], (px, py, c))
                cp.wait_send()
                cp.wait_recv()

    outs = pl.pallas_call(
        body,
        in_specs=[HBM] * K + [SEMS] * (2 * K) + [ANY],
        out_specs=[HBM] * K,
        out_shape=[pltpu.HBM(f.shape, f.dtype) for f in fulls],
        input_output_aliases={k: k for k in range(K)},
        name=name,
        compiler_params=pltpu.CompilerParams(has_side_effects=EFFECT),
    )(*fulls, *ssems, *rsems, after)
    return list(outs)


def _sibling_forward_call(fulls):
    K = len(fulls)

    def body(*refs):
        full = refs[:K]
        ssem, rsem = refs[2 * K:]
        x, y, c, me, others = _place()
        cps = []
        for k in range(K):
            for j, (px, py) in enumerate(others):
                mine = _slab(full[k], 2 * px + py, c, True)
                cps.append(_rcopy(mine, mine, ssem.at[3 * k + j], rsem.at[3 * k + j], (x, y, 1 - c)))
        for cp in cps:
            cp.start()
        for k in range(K):
            for j, (px, py) in enumerate(others):
                theirs = _slab(full[k], 2 * px + py, 1 - c, True)
                _rcopy(theirs, theirs, ssem.at[3 * k + j], rsem.at[3 * k + j], (x, y, 1 - c)).wait_recv()
        for cp in cps:
            cp.wait_send()

    outs = pl.pallas_call(
        body,
        in_specs=[ANY] * K,
        out_specs=[ANY] * K,
        out_shape=[SDS(f.shape, f.dtype) for f in fulls],
        input_output_aliases={k: k for k in range(K)},
        scratch_shapes=[pltpu.SemaphoreType.DMA((3 * K,)), pltpu.SemaphoreType.DMA((3 * K,))],
        name="gather_sibling_forward",
    )(*fulls)
    return list(outs)


def _grad_copies(srcs, lands, ssem, rsem):
    x, y, c, me, others = _place()
    cps, k = [], 0
    for src, land in zip(srcs, lands):
        if len(src.shape) == 3:
            for j, (px, py) in enumerate(others):
                cps.append(_rcopy(src.at[2 * px + py], land.at[j], ssem.at[k + j], rsem.at[k + j], (px, py, c)))
            k += 3
        else:
            for r in range(1, NDEV):
                cps.append(_rcopy(src, land.at[4 * x + 2 * y + c], ssem.at[k + r - 1], rsem.at[k + r - 1],
                                  _flipped(x, y, c, r)))
            k += NDEV - 1
    return cps


def _n_grad_copies(srcs):
    return sum(3 if len(s.shape) == 3 else NDEV - 1 for s in srcs)


def _grad_start_call(srcs, name):
    srcs = list(srcs)
    K = len(srcs)
    lands = [lax.empty(((3,) + s.shape[1:]) if len(s.shape) == 3 else ((NDEV,) + s.shape), s.dtype) for s in srcs]
    n = _n_grad_copies(srcs)

    def body(*refs):
        ssem, rsem, token = refs[2 * K], refs[2 * K + 1], refs[-1]
        for cp in _grad_copies(refs[:K], refs[K:2 * K], ssem, rsem):
            cp.start()
        token[...] = jnp.zeros_like(token)

    outs = pl.pallas_call(
        body,
        in_specs=[HBM] * (2 * K),
        out_specs=[SEMS, SEMS] + [HBM] * (2 * K) + [pl.BlockSpec(memory_space=pltpu.VMEM)],
        out_shape=[pltpu.SemaphoreType.DMA((n,)), pltpu.SemaphoreType.DMA((n,))]
        + [pltpu.HBM(a.shape, a.dtype) for a in srcs + lands] + [SDS((8, 128), F32)],
        input_output_aliases={k: 2 + k for k in range(2 * K)},
        name=name,
        compiler_params=pltpu.CompilerParams(has_side_effects=EFFECT),
    )(*[_hbm(a) for a in srcs + lands])
    return list(outs[2:2 + K]), list(outs[2 + K:2 + 2 * K]), outs[0], outs[1], outs[-1]


def _grad_wait_call(srcs, lands, ssem, rsem, after, name):
    K = len(srcs)

    def body(*refs):
        for cp in _grad_copies(refs[:K], refs[K:2 * K], refs[2 * K], refs[2 * K + 1]):
            cp.wait_send()
            cp.wait_recv()

    arrs = list(srcs) + list(lands)
    outs = pl.pallas_call(
        body,
        in_specs=[HBM] * (2 * K) + [SEMS, SEMS, ANY],
        out_specs=[HBM] * (2 * K),
        out_shape=[pltpu.HBM(a.shape, a.dtype) for a in arrs],
        input_output_aliases={k: k for k in range(2 * K)},
        name=name,
        compiler_params=pltpu.CompilerParams(has_side_effects=EFFECT),
    )(*arrs, ssem, rsem, after)
    return list(outs[:K]), list(outs[K:])


def _sibling_swap_call(arrs):
    K = len(arrs)

    def body(*refs):
        a_refs, t_refs = refs[:K], refs[K:2 * K]
        ssem, rsem = refs[2 * K:]
        x, y, c, me, others = _place()
        cps = [_rcopy(a_refs[k], t_refs[k], ssem.at[k], rsem.at[k], (x, y, 1 - c)) for k in range(K)]
        for cp in cps:
            cp.start()
        for cp in cps:
            cp.wait()

    return pl.pallas_call(
        body,
        in_specs=[ANY] * K,
        out_specs=[ANY] * K,
        out_shape=[SDS(a.shape, a.dtype) for a in arrs],
        scratch_shapes=[pltpu.SemaphoreType.DMA((K,)), pltpu.SemaphoreType.DMA((K,))],
        name="grad_sibling_swap",
    )(*arrs)


def _small_allreduce_call(a):
    R, C = a.shape

    def body(a_ref, o_ref, recv_ref, ssem, rsem):
        x, y, c, me, others = _place()
        dev = 4 * x + 2 * y + c
        recv_ref[pl.ds(dev, 1)] = a_ref[...][None]
        cps = [_rcopy(a_ref, recv_ref.at[dev], ssem.at[r - 1], rsem.at[r - 1], _flipped(x, y, c, r))
               for r in range(1, NDEV)]
        for cp in cps:
            cp.start()
        for cp in cps:
            cp.wait()
        acc = recv_ref[0]
        for s in range(1, NDEV):
            acc = acc + recv_ref[s]
        o_ref[...] = acc

    return pl.pallas_call(
        body,
        in_specs=[pl.BlockSpec(memory_space=pltpu.VMEM)],
        out_specs=pl.BlockSpec(memory_space=pltpu.VMEM),
        out_shape=SDS((R, C), F32),
        scratch_shapes=[pltpu.VMEM((NDEV, R, C), F32), pltpu.SemaphoreType.DMA((NDEV - 1,)),
                        pltpu.SemaphoreType.DMA((NDEV - 1,))],
        name="small_allreduce",
    )(a)


def _rows_tile(H, C):
    for cand in (512, 256, 128, 64, 32, 16, 8):
        if H % cand == 0 and cand * C * 4 <= 2 * 1024 * 1024:
            return cand
    raise ValueError((H, C))


def _sum_recv_call(own, recv, chip_idx, stack, l):
    _, R, C = own.shape
    tr = _rows_tile(R, C)

    def body(chip_ref, own_ref, r0, r1, r2, stack_ref, o_ref):
        o_ref[...] = (((own_ref[...].astype(F32) + r0[...].astype(F32)) + r1[...].astype(F32))
                      + r2[...].astype(F32)).astype(MM)

    return pl.pallas_call(
        body,
        grid_spec=pltpu.PrefetchScalarGridSpec(
            num_scalar_prefetch=1,
            grid=(R // tr,),
            in_specs=[pl.BlockSpec((1, tr, C), lambda i, chip_ref: (chip_ref[0], i, 0))]
            + [pl.BlockSpec((1, tr, C), functools.partial(lambda i, chip_ref, s: (s, i, 0), s=s)) for s in range(3)]
            + [ANY],
            out_specs=pl.BlockSpec((1, tr, C), lambda i, chip_ref: (l, i, 0)),
        ),
        out_shape=SDS(stack.shape, MM),
        input_output_aliases={5: 0},
        name="grad_sum_recv",
        compiler_params=_cp(("arbitrary",)),
    )(chip_idx, own, recv, recv, recv, stack)


def _sum_small_call(own, recv, dev_idx):
    RS, C = own.shape
    tr = _rows_tile(RS, C)

    def body(dev_ref, own_ref, *refs):
        o_ref = refs[NDEV]
        dev = dev_ref[0]
        acc = jnp.where(dev == 0, own_ref[...], refs[0][0]).astype(F32)
        for s in range(1, NDEV):
            acc = acc + jnp.where(dev == s, own_ref[...], refs[s][0]).astype(F32)
        o_ref[...] = acc

    return pl.pallas_call(
        body,
        grid_spec=pltpu.PrefetchScalarGridSpec(
            num_scalar_prefetch=1,
            grid=(RS // tr,),
            in_specs=[pl.BlockSpec((tr, C), lambda i, dev_ref: (i, 0))]
            + [pl.BlockSpec((1, tr, C), functools.partial(
                lambda i, dev_ref, s: (jnp.where(dev_ref[0] == s, (s + 1) % NDEV, s), i, 0), s=s)) for s in range(NDEV)],
            out_specs=pl.BlockSpec((tr, C), lambda i, dev_ref: (i, 0)),
        ),
        out_shape=SDS((RS, C), F32),
        name="grad_sum_small",
        compiler_params=_cp(("arbitrary",)),
    )(dev_idx, own, *([recv] * NDEV))


SMALL_ROWS_ALIGN = 128


def _pack_small(parts):
    flat = jnp.concatenate([p.reshape(-1) for p in parts])
    rows = -(-flat.shape[0] // (128 * SMALL_ROWS_ALIGN)) * SMALL_ROWS_ALIGN
    flat = jnp.pad(flat, (0, rows * 128 - flat.shape[0]))
    return flat.reshape(rows, 128)


def _unpack_small(packed, like):
    flat = packed.reshape(-1)
    out, off = [], 0
    for p in like:
        n = int(np.prod(p.shape))
        out.append(flat[off:off + n].reshape(p.shape))
        off += n
    return out


def kernel(x, norm_g, w_in, q_norm, k_norm, sinks, w_s, b_s, w_out, loss_target, m_norm_g, m_w_in, m_q_norm, m_k_norm, m_sinks, m_w_s, m_b_s, m_w_out, v_norm_g, v_w_in, v_q_norm, v_k_norm, v_sinks, v_w_s, v_b_s, v_w_out):
    L = norm_g.shape[0]
    xi, yi, ci = lax.axis_index("x"), lax.axis_index("y"), lax.axis_index("c")
    chip_idx = (2 * xi + yi).astype(jnp.int32).reshape(1)
    dev_idx = (4 * xi + 2 * yi + ci).astype(jnp.int32).reshape(1)
    bias = _alibi_bias()
    b2 = _half_sum_matrix()
    tri =jnp.tril(jnp.ones((WIN, WIN), F32))

    fin = _cast_to_slab_call(w_in, chip_idx, "cast_w_in")
    fout = _cast_to_slab_call(w_out, chip_idx, "cast_w_out")
    fulls = [a for l in range(L) for a in (fin[l], fout[l])]
    g_ssems, g_rsems, fulls = _gather_start_call(fulls, 2)

    saved = []
    xs = x[0]
    dy = loss = None
    for l in range(L):
        sl = slice(2 * l, 2 * l + 2)
        w_in_l, w_out_l = _gather_wait_call(fulls[sl], g_ssems[sl], g_rsems[sl], xs, l == 0, f"gather_wait_{l}")
        if l == 0:
            w_in_l, w_out_l = _sibling_forward_call([w_in_l, w_out_l])
        w_out_l = w_out_l.reshape(D, D)
        proj, h = _fwd_in_call(xs, norm_g[l:l + 1], w_in_l)
        ws_tril = (w_s[l] * tri).astype(MM)
        b_exp = jnp.repeat(b_s[l].T, HD, axis=1)
        wq2 = jnp.tile(q_norm[l:l + 1], (1, 2)) * SCALE
        wk2 = jnp.tile(k_norm[l:l + 1], (1, 2))
        mix = _fwd_mix_call(proj, bias, wq2, wk2, b2, sinks[l], ws_tril, b_exp)
        saved.append((xs, proj, h, mix, ws_tril, b_exp, w_in_l, w_out_l, wq2, wk2))
        if l < L - 1:
            xs = _fwd_out_call(xs, mix, w_out_l)
        else:
            dy, loss = _fwd_out_loss_call(xs, mix, w_out_l, loss_target[0])

    s_in = lax.empty((L, D, SHW), MM)
    s_out = lax.empty((L, SHR, D), MM)
    ws_sums, tiny_sums = [None] * L, [None] * L

    def finish(pending, after):
        nonlocal s_in, s_out
        l, exchanges = pending
        for tag, kinds, srcs, lands, ssem, rsem in exchanges:
            srcs, lands = _grad_wait_call(srcs, lands, ssem, rsem, after, f"grad_wait_{l}{tag}")
            for kind, src, land in zip(kinds, srcs, lands):
                if kind == "in":
                    s_in = _sum_recv_call(src, land, chip_idx, s_in, l)
                elif kind == "out":
                    s_out = _sum_recv_call(src, land, chip_idx, s_out, l)
                elif kind == "ws":
                    ws_sums[l] = _sum_small_call(src, land, dev_idx)
                else:
                    tiny_sums[l] = _sum_small_call(src, land, dev_idx)

    def start(l, tag, kinds, srcs):
        srcs, lands, ssem, rsem, token = _grad_start_call(srcs, f"grad_start_{l}{tag}")
        return (tag, kinds, srcs, lands, ssem, rsem), token

    pending = None
    d_norm_g = [None] * L
    for l in reversed(range(L)):
        xs, proj, h, mix, ws_tril, b_exp, w_in_l, w_out_l, wq2, wk2 = saved[l]
        g_w_out = _grad_w_out_call(mix, dy).reshape(NCHIP, SHR, D)
        exchanges, token = [], jnp.zeros((8, 128), F32)
        if l == 0:
            ex, token = start(l, "_out", ["out"], [g_w_out])
            exchanges.append(ex)
        dmix = _bwd_out_call(dy, w_out_l, token)
        ws_tril_t = jnp.swapaxes(ws_tril, 1, 2)
        dproj, dwq, dwk, dsk, dws, dbs = _bwd_mix_call(
            proj, dmix, bias, wq2, wk2, b2, sinks[l], ws_tril, ws_tril_t, b_exp)
        dwq, dwk = dwq[:, :HD] + dwq[:, HD:], dwk[:, :HD] + dwk[:, HD:]
        g_ws = dws.reshape(NG * WIN, WIN).astype(MM)
        g_tiny = _pack_small([dwq, dwk, dsk[:, 0], dbs]).astype(MM)
        token = jnp.zeros((8, 128), F32)
        if l == 0:
            ex, token = start(l, "_small", ["ws", "tiny"], [g_ws, g_tiny])
            exchanges.append(ex)
        g_w_in = _grad_w_in_call(h, dproj, token)
        if l == 0:
            ex, token = start(l, "", ["in"], [g_w_in])
        else:
            ex, token = start(l, "", ["in", "out", "ws", "tiny"], [g_w_in, g_w_out, g_ws, g_tiny])
        exchanges.append(ex)
        dy, d_norm_g[l] = _bwd_in_call(dproj, w_in_l, xs, dy, norm_g[l:l + 1], token)
        if pending is not None:
            finish(pending, dy)
        pending = (l, exchanges)
    finish(pending, dy)
    grad_x = dy
    g_norm_g = _small_allreduce_call(jnp.concatenate(d_norm_g, axis=0))

    t_in, t_out = _sibling_swap_call([s_in, s_out])
    g_w_in, d_in, nm_in, nv_in = _adam_call(
        w_in.reshape(L * D, SHW), [s_in.reshape(L * D, SHW), t_in.reshape(L * D, SHW)],
        m_w_in.reshape(L * D, SHW), v_w_in.reshape(L * D, SHW), "adam_w_in")
    g_w_out, d_out, nm_out, nv_out = _adam_call(
        w_out.reshape(L * SHR, D), [s_out.reshape(L * SHR, D), t_out.reshape(L * SHR, D)],
        m_w_out.reshape(L * SHR, D), v_w_out.reshape(L * SHR, D), "adam_w_out")

    def pack_layers(parts):
        return jnp.concatenate([_pack_small([p[l] for p in parts]) for l in range(L)], axis=0)

    ws_rows = (L * NG * WIN, WIN)
    ws_outs = _adam_call(w_s.reshape(ws_rows), [jnp.concatenate(ws_sums, axis=0)],
                         m_w_s.reshape(ws_rows), v_w_s.reshape(ws_rows), "adam_w_s")
    tiny_like = [q_norm, k_norm, sinks, b_s]
    tiny_outs = _adam_call(
        pack_layers(tiny_like), [jnp.concatenate(tiny_sums, axis=0)],
        pack_layers([m_q_norm, m_k_norm, m_sinks, m_b_s]),
        pack_layers([v_q_norm, v_k_norm, v_sinks, v_b_s]), "adam_tiny")
    norm_outs = _adam_call(norm_g, [g_norm_g], m_norm_g, v_norm_g, "adam_norm_g")

    def full(i, win, wout):
        tiny = tiny_outs[i]
        rows = tiny.shape[0] // L
        per_layer = [_unpack_small(tiny[l * rows:(l + 1) * rows], [p[l] for p in tiny_like]) for l in range(L)]
        qn, kn, sk, bs = [jnp.stack([per_layer[l][k] for l in range(L)]) for k in range(4)]
        return [norm_outs[i], win.reshape(w_in.shape), qn, kn, sk, ws_outs[i].reshape(w_s.shape), bs,
                wout.reshape(w_out.shape)]

    loss_all = lax.psum(loss[0, 0], ("x", "y", "c"))
    return (loss_all, grad_x[None], *full(0, g_w_in, g_w_out), *full(1, d_in, d_out),
            *full(2, nm_in, nm_out), *full(3, nv_in, nv_out))
```

```python
import functools
import math

import numpy as np
import jax
import jax.numpy as jnp
from jax import lax
from jax.experimental import pallas as pl
from jax.experimental.pallas import tpu as pltpu

F32 = jnp.float32
MM = jnp.bfloat16

D = 2048
HD = 64
DA = 1024
DKV = 256
DG = 1024
NQ, NKV, GRP, NG = 16, 4, 4, 16
WIN = 128
DIN = 5632
C_Q, C_K, C_V, C_GA, C_U, C_VS, C_GB = 0, 1024, 1280, 1536, 2560, 3584, 4608
NCHIP = 4
SHW = DIN // NCHIP
SHR = D // NCHIP
EPS = 1e-6
NEG = -1e30
SCALE = HD ** -0.5
INV_SQRT2 = 1.0 / math.sqrt(2.0)
INV_SQRT_2PI = 1.0 / math.sqrt(2.0 * math.pi)
LR, B1, B2, ADAM_EPS, WD, STEP = 0.001, 0.9, 0.999, 1e-08, 0.01, 10
VMEM_LIMIT = 56 * 1024 * 1024

SDS = jax.ShapeDtypeStruct
NT = (((1,), (1,)), ((), ()))
TN = (((0,), (0,)), ((), ()))


def _cp(sem=None):
    return pltpu.CompilerParams(dimension_semantics=sem, vmem_limit_bytes=VMEM_LIMIT)


def _sigmoid(x):
    return 1.0 / (1.0 + jnp.exp(-x))


def _gelu(x):
    return 0.5 * x * (1.0 + lax.erf(x * INV_SQRT2))


def _gelu_and_grad(x):
    cdf = 0.5 * (1.0 + lax.erf(x * INV_SQRT2))
    return x * cdf, cdf + x * jnp.exp(-0.5 * x * x) * INV_SQRT_2PI


def _alibi_bias():
    slopes = 2.0 ** (-8.0 * np.arange(1, NQ + 1) / NQ)
    dist = (np.arange(WIN)[:, None] + WIN) - np.arange(2 * WIN)[None, :]
    ok = (dist >= 0) & (dist < WIN)
    first = ok & (np.arange(2 * WIN)[None, :] >= WIN)
    val = -slopes[:, None, None] * dist[None].astype(np.float64)
    return jnp.asarray(np.stack([np.where(first[None], val, NEG), np.where(ok[None], val, NEG)]), dtype=F32)


def _half_sum_matrix():
    half = np.arange(LANE) // HD
    return jnp.asarray(half[:, None] == half[None, :], dtype=MM)


LANE = 128
NQT = DA // LANE
NKT = DKV // LANE


def _tiles(ref, c0, n):
    return jnp.concatenate([ref[:, c0 + j * LANE:c0 + (j + 1) * LANE] for j in range(n)], axis=0)


def _split(x):
    hi = x.astype(MM)
    return hi, (x - hi.astype(F32)).astype(MM)


def _half_sums(x, b2):
    hi, lo = _split(x)
    return jnp.dot(hi, b2, preferred_element_type=F32) + jnp.dot(lo, b2, preferred_element_type=F32)


def _attn_fwd(pm_ref, kvp_ref, bias_ref, wq2, wk2, b2, sink_ref):
    lo_half = lax.broadcasted_iota(jnp.int32, (1, LANE), 1) < HD
    q_ts = _tiles(pm_ref, C_Q, NQT)
    rq = lax.rsqrt(_half_sums(q_ts * q_ts, b2) * (1.0 / HD) + EPS)
    qs = (q_ts * rq * wq2).astype(MM)
    k_ts = jnp.concatenate([a[:, c0 + t * LANE:c0 + (t + 1) * LANE] for t in range(NKT)
                            for a, c0 in ((kvp_ref, 0), (pm_ref, C_K))], axis=0)
    rk = lax.rsqrt(_half_sums(k_ts * k_ts, b2) * (1.0 / HD) + EPS)
    kn = (k_ts * rk * wk2).astype(MM)
    v_ts = jnp.concatenate([a[:, c0 + t * LANE:c0 + (t + 1) * LANE] for t in range(NKT)
                            for a, c0 in ((kvp_ref, DKV), (pm_ref, C_V))], axis=0).astype(MM)
    ones = jnp.ones((2 * WIN, LANE), MM)
    km, vm = {}, {}
    for hk in range(NKV):
        t, eh = hk // 2, hk % 2
        sel = lo_half if eh == 0 else jnp.logical_not(lo_half)
        rows = slice(t * 2 * WIN, (t + 1) * 2 * WIN)
        k_same = jnp.where(sel, kn[rows], jnp.zeros_like(kn[rows]))
        v_same = jnp.where(sel, v_ts[rows], jnp.zeros_like(v_ts[rows]))
        km[hk, eh], km[hk, 1 - eh] = k_same, pltpu.roll(k_same, HD, axis=1)
        vm[hk, eh], vm[hk, 1 - eh] = v_same, pltpu.roll(v_same, HD, axis=1)
    hs = range(NQ)
    s = [lax.dot_general(qs[(h // 2) * WIN:(h // 2 + 1) * WIN], km[h // GRP, h % 2], NT, preferred_element_type=F32)
         + bias_ref[0, h] for h in hs]
    m = [jnp.maximum(jnp.max(s[h], axis=-1, keepdims=True), sink_ref[h]) for h in hs]
    p = [jnp.exp(s[h] - m[h]) for h in hs]
    pb = [p[h].astype(MM) for h in hs]
    res = [jnp.dot(pb[h], jnp.concatenate([vm[h // GRP, h % 2], ones], axis=1), preferred_element_type=F32) for h in hs]
    esink = [jnp.exp(sink_ref[h] - m[h]) for h in hs]
    inv = [1.0 / (res[h][:, LANE:] + esink[h]) for h in hs]
    heads = [dict(p=p[h], pb=pb[h], inv=inv[h], esink=esink[h], o=res[h][:, :LANE] * inv[h]) for h in hs]
    return dict(lo_half=lo_half, q_ts=q_ts, rq=rq, qs=qs, k_ts=k_ts, rk=rk, km=km, vm=vm, heads=heads)


def _sgu_mix(w_ref, zt, lo_half, j):
    zero = jnp.zeros_like(zt)
    return (jnp.dot(w_ref[2 * j], jnp.where(lo_half, zt, zero), preferred_element_type=F32)
            + jnp.dot(w_ref[2 * j + 1], jnp.where(lo_half, zero, zt), preferred_element_type=F32))


def _fwd_mix_call(proj, bias, wq2, wk2, b2, sinks, ws_tril, b_exp):
    T = proj.shape[0]
    nb = T // WIN

    def body(sink_ref, pm_ref, kvp_ref, bias_ref, wq_ref, wk_ref, b2_ref, ws_ref, be_ref, mix_ref):
        lo_half = lax.broadcasted_iota(jnp.int32, (1, LANE), 1) < HD
        zu = _gelu(pm_ref[:, C_U:C_U + DG])
        zv = _gelu(pm_ref[:, C_VS:C_VS + DG]).astype(MM)
        mixed = jnp.concatenate(
            [_sgu_mix(ws_ref, zv[:, j * LANE:(j + 1) * LANE], lo_half, j) for j in range(NG // 2)], axis=1)
        mixed = mixed + be_ref[...]
        gb = pm_ref[:, C_GB:C_GB + DG]
        mix_ref[:, DA:DA + DG] = (zu * mixed * (gb * _sigmoid(gb))).astype(MM)
        a = _attn_fwd(pm_ref, kvp_ref, bias_ref, wq_ref[...], wk_ref[...], b2_ref[...], sink_ref)
        for j in range(NQT):
            cols = slice(j * LANE, (j + 1) * LANE)
            ga = pm_ref[:, C_GA + j * LANE:C_GA + (j + 1) * LANE]
            attn = a["heads"][2 * j]["o"] + a["heads"][2 * j + 1]["o"]
            mix_ref[:, cols] = (attn * (ga * _sigmoid(ga))).astype(MM)

    return pl.pallas_call(
        body,
        grid=(nb,),
        in_specs=[
            pl.BlockSpec(memory_space=pltpu.SMEM),
            pl.BlockSpec((WIN, DIN), lambda n: (n, 0)),
            pl.BlockSpec((WIN, 2 * DKV), lambda n: (jnp.maximum(n - 1, 0), C_K // (2 * DKV))),
            pl.BlockSpec((1, NQ, WIN, 2 * WIN), lambda n: (jnp.minimum(n, 1), 0, 0, 0)),
            pl.BlockSpec((1, LANE), lambda n: (0, 0)),
            pl.BlockSpec((1, LANE), lambda n: (0, 0)),
            pl.BlockSpec((LANE, LANE), lambda n: (0, 0)),
            pl.BlockSpec((NG, WIN, WIN), lambda n: (0, 0, 0)),
            pl.BlockSpec((WIN, DG), lambda n: (0, 0)),
        ],
        out_specs=pl.BlockSpec((WIN, D), lambda n: (n, 0)),
        out_shape=SDS((T, D), MM),
        name="fwd_mix",
        compiler_params=_cp(("arbitrary",)),
    )(sinks, proj, proj, bias, wq2, wk2, b2, ws_tril, b_exp)


def _bwd_mix_call(proj, dmix, bias, wq2, wk2, b2, sinks, ws_tril, ws_tril_t, b_exp):
    T = proj.shape[0]
    nb = T // WIN

    def body(sink_ref, pm_ref, kvp_ref, dm_ref, bias_ref, wq_ref, wk_ref, b2_ref, ws_ref, wst_ref, be_ref,
             dp_ref, dwq_ref, dwk_ref, dsk_ref, dws_ref, dbs_ref, carry_ref, dbacc_ref):
        n = pl.program_id(0)

        @pl.when(n == 0)
        def _():
            carry_ref[...] = jnp.zeros_like(carry_ref)
            dbacc_ref[...] = jnp.zeros_like(dbacc_ref)
            dwq_ref[...] = jnp.zeros_like(dwq_ref)
            dwk_ref[...] = jnp.zeros_like(dwk_ref)
            dsk_ref[...] = jnp.zeros_like(dsk_ref)
            dws_ref[...] = jnp.zeros_like(dws_ref)
            dbs_ref[...] = jnp.zeros_like(dbs_ref)

        @pl.when(n < nb)
        def _():
            dp_ref[:, C_Q:C_K] = carry_ref[:, C_Q:C_K].astype(MM)
            dp_ref[:, C_GA:DIN] = carry_ref[:, C_GA:DIN].astype(MM)

            lo_half = lax.broadcasted_iota(jnp.int32, (1, LANE), 1) < HD
            u = pm_ref[:, C_U:C_U + DG]
            vs = pm_ref[:, C_VS:C_VS + DG]
            gb = pm_ref[:, C_GB:C_GB + DG]
            zu, dzu = _gelu_and_grad(u)
            zv, dzv = _gelu_and_grad(vs)
            zvb = zv.astype(MM)
            mixed = jnp.concatenate(
                [_sgu_mix(ws_ref, zvb[:, j * LANE:(j + 1) * LANE], lo_half, j) for j in range(NG // 2)], axis=1)
            mixed = mixed + be_ref[...]
            sgb = _sigmoid(gb)
            d_sgu = dm_ref[:, DA:DA + DG]
            carry_ref[:, C_GB:DIN] = d_sgu * zu * mixed * (sgb * (1.0 + gb * (1.0 - sgb)))
            d_mixed = d_sgu * zu * (gb * sgb)
            carry_ref[:, C_U:C_VS] = d_sgu * mixed * (gb * sgb) * dzu
            dbacc_ref[...] += d_mixed
            dmb = d_mixed.astype(MM)
            dzv_tiles = [_sgu_mix(wst_ref, dmb[:, j * LANE:(j + 1) * LANE], lo_half, j) for j in range(NG // 2)]
            carry_ref[:, C_VS:C_GB] = jnp.concatenate(dzv_tiles, axis=1) * dzv
            for j in range(NG // 2):
                dt = dmb[:, j * LANE:(j + 1) * LANE]
                zt = zvb[:, j * LANE:(j + 1) * LANE]
                zero = jnp.zeros_like(dt)
                dws_ref[2 * j] += lax.dot_general(jnp.where(lo_half, dt, zero), zt, NT, preferred_element_type=F32)
                dws_ref[2 * j + 1] += lax.dot_general(jnp.where(lo_half, zero, dt), zt, NT, preferred_element_type=F32)

            wq2, wk2, b2 = wq_ref[...], wk_ref[...], b2_ref[...]
            a = _attn_fwd(pm_ref, kvp_ref, bias_ref, wq2, wk2, b2, sink_ref)
            heads, km, vm, qs = a["heads"], a["km"], a["vm"], a["qs"]

            row_lo = lax.broadcasted_iota(jnp.int32, (LANE, LANE), 0) < HD
            pick = [jnp.where(row_lo, 1.0, 0.0).astype(MM), jnp.where(row_lo, 0.0, 1.0).astype(MM)]
            chan_lo = lax.broadcasted_iota(jnp.int32, (LANE, 1), 0) < HD
            tiles, hs = range(NQT), range(NQ)
            sel_t = [chan_lo, jnp.logical_not(chan_lo)]
            d_o, attn = [], []
            for j in tiles:
                cols = slice(C_GA + j * LANE, C_GA + (j + 1) * LANE)
                ga = pm_ref[:, cols]
                sga = _sigmoid(ga)
                d_gated = dm_ref[:, j * LANE:(j + 1) * LANE]
                attn.append(heads[2 * j]["o"] + heads[2 * j + 1]["o"])
                carry_ref[:, cols] = d_gated * attn[j] * (sga * (1.0 + ga * (1.0 - sga)))
                d_o.append(d_gated * (ga * sga))
            d_ob = [d_o[j].astype(MM) for j in tiles]
            dlt = [(d_o[j] * attn[j]).astype(MM) for j in tiles]
            d_os_t = [(d_o[j] * jnp.where(lo_half, heads[2 * j]["inv"], heads[2 * j + 1]["inv"])).astype(MM).T
                      for j in tiles]
            qs_t = [qs[j * WIN:(j + 1) * WIN].T for j in tiles]
            zero_t = jnp.zeros_like(qs_t[0])
            dv_h = [jnp.dot(jnp.where(sel_t[h % 2], d_os_t[h // 2], zero_t), heads[h]["pb"], preferred_element_type=F32)
                    for h in hs]
            d_p = [lax.dot_general(d_ob[h // 2], vm[h // GRP, h % 2], NT, preferred_element_type=F32) for h in hs]
            delta = [jnp.dot(dlt[h // 2], pick[h % 2], preferred_element_type=F32) for h in hs]
            for h in hs:
                dsk_ref[h:h + 1, :] -= jnp.sum(heads[h]["esink"] * heads[h]["inv"] * delta[h], axis=0, keepdims=True)
            d_s = [(heads[h]["p"] * ((d_p[h] - jnp.concatenate([delta[h], delta[h]], axis=1))
                                     * jnp.concatenate([heads[h]["inv"], heads[h]["inv"]], axis=1))).astype(MM)
                   for h in hs]
            dqs_h = [jnp.dot(d_s[h], km[h // GRP, h % 2], preferred_element_type=F32) for h in hs]
            dqs_tiles = [dqs_h[2 * j] + dqs_h[2 * j + 1] for j in tiles]
            dk_h = [jnp.dot(jnp.where(sel_t[h % 2], qs_t[h // 2], zero_t), d_s[h], preferred_element_type=F32)
                    for h in hs]
            dk_acc, dv_acc = {}, {}
            for h in hs:
                key = (h // GRP, h % 2 == (h // GRP) % 2)
                dk_acc[key] = dk_h[h] if key not in dk_acc else dk_acc[key] + dk_h[h]
                dv_acc[key] = dv_h[h] if key not in dv_acc else dv_acc[key] + dv_h[h]

            dqs_ts = jnp.concatenate(dqs_tiles, axis=0)
            q_ts, rq = a["q_ts"], a["rq"]
            gq = dqs_ts * wq2
            d_q = rq * gq - q_ts * (rq * rq * rq) * (_half_sums(gq * q_ts, b2) * (1.0 / HD))
            dwq_ref[...] += SCALE * jnp.sum(dqs_ts * q_ts * rq, axis=0, keepdims=True)
            for j in range(NQT):
                carry_ref[:, C_Q + j * LANE:C_Q + (j + 1) * LANE] = d_q[j * WIN:(j + 1) * WIN]

            def swap_halves(xt):
                return jnp.concatenate([xt[HD:], xt[:HD]], axis=0)

            dkn_tiles, dv_tiles = [], []
            for t in range(NKT):
                for acc, out in ((dk_acc, dkn_tiles), (dv_acc, dv_tiles)):
                    parts = [acc[hk, True] + swap_halves(acc[hk, False]) for hk in (2 * t, 2 * t + 1)]
                    out.append((parts[0] + parts[1]).T)
            dkn_ts = jnp.concatenate(dkn_tiles, axis=0)
            dv_ts = jnp.concatenate(dv_tiles, axis=0)
            k_ts, rk = a["k_ts"], a["rk"]
            gk = dkn_ts * wk2
            d_k = rk * gk - k_ts * (rk * rk * rk) * (_half_sums(gk * k_ts, b2) * (1.0 / HD))
            dwk_ref[...] += jnp.sum(dkn_ts * k_ts * rk, axis=0, keepdims=True)
            for t in range(NKT):
                for base, val in ((C_K, d_k), (C_V, dv_ts)):
                    cols = slice(base + t * LANE, base + (t + 1) * LANE)
                    r0 = t * 2 * WIN
                    dp_ref[:, cols] = (carry_ref[:, cols] + val[r0:r0 + WIN]).astype(MM)
                    carry_ref[:, cols] = val[r0 + WIN:r0 + 2 * WIN]

        @pl.when(n == nb)
        def _():
            dp_ref[...] = carry_ref[...].astype(MM)
            lo_half = lax.broadcasted_iota(jnp.int32, (8, LANE), 1) < HD
            ones = [jnp.where(lo_half, 1.0, 0.0).astype(MM), jnp.where(lo_half, 0.0, 1.0).astype(MM)]
            hi, lo = _split(dbacc_ref[...])
            for h in range(NG):
                sl = slice((h // 2) * LANE, (h // 2 + 1) * LANE)
                r = (lax.dot_general(ones[h % 2], hi[:, sl], NT, preferred_element_type=F32)
                     + lax.dot_general(ones[h % 2], lo[:, sl], NT, preferred_element_type=F32))
                dbs_ref[h:h + 1, :] = r[0:1, :]
            row = lax.broadcasted_iota(jnp.int32, (WIN, WIN), 0)
            cl = lax.broadcasted_iota(jnp.int32, (WIN, WIN), 1)
            for h in range(NG):
                dws_ref[h] = jnp.where(row >= cl, dws_ref[h], 0.0)

    last = nb - 1
    return pl.pallas_call(
        body,
        grid_spec=pltpu.PrefetchScalarGridSpec(
            num_scalar_prefetch=0,
            grid=(nb + 1,),
            in_specs=[
                pl.BlockSpec(memory_space=pltpu.SMEM),
                pl.BlockSpec((WIN, DIN), lambda n: (jnp.minimum(n, last), 0)),
                pl.BlockSpec((WIN, 2 * DKV), lambda n: (jnp.maximum(jnp.minimum(n, last) - 1, 0), C_K // (2 * DKV))),
                pl.BlockSpec((WIN, D), lambda n: (jnp.minimum(n, last), 0)),
                pl.BlockSpec((1, NQ, WIN, 2 * WIN), lambda n: (jnp.minimum(n, 1), 0, 0, 0)),
                pl.BlockSpec((1, LANE), lambda n: (0, 0)),
                pl.BlockSpec((1, LANE), lambda n: (0, 0)),
                pl.BlockSpec((LANE, LANE), lambda n: (0, 0)),
                pl.BlockSpec((NG, WIN, WIN), lambda n: (0, 0, 0)),
                pl.BlockSpec((NG, WIN, WIN), lambda n: (0, 0, 0)),
                pl.BlockSpec((WIN, DG), lambda n: (0, 0)),
            ],
            out_specs=[
                pl.BlockSpec((WIN, DIN), lambda n: (jnp.maximum(n - 1, 0), 0)),
                pl.BlockSpec((1, LANE), lambda n: (0, 0)),
                pl.BlockSpec((1, LANE), lambda n: (0, 0)),
                pl.BlockSpec((NQ, WIN), lambda n: (0, 0)),
                pl.BlockSpec((NG, WIN, WIN), lambda n: (0, 0, 0)),
                pl.BlockSpec((NG, WIN), lambda n: (0, 0)),
            ],
            scratch_shapes=[pltpu.VMEM((WIN, DIN), F32), pltpu.VMEM((WIN, DG), F32)],
        ),
        out_shape=[SDS((T, DIN), MM), SDS((1, LANE), F32), SDS((1, LANE), F32), SDS((NQ, WIN), F32),
                   SDS((NG, WIN, WIN), F32), SDS((NG, WIN), F32)],
        name="bwd_mix",
        compiler_params=_cp(("arbitrary",)),
    )(sinks, proj, proj, dmix, bias, wq2, wk2, b2, ws_tril, ws_tril_t, b_exp)


WEIGHT_RESIDENT_ROWS = 256
GRAD_TOKEN_TILE = 1024


def _row_tile(T):
    return min(512, T)


def _fwd_in_call(x, g_row, w_sh):
    T = x.shape[0]
    tm = min(WEIGHT_RESIDENT_ROWS, T)

    def body(x_ref, g_ref, w_hbm, proj_ref, h_ref, w_vmem, sem):
        @pl.when(pl.program_id(0) == 0)
        def _():
            cps = [pltpu.make_async_copy(w_hbm.at[j], w_vmem.at[:, pl.ds(j * SHW, SHW)], sem.at[j]) for j in range(NCHIP)]
            for cp in cps:
                cp.start()
            for cp in cps:
                cp.wait()

        xv = x_ref[...]
        r = lax.rsqrt(jnp.mean(xv * xv, axis=-1, keepdims=True) + EPS)
        h = (xv * r * g_ref[...]).astype(MM)
        h_ref[...] = h
        proj_ref[...] = jnp.dot(h, w_vmem[...], preferred_element_type=F32)

    return pl.pallas_call(
        body,
        grid=(T // tm,),
        in_specs=[pl.BlockSpec((tm, D), lambda i: (i, 0)),
                  pl.BlockSpec((1, D), lambda i: (0, 0)),
                  pl.BlockSpec(memory_space=pl.ANY)],
        out_specs=[pl.BlockSpec((tm, DIN), lambda i: (i, 0)),
                   pl.BlockSpec((tm, D), lambda i: (i, 0))],
        out_shape=[SDS((T, DIN), F32), SDS((T, D), MM)],
        scratch_shapes=[pltpu.VMEM((D, DIN), MM), pltpu.SemaphoreType.DMA((NCHIP,))],
        name="fwd_in",
        compiler_params=_cp(("arbitrary",)),
    )(x, g_row, w_sh)


def _fwd_out_call(x, mix, w_out):
    T = x.shape[0]
    tm = _row_tile(T)

    def body(x_ref, mix_ref, w_ref, y_ref):
        y_ref[...] = x_ref[...] + jnp.dot(mix_ref[...], w_ref[...], preferred_element_type=F32)

    return pl.pallas_call(
        body,
        grid=(T // tm,),
        in_specs=[pl.BlockSpec((tm, D), lambda i: (i, 0)),
                  pl.BlockSpec((tm, D), lambda i: (i, 0)),
                  pl.BlockSpec((D, D), lambda i: (0, 0))],
        out_specs=pl.BlockSpec((tm, D), lambda i: (i, 0)),
        out_shape=SDS((T, D), F32),
        name="fwd_out",
        compiler_params=_cp(("arbitrary",)),
    )(x, mix, w_out)


def _fwd_out_loss_call(x, mix, w_out, target):
    T = x.shape[0]
    tm = _row_tile(T)

    def body(x_ref, mix_ref, w_ref, t_ref, dy_ref, loss_ref):
        @pl.when(pl.program_id(0) == 0)
        def _():
            loss_ref[...] = jnp.zeros_like(loss_ref)

        e = x_ref[...] + jnp.dot(mix_ref[...], w_ref[...], preferred_element_type=F32) - t_ref[...]
        dy_ref[...] = e * (1.0 / D)
        loss_ref[...] += (0.5 / D) * jnp.sum(jnp.sum(e * e, axis=1, keepdims=True), axis=0, keepdims=True)

    return pl.pallas_call(
        body,
        grid=(T // tm,),
        in_specs=[pl.BlockSpec((tm, D), lambda i: (i, 0)),
                  pl.BlockSpec((tm, D), lambda i: (i, 0)),
                  pl.BlockSpec((D, D), lambda i: (0, 0)),
                  pl.BlockSpec((tm, D), lambda i: (i, 0))],
        out_specs=[pl.BlockSpec((tm, D), lambda i: (i, 0)),
                   pl.BlockSpec((1, 1), lambda i: (0, 0))],
        out_shape=[SDS((T, D), F32), SDS((1, 1), F32)],
        name="fwd_out_loss",
        compiler_params=_cp(("arbitrary",)),
    )(x, mix, w_out, target)


def _bwd_out_call(dy, w_out, token):
    T = dy.shape[0]
    tm = _row_tile(T)

    def body(dy_ref, w_ref, token_ref, o_ref):
        o_ref[...] = lax.dot_general(dy_ref[...].astype(MM), w_ref[...], NT, preferred_element_type=F32)

    return pl.pallas_call(
        body,
        grid=(T // tm,),
        in_specs=[pl.BlockSpec((tm, D), lambda i: (i, 0)),
                  pl.BlockSpec((D, D), lambda i: (0, 0)),
                  pl.BlockSpec(memory_space=pl.ANY)],
        out_specs=pl.BlockSpec((tm, D), lambda i: (i, 0)),
        out_shape=SDS((T, D), F32),
        name="bwd_out",
        compiler_params=_cp(("arbitrary",)),
    )(dy, w_out, token)


def _bwd_in_call(dproj, w_sh, x, dy, g_row, token):
    T = x.shape[0]
    tm = min(WEIGHT_RESIDENT_ROWS, T)

    def body(dp_ref, w_hbm, x_ref, dy_ref, g_ref, token_ref, dx_ref, dg_ref, w_vmem, sem):
        @pl.when(pl.program_id(0) == 0)
        def _():
            cps = [pltpu.make_async_copy(w_hbm.at[j], w_vmem.at[:, pl.ds(j * SHW, SHW)], sem.at[j]) for j in range(NCHIP)]
            for cp in cps:
                cp.start()
            dg_ref[...] = jnp.zeros_like(dg_ref)
            for cp in cps:
                cp.wait()

        dh = lax.dot_general(dp_ref[...], w_vmem[...], NT, preferred_element_type=F32)
        xv = x_ref[...]
        r = lax.rsqrt(jnp.mean(xv * xv, axis=-1, keepdims=True) + EPS)
        gd = dh * g_ref[...]
        dx_ref[...] = dy_ref[...] + r * gd - xv * ((r * r * r) * jnp.mean(gd * xv, axis=-1, keepdims=True))
        dg_ref[...] += jnp.sum(dh * xv * r, axis=0, keepdims=True)

    return pl.pallas_call(
        body,
        grid=(T // tm,),
        in_specs=[pl.BlockSpec((tm, DIN), lambda i: (i, 0)),
                  pl.BlockSpec(memory_space=pl.ANY),
                  pl.BlockSpec((tm, D), lambda i: (i, 0)),
                  pl.BlockSpec((tm, D), lambda i: (i, 0)),
                  pl.BlockSpec((1, D), lambda i: (0, 0)),
                  pl.BlockSpec(memory_space=pl.ANY)],
        out_specs=[pl.BlockSpec((tm, D), lambda i: (i, 0)),
                   pl.BlockSpec((1, D), lambda i: (0, 0))],
        out_shape=[SDS((T, D), F32), SDS((1, D), F32)],
        scratch_shapes=[pltpu.VMEM((D, DIN), MM), pltpu.SemaphoreType.DMA((NCHIP,))],
        name="bwd_in",
        compiler_params=_cp(("arbitrary",)),
    )(dproj, w_sh, x, dy, g_row, token)


def _grad_w_in_call(h, dproj, token):
    T = h.shape[0]
    tt = min(GRAD_TOKEN_TILE, T)
    nt = T // tt

    def body(h_ref, dp_ref, token_ref, o_ref, acc_ref):
        t = pl.program_id(1)

        @pl.when(t == 0)
        def _():
            acc_ref[...] = jnp.zeros_like(acc_ref)

        acc_ref[...] += lax.dot_general(h_ref[...], dp_ref[...], TN, preferred_element_type=F32)

        @pl.when(t == nt - 1)
        def _():
            o_ref[0] = acc_ref[...].astype(MM)

    return pl.pallas_call(
        body,
        grid=(NCHIP, nt),
        in_specs=[pl.BlockSpec((tt, D), lambda j, t: (t, 0)),
                  pl.BlockSpec((tt, SHW), lambda j, t: (t, j)),
                  pl.BlockSpec(memory_space=pl.ANY)],
        out_specs=pl.BlockSpec((1, D, SHW), lambda j, t: (j, 0, 0)),
        out_shape=SDS((NCHIP, D, SHW), MM),
        scratch_shapes=[pltpu.VMEM((D, SHW), F32)],
        name="grad_w_in",
        compiler_params=_cp(("arbitrary", "arbitrary")),
    )(h, dproj, token)


def _grad_w_out_call(mix, dy):
    T = mix.shape[0]
    tt = min(GRAD_TOKEN_TILE, T)
    nt = T // tt
    tn = 1024

    def body(m_ref, dy_ref, o_ref, acc_ref):
        t = pl.program_id(1)

        @pl.when(t == 0)
        def _():
            acc_ref[...] = jnp.zeros_like(acc_ref)

        acc_ref[...] += lax.dot_general(m_ref[...], dy_ref[...].astype(MM), TN, preferred_element_type=F32)

        @pl.when(t == nt - 1)
        def _():
            o_ref[...] = acc_ref[...].astype(MM)

    return pl.pallas_call(
        body,
        grid=(D // tn, nt),
        in_specs=[pl.BlockSpec((tt, D), lambda j, t: (t, 0)),
                  pl.BlockSpec((tt, tn), lambda j, t: (t, j))],
        out_specs=pl.BlockSpec((D, tn), lambda j, t: (0, j)),
        out_shape=SDS((D, D), MM),
        scratch_shapes=[pltpu.VMEM((D, tn), F32)],
        name="grad_w_out",
        compiler_params=_cp(("arbitrary", "arbitrary")),
    )(mix, dy)


def _cast_to_slab_call(w, chip_idx, layers, after, name):
    _, R, C = w.shape
    n = len(layers)
    tr = 256

    def body(chip_ref, *refs):
        for k in range(n):
            refs[n + 1 + k][...] = refs[k][...].astype(MM)

    return pl.pallas_call(
        body,
        grid_spec=pltpu.PrefetchScalarGridSpec(
            num_scalar_prefetch=1,
            grid=(R // tr,),
            in_specs=[pl.BlockSpec((1, tr, C), functools.partial(lambda i, chip_ref, l: (l, i, 0), l=l))
                      for l in layers] + [ANY],
            out_specs=[pl.BlockSpec((1, tr, C), lambda i, chip_ref: (chip_ref[0], i, 0))] * n,
        ),
        out_shape=[SDS((NCHIP, R, C), MM)] * n,
        name=name,
        compiler_params=_cp(("arbitrary",)),
    )(chip_idx, *([w] * n), after)


def _adam_call(w, g_parts, m, v, name):
    R, C = w.shape
    tr = R
    for cand in (512, 256, 128, 64, 32, 16, 8):
        if R % cand == 0 and cand * C * 4 <= 1024 * 1024:
            tr = cand
            break
    c1 = 1.0 - B1 ** STEP
    c2 = 1.0 - B2 ** STEP
    ng = len(g_parts)

    def body(*refs):
        w_ref, m_ref, v_ref = refs[0], refs[1 + ng], refs[2 + ng]
        g_ref, d_ref, nm_ref, nv_ref = refs[3 + ng:]
        gv = refs[1][...].astype(F32)
        for k in range(1, ng):
            gv = gv + refs[1 + k][...].astype(F32)
        nm = B1 * m_ref[...] + (1.0 - B1) * gv
        nv = B2 * v_ref[...] + (1.0 - B2) * (gv * gv)
        g_ref[...] = gv
        nm_ref[...] = nm
        nv_ref[...] = nv
        d_ref[...] = -LR * ((nm / c1) / (jnp.sqrt(nv / c2) + ADAM_EPS) + WD * w_ref[...])

    spec = pl.BlockSpec((tr, C), lambda i: (i, 0))
    return pl.pallas_call(
        body,
        grid=(R // tr,),
        in_specs=[spec] * (3 + ng),
        out_specs=[spec] * 4,
        out_shape=[SDS((R, C), F32)] * 4,
        name=name,
        compiler_params=_cp(("arbitrary",)),
    )(w, *g_parts, m, v)


MESH = pl.DeviceIdType.MESH
ANY = pl.BlockSpec(memory_space=pl.ANY)
HBM = pl.BlockSpec(memory_space=pltpu.HBM)
SEMS = pl.BlockSpec(memory_space=pltpu.SEMAPHORE)
EFFECT = pltpu.SideEffectType.DATAFLOW_SIDE_EFFECTING
NDEV = 8


def _hbm(a):
    return pltpu.with_memory_space_constraint(a, pltpu.HBM)


def _place():
    x, y, c = lax.axis_index("x"), lax.axis_index("y"), lax.axis_index("c")
    others = [(1 - x, y), (x, 1 - y), (1 - x, 1 - y)]
    return x, y, c, 2 * x + y, others


def _flipped(x, y, c, r):
    return (1 - x if r & 4 else x, 1 - y if r & 2 else y, 1 - c if r & 1 else c)


def _rcopy(src, dst, ssem, rsem, dev):
    return pltpu.make_async_remote_copy(src_ref=src, dst_ref=dst, send_sem=ssem, recv_sem=rsem,
                                        device_id=dev, device_id_type=MESH)


def _slab(ref, chip, c, halved):
    if not halved:
        return ref.at[chip]
    h = ref.shape[1] // 2
    return ref.at[chip, pl.ds(c * h, h), :]


def _gather_start_call(fulls, n_halved, name):
    K = len(fulls)

    def body(*refs):
        full, ssem, rsem = refs[:K], refs[K:2 * K], refs[2 * K:3 * K]
        x, y, c, me, others = _place()
        for k in range(K):
            for j, (px, py) in enumerate(others):
                part = _slab(full[k], me, c, k < n_halved)
                _rcopy(part, part, ssem[k].at[j], rsem[k].at[j], (px, py, c)).start()

    outs = pl.pallas_call(
        body,
        in_specs=[HBM] * K,
        out_specs=[SEMS] * (2 * K) + [HBM] * K,
        out_shape=[pltpu.SemaphoreType.DMA((3,))] * (2 * K) + [pltpu.HBM(f.shape, f.dtype) for f in fulls],
        input_output_aliases={k: 2 * K + k for k in range(K)},
        name=name,
        compiler_params=pltpu.CompilerParams(has_side_effects=EFFECT),
    )(*[_hbm(f) for f in fulls])
    return list(outs[:K]), list(outs[K:2 * K]), list(outs[2 * K:])


def _gather_wait_call(fulls, ssems, rsems, after, halved, name):
    K = len(fulls)

    def body(*refs):
        full, ssem, rsem = refs[:K], refs[K:2 * K], refs[2 * K:3 * K]
        x, y, c, me, others = _place()
        for k in range(K):
            for j, (px, py) in enumerate(others):
                cp = _rcopy(_slab(full[k], me, c, halved), _slab(full[k], 2 * px + py, c, halved),
                            ssem[k].at[j], rsem[k].at[j], (px, py, c))
                cp.wait_send()
                cp.wait_recv()

    outs = pl.pallas_call(
        body,
        in_specs=[HBM] * K + [SEMS] * (2 * K) + [ANY],
        out_specs=[HBM] * K,
        out_shape=[pltpu.HBM(f.shape, f.dtype) for f in fulls],
        input_output_aliases={k: k for k in range(K)},
        name=name,
        compiler_params=pltpu.CompilerParams(has_side_effects=EFFECT),
    )(*fulls, *ssems, *rsems, after)
    return list(outs)


def _sibling_forward_call(fulls, name):
    K = len(fulls)

    def body(*refs):
        full = refs[:K]
        ssem, rsem = refs[2 * K:]
        x, y, c, me, others = _place()
        cps = []
        for k in range(K):
            for j, (px, py) in enumerate(others):
                mine = _slab(full[k], 2 * px + py, c, True)
                cps.append(_rcopy(mine, mine, ssem.at[3 * k + j], rsem.at[3 * k + j], (x, y, 1 - c)))
        for cp in cps:
            cp.start()
        for k in range(K):
            for j, (px, py) in enumerate(others):
                theirs = _slab(full[k], 2 * px + py, 1 - c, True)
                _rcopy(theirs, theirs, ssem.at[3 * k + j], rsem.at[3 * k + j], (x, y, 1 - c)).wait_recv()
        for cp in cps:
            cp.wait_send()

    outs = pl.pallas_call(
        body,
        in_specs=[ANY] * K,
        out_specs=[ANY] * K,
        out_shape=[SDS(f.shape, f.dtype) for f in fulls],
        input_output_aliases={k: k for k in range(K)},
        scratch_shapes=[pltpu.SemaphoreType.DMA((3 * K,)), pltpu.SemaphoreType.DMA((3 * K,))],
        name=name,
    )(*fulls)
    return list(outs)


def _grad_copies(srcs, lands, ssem, rsem, to_sibling):
    x, y, c, me, others = _place()
    if to_sibling:
        return [_rcopy(s, l, ssem.at[k], rsem.at[k], (x, y, 1 - c)) for k, (s, l) in enumerate(zip(srcs, lands))]
    cps, k = [], 0
    for src, land in zip(srcs, lands):
        if len(src.shape) == 3:
            for j, (px, py) in enumerate(others):
                cps.append(_rcopy(src.at[2 * px + py], land.at[j], ssem.at[k + j], rsem.at[k + j], (px, py, c)))
            k += 3
        else:
            for r in range(1, NDEV):
                cps.append(_rcopy(src, land.at[4 * x + 2 * y + c], ssem.at[k + r - 1], rsem.at[k + r - 1],
                                  _flipped(x, y, c, r)))
            k += NDEV - 1
    return cps


def _grad_start_call(srcs, name, to_sibling=False):
    srcs = list(srcs)
    K = len(srcs)
    if to_sibling:
        lands = [lax.empty(s.shape, s.dtype) for s in srcs]
        n = K
    else:
        lands = [lax.empty(((3,) + s.shape[1:]) if len(s.shape) == 3 else ((NDEV,) + s.shape), s.dtype) for s in srcs]
        n = sum(3 if len(s.shape) == 3 else NDEV - 1 for s in srcs)

    def body(*refs):
        ssem, rsem, token = refs[2 * K], refs[2 * K + 1], refs[-1]
        for cp in _grad_copies(refs[:K], refs[K:2 * K], ssem, rsem, to_sibling):
            cp.start()
        token[...] = jnp.zeros_like(token)

    outs = pl.pallas_call(
        body,
        in_specs=[HBM] * (2 * K),
        out_specs=[SEMS, SEMS] + [HBM] * (2 * K) + [pl.BlockSpec(memory_space=pltpu.VMEM)],
        out_shape=[pltpu.SemaphoreType.DMA((n,)), pltpu.SemaphoreType.DMA((n,))]
        + [pltpu.HBM(a.shape, a.dtype) for a in srcs + lands] + [SDS((8, 128), F32)],
        input_output_aliases={k: 2 + k for k in range(2 * K)},
        name=name,
        compiler_params=pltpu.CompilerParams(has_side_effects=EFFECT),
    )(*[_hbm(a) for a in srcs + lands])
    return list(outs[2:2 + K]), list(outs[2 + K:2 + 2 * K]), outs[0], outs[1], outs[-1]


def _grad_wait_call(srcs, lands, ssem, rsem, after, name, to_sibling=False):
    K = len(srcs)

    def body(*refs):
        for cp in _grad_copies(refs[:K], refs[K:2 * K], refs[2 * K], refs[2 * K + 1], to_sibling):
            cp.wait_send()
            cp.wait_recv()

    arrs = list(srcs) + list(lands)
    outs = pl.pallas_call(
        body,
        in_specs=[HBM] * (2 * K) + [SEMS, SEMS, ANY],
        out_specs=[HBM] * (2 * K),
        out_shape=[pltpu.HBM(a.shape, a.dtype) for a in arrs],
        input_output_aliases={k: k for k in range(2 * K)},
        name=name,
        compiler_params=pltpu.CompilerParams(has_side_effects=EFFECT),
    )(*arrs, ssem, rsem, after)
    return list(outs[:K]), list(outs[K:])


def _small_allreduce_call(a):
    R, C = a.shape

    def body(a_ref, o_ref, recv_ref, ssem, rsem):
        x, y, c, me, others = _place()
        dev = 4 * x + 2 * y + c
        recv_ref[pl.ds(dev, 1)] = a_ref[...][None]
        cps = [_rcopy(a_ref, recv_ref.at[dev], ssem.at[r - 1], rsem.at[r - 1], _flipped(x, y, c, r))
               for r in range(1, NDEV)]
        for cp in cps:
            cp.start()
        for cp in cps:
            cp.wait()
        acc = recv_ref[0]
        for s in range(1, NDEV):
            acc = acc + recv_ref[s]
        o_ref[...] = acc

    return pl.pallas_call(
        body,
        in_specs=[pl.BlockSpec(memory_space=pltpu.VMEM)],
        out_specs=pl.BlockSpec(memory_space=pltpu.VMEM),
        out_shape=SDS((R, C), F32),
        scratch_shapes=[pltpu.VMEM((NDEV, R, C), F32), pltpu.SemaphoreType.DMA((NDEV - 1,)),
                        pltpu.SemaphoreType.DMA((NDEV - 1,))],
        name="small_allreduce",
    )(a)


def _rows_tile(H, C):
    for cand in (512, 256, 128, 64, 32, 16, 8):
        if H % cand == 0 and cand * C * 4 <= 2 * 1024 * 1024:
            return cand
    raise ValueError((H, C))


def _sum_recv_call(own, recv, chip_idx, stack, l):
    _, R, C = own.shape
    tr = _rows_tile(R, C)

    def body(chip_ref, own_ref, r0, r1, r2, stack_ref, o_ref):
        o_ref[...] = (((own_ref[...].astype(F32) + r0[...].astype(F32)) + r1[...].astype(F32))
                      + r2[...].astype(F32)).astype(MM)

    return pl.pallas_call(
        body,
        grid_spec=pltpu.PrefetchScalarGridSpec(
            num_scalar_prefetch=1,
            grid=(R // tr,),
            in_specs=[pl.BlockSpec((1, tr, C), lambda i, chip_ref: (chip_ref[0], i, 0))]
            + [pl.BlockSpec((1, tr, C), functools.partial(lambda i, chip_ref, s: (s, i, 0), s=s)) for s in range(3)]
            + [ANY],
            out_specs=pl.BlockSpec((1, tr, C), lambda i, chip_ref: (l, i, 0)),
        ),
        out_shape=SDS(stack.shape, MM),
        input_output_aliases={5: 0},
        name="grad_sum_recv",
        compiler_params=_cp(("arbitrary",)),
    )(chip_idx, own, recv, recv, recv, stack)


def _sum_small_call(own, recv, dev_idx):
    RS, C = own.shape
    tr = _rows_tile(RS, C)

    def body(dev_ref, own_ref, *refs):
        o_ref = refs[NDEV]
        dev = dev_ref[0]
        acc = jnp.where(dev == 0, own_ref[...], refs[0][0]).astype(F32)
        for s in range(1, NDEV):
            acc = acc + jnp.where(dev == s, own_ref[...], refs[s][0]).astype(F32)
        o_ref[...] = acc

    return pl.pallas_call(
        body,
        grid_spec=pltpu.PrefetchScalarGridSpec(
            num_scalar_prefetch=1,
            grid=(RS // tr,),
            in_specs=[pl.BlockSpec((tr, C), lambda i, dev_ref: (i, 0))]
            + [pl.BlockSpec((1, tr, C), functools.partial(
                lambda i, dev_ref, s: (jnp.where(dev_ref[0] == s, (s + 1) % NDEV, s), i, 0), s=s)) for s in range(NDEV)],
            out_specs=pl.BlockSpec((tr, C), lambda i, dev_ref: (i, 0)),
        ),
        out_shape=SDS((RS, C), F32),
        name="grad_sum_small",
        compiler_params=_cp(("arbitrary",)),
    )(dev_idx, own, *([recv] * NDEV))


SMALL_ROWS_ALIGN = 128


def _pack_small(parts):
    flat = jnp.concatenate([p.reshape(-1) for p in parts])
    rows = -(-flat.shape[0] // (128 * SMALL_ROWS_ALIGN)) * SMALL_ROWS_ALIGN
    flat = jnp.pad(flat, (0, rows * 128 - flat.shape[0]))
    return flat.reshape(rows, 128)


def _unpack_small(packed, like):
    flat = packed.reshape(-1)
    out, off = [], 0
    for p in like:
        n = int(np.prod(p.shape))
        out.append(flat[off:off + n].reshape(p.shape))
        off += n
    return out


def kernel(x, norm_g, w_in, q_norm, k_norm, sinks, w_s, b_s, w_out, loss_target, m_norm_g, m_w_in, m_q_norm, m_k_norm, m_sinks, m_w_s, m_b_s, m_w_out, v_norm_g, v_w_in, v_q_norm, v_k_norm, v_sinks, v_w_s, v_b_s, v_w_out):
    L = norm_g.shape[0]
    xi, yi, ci = lax.axis_index("x"), lax.axis_index("y"), lax.axis_index("c")
    chip_idx = (2 * xi + yi).astype(jnp.int32).reshape(1)
    dev_idx = (4 * xi + 2 * yi + ci).astype(jnp.int32).reshape(1)
    bias = _alibi_bias()
    b2 = _half_sum_matrix()
    tri =jnp.tril(jnp.ones((WIN, WIN), F32))

    rest = list(range(1, L))
    fulls0 = _cast_to_slab_call(w_in, chip_idx, [0], chip_idx, "cast_w_in_0") \
        + _cast_to_slab_call(w_out, chip_idx, [0], chip_idx, "cast_w_out_0")
    ss0, rs0, fulls0 = _gather_start_call(fulls0, 2, "gather_start_0")
    fin = _cast_to_slab_call(w_in, chip_idx, rest, fulls0[0], "cast_w_in")
    fout = _cast_to_slab_call(w_out, chip_idx, rest, fulls0[1], "cast_w_out")
    g_ssems, g_rsems, fulls = _gather_start_call([a for pair in zip(fin, fout) for a in pair], 0, "gather_start")

    saved = []
    xs = x[0]
    dy = loss = None
    for l in range(L):
        if l == 0:
            (w_in_l,) = _gather_wait_call(fulls0[:1], ss0[:1], rs0[:1], fulls[0], True, "gather_wait_0")
            (w_in_l,) = _sibling_forward_call([w_in_l], "gather_forward_0")
        else:
            sl = slice(2 * (l - 1), 2 * l)
            w_in_l, w_out_l = _gather_wait_call(fulls[sl], g_ssems[sl], g_rsems[sl], xs, False, f"gather_wait_{l}")
        proj, h = _fwd_in_call(xs, norm_g[l:l + 1], w_in_l)
        ws_tril = (w_s[l] * tri).astype(MM)
        b_exp = jnp.repeat(b_s[l].T, HD, axis=1)
        wq2 = jnp.tile(q_norm[l:l + 1], (1, 2)) * SCALE
        wk2 = jnp.tile(k_norm[l:l + 1], (1, 2))
        mix = _fwd_mix_call(proj, bias, wq2, wk2, b2, sinks[l], ws_tril, b_exp)
        if l == 0:
            (w_out_l,) = _gather_wait_call(fulls0[1:], ss0[1:], rs0[1:], mix, True, "gather_wait_0_out")
            (w_out_l,) = _sibling_forward_call([w_out_l], "gather_forward_0_out")
        w_out_l = w_out_l.reshape(D, D)
        saved.append((xs, proj, h, mix, ws_tril, b_exp, w_in_l, w_out_l, wq2, wk2))
        if l < L - 1:
            xs = _fwd_out_call(xs, mix, w_out_l)
        else:
            dy, loss = _fwd_out_loss_call(xs, mix, w_out_l, loss_target[0])

    s_in = lax.empty((L, D, SHW), MM)
    s_out = lax.empty((L, SHR, D), MM)
    ws_sums, tiny_sums = [None] * L, [None] * L

    def finish(pending, after):
        nonlocal s_in, s_out
        l, exchanges = pending
        for tag, kinds, srcs, lands, ssem, rsem in exchanges:
            srcs, lands = _grad_wait_call(srcs, lands, ssem, rsem, after, f"grad_wait_{l}{tag}")
            for kind, src, land in zip(kinds, srcs, lands):
                if kind == "in":
                    s_in = _sum_recv_call(src, land, chip_idx, s_in, l)
                elif kind == "out":
                    s_out = _sum_recv_call(src, land, chip_idx, s_out, l)
                elif kind == "ws":
                    ws_sums[l] = _sum_small_call(src, land, dev_idx)
                else:
                    tiny_sums[l] = _sum_small_call(src, land, dev_idx)

    def start(l, tag, kinds, srcs):
        srcs, lands, ssem, rsem, token = _grad_start_call(srcs, f"grad_start_{l}{tag}")
        return (tag, kinds, srcs, lands, ssem, rsem), token

    pending = None
    d_norm_g = [None] * L
    for l in reversed(range(L)):
        xs, proj, h, mix, ws_tril, b_exp, w_in_l, w_out_l, wq2, wk2 = saved[l]
        g_w_out = _grad_w_out_call(mix, dy).reshape(NCHIP, SHR, D)
        exchanges, token = [], jnp.zeros((8, 128), F32)
        if l == 0:
            ex, token = start(l, "_out", ["out"], [g_w_out])
            exchanges.append(ex)
        dmix = _bwd_out_call(dy, w_out_l, token)
        ws_tril_t = jnp.swapaxes(ws_tril, 1, 2)
        dproj, dwq, dwk, dsk, dws, dbs = _bwd_mix_call(
            proj, dmix, bias, wq2, wk2, b2, sinks[l], ws_tril, ws_tril_t, b_exp)
        dwq, dwk = dwq[:, :HD] + dwq[:, HD:], dwk[:, :HD] + dwk[:, HD:]
        g_ws = dws.reshape(NG * WIN, WIN).astype(MM)
        g_tiny = _pack_small([dwq, dwk, dsk[:, 0], dbs]).astype(MM)
        token = jnp.zeros((8, 128), F32)
        if l == 0:
            ex, token = start(l, "_small", ["ws", "tiny"], [g_ws, g_tiny])
            exchanges.append(ex)
        g_w_in = _grad_w_in_call(h, dproj, token)
        if l == 0:
            ex, token = start(l, "", ["in"], [g_w_in])
        else:
            ex, token = start(l, "", ["in", "out", "ws", "tiny"], [g_w_in, g_w_out, g_ws, g_tiny])
        exchanges.append(ex)
        dy, d_norm_g[l] = _bwd_in_call(dproj, w_in_l, xs, dy, norm_g[l:l + 1], token)
        if pending is not None:
            finish(pending, dy)
        pending = (l, exchanges)
    finish(pending, dy)
    grad_x = dy

    swap_srcs, swap_lands, sw_ssem, sw_rsem, _ = _grad_start_call([s_in, s_out], "grad_swap_start", to_sibling=True)
    g_norm_g = _small_allreduce_call(jnp.concatenate(d_norm_g, axis=0))

    def pack_layers(parts):
        return jnp.concatenate([_pack_small([p[l] for p in parts]) for l in range(L)], axis=0)

    ws_rows = (L * NG * WIN, WIN)
    ws_outs = _adam_call(w_s.reshape(ws_rows), [jnp.concatenate(ws_sums, axis=0)],
                         m_w_s.reshape(ws_rows), v_w_s.reshape(ws_rows), "adam_w_s")
    tiny_like = [q_norm, k_norm, sinks, b_s]
    tiny_outs = _adam_call(
        pack_layers(tiny_like), [jnp.concatenate(tiny_sums, axis=0)],
        pack_layers([m_q_norm, m_k_norm, m_sinks, m_b_s]),
        pack_layers([v_q_norm, v_k_norm, v_sinks, v_b_s]), "adam_tiny")
    norm_outs = _adam_call(norm_g, [g_norm_g], m_norm_g, v_norm_g, "adam_norm_g")

    small_done = ws_outs[1][:8] + tiny_outs[1][:8] + norm_outs[1][:1, :WIN]
    (s_in, s_out), (t_in, t_out) = _grad_wait_call(swap_srcs, swap_lands, sw_ssem, sw_rsem, small_done,
                                                   "grad_swap_wait", to_sibling=True)
    g_w_in, d_in, nm_in, nv_in = _adam_call(
        w_in.reshape(L * D, SHW), [s_in.reshape(L * D, SHW), t_in.reshape(L * D, SHW)],
        m_w_in.reshape(L * D, SHW), v_w_in.reshape(L * D, SHW), "adam_w_in")
    g_w_out, d_out, nm_out, nv_out = _adam_call(
        w_out.reshape(L * SHR, D), [s_out.reshape(L * SHR, D), t_out.reshape(L * SHR, D)],
        m_w_out.reshape(L * SHR, D), v_w_out.reshape(L * SHR, D), "adam_w_out")

    def full(i, win, wout):
        tiny = tiny_outs[i]
        rows = tiny.shape[0] // L
        per_layer = [_unpack_small(tiny[l * rows:(l + 1) * rows], [p[l] for p in tiny_like]) for l in range(L)]
        qn, kn, sk, bs = [jnp.stack([per_layer[l][k] for l in range(L)]) for k in range(4)]
        return [norm_outs[i], win.reshape(w_in.shape), qn, kn, sk, ws_outs[i].reshape(w_s.shape), bs,
                wout.reshape(w_out.shape)]

    loss_all = lax.psum(loss[0, 0], ("x", "y", "c"))
    return (loss_all, grad_x[None], *full(0, g_w_in, g_w_out), *full(1, d_in, d_out),
            *full(2, nm_in, nm_out), *full(3, nv_in, nv_out))
```

```python
import functools
import math

import numpy as np
import jax
import jax.numpy as jnp
from jax import lax
from jax.experimental import pallas as pl
from jax.experimental.pallas import tpu as pltpu

F32 = jnp.float32
MM = jnp.bfloat16

D = 2048
HD = 64
DA = 1024
DKV = 256
DG = 1024
NQ, NKV, GRP, NG = 16, 4, 4, 16
WIN = 128
DIN = 5632
C_Q, C_K, C_V, C_GA, C_U, C_VS, C_GB = 0, 1024, 1280, 1536, 2560, 3584, 4608
NCHIP = 4
SHW = DIN // NCHIP
SHR = D // NCHIP
EPS = 1e-6
NEG = -1e30
SCALE = HD ** -0.5
INV_SQRT2 = 1.0 / math.sqrt(2.0)
INV_SQRT_2PI = 1.0 / math.sqrt(2.0 * math.pi)
LR, B1, B2, ADAM_EPS, WD, STEP = 0.001, 0.9, 0.999, 1e-08, 0.01, 10
VMEM_LIMIT = 56 * 1024 * 1024

SDS = jax.ShapeDtypeStruct
NT = (((1,), (1,)), ((), ()))
TN = (((0,), (0,)), ((), ()))


def _cp(sem=None):
    return pltpu.CompilerParams(dimension_semantics=sem, vmem_limit_bytes=VMEM_LIMIT)


def _sigmoid(x):
    return 1.0 / (1.0 + jnp.exp(-x))


def _gelu(x):
    return 0.5 * x * (1.0 + lax.erf(x * INV_SQRT2))


def _gelu_and_grad(x):
    cdf = 0.5 * (1.0 + lax.erf(x * INV_SQRT2))
    return x * cdf, cdf + x * jnp.exp(-0.5 * x * x) * INV_SQRT_2PI


def _alibi_bias():
    slopes = 2.0 ** (-8.0 * np.arange(1, NQ + 1) / NQ)
    dist = (np.arange(WIN)[:, None] + WIN) - np.arange(2 * WIN)[None, :]
    ok = (dist >= 0) & (dist < WIN)
    first = ok & (np.arange(2 * WIN)[None, :] >= WIN)
    val = -slopes[:, None, None] * dist[None].astype(np.float64)
    return jnp.asarray(np.stack([np.where(first[None], val, NEG), np.where(ok[None], val, NEG)]), dtype=F32)


def _half_sum_matrix():
    half = np.arange(LANE) // HD
    return jnp.asarray(half[:, None] == half[None, :], dtype=MM)


LANE = 128
NQT = DA // LANE
NKT = DKV // LANE


def _tiles(ref, c0, n):
    return jnp.concatenate([ref[:, c0 + j * LANE:c0 + (j + 1) * LANE] for j in range(n)], axis=0)


def _split(x):
    hi = x.astype(MM)
    return hi, (x - hi.astype(F32)).astype(MM)


def _half_sums(x, b2):
    hi, lo = _split(x)
    return jnp.dot(hi, b2, preferred_element_type=F32) + jnp.dot(lo, b2, preferred_element_type=F32)


def _attn_fwd(pm_ref, kvp_ref, bias_ref, wq2, wk2, b2, sink_ref):
    lo_half = lax.broadcasted_iota(jnp.int32, (1, LANE), 1) < HD
    q_ts = _tiles(pm_ref, C_Q, NQT)
    rq = lax.rsqrt(_half_sums(q_ts * q_ts, b2) * (1.0 / HD) + EPS)
    qs = (q_ts * rq * wq2).astype(MM)
    k_ts = jnp.concatenate([a[:, c0 + t * LANE:c0 + (t + 1) * LANE] for t in range(NKT)
                            for a, c0 in ((kvp_ref, 0), (pm_ref, C_K))], axis=0)
    rk = lax.rsqrt(_half_sums(k_ts * k_ts, b2) * (1.0 / HD) + EPS)
    kn = (k_ts * rk * wk2).astype(MM)
    v_ts = jnp.concatenate([a[:, c0 + t * LANE:c0 + (t + 1) * LANE] for t in range(NKT)
                            for a, c0 in ((kvp_ref, DKV), (pm_ref, C_V))], axis=0).astype(MM)
    ones = jnp.ones((2 * WIN, LANE), MM)
    km, vm = {}, {}
    for hk in range(NKV):
        t, eh = hk // 2, hk % 2
        sel = lo_half if eh == 0 else jnp.logical_not(lo_half)
        rows = slice(t * 2 * WIN, (t + 1) * 2 * WIN)
        k_same = jnp.where(sel, kn[rows], jnp.zeros_like(kn[rows]))
        v_same = jnp.where(sel, v_ts[rows], jnp.zeros_like(v_ts[rows]))
        km[hk, eh], km[hk, 1 - eh] = k_same, pltpu.roll(k_same, HD, axis=1)
        vm[hk, eh], vm[hk, 1 - eh] = v_same, pltpu.roll(v_same, HD, axis=1)
    hs = range(NQ)
    s = [lax.dot_general(qs[(h // 2) * WIN:(h // 2 + 1) * WIN], km[h // GRP, h % 2], NT, preferred_element_type=F32)
         + bias_ref[0, h] for h in hs]
    m = [jnp.maximum(jnp.max(s[h], axis=-1, keepdims=True), sink_ref[h]) for h in hs]
    p = [jnp.exp(s[h] - m[h]) for h in hs]
    pb = [p[h].astype(MM) for h in hs]
    res = [jnp.dot(pb[h], jnp.concatenate([vm[h // GRP, h % 2], ones], axis=1), preferred_element_type=F32) for h in hs]
    esink = [jnp.exp(sink_ref[h] - m[h]) for h in hs]
    inv = [1.0 / (res[h][:, LANE:] + esink[h]) for h in hs]
    heads = [dict(p=p[h], pb=pb[h], inv=inv[h], esink=esink[h], o=res[h][:, :LANE] * inv[h]) for h in hs]
    return dict(lo_half=lo_half, q_ts=q_ts, rq=rq, qs=qs, k_ts=k_ts, rk=rk, km=km, vm=vm, heads=heads)


def _sgu_mix(w_ref, zt, lo_half, j):
    zero = jnp.zeros_like(zt)
    return (jnp.dot(w_ref[2 * j], jnp.where(lo_half, zt, zero), preferred_element_type=F32)
            + jnp.dot(w_ref[2 * j + 1], jnp.where(lo_half, zero, zt), preferred_element_type=F32))


def _fwd_mix_call(proj, bias, wq2, wk2, b2, sinks, ws_tril, b_exp):
    T = proj.shape[0]
    nb = T // WIN

    def body(sink_ref, pm_ref, kvp_ref, bias_ref, wq_ref, wk_ref, b2_ref, ws_ref, be_ref, mix_ref):
        lo_half = lax.broadcasted_iota(jnp.int32, (1, LANE), 1) < HD
        zu = _gelu(pm_ref[:, C_U:C_U + DG])
        zv = _gelu(pm_ref[:, C_VS:C_VS + DG]).astype(MM)
        mixed = jnp.concatenate(
            [_sgu_mix(ws_ref, zv[:, j * LANE:(j + 1) * LANE], lo_half, j) for j in range(NG // 2)], axis=1)
        mixed = mixed + be_ref[...]
        gb = pm_ref[:, C_GB:C_GB + DG]
        mix_ref[:, DA:DA + DG] = (zu * mixed * (gb * _sigmoid(gb))).astype(MM)
        a = _attn_fwd(pm_ref, kvp_ref, bias_ref, wq_ref[...], wk_ref[...], b2_ref[...], sink_ref)
        for j in range(NQT):
            cols = slice(j * LANE, (j + 1) * LANE)
            ga = pm_ref[:, C_GA + j * LANE:C_GA + (j + 1) * LANE]
            attn = a["heads"][2 * j]["o"] + a["heads"][2 * j + 1]["o"]
            mix_ref[:, cols] = (attn * (ga * _sigmoid(ga))).astype(MM)

    return pl.pallas_call(
        body,
        grid=(nb,),
        in_specs=[
            pl.BlockSpec(memory_space=pltpu.SMEM),
            pl.BlockSpec((WIN, DIN), lambda n: (n, 0)),
            pl.BlockSpec((WIN, 2 * DKV), lambda n: (jnp.maximum(n - 1, 0), C_K // (2 * DKV))),
            pl.BlockSpec((1, NQ, WIN, 2 * WIN), lambda n: (jnp.minimum(n, 1), 0, 0, 0)),
            pl.BlockSpec((1, LANE), lambda n: (0, 0)),
            pl.BlockSpec((1, LANE), lambda n: (0, 0)),
            pl.BlockSpec((LANE, LANE), lambda n: (0, 0)),
            pl.BlockSpec((NG, WIN, WIN), lambda n: (0, 0, 0)),
            pl.BlockSpec((WIN, DG), lambda n: (0, 0)),
        ],
        out_specs=pl.BlockSpec((WIN, D), lambda n: (n, 0)),
        out_shape=SDS((T, D), MM),
        name="fwd_mix",
        compiler_params=_cp(("arbitrary",)),
    )(sinks, proj, proj, bias, wq2, wk2, b2, ws_tril, b_exp)


def _bwd_mix_call(proj, dmix, bias, wq2, wk2, b2, sinks, ws_tril, ws_tril_t, b_exp):
    T = proj.shape[0]
    nb = T // WIN

    def body(sink_ref, pm_ref, kvp_ref, dm_ref, bias_ref, wq_ref, wk_ref, b2_ref, ws_ref, wst_ref, be_ref,
             dp_ref, dwq_ref, dwk_ref, dsk_ref, dws_ref, dbs_ref, carry_ref, dbacc_ref):
        n = pl.program_id(0)

        @pl.when(n == 0)
        def _():
            carry_ref[...] = jnp.zeros_like(carry_ref)
            dbacc_ref[...] = jnp.zeros_like(dbacc_ref)
            dwq_ref[...] = jnp.zeros_like(dwq_ref)
            dwk_ref[...] = jnp.zeros_like(dwk_ref)
            dsk_ref[...] = jnp.zeros_like(dsk_ref)
            dws_ref[...] = jnp.zeros_like(dws_ref)
            dbs_ref[...] = jnp.zeros_like(dbs_ref)

        @pl.when(n < nb)
        def _():
            dp_ref[:, C_Q:C_K] = carry_ref[:, C_Q:C_K].astype(MM)
            dp_ref[:, C_GA:DIN] = carry_ref[:, C_GA:DIN].astype(MM)

            lo_half = lax.broadcasted_iota(jnp.int32, (1, LANE), 1) < HD
            u = pm_ref[:, C_U:C_U + DG]
            vs = pm_ref[:, C_VS:C_VS + DG]
            gb = pm_ref[:, C_GB:C_GB + DG]
            zu, dzu = _gelu_and_grad(u)
            zv, dzv = _gelu_and_grad(vs)
            zvb = zv.astype(MM)
            mixed = jnp.concatenate(
                [_sgu_mix(ws_ref, zvb[:, j * LANE:(j + 1) * LANE], lo_half, j) for j in range(NG // 2)], axis=1)
            mixed = mixed + be_ref[...]
            sgb = _sigmoid(gb)
            d_sgu = dm_ref[:, DA:DA + DG]
            carry_ref[:, C_GB:DIN] = d_sgu * zu * mixed * (sgb * (1.0 + gb * (1.0 - sgb)))
            d_mixed = d_sgu * zu * (gb * sgb)
            carry_ref[:, C_U:C_VS] = d_sgu * mixed * (gb * sgb) * dzu
            dbacc_ref[...] += d_mixed
            dmb = d_mixed.astype(MM)
            dzv_tiles = [_sgu_mix(wst_ref, dmb[:, j * LANE:(j + 1) * LANE], lo_half, j) for j in range(NG // 2)]
            carry_ref[:, C_VS:C_GB] = jnp.concatenate(dzv_tiles, axis=1) * dzv
            for j in range(NG // 2):
                dt = dmb[:, j * LANE:(j + 1) * LANE]
                zt = zvb[:, j * LANE:(j + 1) * LANE]
                zero = jnp.zeros_like(dt)
                dws_ref[2 * j] += lax.dot_general(jnp.where(lo_half, dt, zero), zt, NT, preferred_element_type=F32)
                dws_ref[2 * j + 1] += lax.dot_general(jnp.where(lo_half, zero, dt), zt, NT, preferred_element_type=F32)

            wq2, wk2, b2 = wq_ref[...], wk_ref[...], b2_ref[...]
            a = _attn_fwd(pm_ref, kvp_ref, bias_ref, wq2, wk2, b2, sink_ref)
            heads, km, vm, qs = a["heads"], a["km"], a["vm"], a["qs"]

            row_lo = lax.broadcasted_iota(jnp.int32, (LANE, LANE), 0) < HD
            pick = [jnp.where(row_lo, 1.0, 0.0).astype(MM), jnp.where(row_lo, 0.0, 1.0).astype(MM)]
            chan_lo = lax.broadcasted_iota(jnp.int32, (LANE, 1), 0) < HD
            tiles, hs = range(NQT), range(NQ)
            sel_t = [chan_lo, jnp.logical_not(chan_lo)]
            d_o, attn = [], []
            for j in tiles:
                cols = slice(C_GA + j * LANE, C_GA + (j + 1) * LANE)
                ga = pm_ref[:, cols]
                sga = _sigmoid(ga)
                d_gated = dm_ref[:, j * LANE:(j + 1) * LANE]
                attn.append(heads[2 * j]["o"] + heads[2 * j + 1]["o"])
                carry_ref[:, cols] = d_gated * attn[j] * (sga * (1.0 + ga * (1.0 - sga)))
                d_o.append(d_gated * (ga * sga))
            d_ob = [d_o[j].astype(MM) for j in tiles]
            dlt = [(d_o[j] * attn[j]).astype(MM) for j in tiles]
            d_os_t = [(d_o[j] * jnp.where(lo_half, heads[2 * j]["inv"], heads[2 * j + 1]["inv"])).astype(MM).T
                      for j in tiles]
            qs_t = [qs[j * WIN:(j + 1) * WIN].T for j in tiles]
            zero_t = jnp.zeros_like(qs_t[0])
            dv_h = [jnp.dot(jnp.where(sel_t[h % 2], d_os_t[h // 2], zero_t), heads[h]["pb"], preferred_element_type=F32)
                    for h in hs]
            d_p = [lax.dot_general(d_ob[h // 2], vm[h // GRP, h % 2], NT, preferred_element_type=F32) for h in hs]
            delta = [jnp.dot(dlt[h // 2], pick[h % 2], preferred_element_type=F32) for h in hs]
            for h in hs:
                dsk_ref[h:h + 1, :] -= jnp.sum(heads[h]["esink"] * heads[h]["inv"] * delta[h], axis=0, keepdims=True)
            d_s = [(heads[h]["p"] * ((d_p[h] - jnp.concatenate([delta[h], delta[h]], axis=1))
                                     * jnp.concatenate([heads[h]["inv"], heads[h]["inv"]], axis=1))).astype(MM)
                   for h in hs]
            dqs_h = [jnp.dot(d_s[h], km[h // GRP, h % 2], preferred_element_type=F32) for h in hs]
            dqs_tiles = [dqs_h[2 * j] + dqs_h[2 * j + 1] for j in tiles]
            dk_h = [jnp.dot(jnp.where(sel_t[h % 2], qs_t[h // 2], zero_t), d_s[h], preferred_element_type=F32)
                    for h in hs]
            dk_acc, dv_acc = {}, {}
            for h in hs:
                key = (h // GRP, h % 2 == (h // GRP) % 2)
                dk_acc[key] = dk_h[h] if key not in dk_acc else dk_acc[key] + dk_h[h]
                dv_acc[key] = dv_h[h] if key not in dv_acc else dv_acc[key] + dv_h[h]

            dqs_ts = jnp.concatenate(dqs_tiles, axis=0)
            q_ts, rq = a["q_ts"], a["rq"]
            gq = dqs_ts * wq2
            d_q = rq * gq - q_ts * (rq * rq * rq) * (_half_sums(gq * q_ts, b2) * (1.0 / HD))
            dwq_ref[...] += SCALE * jnp.sum(dqs_ts * q_ts * rq, axis=0, keepdims=True)
            for j in range(NQT):
                carry_ref[:, C_Q + j * LANE:C_Q + (j + 1) * LANE] = d_q[j * WIN:(j + 1) * WIN]

            def swap_halves(xt):
                return jnp.concatenate([xt[HD:], xt[:HD]], axis=0)

            dkn_tiles, dv_tiles = [], []
            for t in range(NKT):
                for acc, out in ((dk_acc, dkn_tiles), (dv_acc, dv_tiles)):
                    parts = [acc[hk, True] + swap_halves(acc[hk, False]) for hk in (2 * t, 2 * t + 1)]
                    out.append((parts[0] + parts[1]).T)
            dkn_ts = jnp.concatenate(dkn_tiles, axis=0)
            dv_ts = jnp.concatenate(dv_tiles, axis=0)
            k_ts, rk = a["k_ts"], a["rk"]
            gk = dkn_ts * wk2
            d_k = rk * gk - k_ts * (rk * rk * rk) * (_half_sums(gk * k_ts, b2) * (1.0 / HD))
            dwk_ref[...] += jnp.sum(dkn_ts * k_ts * rk, axis=0, keepdims=True)
            for t in range(NKT):
                for base, val in ((C_K, d_k), (C_V, dv_ts)):
                    cols = slice(base + t * LANE, base + (t + 1) * LANE)
                    r0 = t * 2 * WIN
                    dp_ref[:, cols] = (carry_ref[:, cols] + val[r0:r0 + WIN]).astype(MM)
                    carry_ref[:, cols] = val[r0 + WIN:r0 + 2 * WIN]

        @pl.when(n == nb)
        def _():
            dp_ref[...] = carry_ref[...].astype(MM)
            lo_half = lax.broadcasted_iota(jnp.int32, (8, LANE), 1) < HD
            ones = [jnp.where(lo_half, 1.0, 0.0).astype(MM), jnp.where(lo_half, 0.0, 1.0).astype(MM)]
            hi, lo = _split(dbacc_ref[...])
            for h in range(NG):
                sl = slice((h // 2) * LANE, (h // 2 + 1) * LANE)
                r = (lax.dot_general(ones[h % 2], hi[:, sl], NT, preferred_element_type=F32)
                     + lax.dot_general(ones[h % 2], lo[:, sl], NT, preferred_element_type=F32))
                dbs_ref[h:h + 1, :] = r[0:1, :]
            row = lax.broadcasted_iota(jnp.int32, (WIN, WIN), 0)
            cl = lax.broadcasted_iota(jnp.int32, (WIN, WIN), 1)
            for h in range(NG):
                dws_ref[h] = jnp.where(row >= cl, dws_ref[h], 0.0)

    last = nb - 1
    return pl.pallas_call(
        body,
        grid_spec=pltpu.PrefetchScalarGridSpec(
            num_scalar_prefetch=0,
            grid=(nb + 1,),
            in_specs=[
                pl.BlockSpec(memory_space=pltpu.SMEM),
                pl.BlockSpec((WIN, DIN), lambda n: (jnp.minimum(n, last), 0)),
                pl.BlockSpec((WIN, 2 * DKV), lambda n: (jnp.maximum(jnp.minimum(n, last) - 1, 0), C_K // (2 * DKV))),
                pl.BlockSpec((WIN, D), lambda n: (jnp.minimum(n, last), 0)),
                pl.BlockSpec((1, NQ, WIN, 2 * WIN), lambda n: (jnp.minimum(n, 1), 0, 0, 0)),
                pl.BlockSpec((1, LANE), lambda n: (0, 0)),
                pl.BlockSpec((1, LANE), lambda n: (0, 0)),
                pl.BlockSpec((LANE, LANE), lambda n: (0, 0)),
                pl.BlockSpec((NG, WIN, WIN), lambda n: (0, 0, 0)),
                pl.BlockSpec((NG, WIN, WIN), lambda n: (0, 0, 0)),
                pl.BlockSpec((WIN, DG), lambda n: (0, 0)),
            ],
            out_specs=[
                pl.BlockSpec((WIN, DIN), lambda n: (jnp.maximum(n - 1, 0), 0)),
                pl.BlockSpec((1, LANE), lambda n: (0, 0)),
                pl.BlockSpec((1, LANE), lambda n: (0, 0)),
                pl.BlockSpec((NQ, WIN), lambda n: (0, 0)),
                pl.BlockSpec((NG, WIN, WIN), lambda n: (0, 0, 0)),
                pl.BlockSpec((NG, WIN), lambda n: (0, 0)),
            ],
            scratch_shapes=[pltpu.VMEM((WIN, DIN), F32), pltpu.VMEM((WIN, DG), F32)],
        ),
        out_shape=[SDS((T, DIN), MM), SDS((1, LANE), F32), SDS((1, LANE), F32), SDS((NQ, WIN), F32),
                   SDS((NG, WIN, WIN), F32), SDS((NG, WIN), F32)],
        name="bwd_mix",
        compiler_params=_cp(("arbitrary",)),
    )(sinks, proj, proj, dmix, bias, wq2, wk2, b2, ws_tril, ws_tril_t, b_exp)


WEIGHT_RESIDENT_ROWS = 256
GRAD_TOKEN_TILE = 1024


def _row_tile(T):
    return min(512, T)


def _fwd_in_call(x, g_row, w_sh):
    T = x.shape[0]
    tm = min(WEIGHT_RESIDENT_ROWS, T)

    def body(x_ref, g_ref, w_hbm, proj_ref, h_ref, w_vmem, sem):
        @pl.when(pl.program_id(0) == 0)
        def _():
            cps = [pltpu.make_async_copy(w_hbm.at[j], w_vmem.at[:, pl.ds(j * SHW, SHW)], sem.at[j]) for j in range(NCHIP)]
            for cp in cps:
                cp.start()
            for cp in cps:
                cp.wait()

        xv = x_ref[...]
        r = lax.rsqrt(jnp.mean(xv * xv, axis=-1, keepdims=True) + EPS)
        h = (xv * r * g_ref[...]).astype(MM)
        h_ref[...] = h
        proj_ref[...] = jnp.dot(h, w_vmem[...], preferred_element_type=F32)

    return pl.pallas_call(
        body,
        grid=(T // tm,),
        in_specs=[pl.BlockSpec((tm, D), lambda i: (i, 0)),
                  pl.BlockSpec((1, D), lambda i: (0, 0)),
                  pl.BlockSpec(memory_space=pl.ANY)],
        out_specs=[pl.BlockSpec((tm, DIN), lambda i: (i, 0)),
                   pl.BlockSpec((tm, D), lambda i: (i, 0))],
        out_shape=[SDS((T, DIN), F32), SDS((T, D), MM)],
        scratch_shapes=[pltpu.VMEM((D, DIN), MM), pltpu.SemaphoreType.DMA((NCHIP,))],
        name="fwd_in",
        compiler_params=_cp(("arbitrary",)),
    )(x, g_row, w_sh)


def _fwd_out_call(x, mix, w_out):
    T = x.shape[0]
    tm = _row_tile(T)

    def body(x_ref, mix_ref, w_ref, y_ref):
        y_ref[...] = x_ref[...] + jnp.dot(mix_ref[...], w_ref[...], preferred_element_type=F32)

    return pl.pallas_call(
        body,
        grid=(T // tm,),
        in_specs=[pl.BlockSpec((tm, D), lambda i: (i, 0)),
                  pl.BlockSpec((tm, D), lambda i: (i, 0)),
                  pl.BlockSpec((D, D), lambda i: (0, 0))],
        out_specs=pl.BlockSpec((tm, D), lambda i: (i, 0)),
        out_shape=SDS((T, D), F32),
        name="fwd_out",
        compiler_params=_cp(("arbitrary",)),
    )(x, mix, w_out)


def _fwd_out_loss_call(x, mix, w_out, target):
    T = x.shape[0]
    tm = _row_tile(T)

    def body(x_ref, mix_ref, w_ref, t_ref, dy_ref, loss_ref):
        @pl.when(pl.program_id(0) == 0)
        def _():
            loss_ref[...] = jnp.zeros_like(loss_ref)

        e = x_ref[...] + jnp.dot(mix_ref[...], w_ref[...], preferred_element_type=F32) - t_ref[...]
        dy_ref[...] = e * (1.0 / D)
        loss_ref[...] += (0.5 / D) * jnp.sum(jnp.sum(e * e, axis=1, keepdims=True), axis=0, keepdims=True)

    return pl.pallas_call(
        body,
        grid=(T // tm,),
        in_specs=[pl.BlockSpec((tm, D), lambda i: (i, 0)),
                  pl.BlockSpec((tm, D), lambda i: (i, 0)),
                  pl.BlockSpec((D, D), lambda i: (0, 0)),
                  pl.BlockSpec((tm, D), lambda i: (i, 0))],
        out_specs=[pl.BlockSpec((tm, D), lambda i: (i, 0)),
                   pl.BlockSpec((1, 1), lambda i: (0, 0))],
        out_shape=[SDS((T, D), F32), SDS((1, 1), F32)],
        name="fwd_out_loss",
        compiler_params=_cp(("arbitrary",)),
    )(x, mix, w_out, target)


def _bwd_out_call(dy, w_out, token):
    T = dy.shape[0]
    tm = _row_tile(T)

    def body(dy_ref, w_ref, token_ref, o_ref):
        o_ref[...] = lax.dot_general(dy_ref[...].astype(MM), w_ref[...], NT, preferred_element_type=F32)

    return pl.pallas_call(
        body,
        grid=(T // tm,),
        in_specs=[pl.BlockSpec((tm, D), lambda i: (i, 0)),
                  pl.BlockSpec((D, D), lambda i: (0, 0)),
                  pl.BlockSpec(memory_space=pl.ANY)],
        out_specs=pl.BlockSpec((tm, D), lambda i: (i, 0)),
        out_shape=SDS((T, D), F32),
        name="bwd_out",
        compiler_params=_cp(("arbitrary",)),
    )(dy, w_out, token)


def _bwd_in_call(dproj, w_sh, x, dy, g_row, token):
    T = x.shape[0]
    tm = min(WEIGHT_RESIDENT_ROWS, T)

    def body(dp_ref, w_hbm, x_ref, dy_ref, g_ref, token_ref, dx_ref, dg_ref, w_vmem, sem):
        @pl.when(pl.program_id(0) == 0)
        def _():
            cps = [pltpu.make_async_copy(w_hbm.at[j], w_vmem.at[:, pl.ds(j * SHW, SHW)], sem.at[j]) for j in range(NCHIP)]
            for cp in cps:
                cp.start()
            dg_ref[...] = jnp.zeros_like(dg_ref)
            for cp in cps:
                cp.wait()

        dh = lax.dot_general(dp_ref[...], w_vmem[...], NT, preferred_element_type=F32)
        xv = x_ref[...]
        r = lax.rsqrt(jnp.mean(xv * xv, axis=-1, keepdims=True) + EPS)
        gd = dh * g_ref[...]
        dx_ref[...] = dy_ref[...] + r * gd - xv * ((r * r * r) * jnp.mean(gd * xv, axis=-1, keepdims=True))
        dg_ref[...] += jnp.sum(dh * xv * r, axis=0, keepdims=True)

    return pl.pallas_call(
        body,
        grid=(T // tm,),
        in_specs=[pl.BlockSpec((tm, DIN), lambda i: (i, 0)),
                  pl.BlockSpec(memory_space=pl.ANY),
                  pl.BlockSpec((tm, D), lambda i: (i, 0)),
                  pl.BlockSpec((tm, D), lambda i: (i, 0)),
                  pl.BlockSpec((1, D), lambda i: (0, 0)),
                  pl.BlockSpec(memory_space=pl.ANY)],
        out_specs=[pl.BlockSpec((tm, D), lambda i: (i, 0)),
                   pl.BlockSpec((1, D), lambda i: (0, 0))],
        out_shape=[SDS((T, D), F32), SDS((1, D), F32)],
        scratch_shapes=[pltpu.VMEM((D, DIN), MM), pltpu.SemaphoreType.DMA((NCHIP,))],
        name="bwd_in",
        compiler_params=_cp(("arbitrary",)),
    )(dproj, w_sh, x, dy, g_row, token)


def _grad_w_in_call(h, dproj, token):
    T = h.shape[0]
    tt = min(GRAD_TOKEN_TILE, T)
    nt = T // tt
    dh = D // 2

    def body(h_ref, dp_ref, token_ref, o_ref, acc_ref):
        t = pl.program_id(2)

        @pl.when(t == 0)
        def _():
            acc_ref[...] = jnp.zeros_like(acc_ref)

        acc_ref[...] += lax.dot_general(h_ref[...], dp_ref[...], TN, preferred_element_type=F32)

        @pl.when(t == nt - 1)
        def _():
            o_ref[0] = acc_ref[:, 0:SHW].astype(MM)
            o_ref[1] = acc_ref[:, SHW:2 * SHW].astype(MM)

    return pl.pallas_call(
        body,
        grid=(NCHIP // 2, 2, nt),
        in_specs=[pl.BlockSpec((tt, dh), lambda b, m, t: (t, m)),
                  pl.BlockSpec((tt, 2 * SHW), lambda b, m, t: (t, b)),
                  pl.BlockSpec(memory_space=pl.ANY)],
        out_specs=pl.BlockSpec((2, dh, SHW), lambda b, m, t: (b, m, 0)),
        out_shape=SDS((NCHIP, D, SHW), MM),
        scratch_shapes=[pltpu.VMEM((dh, 2 * SHW), F32)],
        name="grad_w_in",
        compiler_params=_cp(("arbitrary", "arbitrary", "arbitrary")),
    )(h, dproj, token)


def _grad_w_out_call(mix, dy):
    T = mix.shape[0]
    tt = min(GRAD_TOKEN_TILE, T)
    nt = T // tt
    tn = 1024

    def body(m_ref, dy_ref, o_ref, acc_ref):
        t = pl.program_id(1)

        @pl.when(t == 0)
        def _():
            acc_ref[...] = jnp.zeros_like(acc_ref)

        acc_ref[...] += lax.dot_general(m_ref[...], dy_ref[...].astype(MM), TN, preferred_element_type=F32)

        @pl.when(t == nt - 1)
        def _():
            o_ref[...] = acc_ref[...].astype(MM)

    return pl.pallas_call(
        body,
        grid=(D // tn, nt),
        in_specs=[pl.BlockSpec((tt, D), lambda j, t: (t, 0)),
                  pl.BlockSpec((tt, tn), lambda j, t: (t, j))],
        out_specs=pl.BlockSpec((D, tn), lambda j, t: (0, j)),
        out_shape=SDS((D, D), MM),
        scratch_shapes=[pltpu.VMEM((D, tn), F32)],
        name="grad_w_out",
        compiler_params=_cp(("arbitrary", "arbitrary")),
    )(mix, dy)


def _cast_to_slab_call(w, chip_idx, layers, after, name):
    _, R, C = w.shape
    n = len(layers)
    tr = 256

    def body(chip_ref, *refs):
        for k in range(n):
            refs[n + 1 + k][...] = refs[k][...].astype(MM)

    return pl.pallas_call(
        body,
        grid_spec=pltpu.PrefetchScalarGridSpec(
            num_scalar_prefetch=1,
            grid=(R // tr,),
            in_specs=[pl.BlockSpec((1, tr, C), functools.partial(lambda i, chip_ref, l: (l, i, 0), l=l))
                      for l in layers] + [ANY],
            out_specs=[pl.BlockSpec((1, tr, C), lambda i, chip_ref: (chip_ref[0], i, 0))] * n,
        ),
        out_shape=[SDS((NCHIP, R, C), MM)] * n,
        name=name,
        compiler_params=_cp(("arbitrary",)),
    )(chip_idx, *([w] * n), after)


def _adam_call(w, g_parts, m, v, name):
    R, C = w.shape
    tr = R
    for cand in (512, 256, 128, 64, 32, 16, 8):
        if R % cand == 0 and cand * C * 4 <= 1024 * 1024:
            tr = cand
            break
    c1 = 1.0 - B1 ** STEP
    c2 = 1.0 - B2 ** STEP
    ng = len(g_parts)

    def body(*refs):
        w_ref, m_ref, v_ref = refs[0], refs[1 + ng], refs[2 + ng]
        g_ref, d_ref, nm_ref, nv_ref = refs[3 + ng:]
        gv = refs[1][...].astype(F32)
        for k in range(1, ng):
            gv = gv + refs[1 + k][...].astype(F32)
        nm = B1 * m_ref[...] + (1.0 - B1) * gv
        nv = B2 * v_ref[...] + (1.0 - B2) * (gv * gv)
        g_ref[...] = gv
        nm_ref[...] = nm
        nv_ref[...] = nv
        d_ref[...] = -LR * ((nm / c1) / (jnp.sqrt(nv / c2) + ADAM_EPS) + WD * w_ref[...])

    spec = pl.BlockSpec((tr, C), lambda i: (i, 0))
    return pl.pallas_call(
        body,
        grid=(R // tr,),
        in_specs=[spec] * (3 + ng),
        out_specs=[spec] * 4,
        out_shape=[SDS((R, C), F32)] * 4,
        name=name,
        compiler_params=_cp(("arbitrary",)),
    )(w, *g_parts, m, v)


MESH = pl.DeviceIdType.MESH
ANY = pl.BlockSpec(memory_space=pl.ANY)
HBM = pl.BlockSpec(memory_space=pltpu.HBM)
SEMS = pl.BlockSpec(memory_space=pltpu.SEMAPHORE)
EFFECT = pltpu.SideEffectType.DATAFLOW_SIDE_EFFECTING
NDEV = 8


def _hbm(a):
    return pltpu.with_memory_space_constraint(a, pltpu.HBM)


def _place():
    x, y, c = lax.axis_index("x"), lax.axis_index("y"), lax.axis_index("c")
    others = [(1 - x, y), (x, 1 - y), (1 - x, 1 - y)]
    return x, y, c, 2 * x + y, others


def _flipped(x, y, c, r):
    return (1 - x if r & 4 else x, 1 - y if r & 2 else y, 1 - c if r & 1 else c)


def _rcopy(src, dst, ssem, rsem, dev):
    return pltpu.make_async_remote_copy(src_ref=src, dst_ref=dst, send_sem=ssem, recv_sem=rsem,
                                        device_id=dev, device_id_type=MESH)


def _slab(ref, chip, c, halved):
    if not halved:
        return ref.at[chip]
    h = ref.shape[1] // 2
    return ref.at[chip, pl.ds(c * h, h), :]


def _gather_start_call(fulls, n_halved, name):
    K = len(fulls)

    def body(*refs):
        full, ssem, rsem = refs[:K], refs[K:2 * K], refs[2 * K:3 * K]
        x, y, c, me, others = _place()
        for k in range(K):
            for j, (px, py) in enumerate(others):
                part = _slab(full[k], me, c, k < n_halved)
                _rcopy(part, part, ssem[k].at[j], rsem[k].at[j], (px, py, c)).start()

    outs = pl.pallas_call(
        body,
        in_specs=[HBM] * K,
        out_specs=[SEMS] * (2 * K) + [HBM] * K,
        out_shape=[pltpu.SemaphoreType.DMA((3,))] * (2 * K) + [pltpu.HBM(f.shape, f.dtype) for f in fulls],
        input_output_aliases={k: 2 * K + k for k in range(K)},
        name=name,
        compiler_params=pltpu.CompilerParams(has_side_effects=EFFECT),
    )(*[_hbm(f) for f in fulls])
    return list(outs[:K]), list(outs[K:2 * K]), list(outs[2 * K:])


def _gather_wait_call(fulls, ssems, rsems, after, halved, name):
    K = len(fulls)

    def body(*refs):
        full, ssem, rsem = refs[:K], refs[K:2 * K], refs[2 * K:3 * K]
        x, y, c, me, others = _place()
        for k in range(K):
            for j, (px, py) in enumerate(others):
                cp = _rcopy(_slab(full[k], me, c, halved), _slab(full[k], 2 * px + py, c, halved),
                            ssem[k].at[j], rsem[k].at[j], (px, py, c))
                cp.wait_send()
                cp.wait_recv()

    outs = pl.pallas_call(
        body,
        in_specs=[HBM] * K + [SEMS] * (2 * K) + [ANY],
        out_specs=[HBM] * K,
        out_shape=[pltpu.HBM(f.shape, f.dtype) for f in fulls],
        input_output_aliases={k: k for k in range(K)},
        name=name,
        compiler_params=pltpu.CompilerParams(has_side_effects=EFFECT),
    )(*fulls, *ssems, *rsems, after)
    return list(outs)


def _sibling_forward_call(fulls, name):
    K = len(fulls)

    def body(*refs):
        full = refs[:K]
        ssem, rsem = refs[2 * K:]
        x, y, c, me, others = _place()
        cps = []
        for k in range(K):
            for j, (px, py) in enumerate(others):
                mine = _slab(full[k], 2 * px + py, c, True)
                cps.append(_rcopy(mine, mine, ssem.at[3 * k + j], rsem.at[3 * k + j], (x, y, 1 - c)))
        for cp in cps:
            cp.start()
        for k in range(K):
            for j, (px, py) in enumerate(others):
                theirs = _slab(full[k], 2 * px + py, 1 - c, True)
                _rcopy(theirs, theirs, ssem.at[3 * k + j], rsem.at[3 * k + j], (x, y, 1 - c)).wait_recv()
        for cp in cps:
            cp.wait_send()

    outs = pl.pallas_call(
        body,
        in_specs=[ANY] * K,
        out_specs=[ANY] * K,
        out_shape=[SDS(f.shape, f.dtype) for f in fulls],
        input_output_aliases={k: k for k in range(K)},
        scratch_shapes=[pltpu.SemaphoreType.DMA((3 * K,)), pltpu.SemaphoreType.DMA((3 * K,))],
        name=name,
    )(*fulls)
    return list(outs)


def _grad_copies(srcs, lands, ssem, rsem, to_sibling):
    x, y, c, me, others = _place()
    if to_sibling:
        return [_rcopy(s, l, ssem.at[k], rsem.at[k], (x, y, 1 - c)) for k, (s, l) in enumerate(zip(srcs, lands))]
    cps, k = [], 0
    for src, land in zip(srcs, lands):
        if len(src.shape) == 3:
            for j, (px, py) in enumerate(others):
                cps.append(_rcopy(src.at[2 * px + py], land.at[j], ssem.at[k + j], rsem.at[k + j], (px, py, c)))
            k += 3
        else:
            for r in range(1, NDEV):
                cps.append(_rcopy(src, land.at[4 * x + 2 * y + c], ssem.at[k + r - 1], rsem.at[k + r - 1],
                                  _flipped(x, y, c, r)))
            k += NDEV - 1
    return cps


def _grad_start_call(srcs, name, to_sibling=False):
    srcs = list(srcs)
    K = len(srcs)
    if to_sibling:
        lands = [lax.empty(s.shape, s.dtype) for s in srcs]
        n = K
    else:
        lands = [lax.empty(((3,) + s.shape[1:]) if len(s.shape) == 3 else ((NDEV,) + s.shape), s.dtype) for s in srcs]
        n = sum(3 if len(s.shape) == 3 else NDEV - 1 for s in srcs)

    def body(*refs):
        ssem, rsem, token = refs[2 * K], refs[2 * K + 1], refs[-1]
        for cp in _grad_copies(refs[:K], refs[K:2 * K], ssem, rsem, to_sibling):
            cp.start()
        token[...] = jnp.zeros_like(token)

    outs = pl.pallas_call(
        body,
        in_specs=[HBM] * (2 * K),
        out_specs=[SEMS, SEMS] + [HBM] * (2 * K) + [pl.BlockSpec(memory_space=pltpu.VMEM)],
        out_shape=[pltpu.SemaphoreType.DMA((n,)), pltpu.SemaphoreType.DMA((n,))]
        + [pltpu.HBM(a.shape, a.dtype) for a in srcs + lands] + [SDS((8, 128), F32)],
        input_output_aliases={k: 2 + k for k in range(2 * K)},
        name=name,
        compiler_params=pltpu.CompilerParams(has_side_effects=EFFECT),
    )(*[_hbm(a) for a in srcs + lands])
    return list(outs[2:2 + K]), list(outs[2 + K:2 + 2 * K]), outs[0], outs[1], outs[-1]


def _grad_wait_call(srcs, lands, ssem, rsem, after, name, to_sibling=False):
    K = len(srcs)

    def body(*refs):
        for cp in _grad_copies(refs[:K], refs[K:2 * K], refs[2 * K], refs[2 * K + 1], to_sibling):
            cp.wait_send()
            cp.wait_recv()

    arrs = list(srcs) + list(lands)
    outs = pl.pallas_call(
        body,
        in_specs=[HBM] * (2 * K) + [SEMS, SEMS, ANY],
        out_specs=[HBM] * (2 * K),
        out_shape=[pltpu.HBM(a.shape, a.dtype) for a in arrs],
        input_output_aliases={k: k for k in range(2 * K)},
        name=name,
        compiler_params=pltpu.CompilerParams(has_side_effects=EFFECT),
    )(*arrs, ssem, rsem, after)
    return list(outs[:K]), list(outs[K:])


def _small_allreduce_call(a):
    R, C = a.shape

    def body(a_ref, o_ref, recv_ref, ssem, rsem):
        x, y, c, me, others = _place()
        dev = 4 * x + 2 * y + c
        recv_ref[pl.ds(dev, 1)] = a_ref[...][None]
        cps = [_rcopy(a_ref, recv_ref.at[dev], ssem.at[r - 1], rsem.at[r - 1], _flipped(x, y, c, r))
               for r in range(1, NDEV)]
        for cp in cps:
            cp.start()
        for cp in cps:
            cp.wait()
        acc = recv_ref[0]
        for s in range(1, NDEV):
            acc = acc + recv_ref[s]
        o_ref[...] = acc

    return pl.pallas_call(
        body,
        in_specs=[pl.BlockSpec(memory_space=pltpu.VMEM)],
        out_specs=pl.BlockSpec(memory_space=pltpu.VMEM),
        out_shape=SDS((R, C), F32),
        scratch_shapes=[pltpu.VMEM((NDEV, R, C), F32), pltpu.SemaphoreType.DMA((NDEV - 1,)),
                        pltpu.SemaphoreType.DMA((NDEV - 1,))],
        name="small_allreduce",
    )(a)


def _rows_tile(H, C):
    for cand in (512, 256, 128, 64, 32, 16, 8):
        if H % cand == 0 and cand * C * 4 <= 2 * 1024 * 1024:
            return cand
    raise ValueError((H, C))


def _sum_recv_call(own, recv, chip_idx, stack, l):
    _, R, C = own.shape
    tr = _rows_tile(R, C)

    def body(chip_ref, own_ref, r0, r1, r2, stack_ref, o_ref):
        o_ref[...] = (((own_ref[...].astype(F32) + r0[...].astype(F32)) + r1[...].astype(F32))
                      + r2[...].astype(F32)).astype(MM)

    return pl.pallas_call(
        body,
        grid_spec=pltpu.PrefetchScalarGridSpec(
            num_scalar_prefetch=1,
            grid=(R // tr,),
            in_specs=[pl.BlockSpec((1, tr, C), lambda i, chip_ref: (chip_ref[0], i, 0))]
            + [pl.BlockSpec((1, tr, C), functools.partial(lambda i, chip_ref, s: (s, i, 0), s=s)) for s in range(3)]
            + [ANY],
            out_specs=pl.BlockSpec((1, tr, C), lambda i, chip_ref: (l, i, 0)),
        ),
        out_shape=SDS(stack.shape, MM),
        input_output_aliases={5: 0},
        name="grad_sum_recv",
        compiler_params=_cp(("arbitrary",)),
    )(chip_idx, own, recv, recv, recv, stack)


def _sum_small_call(own, recv, dev_idx):
    RS, C = own.shape
    tr = _rows_tile(RS, C)

    def body(dev_ref, own_ref, *refs):
        o_ref = refs[NDEV]
        dev = dev_ref[0]
        acc = jnp.where(dev == 0, own_ref[...], refs[0][0]).astype(F32)
        for s in range(1, NDEV):
            acc = acc + jnp.where(dev == s, own_ref[...], refs[s][0]).astype(F32)
        o_ref[...] = acc

    return pl.pallas_call(
        body,
        grid_spec=pltpu.PrefetchScalarGridSpec(
            num_scalar_prefetch=1,
            grid=(RS // tr,),
            in_specs=[pl.BlockSpec((tr, C), lambda i, dev_ref: (i, 0))]
            + [pl.BlockSpec((1, tr, C), functools.partial(
                lambda i, dev_ref, s: (jnp.where(dev_ref[0] == s, (s + 1) % NDEV, s), i, 0), s=s)) for s in range(NDEV)],
            out_specs=pl.BlockSpec((tr, C), lambda i, dev_ref: (i, 0)),
        ),
        out_shape=SDS((RS, C), F32),
        name="grad_sum_small",
        compiler_params=_cp(("arbitrary",)),
    )(dev_idx, own, *([recv] * NDEV))


SMALL_ROWS_ALIGN = 128


def _pack_small(parts):
    flat = jnp.concatenate([p.reshape(-1) for p in parts])
    rows = -(-flat.shape[0] // (128 * SMALL_ROWS_ALIGN)) * SMALL_ROWS_ALIGN
    flat = jnp.pad(flat, (0, rows * 128 - flat.shape[0]))
    return flat.reshape(rows, 128)


def _unpack_small(packed, like):
    flat = packed.reshape(-1)
    out, off = [], 0
    for p in like:
        n = int(np.prod(p.shape))
        out.append(flat[off:off + n].reshape(p.shape))
        off += n
    return out


def kernel(x, norm_g, w_in, q_norm, k_norm, sinks, w_s, b_s, w_out, loss_target, m_norm_g, m_w_in, m_q_norm, m_k_norm, m_sinks, m_w_s, m_b_s, m_w_out, v_norm_g, v_w_in, v_q_norm, v_k_norm, v_sinks, v_w_s, v_b_s, v_w_out):
    L = norm_g.shape[0]
    xi, yi, ci = lax.axis_index("x"), lax.axis_index("y"), lax.axis_index("c")
    chip_idx = (2 * xi + yi).astype(jnp.int32).reshape(1)
    dev_idx = (4 * xi + 2 * yi + ci).astype(jnp.int32).reshape(1)
    bias = _alibi_bias()
    b2 = _half_sum_matrix()
    tri =jnp.tril(jnp.ones((WIN, WIN), F32))

    rest = list(range(1, L))
    fulls0 = _cast_to_slab_call(w_in, chip_idx, [0], chip_idx, "cast_w_in_0") \
        + _cast_to_slab_call(w_out, chip_idx, [0], chip_idx, "cast_w_out_0")
    ss0, rs0, fulls0 = _gather_start_call(fulls0, 2, "gather_start_0")
    fin = _cast_to_slab_call(w_in, chip_idx, rest, fulls0[0], "cast_w_in")
    fout = _cast_to_slab_call(w_out, chip_idx, rest, fulls0[1], "cast_w_out")
    g_ssems, g_rsems, fulls = _gather_start_call([a for pair in zip(fin, fout) for a in pair], 0, "gather_start")

    saved = []
    xs = x[0]
    dy = loss = None
    for l in range(L):
        if l == 0:
            (w_in_l,) = _gather_wait_call(fulls0[:1], ss0[:1], rs0[:1], fulls[0], True, "gather_wait_0")
            (w_in_l,) = _sibling_forward_call([w_in_l], "gather_forward_0")
        else:
            sl = slice(2 * (l - 1), 2 * l)
            w_in_l, w_out_l = _gather_wait_call(fulls[sl], g_ssems[sl], g_rsems[sl], xs, False, f"gather_wait_{l}")
        proj, h = _fwd_in_call(xs, norm_g[l:l + 1], w_in_l)
        ws_tril = (w_s[l] * tri).astype(MM)
        b_exp = jnp.repeat(b_s[l].T, HD, axis=1)
        wq2 = jnp.tile(q_norm[l:l + 1], (1, 2)) * SCALE
        wk2 = jnp.tile(k_norm[l:l + 1], (1, 2))
        mix = _fwd_mix_call(proj, bias, wq2, wk2, b2, sinks[l], ws_tril, b_exp)
        if l == 0:
            (w_out_l,) = _gather_wait_call(fulls0[1:], ss0[1:], rs0[1:], mix, True, "gather_wait_0_out")
            (w_out_l,) = _sibling_forward_call([w_out_l], "gather_forward_0_out")
        w_out_l = w_out_l.reshape(D, D)
        saved.append((xs, proj, h, mix, ws_tril, b_exp, w_in_l, w_out_l, wq2, wk2))
        if l < L - 1:
            xs = _fwd_out_call(xs, mix, w_out_l)
        else:
            dy, loss = _fwd_out_loss_call(xs, mix, w_out_l, loss_target[0])

    s_in = lax.empty((L, D, SHW), MM)
    s_out = lax.empty((L, SHR, D), MM)
    ws_sums, tiny_sums = [None] * L, [None] * L

    def finish(pending, after):
        nonlocal s_in, s_out
        l, exchanges = pending
        for tag, kinds, srcs, lands, ssem, rsem in exchanges:
            srcs, lands = _grad_wait_call(srcs, lands, ssem, rsem, after, f"grad_wait_{l}{tag}")
            for kind, src, land in zip(kinds, srcs, lands):
                if kind == "in":
                    s_in = _sum_recv_call(src, land, chip_idx, s_in, l)
                elif kind == "out":
                    s_out = _sum_recv_call(src, land, chip_idx, s_out, l)
                elif kind == "ws":
                    ws_sums[l] = _sum_small_call(src, land, dev_idx)
                else:
                    tiny_sums[l] = _sum_small_call(src, land, dev_idx)

    def start(l, tag, kinds, srcs):
        srcs, lands, ssem, rsem, token = _grad_start_call(srcs, f"grad_start_{l}{tag}")
        return (tag, kinds, srcs, lands, ssem, rsem), token

    pending = None
    d_norm_g = [None] * L
    for l in reversed(range(L)):
        xs, proj, h, mix, ws_tril, b_exp, w_in_l, w_out_l, wq2, wk2 = saved[l]
        g_w_out = _grad_w_out_call(mix, dy).reshape(NCHIP, SHR, D)
        exchanges, token = [], jnp.zeros((8, 128), F32)
        if l == 0:
            ex, token = start(l, "_out", ["out"], [g_w_out])
            exchanges.append(ex)
        dmix = _bwd_out_call(dy, w_out_l, token)
        ws_tril_t = jnp.swapaxes(ws_tril, 1, 2)
        dproj, dwq, dwk, dsk, dws, dbs = _bwd_mix_call(
            proj, dmix, bias, wq2, wk2, b2, sinks[l], ws_tril, ws_tril_t, b_exp)
        dwq, dwk = dwq[:, :HD] + dwq[:, HD:], dwk[:, :HD] + dwk[:, HD:]
        g_ws = dws.reshape(NG * WIN, WIN).astype(MM)
        g_tiny = _pack_small([dwq, dwk, dsk[:, 0], dbs]).astype(MM)
        token = jnp.zeros((8, 128), F32)
        if l == 0:
            ex, token = start(l, "_small", ["ws", "tiny"], [g_ws, g_tiny])
            exchanges.append(ex)
        g_w_in = _grad_w_in_call(h, dproj, token)
        if l == 0:
            ex, token = start(l, "", ["in"], [g_w_in])
        else:
            ex, token = start(l, "", ["in", "out", "ws", "tiny"], [g_w_in, g_w_out, g_ws, g_tiny])
        exchanges.append(ex)
        dy, d_norm_g[l] = _bwd_in_call(dproj, w_in_l, xs, dy, norm_g[l:l + 1], token)
        if pending is not None:
            finish(pending, dy)
        pending = (l, exchanges)
    finish(pending, dy)
    grad_x = dy

    swap_srcs, swap_lands, sw_ssem, sw_rsem, _ = _grad_start_call([s_in, s_out], "grad_swap_start", to_sibling=True)
    g_norm_g = _small_allreduce_call(jnp.concatenate(d_norm_g, axis=0))

    def pack_layers(parts):
        return jnp.concatenate([_pack_small([p[l] for p in parts]) for l in range(L)], axis=0)

    ws_rows = (L * NG * WIN, WIN)
    ws_outs = _adam_call(w_s.reshape(ws_rows), [jnp.concatenate(ws_sums, axis=0)],
                         m_w_s.reshape(ws_rows), v_w_s.reshape(ws_rows), "adam_w_s")
    tiny_like = [q_norm, k_norm, sinks, b_s]
    tiny_outs = _adam_call(
        pack_layers(tiny_like), [jnp.concatenate(tiny_sums, axis=0)],
        pack_layers([m_q_norm, m_k_norm, m_sinks, m_b_s]),
        pack_layers([v_q_norm, v_k_norm, v_sinks, v_b_s]), "adam_tiny")
    norm_outs = _adam_call(norm_g, [g_norm_g], m_norm_g, v_norm_g, "adam_norm_g")

    small_done = ws_outs[1][:8] + tiny_outs[1][:8] + norm_outs[1][:1, :WIN]
    (s_in, s_out), (t_in, t_out) = _grad_wait_call(swap_srcs, swap_lands, sw_ssem, sw_rsem, small_done,
                                                   "grad_swap_wait", to_sibling=True)
    g_w_in, d_in, nm_in, nv_in = _adam_call(
        w_in.reshape(L * D, SHW), [s_in.reshape(L * D, SHW), t_in.reshape(L * D, SHW)],
        m_w_in.reshape(L * D, SHW), v_w_in.reshape(L * D, SHW), "adam_w_in")
    g_w_out, d_out, nm_out, nv_out = _adam_call(
        w_out.reshape(L * SHR, D), [s_out.reshape(L * SHR, D), t_out.reshape(L * SHR, D)],
        m_w_out.reshape(L * SHR, D), v_w_out.reshape(L * SHR, D), "adam_w_out")

    def full(i, win, wout):
        tiny = tiny_outs[i]
        rows = tiny.shape[0] // L
        per_layer = [_unpack_small(tiny[l * rows:(l + 1) * rows], [p[l] for p in tiny_like]) for l in range(L)]
        qn, kn, sk, bs = [jnp.stack([per_layer[l][k] for l in range(L)]) for k in range(4)]
        return [norm_outs[i], win.reshape(w_in.shape), qn, kn, sk, ws_outs[i].reshape(w_s.shape), bs,
                wout.reshape(w_out.shape)]

    loss_all = lax.psum(loss[0, 0], ("x", "y", "c"))
    return (loss_all, grad_x[None], *full(0, g_w_in, g_w_out), *full(1, d_in, d_out),
            *full(2, nm_in, nm_out), *full(3, nv_in, nv_out))
```

```python
import functools
import math

import numpy as np
import jax
import jax.numpy as jnp
from jax import lax
from jax.experimental import pallas as pl
from jax.experimental.pallas import tpu as pltpu

F32 = jnp.float32
MM = jnp.bfloat16

D = 2048
HD = 64
DA = 1024
DKV = 256
DG = 1024
NQ, NKV, GRP, NG = 16, 4, 4, 16
WIN = 128
DIN = 5632
C_Q, C_K, C_V, C_GA, C_U, C_VS, C_GB = 0, 1024, 1280, 1536, 2560, 3584, 4608
NCHIP = 4
SHW = DIN // NCHIP
SHR = D // NCHIP
EPS = 1e-6
NEG = -1e30
SCALE = HD ** -0.5
INV_SQRT2 = 1.0 / math.sqrt(2.0)
INV_SQRT_2PI = 1.0 / math.sqrt(2.0 * math.pi)
LR, B1, B2, ADAM_EPS, WD, STEP = 0.001, 0.9, 0.999, 1e-08, 0.01, 10
VMEM_LIMIT = 56 * 1024 * 1024

SDS = jax.ShapeDtypeStruct
NT = (((1,), (1,)), ((), ()))
TN = (((0,), (0,)), ((), ()))


def _cp(sem=None):
    return pltpu.CompilerParams(dimension_semantics=sem, vmem_limit_bytes=VMEM_LIMIT)


def _sigmoid(x):
    return 1.0 / (1.0 + jnp.exp(-x))


def _gelu(x):
    return 0.5 * x * (1.0 + lax.erf(x * INV_SQRT2))


def _gelu_and_grad(x):
    cdf = 0.5 * (1.0 + lax.erf(x * INV_SQRT2))
    return x * cdf, cdf + x * jnp.exp(-0.5 * x * x) * INV_SQRT_2PI


def _alibi_bias():
    slopes = 2.0 ** (-8.0 * np.arange(1, NQ + 1) / NQ)
    dist = (np.arange(WIN)[:, None] + WIN) - np.arange(2 * WIN)[None, :]
    ok = (dist >= 0) & (dist < WIN)
    first = ok & (np.arange(2 * WIN)[None, :] >= WIN)
    val = -slopes[:, None, None] * dist[None].astype(np.float64)
    return jnp.asarray(np.stack([np.where(first[None], val, NEG), np.where(ok[None], val, NEG)]), dtype=F32)


def _half_sum_matrix():
    half = np.arange(LANE) // HD
    return jnp.asarray(half[:, None] == half[None, :], dtype=MM)


LANE = 128
NQT = DA // LANE
NKT = DKV // LANE


def _tiles(ref, c0, n):
    return jnp.concatenate([ref[:, c0 + j * LANE:c0 + (j + 1) * LANE] for j in range(n)], axis=0)


def _split(x):
    hi = x.astype(MM)
    return hi, (x - hi.astype(F32)).astype(MM)


def _half_sums(x, b2):
    hi, lo = _split(x)
    return jnp.dot(hi, b2, preferred_element_type=F32) + jnp.dot(lo, b2, preferred_element_type=F32)


def _attn_fwd(pm_ref, kvp_ref, bias_ref, wq2, wk2, b2, sink_ref):
    lo_half = lax.broadcasted_iota(jnp.int32, (1, LANE), 1) < HD
    q_ts = _tiles(pm_ref, C_Q, NQT)
    rq = lax.rsqrt(_half_sums(q_ts * q_ts, b2) * (1.0 / HD) + EPS)
    qs = (q_ts * rq * wq2).astype(MM)
    k_ts = jnp.concatenate([a[:, c0 + t * LANE:c0 + (t + 1) * LANE] for t in range(NKT)
                            for a, c0 in ((kvp_ref, 0), (pm_ref, C_K))], axis=0)
    rk = lax.rsqrt(_half_sums(k_ts * k_ts, b2) * (1.0 / HD) + EPS)
    kn = (k_ts * rk * wk2).astype(MM)
    v_ts = jnp.concatenate([a[:, c0 + t * LANE:c0 + (t + 1) * LANE] for t in range(NKT)
                            for a, c0 in ((kvp_ref, DKV), (pm_ref, C_V))], axis=0).astype(MM)
    ones = jnp.ones((2 * WIN, LANE), MM)
    km, vm = {}, {}
    for hk in range(NKV):
        t, eh = hk // 2, hk % 2
        sel = lo_half if eh == 0 else jnp.logical_not(lo_half)
        rows = slice(t * 2 * WIN, (t + 1) * 2 * WIN)
        k_same = jnp.where(sel, kn[rows], jnp.zeros_like(kn[rows]))
        v_same = jnp.where(sel, v_ts[rows], jnp.zeros_like(v_ts[rows]))
        km[hk, eh], km[hk, 1 - eh] = k_same, pltpu.roll(k_same, HD, axis=1)
        vm[hk, eh], vm[hk, 1 - eh] = v_same, pltpu.roll(v_same, HD, axis=1)
    hs = range(NQ)
    s = [lax.dot_general(qs[(h // 2) * WIN:(h // 2 + 1) * WIN], km[h // GRP, h % 2], NT, preferred_element_type=F32)
         + bias_ref[0, h] for h in hs]
    m = [jnp.maximum(jnp.max(s[h], axis=-1, keepdims=True), sink_ref[h]) for h in hs]
    p = [jnp.exp(s[h] - m[h]) for h in hs]
    pb = [p[h].astype(MM) for h in hs]
    res = [jnp.dot(pb[h], jnp.concatenate([vm[h // GRP, h % 2], ones], axis=1), preferred_element_type=F32) for h in hs]
    esink = [jnp.exp(sink_ref[h] - m[h]) for h in hs]
    inv = [1.0 / (res[h][:, LANE:] + esink[h]) for h in hs]
    heads = [dict(p=p[h], pb=pb[h], inv=inv[h], esink=esink[h], o=res[h][:, :LANE] * inv[h]) for h in hs]
    return dict(lo_half=lo_half, q_ts=q_ts, rq=rq, qs=qs, k_ts=k_ts, rk=rk, km=km, vm=vm, heads=heads)


def _sgu_mix(w_ref, zt, lo_half, j):
    zero = jnp.zeros_like(zt)
    return (jnp.dot(w_ref[2 * j], jnp.where(lo_half, zt, zero), preferred_element_type=F32)
            + jnp.dot(w_ref[2 * j + 1], jnp.where(lo_half, zero, zt), preferred_element_type=F32))


def _fwd_mix_call(proj, bias, wq2, wk2, b2, sinks, ws_tril, b_exp):
    T = proj.shape[0]
    nb = T // WIN

    def body(sink_ref, pm_ref, kvp_ref, bias_ref, wq_ref, wk_ref, b2_ref, ws_ref, be_ref, mix_ref):
        lo_half = lax.broadcasted_iota(jnp.int32, (1, LANE), 1) < HD
        zu = _gelu(pm_ref[:, C_U:C_U + DG])
        zv = _gelu(pm_ref[:, C_VS:C_VS + DG]).astype(MM)
        mixed = jnp.concatenate(
            [_sgu_mix(ws_ref, zv[:, j * LANE:(j + 1) * LANE], lo_half, j) for j in range(NG // 2)], axis=1)
        mixed = mixed + be_ref[...]
        gb = pm_ref[:, C_GB:C_GB + DG]
        mix_ref[:, DA:DA + DG] = (zu * mixed * (gb * _sigmoid(gb))).astype(MM)
        a = _attn_fwd(pm_ref, kvp_ref, bias_ref, wq_ref[...], wk_ref[...], b2_ref[...], sink_ref)
        for j in range(NQT):
            cols = slice(j * LANE, (j + 1) * LANE)
            ga = pm_ref[:, C_GA + j * LANE:C_GA + (j + 1) * LANE]
            attn = a["heads"][2 * j]["o"] + a["heads"][2 * j + 1]["o"]
            mix_ref[:, cols] = (attn * (ga * _sigmoid(ga))).astype(MM)

    return pl.pallas_call(
        body,
        grid=(nb,),
        in_specs=[
            pl.BlockSpec(memory_space=pltpu.SMEM),
            pl.BlockSpec((WIN, DIN), lambda n: (n, 0)),
            pl.BlockSpec((WIN, 2 * DKV), lambda n: (jnp.maximum(n - 1, 0), C_K // (2 * DKV))),
            pl.BlockSpec((1, NQ, WIN, 2 * WIN), lambda n: (jnp.minimum(n, 1), 0, 0, 0)),
            pl.BlockSpec((1, LANE), lambda n: (0, 0)),
            pl.BlockSpec((1, LANE), lambda n: (0, 0)),
            pl.BlockSpec((LANE, LANE), lambda n: (0, 0)),
            pl.BlockSpec((NG, WIN, WIN), lambda n: (0, 0, 0)),
            pl.BlockSpec((WIN, DG), lambda n: (0, 0)),
        ],
        out_specs=pl.BlockSpec((WIN, D), lambda n: (n, 0)),
        out_shape=SDS((T, D), MM),
        name="fwd_mix",
        compiler_params=_cp(("arbitrary",)),
    )(sinks, proj, proj, bias, wq2, wk2, b2, ws_tril, b_exp)


def _bwd_mix_call(proj, dmix, bias, wq2, wk2, b2, sinks, ws_tril, ws_tril_t, b_exp):
    T = proj.shape[0]
    nb = T // WIN

    def body(sink_ref, pm_ref, kvp_ref, dm_ref, bias_ref, wq_ref, wk_ref, b2_ref, ws_ref, wst_ref, be_ref,
             dp_ref, dwq_ref, dwk_ref, dsk_ref, dws_ref, dbs_ref, carry_ref, ckv_ref, dbacc_ref):
        n = pl.program_id(0)

        @pl.when(n == 0)
        def _():
            carry_ref[...] = jnp.zeros_like(carry_ref)
            ckv_ref[...] = jnp.zeros_like(ckv_ref)
            dbacc_ref[...] = jnp.zeros_like(dbacc_ref)
            dwq_ref[...] = jnp.zeros_like(dwq_ref)
            dwk_ref[...] = jnp.zeros_like(dwk_ref)
            dsk_ref[...] = jnp.zeros_like(dsk_ref)
            dws_ref[...] = jnp.zeros_like(dws_ref)
            dbs_ref[...] = jnp.zeros_like(dbs_ref)

        @pl.when(n < nb)
        def _():
            dp_ref[:, C_Q:C_K] = carry_ref[:, C_Q:C_K]
            dp_ref[:, C_GA:DIN] = carry_ref[:, C_GA:DIN]

            lo_half = lax.broadcasted_iota(jnp.int32, (1, LANE), 1) < HD
            u = pm_ref[:, C_U:C_U + DG]
            vs = pm_ref[:, C_VS:C_VS + DG]
            gb = pm_ref[:, C_GB:C_GB + DG]
            zu, dzu = _gelu_and_grad(u)
            zv, dzv = _gelu_and_grad(vs)
            zvb = zv.astype(MM)
            mixed = jnp.concatenate(
                [_sgu_mix(ws_ref, zvb[:, j * LANE:(j + 1) * LANE], lo_half, j) for j in range(NG // 2)], axis=1)
            mixed = mixed + be_ref[...]
            sgb = _sigmoid(gb)
            d_sgu = dm_ref[:, DA:DA + DG]
            carry_ref[:, C_GB:DIN] = (d_sgu * zu * mixed * (sgb * (1.0 + gb * (1.0 - sgb)))).astype(MM)
            d_mixed = d_sgu * zu * (gb * sgb)
            carry_ref[:, C_U:C_VS] = (d_sgu * mixed * (gb * sgb) * dzu).astype(MM)
            dbacc_ref[...] += d_mixed
            dmb = d_mixed.astype(MM)
            dzv_tiles = [_sgu_mix(wst_ref, dmb[:, j * LANE:(j + 1) * LANE], lo_half, j) for j in range(NG // 2)]
            carry_ref[:, C_VS:C_GB] = (jnp.concatenate(dzv_tiles, axis=1) * dzv).astype(MM)
            for j in range(NG // 2):
                dt = dmb[:, j * LANE:(j + 1) * LANE]
                zt = zvb[:, j * LANE:(j + 1) * LANE]
                zero = jnp.zeros_like(dt)
                dws_ref[2 * j] += lax.dot_general(jnp.where(lo_half, dt, zero), zt, NT, preferred_element_type=F32)
                dws_ref[2 * j + 1] += lax.dot_general(jnp.where(lo_half, zero, dt), zt, NT, preferred_element_type=F32)

            wq2, wk2, b2 = wq_ref[...], wk_ref[...], b2_ref[...]
            a = _attn_fwd(pm_ref, kvp_ref, bias_ref, wq2, wk2, b2, sink_ref)
            heads, km, vm, qs = a["heads"], a["km"], a["vm"], a["qs"]

            row_lo = lax.broadcasted_iota(jnp.int32, (LANE, LANE), 0) < HD
            pick = [jnp.where(row_lo, 1.0, 0.0).astype(MM), jnp.where(row_lo, 0.0, 1.0).astype(MM)]
            chan_lo = lax.broadcasted_iota(jnp.int32, (LANE, 1), 0) < HD
            tiles, hs = range(NQT), range(NQ)
            sel_t = [chan_lo, jnp.logical_not(chan_lo)]
            d_o, attn = [], []
            for j in tiles:
                cols = slice(C_GA + j * LANE, C_GA + (j + 1) * LANE)
                ga = pm_ref[:, cols]
                sga = _sigmoid(ga)
                d_gated = dm_ref[:, j * LANE:(j + 1) * LANE]
                attn.append(heads[2 * j]["o"] + heads[2 * j + 1]["o"])
                carry_ref[:, cols] = (d_gated * attn[j] * (sga * (1.0 + ga * (1.0 - sga)))).astype(MM)
                d_o.append(d_gated * (ga * sga))
            d_ob = [d_o[j].astype(MM) for j in tiles]
            dlt = [(d_o[j] * attn[j]).astype(MM) for j in tiles]
            d_os_t = [(d_o[j] * jnp.where(lo_half, heads[2 * j]["inv"], heads[2 * j + 1]["inv"])).astype(MM).T
                      for j in tiles]
            qs_t = [qs[j * WIN:(j + 1) * WIN].T for j in tiles]
            zero_t = jnp.zeros_like(qs_t[0])
            dv_h = [jnp.dot(jnp.where(sel_t[h % 2], d_os_t[h // 2], zero_t), heads[h]["pb"], preferred_element_type=F32)
                    for h in hs]
            d_p = [lax.dot_general(d_ob[h // 2], vm[h // GRP, h % 2], NT, preferred_element_type=F32) for h in hs]
            delta = [jnp.dot(dlt[h // 2], pick[h % 2], preferred_element_type=F32) for h in hs]
            for h in hs:
                dsk_ref[h:h + 1, :] -= jnp.sum(heads[h]["esink"] * heads[h]["inv"] * delta[h], axis=0, keepdims=True)
            d_s = [(heads[h]["p"] * ((d_p[h] - jnp.concatenate([delta[h], delta[h]], axis=1))
                                     * jnp.concatenate([heads[h]["inv"], heads[h]["inv"]], axis=1))).astype(MM)
                   for h in hs]
            dqs_h = [jnp.dot(d_s[h], km[h // GRP, h % 2], preferred_element_type=F32) for h in hs]
            dqs_tiles = [dqs_h[2 * j] + dqs_h[2 * j + 1] for j in tiles]
            dk_h = [jnp.dot(jnp.where(sel_t[h % 2], qs_t[h // 2], zero_t), d_s[h], preferred_element_type=F32)
                    for h in hs]
            dk_acc, dv_acc = {}, {}
            for h in hs:
                key = (h // GRP, h % 2 == (h // GRP) % 2)
                dk_acc[key] = dk_h[h] if key not in dk_acc else dk_acc[key] + dk_h[h]
                dv_acc[key] = dv_h[h] if key not in dv_acc else dv_acc[key] + dv_h[h]

            dqs_ts = jnp.concatenate(dqs_tiles, axis=0)
            q_ts, rq = a["q_ts"], a["rq"]
            gq = dqs_ts * wq2
            d_q = rq * gq - q_ts * (rq * rq * rq) * (_half_sums(gq * q_ts, b2) * (1.0 / HD))
            dwq_ref[...] += SCALE * jnp.sum(dqs_ts * q_ts * rq, axis=0, keepdims=True)
            for j in range(NQT):
                carry_ref[:, C_Q + j * LANE:C_Q + (j + 1) * LANE] = d_q[j * WIN:(j + 1) * WIN].astype(MM)

            def swap_halves(xt):
                return jnp.concatenate([xt[HD:], xt[:HD]], axis=0)

            dkn_tiles, dv_tiles = [], []
            for t in range(NKT):
                for acc, out in ((dk_acc, dkn_tiles), (dv_acc, dv_tiles)):
                    parts = [acc[hk, True] + swap_halves(acc[hk, False]) for hk in (2 * t, 2 * t + 1)]
                    out.append((parts[0] + parts[1]).T)
            dkn_ts = jnp.concatenate(dkn_tiles, axis=0)
            dv_ts = jnp.concatenate(dv_tiles, axis=0)
            k_ts, rk = a["k_ts"], a["rk"]
            gk = dkn_ts * wk2
            d_k = rk * gk - k_ts * (rk * rk * rk) * (_half_sums(gk * k_ts, b2) * (1.0 / HD))
            dwk_ref[...] += jnp.sum(dkn_ts * k_ts * rk, axis=0, keepdims=True)
            for t in range(NKT):
                for base, val in ((C_K, d_k), (C_V, dv_ts)):
                    cols = slice(base + t * LANE, base + (t + 1) * LANE)
                    r0 = t * 2 * WIN
                    kv_cols = slice(cols.start - C_K, cols.stop - C_K)
                    dp_ref[:, cols] = (ckv_ref[:, kv_cols] + val[r0:r0 + WIN]).astype(MM)
                    ckv_ref[:, kv_cols] = val[r0 + WIN:r0 + 2 * WIN]

        @pl.when(n == nb)
        def _():
            dp_ref[:, C_Q:C_K] = carry_ref[:, C_Q:C_K]
            dp_ref[:, C_K:C_GA] = ckv_ref[...].astype(MM)
            dp_ref[:, C_GA:DIN] = carry_ref[:, C_GA:DIN]
            lo_half = lax.broadcasted_iota(jnp.int32, (8, LANE), 1) < HD
            ones = [jnp.where(lo_half, 1.0, 0.0).astype(MM), jnp.where(lo_half, 0.0, 1.0).astype(MM)]
            hi, lo = _split(dbacc_ref[...])
            for h in range(NG):
                sl = slice((h // 2) * LANE, (h // 2 + 1) * LANE)
                r = (lax.dot_general(ones[h % 2], hi[:, sl], NT, preferred_element_type=F32)
                     + lax.dot_general(ones[h % 2], lo[:, sl], NT, preferred_element_type=F32))
                dbs_ref[h:h + 1, :] = r[0:1, :]
            row = lax.broadcasted_iota(jnp.int32, (WIN, WIN), 0)
            cl = lax.broadcasted_iota(jnp.int32, (WIN, WIN), 1)
            for h in range(NG):
                dws_ref[h] = jnp.where(row >= cl, dws_ref[h], 0.0)

    last = nb - 1
    return pl.pallas_call(
        body,
        grid_spec=pltpu.PrefetchScalarGridSpec(
            num_scalar_prefetch=0,
            grid=(nb + 1,),
            in_specs=[
                pl.BlockSpec(memory_space=pltpu.SMEM),
                pl.BlockSpec((WIN, DIN), lambda n: (jnp.minimum(n, last), 0)),
                pl.BlockSpec((WIN, 2 * DKV), lambda n: (jnp.maximum(jnp.minimum(n, last) - 1, 0), C_K // (2 * DKV))),
                pl.BlockSpec((WIN, D), lambda n: (jnp.minimum(n, last), 0)),
                pl.BlockSpec((1, NQ, WIN, 2 * WIN), lambda n: (jnp.minimum(n, 1), 0, 0, 0)),
                pl.BlockSpec((1, LANE), lambda n: (0, 0)),
                pl.BlockSpec((1, LANE), lambda n: (0, 0)),
                pl.BlockSpec((LANE, LANE), lambda n: (0, 0)),
                pl.BlockSpec((NG, WIN, WIN), lambda n: (0, 0, 0)),
                pl.BlockSpec((NG, WIN, WIN), lambda n: (0, 0, 0)),
                pl.BlockSpec((WIN, DG), lambda n: (0, 0)),
            ],
            out_specs=[
                pl.BlockSpec((WIN, DIN), lambda n: (jnp.maximum(n - 1, 0), 0)),
                pl.BlockSpec((1, LANE), lambda n: (0, 0)),
                pl.BlockSpec((1, LANE), lambda n: (0, 0)),
                pl.BlockSpec((NQ, WIN), lambda n: (0, 0)),
                pl.BlockSpec((NG, WIN, WIN), lambda n: (0, 0, 0)),
                pl.BlockSpec((NG, WIN), lambda n: (0, 0)),
            ],
            scratch_shapes=[pltpu.VMEM((WIN, DIN), MM), pltpu.VMEM((WIN, 2 * DKV), F32), pltpu.VMEM((WIN, DG), F32)],
        ),
        out_shape=[SDS((T, DIN), MM), SDS((1, LANE), F32), SDS((1, LANE), F32), SDS((NQ, WIN), F32),
                   SDS((NG, WIN, WIN), F32), SDS((NG, WIN), F32)],
        name="bwd_mix",
        compiler_params=_cp(("arbitrary",)),
    )(sinks, proj, proj, dmix, bias, wq2, wk2, b2, ws_tril, ws_tril_t, b_exp)


WEIGHT_RESIDENT_ROWS = 256
GRAD_TOKEN_TILE = 1024


def _row_tile(T):
    return min(512, T)


def _fwd_in_call(x, g_row, w_sh):
    T = x.shape[0]
    tm = min(WEIGHT_RESIDENT_ROWS, T)

    def body(x_ref, g_ref, w_hbm, proj_ref, h_ref, w_vmem, sem):
        @pl.when(pl.program_id(0) == 0)
        def _():
            cps = [pltpu.make_async_copy(w_hbm.at[j], w_vmem.at[:, pl.ds(j * SHW, SHW)], sem.at[j]) for j in range(NCHIP)]
            for cp in cps:
                cp.start()
            for cp in cps:
                cp.wait()

        xv = x_ref[...]
        r = lax.rsqrt(jnp.mean(xv * xv, axis=-1, keepdims=True) + EPS)
        h = (xv * r * g_ref[...]).astype(MM)
        h_ref[...] = h
        proj_ref[...] = jnp.dot(h, w_vmem[...], preferred_element_type=F32)

    return pl.pallas_call(
        body,
        grid=(T // tm,),
        in_specs=[pl.BlockSpec((tm, D), lambda i: (i, 0)),
                  pl.BlockSpec((1, D), lambda i: (0, 0)),
                  pl.BlockSpec(memory_space=pl.ANY)],
        out_specs=[pl.BlockSpec((tm, DIN), lambda i: (i, 0)),
                   pl.BlockSpec((tm, D), lambda i: (i, 0))],
        out_shape=[SDS((T, DIN), F32), SDS((T, D), MM)],
        scratch_shapes=[pltpu.VMEM((D, DIN), MM), pltpu.SemaphoreType.DMA((NCHIP,))],
        name="fwd_in",
        compiler_params=_cp(("arbitrary",)),
    )(x, g_row, w_sh)


def _fwd_out_call(x, mix, w_out):
    T = x.shape[0]
    tm = _row_tile(T)

    def body(x_ref, mix_ref, w_ref, y_ref):
        y_ref[...] = x_ref[...] + jnp.dot(mix_ref[...], w_ref[...], preferred_element_type=F32)

    return pl.pallas_call(
        body,
        grid=(T // tm,),
        in_specs=[pl.BlockSpec((tm, D), lambda i: (i, 0)),
                  pl.BlockSpec((tm, D), lambda i: (i, 0)),
                  pl.BlockSpec((D, D), lambda i: (0, 0))],
        out_specs=pl.BlockSpec((tm, D), lambda i: (i, 0)),
        out_shape=SDS((T, D), F32),
        name="fwd_out",
        compiler_params=_cp(("arbitrary",)),
    )(x, mix, w_out)


def _fwd_out_loss_call(x, mix, w_out, target):
    T = x.shape[0]
    tm = _row_tile(T)

    def body(x_ref, mix_ref, w_ref, t_ref, dy_ref, loss_ref):
        @pl.when(pl.program_id(0) == 0)
        def _():
            loss_ref[...] = jnp.zeros_like(loss_ref)

        e = x_ref[...] + jnp.dot(mix_ref[...], w_ref[...], preferred_element_type=F32) - t_ref[...]
        dy_ref[...] = e * (1.0 / D)
        loss_ref[...] += (0.5 / D) * jnp.sum(jnp.sum(e * e, axis=1, keepdims=True), axis=0, keepdims=True)

    return pl.pallas_call(
        body,
        grid=(T // tm,),
        in_specs=[pl.BlockSpec((tm, D), lambda i: (i, 0)),
                  pl.BlockSpec((tm, D), lambda i: (i, 0)),
                  pl.BlockSpec((D, D), lambda i: (0, 0)),
                  pl.BlockSpec((tm, D), lambda i: (i, 0))],
        out_specs=[pl.BlockSpec((tm, D), lambda i: (i, 0)),
                   pl.BlockSpec((1, 1), lambda i: (0, 0))],
        out_shape=[SDS((T, D), F32), SDS((1, 1), F32)],
        name="fwd_out_loss",
        compiler_params=_cp(("arbitrary",)),
    )(x, mix, w_out, target)


def _bwd_out_call(dy, w_out, token):
    T = dy.shape[0]
    tm = _row_tile(T)

    def body(dy_ref, w_ref, token_ref, o_ref):
        o_ref[...] = lax.dot_general(dy_ref[...].astype(MM), w_ref[...], NT, preferred_element_type=F32)

    return pl.pallas_call(
        body,
        grid=(T // tm,),
        in_specs=[pl.BlockSpec((tm, D), lambda i: (i, 0)),
                  pl.BlockSpec((D, D), lambda i: (0, 0)),
                  pl.BlockSpec(memory_space=pl.ANY)],
        out_specs=pl.BlockSpec((tm, D), lambda i: (i, 0)),
        out_shape=SDS((T, D), F32),
        name="bwd_out",
        compiler_params=_cp(("arbitrary",)),
    )(dy, w_out, token)


def _bwd_in_call(dproj, w_sh, x, dy, g_row, token):
    T = x.shape[0]
    tm = min(WEIGHT_RESIDENT_ROWS, T)

    def body(dp_ref, w_hbm, x_ref, dy_ref, g_ref, token_ref, dx_ref, dg_ref, w_vmem, sem):
        @pl.when(pl.program_id(0) == 0)
        def _():
            cps = [pltpu.make_async_copy(w_hbm.at[j], w_vmem.at[:, pl.ds(j * SHW, SHW)], sem.at[j]) for j in range(NCHIP)]
            for cp in cps:
                cp.start()
            dg_ref[...] = jnp.zeros_like(dg_ref)
            for cp in cps:
                cp.wait()

        dh = lax.dot_general(dp_ref[...], w_vmem[...], NT, preferred_element_type=F32)
        xv = x_ref[...]
        r = lax.rsqrt(jnp.mean(xv * xv, axis=-1, keepdims=True) + EPS)
        gd = dh * g_ref[...]
        dx_ref[...] = dy_ref[...] + r * gd - xv * ((r * r * r) * jnp.mean(gd * xv, axis=-1, keepdims=True))
        dg_ref[...] += jnp.sum(dh * xv * r, axis=0, keepdims=True)

    return pl.pallas_call(
        body,
        grid=(T // tm,),
        in_specs=[pl.BlockSpec((tm, DIN), lambda i: (i, 0)),
                  pl.BlockSpec(memory_space=pl.ANY),
                  pl.BlockSpec((tm, D), lambda i: (i, 0)),
                  pl.BlockSpec((tm, D), lambda i: (i, 0)),
                  pl.BlockSpec((1, D), lambda i: (0, 0)),
                  pl.BlockSpec(memory_space=pl.ANY)],
        out_specs=[pl.BlockSpec((tm, D), lambda i: (i, 0)),
                   pl.BlockSpec((1, D), lambda i: (0, 0))],
        out_shape=[SDS((T, D), F32), SDS((1, D), F32)],
        scratch_shapes=[pltpu.VMEM((D, DIN), MM), pltpu.SemaphoreType.DMA((NCHIP,))],
        name="bwd_in",
        compiler_params=_cp(("arbitrary",)),
    )(dproj, w_sh, x, dy, g_row, token)


def _grad_w_in_call(h, dproj, token):
    T = h.shape[0]
    tt = min(GRAD_TOKEN_TILE, T)
    nt = T // tt
    dh = D // 2

    def body(h_ref, dp_ref, token_ref, o_ref, acc_ref):
        t = pl.program_id(2)

        @pl.when(t == 0)
        def _():
            acc_ref[...] = jnp.zeros_like(acc_ref)

        acc_ref[...] += lax.dot_general(h_ref[...], dp_ref[...], TN, preferred_element_type=F32)

        @pl.when(t == nt - 1)
        def _():
            o_ref[0] = acc_ref[:, 0:SHW].astype(MM)
            o_ref[1] = acc_ref[:, SHW:2 * SHW].astype(MM)

    return pl.pallas_call(
        body,
        grid=(NCHIP // 2, 2, nt),
        in_specs=[pl.BlockSpec((tt, dh), lambda b, m, t: (t, m)),
                  pl.BlockSpec((tt, 2 * SHW), lambda b, m, t: (t, b)),
                  pl.BlockSpec(memory_space=pl.ANY)],
        out_specs=pl.BlockSpec((2, dh, SHW), lambda b, m, t: (b, m, 0)),
        out_shape=SDS((NCHIP, D, SHW), MM),
        scratch_shapes=[pltpu.VMEM((dh, 2 * SHW), F32)],
        name="grad_w_in",
        compiler_params=_cp(("arbitrary", "arbitrary", "arbitrary")),
    )(h, dproj, token)


def _grad_w_out_call(mix, dy):
    T = mix.shape[0]
    tt = min(GRAD_TOKEN_TILE, T)
    nt = T // tt
    tn = 1024

    def body(m_ref, dy_ref, o_ref, acc_ref):
        t = pl.program_id(1)

        @pl.when(t == 0)
        def _():
            acc_ref[...] = jnp.zeros_like(acc_ref)

        acc_ref[...] += lax.dot_general(m_ref[...], dy_ref[...].astype(MM), TN, preferred_element_type=F32)

        @pl.when(t == nt - 1)
        def _():
            o_ref[...] = acc_ref[...].astype(MM)

    return pl.pallas_call(
        body,
        grid=(D // tn, nt),
        in_specs=[pl.BlockSpec((tt, D), lambda j, t: (t, 0)),
                  pl.BlockSpec((tt, tn), lambda j, t: (t, j))],
        out_specs=pl.BlockSpec((D, tn), lambda j, t: (0, j)),
        out_shape=SDS((D, D), MM),
        scratch_shapes=[pltpu.VMEM((D, tn), F32)],
        name="grad_w_out",
        compiler_params=_cp(("arbitrary", "arbitrary")),
    )(mix, dy)


def _cast_to_slab_call(w, chip_idx, layers, after, name):
    _, R, C = w.shape
    n = len(layers)
    tr = 256

    def body(chip_ref, *refs):
        for k in range(n):
            refs[n + 1 + k][...] = refs[k][...].astype(MM)

    return pl.pallas_call(
        body,
        grid_spec=pltpu.PrefetchScalarGridSpec(
            num_scalar_prefetch=1,
            grid=(R // tr,),
            in_specs=[pl.BlockSpec((1, tr, C), functools.partial(lambda i, chip_ref, l: (l, i, 0), l=l))
                      for l in layers] + [ANY],
            out_specs=[pl.BlockSpec((1, tr, C), lambda i, chip_ref: (chip_ref[0], i, 0))] * n,
        ),
        out_shape=[SDS((NCHIP, R, C), MM)] * n,
        name=name,
        compiler_params=_cp(("arbitrary",)),
    )(chip_idx, *([w] * n), after)


def _adam_call(w, g_parts, m, v, name):
    R, C = w.shape
    tr = R
    for cand in (512, 256, 128, 64, 32, 16, 8):
        if R % cand == 0 and cand * C * 4 <= 1024 * 1024:
            tr = cand
            break
    c1 = 1.0 - B1 ** STEP
    c2 = 1.0 - B2 ** STEP
    ng = len(g_parts)

    def body(*refs):
        w_ref, m_ref, v_ref = refs[0], refs[1 + ng], refs[2 + ng]
        g_ref, d_ref, nm_ref, nv_ref = refs[3 + ng:]
        gv = refs[1][...].astype(F32)
        for k in range(1, ng):
            gv = gv + refs[1 + k][...].astype(F32)
        nm = B1 * m_ref[...] + (1.0 - B1) * gv
        nv = B2 * v_ref[...] + (1.0 - B2) * (gv * gv)
        g_ref[...] = gv
        nm_ref[...] = nm
        nv_ref[...] = nv
        d_ref[...] = -LR * ((nm / c1) / (jnp.sqrt(nv / c2) + ADAM_EPS) + WD * w_ref[...])

    spec = pl.BlockSpec((tr, C), lambda i: (i, 0))
    return pl.pallas_call(
        body,
        grid=(R // tr,),
        in_specs=[spec] * (3 + ng),
        out_specs=[spec] * 4,
        out_shape=[SDS((R, C), F32)] * 4,
        name=name,
        compiler_params=_cp(("arbitrary",)),
    )(w, *g_parts, m, v)


MESH = pl.DeviceIdType.MESH
ANY = pl.BlockSpec(memory_space=pl.ANY)
HBM = pl.BlockSpec(memory_space=pltpu.HBM)
SEMS = pl.BlockSpec(memory_space=pltpu.SEMAPHORE)
EFFECT = pltpu.SideEffectType.DATAFLOW_SIDE_EFFECTING
NDEV = 8


def _hbm(a):
    return pltpu.with_memory_space_constraint(a, pltpu.HBM)


def _place():
    x, y, c = lax.axis_index("x"), lax.axis_index("y"), lax.axis_index("c")
    others = [(1 - x, y), (x, 1 - y), (1 - x, 1 - y)]
    return x, y, c, 2 * x + y, others


def _flipped(x, y, c, r):
    return (1 - x if r & 4 else x, 1 - y if r & 2 else y, 1 - c if r & 1 else c)


def _rcopy(src, dst, ssem, rsem, dev):
    return pltpu.make_async_remote_copy(src_ref=src, dst_ref=dst, send_sem=ssem, recv_sem=rsem,
                                        device_id=dev, device_id_type=MESH)


def _slab(ref, chip, c, halved):
    if not halved:
        return ref.at[chip]
    h = ref.shape[1] // 2
    return ref.at[chip, pl.ds(c * h, h), :]


def _gather_start_call(fulls, n_halved, name):
    K = len(fulls)

    def body(*refs):
        full, ssem, rsem = refs[:K], refs[K:2 * K], refs[2 * K:3 * K]
        x, y, c, me, others = _place()
        for k in range(K):
            for j, (px, py) in enumerate(others):
                part = _slab(full[k], me, c, k < n_halved)
                _rcopy(part, part, ssem[k].at[j], rsem[k].at[j], (px, py, c)).start()

    outs = pl.pallas_call(
        body,
        in_specs=[HBM] * K,
        out_specs=[SEMS] * (2 * K) + [HBM] * K,
        out_shape=[pltpu.SemaphoreType.DMA((3,))] * (2 * K) + [pltpu.HBM(f.shape, f.dtype) for f in fulls],
        input_output_aliases={k: 2 * K + k for k in range(K)},
        name=name,
        compiler_params=pltpu.CompilerParams(has_side_effects=EFFECT),
    )(*[_hbm(f) for f in fulls])
    return list(outs[:K]), list(outs[K:2 * K]), list(outs[2 * K:])


def _gather_wait_call(fulls, ssems, rsems, after, halved, name):
    K = len(fulls)

    def body(*refs):
        full, ssem, rsem = refs[:K], refs[K:2 * K], refs[2 * K:3 * K]
        x, y, c, me, others = _place()
        for k in range(K):
            for j, (px, py) in enumerate(others):
                cp = _rcopy(_slab(full[k], me, c, halved), _slab(full[k], 2 * px + py, c, halved),
                            ssem[k].at[j], rsem[k].at[j], (px, py, c))
                cp.wait_send()
                cp.wait_recv()

    outs = pl.pallas_call(
        body,
        in_specs=[HBM] * K + [SEMS] * (2 * K) + [ANY],
        out_specs=[HBM] * K,
        out_shape=[pltpu.HBM(f.shape, f.dtype) for f in fulls],
        input_output_aliases={k: k for k in range(K)},
        name=name,
        compiler_params=pltpu.CompilerParams(has_side_effects=EFFECT),
    )(*fulls, *ssems, *rsems, after)
    return list(outs)


def _sibling_forward_call(fulls, name):
    K = len(fulls)

    def body(*refs):
        full = refs[:K]
        ssem, rsem = refs[2 * K:]
        x, y, c, me, others = _place()
        cps = []
        for k in range(K):
            for j, (px, py) in enumerate(others):
                mine = _slab(full[k], 2 * px + py, c, True)
                cps.append(_rcopy(mine, mine, ssem.at[3 * k + j], rsem.at[3 * k + j], (x, y, 1 - c)))
        for cp in cps:
            cp.start()
        for k in range(K):
            for j, (px, py) in enumerate(others):
                theirs = _slab(full[k], 2 * px + py, 1 - c, True)
                _rcopy(theirs, theirs, ssem.at[3 * k + j], rsem.at[3 * k + j], (x, y, 1 - c)).wait_recv()
        for cp in cps:
            cp.wait_send()

    outs = pl.pallas_call(
        body,
        in_specs=[ANY] * K,
        out_specs=[ANY] * K,
        out_shape=[SDS(f.shape, f.dtype) for f in fulls],
        input_output_aliases={k: k for k in range(K)},
        scratch_shapes=[pltpu.SemaphoreType.DMA((3 * K,)), pltpu.SemaphoreType.DMA((3 * K,))],
        name=name,
    )(*fulls)
    return list(outs)


def _grad_copies(srcs, lands, ssem, rsem, to_sibling):
    x, y, c, me, others = _place()
    if to_sibling:
        return [_rcopy(s, l, ssem.at[k], rsem.at[k], (x, y, 1 - c)) for k, (s, l) in enumerate(zip(srcs, lands))]
    cps, k = [], 0
    for src, land in zip(srcs, lands):
        if len(src.shape) == 3:
            for j, (px, py) in enumerate(others):
                cps.append(_rcopy(src.at[2 * px + py], land.at[j], ssem.at[k + j], rsem.at[k + j], (px, py, c)))
            k += 3
        else:
            for r in range(1, NDEV):
                cps.append(_rcopy(src, land.at[4 * x + 2 * y + c], ssem.at[k + r - 1], rsem.at[k + r - 1],
                                  _flipped(x, y, c, r)))
            k += NDEV - 1
    return cps


def _grad_start_call(srcs, name, to_sibling=False):
    srcs = list(srcs)
    K = len(srcs)
    if to_sibling:
        lands = [lax.empty(s.shape, s.dtype) for s in srcs]
        n = K
    else:
        lands = [lax.empty(((3,) + s.shape[1:]) if len(s.shape) == 3 else ((NDEV,) + s.shape), s.dtype) for s in srcs]
        n = sum(3 if len(s.shape) == 3 else NDEV - 1 for s in srcs)

    def body(*refs):
        ssem, rsem, token = refs[2 * K], refs[2 * K + 1], refs[-1]
        for cp in _grad_copies(refs[:K], refs[K:2 * K], ssem, rsem, to_sibling):
            cp.start()
        token[...] = jnp.zeros_like(token)

    outs = pl.pallas_call(
        body,
        in_specs=[HBM] * (2 * K),
        out_specs=[SEMS, SEMS] + [HBM] * (2 * K) + [pl.BlockSpec(memory_space=pltpu.VMEM)],
        out_shape=[pltpu.SemaphoreType.DMA((n,)), pltpu.SemaphoreType.DMA((n,))]
        + [pltpu.HBM(a.shape, a.dtype) for a in srcs + lands] + [SDS((8, 128), F32)],
        input_output_aliases={k: 2 + k for k in range(2 * K)},
        name=name,
        compiler_params=pltpu.CompilerParams(has_side_effects=EFFECT),
    )(*[_hbm(a) for a in srcs + lands])
    return list(outs[2:2 + K]), list(outs[2 + K:2 + 2 * K]), outs[0], outs[1], outs[-1]


def _grad_wait_call(srcs, lands, ssem, rsem, after, name, to_sibling=False):
    K = len(srcs)

    def body(*refs):
        for cp in _grad_copies(refs[:K], refs[K:2 * K], refs[2 * K], refs[2 * K + 1], to_sibling):
            cp.wait_send()
            cp.wait_recv()

    arrs = list(srcs) + list(lands)
    outs = pl.pallas_call(
        body,
        in_specs=[HBM] * (2 * K) + [SEMS, SEMS, ANY],
        out_specs=[HBM] * (2 * K),
        out_shape=[pltpu.HBM(a.shape, a.dtype) for a in arrs],
        input_output_aliases={k: k for k in range(2 * K)},
        name=name,
        compiler_params=pltpu.CompilerParams(has_side_effects=EFFECT),
    )(*arrs, ssem, rsem, after)
    return list(outs[:K]), list(outs[K:])


def _small_allreduce_call(a):
    R, C = a.shape

    def body(a_ref, o_ref, recv_ref, ssem, rsem):
        x, y, c, me, others = _place()
        dev = 4 * x + 2 * y + c
        recv_ref[pl.ds(dev, 1)] = a_ref[...][None]
        cps = [_rcopy(a_ref, recv_ref.at[dev], ssem.at[r - 1], rsem.at[r - 1], _flipped(x, y, c, r))
               for r in range(1, NDEV)]
        for cp in cps:
            cp.start()
        for cp in cps:
            cp.wait()
        acc = recv_ref[0]
        for s in range(1, NDEV):
            acc = acc + recv_ref[s]
        o_ref[...] = acc

    return pl.pallas_call(
        body,
        in_specs=[pl.BlockSpec(memory_space=pltpu.VMEM)],
        out_specs=pl.BlockSpec(memory_space=pltpu.VMEM),
        out_shape=SDS((R, C), F32),
        scratch_shapes=[pltpu.VMEM((NDEV, R, C), F32), pltpu.SemaphoreType.DMA((NDEV - 1,)),
                        pltpu.SemaphoreType.DMA((NDEV - 1,))],
        name="small_allreduce",
    )(a)


def _rows_tile(H, C):
    for cand in (512, 256, 128, 64, 32, 16, 8):
        if H % cand == 0 and cand * C * 4 <= 2 * 1024 * 1024:
            return cand
    raise ValueError((H, C))


def _sum_recv_call(own, recv, chip_idx, stack, l):
    _, R, C = own.shape
    tr = _rows_tile(R, C)

    def body(chip_ref, own_ref, r0, r1, r2, stack_ref, o_ref):
        o_ref[...] = (((own_ref[...].astype(F32) + r0[...].astype(F32)) + r1[...].astype(F32))
                      + r2[...].astype(F32)).astype(MM)

    return pl.pallas_call(
        body,
        grid_spec=pltpu.PrefetchScalarGridSpec(
            num_scalar_prefetch=1,
            grid=(R // tr,),
            in_specs=[pl.BlockSpec((1, tr, C), lambda i, chip_ref: (chip_ref[0], i, 0))]
            + [pl.BlockSpec((1, tr, C), functools.partial(lambda i, chip_ref, s: (s, i, 0), s=s)) for s in range(3)]
            + [ANY],
            out_specs=pl.BlockSpec((1, tr, C), lambda i, chip_ref: (l, i, 0)),
        ),
        out_shape=SDS(stack.shape, MM),
        input_output_aliases={5: 0},
        name="grad_sum_recv",
        compiler_params=_cp(("arbitrary",)),
    )(chip_idx, own, recv, recv, recv, stack)


def _sum_small_call(own, recv, dev_idx):
    RS, C = own.shape
    tr = _rows_tile(RS, C)

    def body(dev_ref, own_ref, *refs):
        o_ref = refs[NDEV]
        dev = dev_ref[0]
        acc = jnp.where(dev == 0, own_ref[...], refs[0][0]).astype(F32)
        for s in range(1, NDEV):
            acc = acc + jnp.where(dev == s, own_ref[...], refs[s][0]).astype(F32)
        o_ref[...] = acc

    return pl.pallas_call(
        body,
        grid_spec=pltpu.PrefetchScalarGridSpec(
            num_scalar_prefetch=1,
            grid=(RS // tr,),
            in_specs=[pl.BlockSpec((tr, C), lambda i, dev_ref: (i, 0))]
            + [pl.BlockSpec((1, tr, C), functools.partial(
                lambda i, dev_ref, s: (jnp.where(dev_ref[0] == s, (s + 1) % NDEV, s), i, 0), s=s)) for s in range(NDEV)],
            out_specs=pl.BlockSpec((tr, C), lambda i, dev_ref: (i, 0)),
        ),
        out_shape=SDS((RS, C), F32),
        name="grad_sum_small",
        compiler_params=_cp(("arbitrary",)),
    )(dev_idx, own, *([recv] * NDEV))


SMALL_ROWS_ALIGN = 128


def _pack_small(parts):
    flat = jnp.concatenate([p.reshape(-1) for p in parts])
    rows = -(-flat.shape[0] // (128 * SMALL_ROWS_ALIGN)) * SMALL_ROWS_ALIGN
    flat = jnp.pad(flat, (0, rows * 128 - flat.shape[0]))
    return flat.reshape(rows, 128)


def _unpack_small(packed, like):
    flat = packed.reshape(-1)
    out, off = [], 0
    for p in like:
        n = int(np.prod(p.shape))
        out.append(flat[off:off + n].reshape(p.shape))
        off += n
    return out


def kernel(x, norm_g, w_in, q_norm, k_norm, sinks, w_s, b_s, w_out, loss_target, m_norm_g, m_w_in, m_q_norm, m_k_norm, m_sinks, m_w_s, m_b_s, m_w_out, v_norm_g, v_w_in, v_q_norm, v_k_norm, v_sinks, v_w_s, v_b_s, v_w_out):
    L = norm_g.shape[0]
    xi, yi, ci = lax.axis_index("x"), lax.axis_index("y"), lax.axis_index("c")
    chip_idx = (2 * xi + yi).astype(jnp.int32).reshape(1)
    dev_idx = (4 * xi + 2 * yi + ci).astype(jnp.int32).reshape(1)
    bias = _alibi_bias()
    b2 = _half_sum_matrix()
    tri =jnp.tril(jnp.ones((WIN, WIN), F32))

    rest = list(range(1, L))
    fulls0 = _cast_to_slab_call(w_in, chip_idx, [0], chip_idx, "cast_w_in_0") \
        + _cast_to_slab_call(w_out, chip_idx, [0], chip_idx, "cast_w_out_0")
    ss0, rs0, fulls0 = _gather_start_call(fulls0, 2, "gather_start_0")
    fin = _cast_to_slab_call(w_in, chip_idx, rest, fulls0[0], "cast_w_in")
    fout = _cast_to_slab_call(w_out, chip_idx, rest, fulls0[1], "cast_w_out")
    g_ssems, g_rsems, fulls = _gather_start_call([a for pair in zip(fin, fout) for a in pair], 0, "gather_start")

    saved = []
    xs = x[0]
    dy = loss = None
    for l in range(L):
        if l == 0:
            (w_in_l,) = _gather_wait_call(fulls0[:1], ss0[:1], rs0[:1], fulls[0], True, "gather_wait_0")
            (w_in_l,) = _sibling_forward_call([w_in_l], "gather_forward_0")
        else:
            sl = slice(2 * (l - 1), 2 * l)
            w_in_l, w_out_l = _gather_wait_call(fulls[sl], g_ssems[sl], g_rsems[sl], xs, False, f"gather_wait_{l}")
        proj, h = _fwd_in_call(xs, norm_g[l:l + 1], w_in_l)
        ws_tril = (w_s[l] * tri).astype(MM)
        b_exp = jnp.repeat(b_s[l].T, HD, axis=1)
        wq2 = jnp.tile(q_norm[l:l + 1], (1, 2)) * SCALE
        wk2 = jnp.tile(k_norm[l:l + 1], (1, 2))
        mix = _fwd_mix_call(proj, bias, wq2, wk2, b2, sinks[l], ws_tril, b_exp)
        if l == 0:
            (w_out_l,) = _gather_wait_call(fulls0[1:], ss0[1:], rs0[1:], mix, True, "gather_wait_0_out")
            (w_out_l,) = _sibling_forward_call([w_out_l], "gather_forward_0_out")
        w_out_l = w_out_l.reshape(D, D)
        saved.append((xs, proj, h, mix, ws_tril, b_exp, w_in_l, w_out_l, wq2, wk2))
        if l < L - 1:
            xs = _fwd_out_call(xs, mix, w_out_l)
        else:
            dy, loss = _fwd_out_loss_call(xs, mix, w_out_l, loss_target[0])

    s_in = lax.empty((L, D, SHW), MM)
    s_out = lax.empty((L, SHR, D), MM)
    ws_sums, tiny_sums = [None] * L, [None] * L

    def finish(pending, after):
        nonlocal s_in, s_out
        l, exchanges = pending
        for tag, kinds, srcs, lands, ssem, rsem in exchanges:
            srcs, lands = _grad_wait_call(srcs, lands, ssem, rsem, after, f"grad_wait_{l}{tag}")
            for kind, src, land in zip(kinds, srcs, lands):
                if kind == "in":
                    s_in = _sum_recv_call(src, land, chip_idx, s_in, l)
                elif kind == "out":
                    s_out = _sum_recv_call(src, land, chip_idx, s_out, l)
                elif kind == "ws":
                    ws_sums[l] = _sum_small_call(src, land, dev_idx)
                else:
                    tiny_sums[l] = _sum_small_call(src, land, dev_idx)

    def start(l, tag, kinds, srcs):
        srcs, lands, ssem, rsem, token = _grad_start_call(srcs, f"grad_start_{l}{tag}")
        return (tag, kinds, srcs, lands, ssem, rsem), token

    pending = None
    d_norm_g = [None] * L
    for l in reversed(range(L)):
        xs, proj, h, mix, ws_tril, b_exp, w_in_l, w_out_l, wq2, wk2 = saved[l]
        g_w_out = _grad_w_out_call(mix, dy).reshape(NCHIP, SHR, D)
        exchanges, token = [], jnp.zeros((8, 128), F32)
        if l == 0:
            ex, token = start(l, "_out", ["out"], [g_w_out])
            exchanges.append(ex)
        dmix = _bwd_out_call(dy, w_out_l, token)
        ws_tril_t = jnp.swapaxes(ws_tril, 1, 2)
        dproj, dwq, dwk, dsk, dws, dbs = _bwd_mix_call(
            proj, dmix, bias, wq2, wk2, b2, sinks[l], ws_tril, ws_tril_t, b_exp)
        dwq, dwk = dwq[:, :HD] + dwq[:, HD:], dwk[:, :HD] + dwk[:, HD:]
        g_ws = dws.reshape(NG * WIN, WIN).astype(MM)
        g_tiny = _pack_small([dwq, dwk, dsk[:, 0], dbs]).astype(MM)
        token = jnp.zeros((8, 128), F32)
        if l == 0:
            ex, token = start(l, "_small", ["ws", "tiny"], [g_ws, g_tiny])
            exchanges.append(ex)
        g_w_in = _grad_w_in_call(h, dproj, token)
        if l == 0:
            ex, token = start(l, "", ["in"], [g_w_in])
        else:
            ex, token = start(l, "", ["in", "out", "ws", "tiny"], [g_w_in, g_w_out, g_ws, g_tiny])
        exchanges.append(ex)
        dy, d_norm_g[l] = _bwd_in_call(dproj, w_in_l, xs, dy, norm_g[l:l + 1], token)
        if pending is not None:
            finish(pending, dy)
        pending = (l, exchanges)
    finish(pending, dy)
    grad_x = dy

    swap_srcs, swap_lands, sw_ssem, sw_rsem, _ = _grad_start_call([s_in, s_out], "grad_swap_start", to_sibling=True)
    g_norm_g = _small_allreduce_call(jnp.concatenate(d_norm_g, axis=0))

    def pack_layers(parts):
        return jnp.concatenate([_pack_small([p[l] for p in parts]) for l in range(L)], axis=0)

    ws_rows = (L * NG * WIN, WIN)
    ws_outs = _adam_call(w_s.reshape(ws_rows), [jnp.concatenate(ws_sums, axis=0)],
                         m_w_s.reshape(ws_rows), v_w_s.reshape(ws_rows), "adam_w_s")
    tiny_like = [q_norm, k_norm, sinks, b_s]
    tiny_outs = _adam_call(
        pack_layers(tiny_like), [jnp.concatenate(tiny_sums, axis=0)],
        pack_layers([m_q_norm, m_k_norm, m_sinks, m_b_s]),
        pack_layers([v_q_norm, v_k_norm, v_sinks, v_b_s]), "adam_tiny")
    norm_outs = _adam_call(norm_g, [g_norm_g], m_norm_g, v_norm_g, "adam_norm_g")

    small_done = ws_outs[1][:8] + tiny_outs[1][:8] + norm_outs[1][:1, :WIN]
    (s_in, s_out), (t_in, t_out) = _grad_wait_call(swap_srcs, swap_lands, sw_ssem, sw_rsem, small_done,
                                                   "grad_swap_wait", to_sibling=True)
    g_w_in, d_in, nm_in, nv_in = _adam_call(
        w_in.reshape(L * D, SHW), [s_in.reshape(L * D, SHW), t_in.reshape(L * D, SHW)],
        m_w_in.reshape(L * D, SHW), v_w_in.reshape(L * D, SHW), "adam_w_in")
    g_w_out, d_out, nm_out, nv_out = _adam_call(
        w_out.reshape(L * SHR, D), [s_out.reshape(L * SHR, D), t_out.reshape(L * SHR, D)],
        m_w_out.reshape(L * SHR, D), v_w_out.reshape(L * SHR, D), "adam_w_out")

    def full(i, win, wout):
        tiny = tiny_outs[i]
        rows = tiny.shape[0] // L
        per_layer = [_unpack_small(tiny[l * rows:(l + 1) * rows], [p[l] for p in tiny_like]) for l in range(L)]
        qn, kn, sk, bs = [jnp.stack([per_layer[l][k] for l in range(L)]) for k in range(4)]
        return [norm_outs[i], win.reshape(w_in.shape), qn, kn, sk, ws_outs[i].reshape(w_s.shape), bs,
                wout.reshape(w_out.shape)]

    loss_all = lax.psum(loss[0, 0], ("x", "y", "c"))
    return (loss_all, grad_x[None], *full(0, g_w_in, g_w_out), *full(1, d_in, d_out),
            *full(2, nm_in, nm_out), *full(3, nv_in, nv_out))
```

```python
import functools
import math

import numpy as np
import jax
import jax.numpy as jnp
from jax import lax
from jax.experimental import pallas as pl
from jax.experimental.pallas import tpu as pltpu

F32 = jnp.float32
MM = jnp.bfloat16

D = 2048
HD = 64
DA = 1024
DKV = 256
DG = 1024
NQ, NKV, GRP, NG = 16, 4, 4, 16
WIN = 128
DIN = 5632
C_Q, C_K, C_V, C_GA, C_U, C_VS, C_GB = 0, 1024, 1280, 1536, 2560, 3584, 4608
NCHIP = 4
SHW = DIN // NCHIP
SHR = D // NCHIP
EPS = 1e-6
NEG = -1e30
SCALE = HD ** -0.5
INV_SQRT2 = 1.0 / math.sqrt(2.0)
INV_SQRT_2PI = 1.0 / math.sqrt(2.0 * math.pi)
LR, B1, B2, ADAM_EPS, WD, STEP = 0.001, 0.9, 0.999, 1e-08, 0.01, 10
VMEM_LIMIT = 56 * 1024 * 1024

SDS = jax.ShapeDtypeStruct
NT = (((1,), (1,)), ((), ()))
TN = (((0,), (0,)), ((), ()))


def _cp(sem=None):
    return pltpu.CompilerParams(dimension_semantics=sem, vmem_limit_bytes=VMEM_LIMIT)


def _sigmoid(x):
    return 1.0 / (1.0 + jnp.exp(-x))


def _gelu(x):
    return 0.5 * x * (1.0 + lax.erf(x * INV_SQRT2))


def _gelu_and_grad(x):
    cdf = 0.5 * (1.0 + lax.erf(x * INV_SQRT2))
    return x * cdf, cdf + x * jnp.exp(-0.5 * x * x) * INV_SQRT_2PI


def _alibi_bias():
    slopes = 2.0 ** (-8.0 * np.arange(1, NQ + 1) / NQ)
    dist = (np.arange(WIN)[:, None] + WIN) - np.arange(2 * WIN)[None, :]
    ok = (dist >= 0) & (dist < WIN)
    first = ok & (np.arange(2 * WIN)[None, :] >= WIN)
    val = -slopes[:, None, None] * dist[None].astype(np.float64)
    return jnp.asarray(np.stack([np.where(first[None], val, NEG), np.where(ok[None], val, NEG)]), dtype=F32)


def _half_sum_matrix():
    half = np.arange(LANE) // HD
    return jnp.asarray(half[:, None] == half[None, :], dtype=MM)


LANE = 128
NQT = DA // LANE
NKT = DKV // LANE


class _ColumnsFrom:
    def __init__(self, ref, first):
        self.ref, self.first = ref, first

    def __getitem__(self, idx):
        rows, cols = idx
        return self.ref[rows, self.first + cols.start:self.first + cols.stop]


def _tiles(ref, c0, n):
    return jnp.concatenate([ref[:, c0 + j * LANE:c0 + (j + 1) * LANE] for j in range(n)], axis=0)


def _split(x):
    hi = x.astype(MM)
    return hi, (x - hi.astype(F32)).astype(MM)


def _half_sums(x, b2):
    hi, lo = _split(x)
    return jnp.dot(hi, b2, preferred_element_type=F32) + jnp.dot(lo, b2, preferred_element_type=F32)


def _attn_fwd(pm_ref, kvp_ref, bias_ref, wq2, wk2, b2, sink_ref):
    lo_half = lax.broadcasted_iota(jnp.int32, (1, LANE), 1) < HD
    q_ts = _tiles(pm_ref, C_Q, NQT)
    rq = lax.rsqrt(_half_sums(q_ts * q_ts, b2) * (1.0 / HD) + EPS)
    qs = (q_ts * rq * wq2).astype(MM)
    k_ts = jnp.concatenate([a[:, c0 + t * LANE:c0 + (t + 1) * LANE] for t in range(NKT)
                            for a, c0 in ((kvp_ref, 0), (pm_ref, C_K))], axis=0)
    rk = lax.rsqrt(_half_sums(k_ts * k_ts, b2) * (1.0 / HD) + EPS)
    kn = (k_ts * rk * wk2).astype(MM)
    v_ts = jnp.concatenate([a[:, c0 + t * LANE:c0 + (t + 1) * LANE] for t in range(NKT)
                            for a, c0 in ((kvp_ref, DKV), (pm_ref, C_V))], axis=0).astype(MM)
    ones = jnp.ones((2 * WIN, LANE), MM)
    km, vm = {}, {}
    for hk in range(NKV):
        t, eh = hk // 2, hk % 2
        sel = lo_half if eh == 0 else jnp.logical_not(lo_half)
        rows = slice(t * 2 * WIN, (t + 1) * 2 * WIN)
        k_same = jnp.where(sel, kn[rows], jnp.zeros_like(kn[rows]))
        v_same = jnp.where(sel, v_ts[rows], jnp.zeros_like(v_ts[rows]))
        km[hk, eh], km[hk, 1 - eh] = k_same, pltpu.roll(k_same, HD, axis=1)
        vm[hk, eh], vm[hk, 1 - eh] = v_same, pltpu.roll(v_same, HD, axis=1)
    hs = range(NQ)
    s = [lax.dot_general(qs[(h // 2) * WIN:(h // 2 + 1) * WIN], km[h // GRP, h % 2], NT, preferred_element_type=F32)
         + bias_ref[0, h] for h in hs]
    m = [jnp.maximum(jnp.max(s[h], axis=-1, keepdims=True), sink_ref[h]) for h in hs]
    p = [jnp.exp(s[h] - m[h]) for h in hs]
    pb = [p[h].astype(MM) for h in hs]
    res = [jnp.dot(pb[h], jnp.concatenate([vm[h // GRP, h % 2], ones], axis=1), preferred_element_type=F32) for h in hs]
    esink = [jnp.exp(sink_ref[h] - m[h]) for h in hs]
    inv = [1.0 / (res[h][:, LANE:] + esink[h]) for h in hs]
    heads = [dict(p=p[h], pb=pb[h], inv=inv[h], esink=esink[h], o=res[h][:, :LANE] * inv[h]) for h in hs]
    return dict(lo_half=lo_half, q_ts=q_ts, rq=rq, qs=qs, k_ts=k_ts, rk=rk, km=km, vm=vm, heads=heads)


def _sgu_mix(w_ref, zt, lo_half, j):
    zero = jnp.zeros_like(zt)
    return (jnp.dot(w_ref[2 * j], jnp.where(lo_half, zt, zero), preferred_element_type=F32)
            + jnp.dot(w_ref[2 * j + 1], jnp.where(lo_half, zero, zt), preferred_element_type=F32))


def _fwd_mix_call(proj, bias, wq2, wk2, b2, sinks, ws_tril, b_exp):
    T = proj.shape[0]
    nb = T // WIN
    assert nb % 2 == 0

    def body(sink_ref, pm2_ref, kvp_ref, bias_ref, wq_ref, wk_ref, b2_ref, ws_ref, be_ref, mix2_ref):
        n = pl.program_id(0)
        lo_half = lax.broadcasted_iota(jnp.int32, (1, LANE), 1) < HD
        for sub in range(2):
            pm_ref = pm2_ref.at[pl.ds(sub * WIN, WIN), :]
            mix_ref = mix2_ref.at[pl.ds(sub * WIN, WIN), :]
            prev_kv = kvp_ref if sub == 0 else _ColumnsFrom(pm2_ref.at[pl.ds(0, WIN), :], C_K)
            table = bias_ref.at[pl.ds(jnp.minimum(n, 1), 1)] if sub == 0 else bias_ref.at[pl.ds(1, 1)]
            zu = _gelu(pm_ref[:, C_U:C_U + DG])
            zv = _gelu(pm_ref[:, C_VS:C_VS + DG]).astype(MM)
            mixed = jnp.concatenate(
                [_sgu_mix(ws_ref, zv[:, j * LANE:(j + 1) * LANE], lo_half, j) for j in range(NG // 2)], axis=1)
            mixed = mixed + be_ref[...]
            gb = pm_ref[:, C_GB:C_GB + DG]
            mix_ref[:, DA:DA + DG] = (zu * mixed * (gb * _sigmoid(gb))).astype(MM)
            a = _attn_fwd(pm_ref, prev_kv, table, wq_ref[...], wk_ref[...], b2_ref[...], sink_ref)
            for j in range(NQT):
                cols = slice(j * LANE, (j + 1) * LANE)
                ga = pm_ref[:, C_GA + j * LANE:C_GA + (j + 1) * LANE]
                attn = a["heads"][2 * j]["o"] + a["heads"][2 * j + 1]["o"]
                mix_ref[:, cols] = (attn * (ga * _sigmoid(ga))).astype(MM)

    return pl.pallas_call(
        body,
        grid=(nb // 2,),
        in_specs=[
            pl.BlockSpec(memory_space=pltpu.SMEM),
            pl.BlockSpec((2 * WIN, DIN), lambda n: (n, 0)),
            pl.BlockSpec((WIN, 2 * DKV), lambda n: (jnp.maximum(2 * n - 1, 0), C_K // (2 * DKV))),
            pl.BlockSpec((2, NQ, WIN, 2 * WIN), lambda n: (0, 0, 0, 0)),
            pl.BlockSpec((1, LANE), lambda n: (0, 0)),
            pl.BlockSpec((1, LANE), lambda n: (0, 0)),
            pl.BlockSpec((LANE, LANE), lambda n: (0, 0)),
            pl.BlockSpec((NG, WIN, WIN), lambda n: (0, 0, 0)),
            pl.BlockSpec((WIN, DG), lambda n: (0, 0)),
        ],
        out_specs=pl.BlockSpec((2 * WIN, D), lambda n: (n, 0)),
        out_shape=SDS((T, D), MM),
        name="fwd_mix",
        compiler_params=_cp(("arbitrary",)),
    )(sinks, proj, proj, bias, wq2, wk2, b2, ws_tril, b_exp)


def _bwd_mix_call(proj, dmix, bias, wq2, wk2, b2, sinks, ws_tril, ws_tril_t, b_exp):
    T = proj.shape[0]
    nb = T // WIN
    assert nb % 2 == 0
    ns = nb // 2

    def body(sink_ref, pm2_ref, kvp_ref, dm2_ref, bias_ref, wq_ref, wk_ref, b2_ref, ws_ref, wst_ref, be_ref,
             dp_ref, dwq_ref, dwk_ref, dsk_ref, dws_ref, dbs_ref, carry_ref, dbacc_ref):
        n = pl.program_id(0)

        @pl.when(n == 0)
        def _():
            carry_ref[...] = jnp.zeros_like(carry_ref)
            dbacc_ref[...] = jnp.zeros_like(dbacc_ref)
            dwq_ref[...] = jnp.zeros_like(dwq_ref)
            dwk_ref[...] = jnp.zeros_like(dwk_ref)
            dsk_ref[...] = jnp.zeros_like(dsk_ref)
            dws_ref[...] = jnp.zeros_like(dws_ref)
            dbs_ref[...] = jnp.zeros_like(dbs_ref)

        def one_block(pm_ref, dm_ref, prev_kv, table, crow):
            lo_half = lax.broadcasted_iota(jnp.int32, (1, LANE), 1) < HD
            u = pm_ref[:, C_U:C_U + DG]
            vs = pm_ref[:, C_VS:C_VS + DG]
            gb = pm_ref[:, C_GB:C_GB + DG]
            zu, dzu = _gelu_and_grad(u)
            zv, dzv = _gelu_and_grad(vs)
            zvb = zv.astype(MM)
            mixed = jnp.concatenate(
                [_sgu_mix(ws_ref, zvb[:, j * LANE:(j + 1) * LANE], lo_half, j) for j in range(NG // 2)], axis=1)
            mixed = mixed + be_ref[...]
            sgb = _sigmoid(gb)
            d_sgu = dm_ref[:, DA:DA + DG]
            crow[:, C_GB:DIN] = d_sgu * zu * mixed * (sgb * (1.0 + gb * (1.0 - sgb)))
            d_mixed = d_sgu * zu * (gb * sgb)
            crow[:, C_U:C_VS] = d_sgu * mixed * (gb * sgb) * dzu
            dbacc_ref[...] += d_mixed
            dmb = d_mixed.astype(MM)
            dzv_tiles = [_sgu_mix(wst_ref, dmb[:, j * LANE:(j + 1) * LANE], lo_half, j) for j in range(NG // 2)]
            crow[:, C_VS:C_GB] = jnp.concatenate(dzv_tiles, axis=1) * dzv
            for j in range(NG // 2):
                dt = dmb[:, j * LANE:(j + 1) * LANE]
                zt = zvb[:, j * LANE:(j + 1) * LANE]
                zero = jnp.zeros_like(dt)
                dws_ref[2 * j] += lax.dot_general(jnp.where(lo_half, dt, zero), zt, NT, preferred_element_type=F32)
                dws_ref[2 * j + 1] += lax.dot_general(jnp.where(lo_half, zero, dt), zt, NT, preferred_element_type=F32)

            wq2, wk2, b2 = wq_ref[...], wk_ref[...], b2_ref[...]
            a = _attn_fwd(pm_ref, prev_kv, table, wq2, wk2, b2, sink_ref)
            heads, km, vm, qs = a["heads"], a["km"], a["vm"], a["qs"]

            row_lo = lax.broadcasted_iota(jnp.int32, (LANE, LANE), 0) < HD
            pick = [jnp.where(row_lo, 1.0, 0.0).astype(MM), jnp.where(row_lo, 0.0, 1.0).astype(MM)]
            chan_lo = lax.broadcasted_iota(jnp.int32, (LANE, 1), 0) < HD
            tiles, hs = range(NQT), range(NQ)
            sel_t = [chan_lo, jnp.logical_not(chan_lo)]
            d_o, attn = [], []
            for j in tiles:
                cols = slice(C_GA + j * LANE, C_GA + (j + 1) * LANE)
                ga = pm_ref[:, cols]
                sga = _sigmoid(ga)
                d_gated = dm_ref[:, j * LANE:(j + 1) * LANE]
                attn.append(heads[2 * j]["o"] + heads[2 * j + 1]["o"])
                crow[:, cols] = d_gated * attn[j] * (sga * (1.0 + ga * (1.0 - sga)))
                d_o.append(d_gated * (ga * sga))
            d_ob = [d_o[j].astype(MM) for j in tiles]
            dlt = [(d_o[j] * attn[j]).astype(MM) for j in tiles]
            d_os_t = [(d_o[j] * jnp.where(lo_half, heads[2 * j]["inv"], heads[2 * j + 1]["inv"])).astype(MM).T
                      for j in tiles]
            qs_t = [qs[j * WIN:(j + 1) * WIN].T for j in tiles]
            zero_t = jnp.zeros_like(qs_t[0])
            dv_h = [jnp.dot(jnp.where(sel_t[h % 2], d_os_t[h // 2], zero_t), heads[h]["pb"], preferred_element_type=F32)
                    for h in hs]
            d_p = [lax.dot_general(d_ob[h // 2], vm[h // GRP, h % 2], NT, preferred_element_type=F32) for h in hs]
            delta = [jnp.dot(dlt[h // 2], pick[h % 2], preferred_element_type=F32) for h in hs]
            for h in hs:
                dsk_ref[h:h + 1, :] -= jnp.sum(heads[h]["esink"] * heads[h]["inv"] * delta[h], axis=0, keepdims=True)
            d_s = [(heads[h]["p"] * ((d_p[h] - jnp.concatenate([delta[h], delta[h]], axis=1))
                                     * jnp.concatenate([heads[h]["inv"], heads[h]["inv"]], axis=1))).astype(MM)
                   for h in hs]
            dqs_h = [jnp.dot(d_s[h], km[h // GRP, h % 2], preferred_element_type=F32) for h in hs]
            dqs_tiles = [dqs_h[2 * j] + dqs_h[2 * j + 1] for j in tiles]
            dk_h = [jnp.dot(jnp.where(sel_t[h % 2], qs_t[h // 2], zero_t), d_s[h], preferred_element_type=F32)
                    for h in hs]
            dk_acc, dv_acc = {}, {}
            for h in hs:
                key = (h // GRP, h % 2 == (h // GRP) % 2)
                dk_acc[key] = dk_h[h] if key not in dk_acc else dk_acc[key] + dk_h[h]
                dv_acc[key] = dv_h[h] if key not in dv_acc else dv_acc[key] + dv_h[h]

            dqs_ts = jnp.concatenate(dqs_tiles, axis=0)
            q_ts, rq = a["q_ts"], a["rq"]
            gq = dqs_ts * wq2
            d_q = rq * gq - q_ts * (rq * rq * rq) * (_half_sums(gq * q_ts, b2) * (1.0 / HD))
            dwq_ref[...] += SCALE * jnp.sum(dqs_ts * q_ts * rq, axis=0, keepdims=True)
            for j in range(NQT):
                crow[:, C_Q + j * LANE:C_Q + (j + 1) * LANE] = d_q[j * WIN:(j + 1) * WIN]

            def swap_halves(xt):
                return jnp.concatenate([xt[HD:], xt[:HD]], axis=0)

            dkn_tiles, dv_tiles = [], []
            for t in range(NKT):
                for acc, out in ((dk_acc, dkn_tiles), (dv_acc, dv_tiles)):
                    parts = [acc[hk, True] + swap_halves(acc[hk, False]) for hk in (2 * t, 2 * t + 1)]
                    out.append((parts[0] + parts[1]).T)
            dkn_ts = jnp.concatenate(dkn_tiles, axis=0)
            dv_ts = jnp.concatenate(dv_tiles, axis=0)
            k_ts, rk = a["k_ts"], a["rk"]
            gk = dkn_ts * wk2
            d_k = rk * gk - k_ts * (rk * rk * rk) * (_half_sums(gk * k_ts, b2) * (1.0 / HD))
            dwk_ref[...] += jnp.sum(dkn_ts * k_ts * rk, axis=0, keepdims=True)
            prev_share = []
            for base, val in ((C_K, d_k), (C_V, dv_ts)):
                for t in range(NKT):
                    r0 = t * 2 * WIN
                    crow[:, base + t * LANE:base + (t + 1) * LANE] = val[r0 + WIN:r0 + 2 * WIN]
                    prev_share.append(val[r0:r0 + WIN])
            return jnp.concatenate(prev_share, axis=1)

        @pl.when(n < ns)
        def _():
            dp_ref[0:WIN, :] = carry_ref[0:WIN, :].astype(MM)
            dp_ref[WIN:2 * WIN, C_Q:C_K] = carry_ref[WIN:2 * WIN, C_Q:C_K].astype(MM)
            dp_ref[WIN:2 * WIN, C_GA:DIN] = carry_ref[WIN:2 * WIN, C_GA:DIN].astype(MM)
            waiting_kv = carry_ref[WIN:2 * WIN, C_K:C_GA]
            first = pl.ds(0, WIN)
            second = pl.ds(WIN, WIN)
            share = one_block(pm2_ref.at[first, :], dm2_ref.at[first, :], kvp_ref,
                              bias_ref.at[pl.ds(jnp.minimum(n, 1), 1)], carry_ref.at[first, :])
            dp_ref[WIN:2 * WIN, C_K:C_GA] = (waiting_kv + share).astype(MM)
            share = one_block(pm2_ref.at[second, :], dm2_ref.at[second, :], _ColumnsFrom(pm2_ref.at[first, :], C_K),
                              bias_ref.at[pl.ds(1, 1)], carry_ref.at[second, :])
            carry_ref[0:WIN, C_K:C_GA] += share

        @pl.when(n == ns)
        def _():
            dp_ref[...] = carry_ref[...].astype(MM)
            lo_half = lax.broadcasted_iota(jnp.int32, (8, LANE), 1) < HD
            ones = [jnp.where(lo_half, 1.0, 0.0).astype(MM), jnp.where(lo_half, 0.0, 1.0).astype(MM)]
            hi, lo = _split(dbacc_ref[...])
            for h in range(NG):
                sl = slice((h // 2) * LANE, (h // 2 + 1) * LANE)
                r = (lax.dot_general(ones[h % 2], hi[:, sl], NT, preferred_element_type=F32)
                     + lax.dot_general(ones[h % 2], lo[:, sl], NT, preferred_element_type=F32))
                dbs_ref[h:h + 1, :] = r[0:1, :]
            row = lax.broadcasted_iota(jnp.int32, (WIN, WIN), 0)
            cl = lax.broadcasted_iota(jnp.int32, (WIN, WIN), 1)
            for h in range(NG):
                dws_ref[h] = jnp.where(row >= cl, dws_ref[h], 0.0)

    last = ns - 1
    return pl.pallas_call(
        body,
        grid_spec=pltpu.PrefetchScalarGridSpec(
            num_scalar_prefetch=0,
            grid=(ns + 1,),
            in_specs=[
                pl.BlockSpec(memory_space=pltpu.SMEM),
                pl.BlockSpec((2 * WIN, DIN), lambda n: (jnp.minimum(n, last), 0)),
                pl.BlockSpec((WIN, 2 * DKV),
                             lambda n: (jnp.maximum(2 * jnp.minimum(n, last) - 1, 0), C_K // (2 * DKV))),
                pl.BlockSpec((2 * WIN, D), lambda n: (jnp.minimum(n, last), 0)),
                pl.BlockSpec((2, NQ, WIN, 2 * WIN), lambda n: (0, 0, 0, 0)),
                pl.BlockSpec((1, LANE), lambda n: (0, 0)),
                pl.BlockSpec((1, LANE), lambda n: (0, 0)),
                pl.BlockSpec((LANE, LANE), lambda n: (0, 0)),
                pl.BlockSpec((NG, WIN, WIN), lambda n: (0, 0, 0)),
                pl.BlockSpec((NG, WIN, WIN), lambda n: (0, 0, 0)),
                pl.BlockSpec((WIN, DG), lambda n: (0, 0)),
            ],
            out_specs=[
                pl.BlockSpec((2 * WIN, DIN), lambda n: (jnp.maximum(n - 1, 0), 0)),
                pl.BlockSpec((1, LANE), lambda n: (0, 0)),
                pl.BlockSpec((1, LANE), lambda n: (0, 0)),
                pl.BlockSpec((NQ, WIN), lambda n: (0, 0)),
                pl.BlockSpec((NG, WIN, WIN), lambda n: (0, 0, 0)),
                pl.BlockSpec((NG, WIN), lambda n: (0, 0)),
            ],
            scratch_shapes=[pltpu.VMEM((2 * WIN, DIN), F32), pltpu.VMEM((WIN, DG), F32)],
        ),
        out_shape=[SDS((T, DIN), MM), SDS((1, LANE), F32), SDS((1, LANE), F32), SDS((NQ, WIN), F32),
                   SDS((NG, WIN, WIN), F32), SDS((NG, WIN), F32)],
        name="bwd_mix",
        compiler_params=_cp(("arbitrary",)),
    )(sinks, proj, proj, dmix, bias, wq2, wk2, b2, ws_tril, ws_tril_t, b_exp)


WEIGHT_RESIDENT_ROWS = 256
GRAD_TOKEN_TILE = 1024


def _row_tile(T):
    return min(512, T)


def _fwd_in_call(x, g_row, w_sh):
    T = x.shape[0]
    tm = min(WEIGHT_RESIDENT_ROWS, T)

    def body(x_ref, g_ref, w_hbm, proj_ref, h_ref, w_vmem, sem):
        @pl.when(pl.program_id(0) == 0)
        def _():
            cps = [pltpu.make_async_copy(w_hbm.at[j], w_vmem.at[:, pl.ds(j * SHW, SHW)], sem.at[j]) for j in range(NCHIP)]
            for cp in cps:
                cp.start()
            for cp in cps:
                cp.wait()

        xv = x_ref[...]
        r = lax.rsqrt(jnp.mean(xv * xv, axis=-1, keepdims=True) + EPS)
        h = (xv * r * g_ref[...]).astype(MM)
        h_ref[...] = h
        proj_ref[...] = jnp.dot(h, w_vmem[...], preferred_element_type=F32)

    return pl.pallas_call(
        body,
        grid=(T // tm,),
        in_specs=[pl.BlockSpec((tm, D), lambda i: (i, 0)),
                  pl.BlockSpec((1, D), lambda i: (0, 0)),
                  pl.BlockSpec(memory_space=pl.ANY)],
        out_specs=[pl.BlockSpec((tm, DIN), lambda i: (i, 0)),
                   pl.BlockSpec((tm, D), lambda i: (i, 0))],
        out_shape=[SDS((T, DIN), F32), SDS((T, D), MM)],
        scratch_shapes=[pltpu.VMEM((D, DIN), MM), pltpu.SemaphoreType.DMA((NCHIP,))],
        name="fwd_in",
        compiler_params=_cp(("arbitrary",)),
    )(x, g_row, w_sh)


def _fwd_out_call(x, mix, w_out):
    T = x.shape[0]
    tm = _row_tile(T)

    def body(x_ref, mix_ref, w_ref, y_ref):
        y_ref[...] = x_ref[...] + jnp.dot(mix_ref[...], w_ref[...], preferred_element_type=F32)

    return pl.pallas_call(
        body,
        grid=(T // tm,),
        in_specs=[pl.BlockSpec((tm, D), lambda i: (i, 0)),
                  pl.BlockSpec((tm, D), lambda i: (i, 0)),
                  pl.BlockSpec((D, D), lambda i: (0, 0))],
        out_specs=pl.BlockSpec((tm, D), lambda i: (i, 0)),
        out_shape=SDS((T, D), F32),
        name="fwd_out",
        compiler_params=_cp(("arbitrary",)),
    )(x, mix, w_out)


def _fwd_out_loss_call(x, mix, w_out, target):
    T = x.shape[0]
    tm = _row_tile(T)

    def body(x_ref, mix_ref, w_ref, t_ref, dy_ref, loss_ref):
        @pl.when(pl.program_id(0) == 0)
        def _():
            loss_ref[...] = jnp.zeros_like(loss_ref)

        e = x_ref[...] + jnp.dot(mix_ref[...], w_ref[...], preferred_element_type=F32) - t_ref[...]
        dy_ref[...] = e * (1.0 / D)
        loss_ref[...] += (0.5 / D) * jnp.sum(jnp.sum(e * e, axis=1, keepdims=True), axis=0, keepdims=True)

    return pl.pallas_call(
        body,
        grid=(T // tm,),
        in_specs=[pl.BlockSpec((tm, D), lambda i: (i, 0)),
                  pl.BlockSpec((tm, D), lambda i: (i, 0)),
                  pl.BlockSpec((D, D), lambda i: (0, 0)),
                  pl.BlockSpec((tm, D), lambda i: (i, 0))],
        out_specs=[pl.BlockSpec((tm, D), lambda i: (i, 0)),
                   pl.BlockSpec((1, 1), lambda i: (0, 0))],
        out_shape=[SDS((T, D), F32), SDS((1, 1), F32)],
        name="fwd_out_loss",
        compiler_params=_cp(("arbitrary",)),
    )(x, mix, w_out, target)


def _bwd_out_call(dy, w_out, token):
    T = dy.shape[0]
    tm = _row_tile(T)

    def body(dy_ref, w_ref, token_ref, o_ref):
        o_ref[...] = lax.dot_general(dy_ref[...].astype(MM), w_ref[...], NT, preferred_element_type=F32)

    return pl.pallas_call(
        body,
        grid=(T // tm,),
        in_specs=[pl.BlockSpec((tm, D), lambda i: (i, 0)),
                  pl.BlockSpec((D, D), lambda i: (0, 0)),
                  pl.BlockSpec(memory_space=pl.ANY)],
        out_specs=pl.BlockSpec((tm, D), lambda i: (i, 0)),
        out_shape=SDS((T, D), F32),
        name="bwd_out",
        compiler_params=_cp(("arbitrary",)),
    )(dy, w_out, token)


def _bwd_in_call(dproj, w_sh, x, dy, g_row, token):
    T = x.shape[0]
    tm = min(WEIGHT_RESIDENT_ROWS, T)

    def body(dp_ref, w_hbm, x_ref, dy_ref, g_ref, token_ref, dx_ref, dg_ref, w_vmem, sem):
        @pl.when(pl.program_id(0) == 0)
        def _():
            cps = [pltpu.make_async_copy(w_hbm.at[j], w_vmem.at[:, pl.ds(j * SHW, SHW)], sem.at[j]) for j in range(NCHIP)]
            for cp in cps:
                cp.start()
            dg_ref[...] = jnp.zeros_like(dg_ref)
            for cp in cps:
                cp.wait()

        dh = lax.dot_general(dp_ref[...], w_vmem[...], NT, preferred_element_type=F32)
        xv = x_ref[...]
        r = lax.rsqrt(jnp.mean(xv * xv, axis=-1, keepdims=True) + EPS)
        gd = dh * g_ref[...]
        dx_ref[...] = dy_ref[...] + r * gd - xv * ((r * r * r) * jnp.mean(gd * xv, axis=-1, keepdims=True))
        dg_ref[...] += jnp.sum(dh * xv * r, axis=0, keepdims=True)

    return pl.pallas_call(
        body,
        grid=(T // tm,),
        in_specs=[pl.BlockSpec((tm, DIN), lambda i: (i, 0)),
                  pl.BlockSpec(memory_space=pl.ANY),
                  pl.BlockSpec((tm, D), lambda i: (i, 0)),
                  pl.BlockSpec((tm, D), lambda i: (i, 0)),
                  pl.BlockSpec((1, D), lambda i: (0, 0)),
                  pl.BlockSpec(memory_space=pl.ANY)],
        out_specs=[pl.BlockSpec((tm, D), lambda i: (i, 0)),
                   pl.BlockSpec((1, D), lambda i: (0, 0))],
        out_shape=[SDS((T, D), F32), SDS((1, D), F32)],
        scratch_shapes=[pltpu.VMEM((D, DIN), MM), pltpu.SemaphoreType.DMA((NCHIP,))],
        name="bwd_in",
        compiler_params=_cp(("arbitrary",)),
    )(dproj, w_sh, x, dy, g_row, token)


def _grad_w_in_call(h, dproj, token):
    T = h.shape[0]
    tt = min(GRAD_TOKEN_TILE, T)
    nt = T // tt
    dh = D // 2

    def body(h_ref, dp_ref, token_ref, o_ref, acc_ref):
        t = pl.program_id(2)

        @pl.when(t == 0)
        def _():
            acc_ref[...] = jnp.zeros_like(acc_ref)

        acc_ref[...] += lax.dot_general(h_ref[...], dp_ref[...], TN, preferred_element_type=F32)

        @pl.when(t == nt - 1)
        def _():
            o_ref[0] = acc_ref[:, 0:SHW].astype(MM)
            o_ref[1] = acc_ref[:, SHW:2 * SHW].astype(MM)

    return pl.pallas_call(
        body,
        grid=(NCHIP // 2, 2, nt),
        in_specs=[pl.BlockSpec((tt, dh), lambda b, m, t: (t, m)),
                  pl.BlockSpec((tt, 2 * SHW), lambda b, m, t: (t, b)),
                  pl.BlockSpec(memory_space=pl.ANY)],
        out_specs=pl.BlockSpec((2, dh, SHW), lambda b, m, t: (b, m, 0)),
        out_shape=SDS((NCHIP, D, SHW), MM),
        scratch_shapes=[pltpu.VMEM((dh, 2 * SHW), F32)],
        name="grad_w_in",
        compiler_params=_cp(("arbitrary", "arbitrary", "arbitrary")),
    )(h, dproj, token)


def _grad_w_out_call(mix, dy):
    T = mix.shape[0]
    tt = min(GRAD_TOKEN_TILE, T)
    nt = T // tt
    tn = 1024

    def body(m_ref, dy_ref, o_ref, acc_ref):
        t = pl.program_id(1)

        @pl.when(t == 0)
        def _():
            acc_ref[...] = jnp.zeros_like(acc_ref)

        acc_ref[...] += lax.dot_general(m_ref[...], dy_ref[...].astype(MM), TN, preferred_element_type=F32)

        @pl.when(t == nt - 1)
        def _():
            o_ref[...] = acc_ref[...].astype(MM)

    return pl.pallas_call(
        body,
        grid=(D // tn, nt),
        in_specs=[pl.BlockSpec((tt, D), lambda j, t: (t, 0)),
                  pl.BlockSpec((tt, tn), lambda j, t: (t, j))],
        out_specs=pl.BlockSpec((D, tn), lambda j, t: (0, j)),
        out_shape=SDS((D, D), MM),
        scratch_shapes=[pltpu.VMEM((D, tn), F32)],
        name="grad_w_out",
        compiler_params=_cp(("arbitrary", "arbitrary")),
    )(mix, dy)


def _cast_to_slab_call(w, chip_idx, layers, after, name):
    _, R, C = w.shape
    n = len(layers)
    tr = 256

    def body(chip_ref, *refs):
        for k in range(n):
            refs[n + 1 + k][...] = refs[k][...].astype(MM)

    return pl.pallas_call(
        body,
        grid_spec=pltpu.PrefetchScalarGridSpec(
            num_scalar_prefetch=1,
            grid=(R // tr,),
            in_specs=[pl.BlockSpec((1, tr, C), functools.partial(lambda i, chip_ref, l: (l, i, 0), l=l))
                      for l in layers] + [ANY],
            out_specs=[pl.BlockSpec((1, tr, C), lambda i, chip_ref: (chip_ref[0], i, 0))] * n,
        ),
        out_shape=[SDS((NCHIP, R, C), MM)] * n,
        name=name,
        compiler_params=_cp(("arbitrary",)),
    )(chip_idx, *([w] * n), after)


def _adam_call(w, g_parts, m, v, name):
    R, C = w.shape
    tr = R
    for cand in (512, 256, 128, 64, 32, 16, 8):
        if R % cand == 0 and cand * C * 4 <= 1024 * 1024:
            tr = cand
            break
    c1 = 1.0 - B1 ** STEP
    c2 = 1.0 - B2 ** STEP
    ng = len(g_parts)

    def body(*refs):
        w_ref, m_ref, v_ref = refs[0], refs[1 + ng], refs[2 + ng]
        g_ref, d_ref, nm_ref, nv_ref = refs[3 + ng:]
        gv = refs[1][...].astype(F32)
        for k in range(1, ng):
            gv = gv + refs[1 + k][...].astype(F32)
        nm = B1 * m_ref[...] + (1.0 - B1) * gv
        nv = B2 * v_ref[...] + (1.0 - B2) * (gv * gv)
        g_ref[...] = gv
        nm_ref[...] = nm
        nv_ref[...] = nv
        d_ref[...] = -LR * ((nm / c1) / (jnp.sqrt(nv / c2) + ADAM_EPS) + WD * w_ref[...])

    spec = pl.BlockSpec((tr, C), lambda i: (i, 0))
    return pl.pallas_call(
        body,
        grid=(R // tr,),
        in_specs=[spec] * (3 + ng),
        out_specs=[spec] * 4,
        out_shape=[SDS((R, C), F32)] * 4,
        name=name,
        compiler_params=_cp(("arbitrary",)),
    )(w, *g_parts, m, v)


MESH = pl.DeviceIdType.MESH
ANY = pl.BlockSpec(memory_space=pl.ANY)
HBM = pl.BlockSpec(memory_space=pltpu.HBM)
SEMS = pl.BlockSpec(memory_space=pltpu.SEMAPHORE)
EFFECT = pltpu.SideEffectType.DATAFLOW_SIDE_EFFECTING
NDEV = 8


def _hbm(a):
    return pltpu.with_memory_space_constraint(a, pltpu.HBM)


def _place():
    x, y, c = lax.axis_index("x"), lax.axis_index("y"), lax.axis_index("c")
    others = [(1 - x, y), (x, 1 - y), (1 - x, 1 - y)]
    return x, y, c, 2 * x + y, others


def _flipped(x, y, c, r):
    return (1 - x if r & 4 else x, 1 - y if r & 2 else y, 1 - c if r & 1 else c)


def _rcopy(src, dst, ssem, rsem, dev):
    return pltpu.make_async_remote_copy(src_ref=src, dst_ref=dst, send_sem=ssem, recv_sem=rsem,
                                        device_id=dev, device_id_type=MESH)


def _slab(ref, chip, c, halved):
    if not halved:
        return ref.at[chip]
    h = ref.shape[1] // 2
    return ref.at[chip, pl.ds(c * h, h), :]


def _gather_start_call(fulls, n_halved, name):
    K = len(fulls)

    def body(*refs):
        full, ssem, rsem = refs[:K], refs[K:2 * K], refs[2 * K:3 * K]
        x, y, c, me, others = _place()
        for k in range(K):
            for j, (px, py) in enumerate(others):
                part = _slab(full[k], me, c, k < n_halved)
                _rcopy(part, part, ssem[k].at[j], rsem[k].at[j], (px, py, c)).start()

    outs = pl.pallas_call(
        body,
        in_specs=[HBM] * K,
        out_specs=[SEMS] * (2 * K) + [HBM] * K,
        out_shape=[pltpu.SemaphoreType.DMA((3,))] * (2 * K) + [pltpu.HBM(f.shape, f.dtype) for f in fulls],
        input_output_aliases={k: 2 * K + k for k in range(K)},
        name=name,
        compiler_params=pltpu.CompilerParams(has_side_effects=EFFECT),
    )(*[_hbm(f) for f in fulls])
    return list(outs[:K]), list(outs[K:2 * K]), list(outs[2 * K:])


def _gather_wait_call(fulls, ssems, rsems, after, halved, name):
    K = len(fulls)

    def body(*refs):
        full, ssem, rsem = refs[:K], refs[K:2 * K], refs[2 * K:3 * K]
        x, y, c, me, others = _place()
        for k in range(K):
            for j, (px, py) in enumerate(others):
                cp = _rcopy(_slab(full[k], me, c, halved), _slab(full[k], 2 * px + py, c, halved),
                            ssem[k].at[j], rsem[k].at[j], (px, py, c))
                cp.wait_send()
                cp.wait_recv()

    outs = pl.pallas_call(
        body,
        in_specs=[HBM] * K + [SEMS] * (2 * K) + [ANY],
        out_specs=[HBM] * K,
        out_shape=[pltpu.HBM(f.shape, f.dtype) for f in fulls],
        input_output_aliases={k: k for k in range(K)},
        name=name,
        compiler_params=pltpu.CompilerParams(has_side_effects=EFFECT),
    )(*fulls, *ssems, *rsems, after)
    return list(outs)


def _sibling_forward_call(fulls, name):
    K = len(fulls)

    def body(*refs):
        full = refs[:K]
        ssem, rsem = refs[2 * K:]
        x, y, c, me, others = _place()
        cps = []
        for k in range(K):
            for j, (px, py) in enumerate(others):
                mine = _slab(full[k], 2 * px + py, c, True)
                cps.append(_rcopy(mine, mine, ssem.at[3 * k + j], rsem.at[3 * k + j], (x, y, 1 - c)))
        for cp in cps:
            cp.start()
        for k in range(K):
            for j, (px, py) in enumerate(others):
                theirs = _slab(full[k], 2 * px + py, 1 - c, True)
                _rcopy(theirs, theirs, ssem.at[3 * k + j], rsem.at[3 * k + j], (x, y, 1 - c)).wait_recv()
        for cp in cps:
            cp.wait_send()

    outs = pl.pallas_call(
        body,
        in_specs=[ANY] * K,
        out_specs=[ANY] * K,
        out_shape=[SDS(f.shape, f.dtype) for f in fulls],
        input_output_aliases={k: k for k in range(K)},
        scratch_shapes=[pltpu.SemaphoreType.DMA((3 * K,)), pltpu.SemaphoreType.DMA((3 * K,))],
        name=name,
    )(*fulls)
    return list(outs)


def _grad_copies(srcs, lands, ssem, rsem, to_sibling):
    x, y, c, me, others = _place()
    if to_sibling:
        return [_rcopy(s, l, ssem.at[k], rsem.at[k], (x, y, 1 - c)) for k, (s, l) in enumerate(zip(srcs, lands))]
    cps, k = [], 0
    for src, land in zip(srcs, lands):
        if len(src.shape) == 3:
            for j, (px, py) in enumerate(others):
                cps.append(_rcopy(src.at[2 * px + py], land.at[j], ssem.at[k + j], rsem.at[k + j], (px, py, c)))
            k += 3
        else:
            for r in range(1, NDEV):
                cps.append(_rcopy(src, land.at[4 * x + 2 * y + c], ssem.at[k + r - 1], rsem.at[k + r - 1],
                                  _flipped(x, y, c, r)))
            k += NDEV - 1
    return cps


def _grad_start_call(srcs, name, to_sibling=False):
    srcs = list(srcs)
    K = len(srcs)
    if to_sibling:
        lands = [lax.empty(s.shape, s.dtype) for s in srcs]
        n = K
    else:
        lands = [lax.empty(((3,) + s.shape[1:]) if len(s.shape) == 3 else ((NDEV,) + s.shape), s.dtype) for s in srcs]
        n = sum(3 if len(s.shape) == 3 else NDEV - 1 for s in srcs)

    def body(*refs):
        ssem, rsem, token = refs[2 * K], refs[2 * K + 1], refs[-1]
        for cp in _grad_copies(refs[:K], refs[K:2 * K], ssem, rsem, to_sibling):
            cp.start()
        token[...] = jnp.zeros_like(token)

    outs = pl.pallas_call(
        body,
        in_specs=[HBM] * (2 * K),
        out_specs=[SEMS, SEMS] + [HBM] * (2 * K) + [pl.BlockSpec(memory_space=pltpu.VMEM)],
        out_shape=[pltpu.SemaphoreType.DMA((n,)), pltpu.SemaphoreType.DMA((n,))]
        + [pltpu.HBM(a.shape, a.dtype) for a in srcs + lands] + [SDS((8, 128), F32)],
        input_output_aliases={k: 2 + k for k in range(2 * K)},
        name=name,
        compiler_params=pltpu.CompilerParams(has_side_effects=EFFECT),
    )(*[_hbm(a) for a in srcs + lands])
    return list(outs[2:2 + K]), list(outs[2 + K:2 + 2 * K]), outs[0], outs[1], outs[-1]


def _grad_wait_call(srcs, lands, ssem, rsem, after, name, to_sibling=False):
    K = len(srcs)

    def body(*refs):
        for cp in _grad_copies(refs[:K], refs[K:2 * K], refs[2 * K], refs[2 * K + 1], to_sibling):
            cp.wait_send()
            cp.wait_recv()

    arrs = list(srcs) + list(lands)
    outs = pl.pallas_call(
        body,
        in_specs=[HBM] * (2 * K) + [SEMS, SEMS, ANY],
        out_specs=[HBM] * (2 * K),
        out_shape=[pltpu.HBM(a.shape, a.dtype) for a in arrs],
        input_output_aliases={k: k for k in range(2 * K)},
        name=name,
        compiler_params=pltpu.CompilerParams(has_side_effects=EFFECT),
    )(*arrs, ssem, rsem, after)
    return list(outs[:K]), list(outs[K:])


def _small_allreduce_call(a):
    R, C = a.shape

    def body(a_ref, o_ref, recv_ref, ssem, rsem):
        x, y, c, me, others = _place()
        dev = 4 * x + 2 * y + c
        recv_ref[pl.ds(dev, 1)] = a_ref[...][None]
        cps = [_rcopy(a_ref, recv_ref.at[dev], ssem.at[r - 1], rsem.at[r - 1], _flipped(x, y, c, r))
               for r in range(1, NDEV)]
        for cp in cps:
            cp.start()
        for cp in cps:
            cp.wait()
        acc = recv_ref[0]
        for s in range(1, NDEV):
            acc = acc + recv_ref[s]
        o_ref[...] = acc

    return pl.pallas_call(
        body,
        in_specs=[pl.BlockSpec(memory_space=pltpu.VMEM)],
        out_specs=pl.BlockSpec(memory_space=pltpu.VMEM),
        out_shape=SDS((R, C), F32),
        scratch_shapes=[pltpu.VMEM((NDEV, R, C), F32), pltpu.SemaphoreType.DMA((NDEV - 1,)),
                        pltpu.SemaphoreType.DMA((NDEV - 1,))],
        name="small_allreduce",
    )(a)


def _rows_tile(H, C):
    for cand in (512, 256, 128, 64, 32, 16, 8):
        if H % cand == 0 and cand * C * 4 <= 2 * 1024 * 1024:
            return cand
    raise ValueError((H, C))


def _sum_recv_call(own, recv, chip_idx, stack, l):
    _, R, C = own.shape
    tr = _rows_tile(R, C)

    def body(chip_ref, own_ref, r0, r1, r2, stack_ref, o_ref):
        o_ref[...] = (((own_ref[...].astype(F32) + r0[...].astype(F32)) + r1[...].astype(F32))
                      + r2[...].astype(F32)).astype(MM)

    return pl.pallas_call(
        body,
        grid_spec=pltpu.PrefetchScalarGridSpec(
            num_scalar_prefetch=1,
            grid=(R // tr,),
            in_specs=[pl.BlockSpec((1, tr, C), lambda i, chip_ref: (chip_ref[0], i, 0))]
            + [pl.BlockSpec((1, tr, C), functools.partial(lambda i, chip_ref, s: (s, i, 0), s=s)) for s in range(3)]
            + [ANY],
            out_specs=pl.BlockSpec((1, tr, C), lambda i, chip_ref: (l, i, 0)),
        ),
        out_shape=SDS(stack.shape, MM),
        input_output_aliases={5: 0},
        name="grad_sum_recv",
        compiler_params=_cp(("arbitrary",)),
    )(chip_idx, own, recv, recv, recv, stack)


def _sum_small_call(own, recv, dev_idx):
    RS, C = own.shape
    tr = _rows_tile(RS, C)

    def body(dev_ref, own_ref, *refs):
        o_ref = refs[NDEV]
        dev = dev_ref[0]
        acc = jnp.where(dev == 0, own_ref[...], refs[0][0]).astype(F32)
        for s in range(1, NDEV):
            acc = acc + jnp.where(dev == s, own_ref[...], refs[s][0]).astype(F32)
        o_ref[...] = acc

    return pl.pallas_call(
        body,
        grid_spec=pltpu.PrefetchScalarGridSpec(
            num_scalar_prefetch=1,
            grid=(RS // tr,),
            in_specs=[pl.BlockSpec((tr, C), lambda i, dev_ref: (i, 0))]
            + [pl.BlockSpec((1, tr, C), functools.partial(
                lambda i, dev_ref, s: (jnp.where(dev_ref[0] == s, (s + 1) % NDEV, s), i, 0), s=s)) for s in range(NDEV)],
            out_specs=pl.BlockSpec((tr, C), lambda i, dev_ref: (i, 0)),
        ),
        out_shape=SDS((RS, C), F32),
        name="grad_sum_small",
        compiler_params=_cp(("arbitrary",)),
    )(dev_idx, own, *([recv] * NDEV))


SMALL_ROWS_ALIGN = 128


def _pack_small(parts):
    flat = jnp.concatenate([p.reshape(-1) for p in parts])
    rows = -(-flat.shape[0] // (128 * SMALL_ROWS_ALIGN)) * SMALL_ROWS_ALIGN
    flat = jnp.pad(flat, (0, rows * 128 - flat.shape[0]))
    return flat.reshape(rows, 128)


def _unpack_small(packed, like):
    flat = packed.reshape(-1)
    out, off = [], 0
    for p in like:
        n = int(np.prod(p.shape))
        out.append(flat[off:off + n].reshape(p.shape))
        off += n
    return out


def kernel(x, norm_g, w_in, q_norm, k_norm, sinks, w_s, b_s, w_out, loss_target, m_norm_g, m_w_in, m_q_norm, m_k_norm, m_sinks, m_w_s, m_b_s, m_w_out, v_norm_g, v_w_in, v_q_norm, v_k_norm, v_sinks, v_w_s, v_b_s, v_w_out):
    L = norm_g.shape[0]
    xi, yi, ci = lax.axis_index("x"), lax.axis_index("y"), lax.axis_index("c")
    chip_idx = (2 * xi + yi).astype(jnp.int32).reshape(1)
    dev_idx = (4 * xi + 2 * yi + ci).astype(jnp.int32).reshape(1)
    bias = _alibi_bias()
    b2 = _half_sum_matrix()
    tri =jnp.tril(jnp.ones((WIN, WIN), F32))

    rest = list(range(1, L))
    fulls0 = _cast_to_slab_call(w_in, chip_idx, [0], chip_idx, "cast_w_in_0") \
        + _cast_to_slab_call(w_out, chip_idx, [0], chip_idx, "cast_w_out_0")
    ss0, rs0, fulls0 = _gather_start_call(fulls0, 2, "gather_start_0")
    fin = _cast_to_slab_call(w_in, chip_idx, rest, fulls0[0], "cast_w_in")
    fout = _cast_to_slab_call(w_out, chip_idx, rest, fulls0[1], "cast_w_out")
    g_ssems, g_rsems, fulls = _gather_start_call([a for pair in zip(fin, fout) for a in pair], 0, "gather_start")

    saved = []
    xs = x[0]
    dy = loss = None
    for l in range(L):
        if l == 0:
            (w_in_l,) = _gather_wait_call(fulls0[:1], ss0[:1], rs0[:1], fulls[0], True, "gather_wait_0")
            (w_in_l,) = _sibling_forward_call([w_in_l], "gather_forward_0")
        else:
            sl = slice(2 * (l - 1), 2 * l)
            w_in_l, w_out_l = _gather_wait_call(fulls[sl], g_ssems[sl], g_rsems[sl], xs, False, f"gather_wait_{l}")
        proj, h = _fwd_in_call(xs, norm_g[l:l + 1], w_in_l)
        ws_tril = (w_s[l] * tri).astype(MM)
        b_exp = jnp.repeat(b_s[l].T, HD, axis=1)
        wq2 = jnp.tile(q_norm[l:l + 1], (1, 2)) * SCALE
        wk2 = jnp.tile(k_norm[l:l + 1], (1, 2))
        mix = _fwd_mix_call(proj, bias, wq2, wk2, b2, sinks[l], ws_tril, b_exp)
        if l == 0:
            (w_out_l,) = _gather_wait_call(fulls0[1:], ss0[1:], rs0[1:], mix, True, "gather_wait_0_out")
            (w_out_l,) = _sibling_forward_call([w_out_l], "gather_forward_0_out")
        w_out_l = w_out_l.reshape(D, D)
        saved.append((xs, proj, h, mix, ws_tril, b_exp, w_in_l, w_out_l, wq2, wk2))
        if l < L - 1:
            xs = _fwd_out_call(xs, mix, w_out_l)
        else:
            dy, loss = _fwd_out_loss_call(xs, mix, w_out_l, loss_target[0])

    s_in = lax.empty((L, D, SHW), MM)
    s_out = lax.empty((L, SHR, D), MM)
    ws_sums, tiny_sums = [None] * L, [None] * L

    def finish(pending, after):
        nonlocal s_in, s_out
        l, exchanges = pending
        for tag, kinds, srcs, lands, ssem, rsem in exchanges:
            srcs, lands = _grad_wait_call(srcs, lands, ssem, rsem, after, f"grad_wait_{l}{tag}")
            for kind, src, land in zip(kinds, srcs, lands):
                if kind == "in":
                    s_in = _sum_recv_call(src, land, chip_idx, s_in, l)
                elif kind == "out":
                    s_out = _sum_recv_call(src, land, chip_idx, s_out, l)
                elif kind == "ws":
                    ws_sums[l] = _sum_small_call(src, land, dev_idx)
                else:
                    tiny_sums[l] = _sum_small_call(src, land, dev_idx)

    def start(l, tag, kinds, srcs):
        srcs, lands, ssem, rsem, token = _grad_start_call(srcs, f"grad_start_{l}{tag}")
        return (tag, kinds, srcs, lands, ssem, rsem), token

    pending = None
    d_norm_g = [None] * L
    for l in reversed(range(L)):
        xs, proj, h, mix, ws_tril, b_exp, w_in_l, w_out_l, wq2, wk2 = saved[l]
        g_w_out = _grad_w_out_call(mix, dy).reshape(NCHIP, SHR, D)
        exchanges, token = [], jnp.zeros((8, 128), F32)
        if l == 0:
            ex, token = start(l, "_out", ["out"], [g_w_out])
            exchanges.append(ex)
        dmix = _bwd_out_call(dy, w_out_l, token)
        ws_tril_t = jnp.swapaxes(ws_tril, 1, 2)
        dproj, dwq, dwk, dsk, dws, dbs = _bwd_mix_call(
            proj, dmix, bias, wq2, wk2, b2, sinks[l], ws_tril, ws_tril_t, b_exp)
        dwq, dwk = dwq[:, :HD] + dwq[:, HD:], dwk[:, :HD] + dwk[:, HD:]
        g_ws = dws.reshape(NG * WIN, WIN).astype(MM)
        g_tiny = _pack_small([dwq, dwk, dsk[:, 0], dbs]).astype(MM)
        token = jnp.zeros((8, 128), F32)
        if l == 0:
            ex, token = start(l, "_small", ["ws", "tiny"], [g_ws, g_tiny])
            exchanges.append(ex)
        g_w_in = _grad_w_in_call(h, dproj, token)
        if l == 0:
            ex, token = start(l, "", ["in"], [g_w_in])
        else:
            ex, token = start(l, "", ["in", "out", "ws", "tiny"], [g_w_in, g_w_out, g_ws, g_tiny])
        exchanges.append(ex)
        dy, d_norm_g[l] = _bwd_in_call(dproj, w_in_l, xs, dy, norm_g[l:l + 1], token)
        if pending is not None:
            finish(pending, dy)
        pending = (l, exchanges)
    finish(pending, dy)
    grad_x = dy

    swap_srcs, swap_lands, sw_ssem, sw_rsem, _ = _grad_start_call([s_in, s_out], "grad_swap_start", to_sibling=True)
    g_norm_g = _small_allreduce_call(jnp.concatenate(d_norm_g, axis=0))

    def pack_layers(parts):
        return jnp.concatenate([_pack_small([p[l] for p in parts]) for l in range(L)], axis=0)

    ws_rows = (L * NG * WIN, WIN)
    ws_outs = _adam_call(w_s.reshape(ws_rows), [jnp.concatenate(ws_sums, axis=0)],
                         m_w_s.reshape(ws_rows), v_w_s.reshape(ws_rows), "adam_w_s")
    tiny_like = [q_norm, k_norm, sinks, b_s]
    tiny_outs = _adam_call(
        pack_layers(tiny_like), [jnp.concatenate(tiny_sums, axis=0)],
        pack_layers([m_q_norm, m_k_norm, m_sinks, m_b_s]),
        pack_layers([v_q_norm, v_k_norm, v_sinks, v_b_s]), "adam_tiny")
    norm_outs = _adam_call(norm_g, [g_norm_g], m_norm_g, v_norm_g, "adam_norm_g")

    small_done = ws_outs[1][:8] + tiny_outs[1][:8] + norm_outs[1][:1, :WIN]
    (s_in, s_out), (t_in, t_out) = _grad_wait_call(swap_srcs, swap_lands, sw_ssem, sw_rsem, small_done,
                                                   "grad_swap_wait", to_sibling=True)
    g_w_in, d_in, nm_in, nv_in = _adam_call(
        w_in.reshape(L * D, SHW), [s_in.reshape(L * D, SHW), t_in.reshape(L * D, SHW)],
        m_w_in.reshape(L * D, SHW), v_w_in.reshape(L * D, SHW), "adam_w_in")
    g_w_out, d_out, nm_out, nv_out = _adam_call(
        w_out.reshape(L * SHR, D), [s_out.reshape(L * SHR, D), t_out.reshape(L * SHR, D)],
        m_w_out.reshape(L * SHR, D), v_w_out.reshape(L * SHR, D), "adam_w_out")

    def full(i, win, wout):
        tiny = tiny_outs[i]
        rows = tiny.shape[0] // L
        per_layer = [_unpack_small(tiny[l * rows:(l + 1) * rows], [p[l] for p in tiny_like]) for l in range(L)]
        qn, kn, sk, bs = [jnp.stack([per_layer[l][k] for l in range(L)]) for k in range(4)]
        return [norm_outs[i], win.reshape(w_in.shape), qn, kn, sk, ws_outs[i].reshape(w_s.shape), bs,
                wout.reshape(w_out.shape)]

    loss_all = lax.psum(loss[0, 0], ("x", "y", "c"))
    return (loss_all, grad_x[None], *full(0, g_w_in, g_w_out), *full(1, d_in, d_out),
            *full(2, nm_in, nm_out), *full(3, nv_in, nv_out))
```

```python
import functools
import math

import numpy as np
import jax
import jax.numpy as jnp
from jax import lax
from jax.experimental import pallas as pl
from jax.experimental.pallas import tpu as pltpu

F32 = jnp.float32
MM = jnp.bfloat16

D = 2048
HD = 64
DA = 1024
DKV = 256
DG = 1024
NQ, NKV, GRP, NG = 16, 4, 4, 16
WIN = 128
DIN = 5632
C_Q, C_K, C_V, C_GA, C_U, C_VS, C_GB = 0, 1024, 1280, 1536, 2560, 3584, 4608
NCHIP = 4
SHW = DIN // NCHIP
SHR = D // NCHIP
EPS = 1e-6
NEG = -1e30
SCALE = HD ** -0.5
INV_SQRT2 = 1.0 / math.sqrt(2.0)
INV_SQRT_2PI = 1.0 / math.sqrt(2.0 * math.pi)
LR, B1, B2, ADAM_EPS, WD, STEP = 0.001, 0.9, 0.999, 1e-08, 0.01, 10
VMEM_LIMIT = 56 * 1024 * 1024

SDS = jax.ShapeDtypeStruct
NT = (((1,), (1,)), ((), ()))
TN = (((0,), (0,)), ((), ()))


def _cp(sem=None):
    return pltpu.CompilerParams(dimension_semantics=sem, vmem_limit_bytes=VMEM_LIMIT)


def _sigmoid(x):
    return 1.0 / (1.0 + jnp.exp(-x))


def _gelu(x):
    return 0.5 * x * (1.0 + lax.erf(x * INV_SQRT2))


def _gelu_and_grad(x):
    cdf = 0.5 * (1.0 + lax.erf(x * INV_SQRT2))
    return x * cdf, cdf + x * jnp.exp(-0.5 * x * x) * INV_SQRT_2PI


def _alibi_bias():
    slopes = 2.0 ** (-8.0 * np.arange(1, NQ + 1) / NQ)
    dist = (np.arange(WIN)[:, None] + WIN) - np.arange(2 * WIN)[None, :]
    ok = (dist >= 0) & (dist < WIN)
    first = ok & (np.arange(2 * WIN)[None, :] >= WIN)
    val = -slopes[:, None, None] * dist[None].astype(np.float64)
    return jnp.asarray(np.stack([np.where(first[None], val, NEG), np.where(ok[None], val, NEG)]), dtype=F32)


def _half_sum_matrix():
    half = np.arange(LANE) // HD
    return jnp.asarray(half[:, None] == half[None, :], dtype=MM)


LANE = 128
NQT = DA // LANE
NKT = DKV // LANE


class _ColumnsFrom:
    def __init__(self, ref, first):
        self.ref, self.first = ref, first

    def __getitem__(self, idx):
        rows, cols = idx
        return self.ref[rows, self.first + cols.start:self.first + cols.stop]


def _tiles(ref, c0, n):
    return jnp.concatenate([ref[:, c0 + j * LANE:c0 + (j + 1) * LANE] for j in range(n)], axis=0)


def _split(x):
    hi = x.astype(MM)
    return hi, (x - hi.astype(F32)).astype(MM)


def _half_sums(x, b2):
    hi, lo = _split(x)
    return jnp.dot(hi, b2, preferred_element_type=F32) + jnp.dot(lo, b2, preferred_element_type=F32)


def _attn_fwd(pm_ref, kvp_ref, bias_ref, wq2, wk2, b2, sink_ref, between=None):
    lo_half = lax.broadcasted_iota(jnp.int32, (1, LANE), 1) < HD
    q_ts = _tiles(pm_ref, C_Q, NQT)
    rq = lax.rsqrt(_half_sums(q_ts * q_ts, b2) * (1.0 / HD) + EPS)
    qs = (q_ts * rq * wq2).astype(MM)
    k_ts = jnp.concatenate([a[:, c0 + t * LANE:c0 + (t + 1) * LANE] for t in range(NKT)
                            for a, c0 in ((kvp_ref, 0), (pm_ref, C_K))], axis=0)
    rk = lax.rsqrt(_half_sums(k_ts * k_ts, b2) * (1.0 / HD) + EPS)
    kn = (k_ts * rk * wk2).astype(MM)
    v_ts = jnp.concatenate([a[:, c0 + t * LANE:c0 + (t + 1) * LANE] for t in range(NKT)
                            for a, c0 in ((kvp_ref, DKV), (pm_ref, C_V))], axis=0).astype(MM)
    ones = jnp.ones((2 * WIN, LANE), MM)
    km, vm = {}, {}
    for hk in range(NKV):
        t, eh = hk // 2, hk % 2
        sel = lo_half if eh == 0 else jnp.logical_not(lo_half)
        rows = slice(t * 2 * WIN, (t + 1) * 2 * WIN)
        k_same = jnp.where(sel, kn[rows], jnp.zeros_like(kn[rows]))
        v_same = jnp.where(sel, v_ts[rows], jnp.zeros_like(v_ts[rows]))
        km[hk, eh], km[hk, 1 - eh] = k_same, pltpu.roll(k_same, HD, axis=1)
        vm[hk, eh], vm[hk, 1 - eh] = v_same, pltpu.roll(v_same, HD, axis=1)
    hs = range(NQ)
    s = [lax.dot_general(qs[(h // 2) * WIN:(h // 2 + 1) * WIN], km[h // GRP, h % 2], NT, preferred_element_type=F32)
         + bias_ref[0, h] for h in hs]
    if between is not None:
        between()
    m = [jnp.maximum(jnp.max(s[h], axis=-1, keepdims=True), sink_ref[h]) for h in hs]
    p = [jnp.exp(s[h] - m[h]) for h in hs]
    pb = [p[h].astype(MM) for h in hs]
    res = [jnp.dot(pb[h], jnp.concatenate([vm[h // GRP, h % 2], ones], axis=1), preferred_element_type=F32) for h in hs]
    esink = [jnp.exp(sink_ref[h] - m[h]) for h in hs]
    inv = [1.0 / (res[h][:, LANE:] + esink[h]) for h in hs]
    heads = [dict(p=p[h], pb=pb[h], inv=inv[h], esink=esink[h], o=res[h][:, :LANE] * inv[h]) for h in hs]
    return dict(lo_half=lo_half, q_ts=q_ts, rq=rq, qs=qs, k_ts=k_ts, rk=rk, km=km, vm=vm, heads=heads)


def _sgu_mix(w_ref, zt, lo_half, j):
    zero = jnp.zeros_like(zt)
    return (jnp.dot(w_ref[2 * j], jnp.where(lo_half, zt, zero), preferred_element_type=F32)
            + jnp.dot(w_ref[2 * j + 1], jnp.where(lo_half, zero, zt), preferred_element_type=F32))


def _fwd_mix_call(proj, bias, wq2, wk2, b2, sinks, ws_tril, b_exp, x=None, w_out=None):
    T = proj.shape[0]
    nb = T // WIN
    assert nb % 2 == 0
    ns = nb // 2
    fused = x is not None

    def body(sink_ref, pm2_ref, kvp_ref, bias_ref, wq_ref, wk_ref, b2_ref, ws_ref, be_ref, *rest):
        n = pl.program_id(0)
        if fused:
            x_ref, wo_ref, mix2_ref, y_ref, late_ref = rest
            n = jnp.minimum(n, ns - 1)

            @pl.when(pl.program_id(0) == 0)
            def _():
                late_ref[...] = jnp.zeros_like(late_ref)

        else:
            (mix2_ref,) = rest

        def project(part):
            if fused:
                cols = slice(part * (D // 4), (part + 1) * (D // 4))
                y_ref[:, cols] = x_ref[:, cols] + jnp.dot(late_ref[...], wo_ref[:, cols], preferred_element_type=F32)

        lo_half = lax.broadcasted_iota(jnp.int32, (1, LANE), 1) < HD
        for sub in range(2):
            project(2 * sub)
            pm_ref = pm2_ref.at[pl.ds(sub * WIN, WIN), :]
            mix_ref = mix2_ref.at[pl.ds(sub * WIN, WIN), :]
            prev_kv = kvp_ref if sub == 0 else _ColumnsFrom(pm2_ref.at[pl.ds(0, WIN), :], C_K)
            table = bias_ref.at[pl.ds(jnp.minimum(n, 1), 1)] if sub == 0 else bias_ref.at[pl.ds(1, 1)]
            zu = _gelu(pm_ref[:, C_U:C_U + DG])
            zv = _gelu(pm_ref[:, C_VS:C_VS + DG]).astype(MM)
            mixed = jnp.concatenate(
                [_sgu_mix(ws_ref, zv[:, j * LANE:(j + 1) * LANE], lo_half, j) for j in range(NG // 2)], axis=1)
            mixed = mixed + be_ref[...]
            gb = pm_ref[:, C_GB:C_GB + DG]
            mix_ref[:, DA:DA + DG] = (zu * mixed * (gb * _sigmoid(gb))).astype(MM)
            a = _attn_fwd(pm_ref, prev_kv, table, wq_ref[...], wk_ref[...], b2_ref[...], sink_ref,
                          functools.partial(project, 2 * sub + 1))
            for j in range(NQT):
                cols = slice(j * LANE, (j + 1) * LANE)
                ga = pm_ref[:, C_GA + j * LANE:C_GA + (j + 1) * LANE]
                attn = a["heads"][2 * j]["o"] + a["heads"][2 * j + 1]["o"]
                mix_ref[:, cols] = (attn * (ga * _sigmoid(ga))).astype(MM)
        if fused:
            late_ref[...] = mix2_ref[...]

    def specs(at, late):
        return [
            pl.BlockSpec(memory_space=pltpu.SMEM),
            pl.BlockSpec((2 * WIN, DIN), lambda n: (at(n), 0)),
            pl.BlockSpec((WIN, 2 * DKV), lambda n: (jnp.maximum(2 * at(n) - 1, 0), C_K // (2 * DKV))),
            pl.BlockSpec((2, NQ, WIN, 2 * WIN), lambda n: (0, 0, 0, 0)),
            pl.BlockSpec((1, LANE), lambda n: (0, 0)),
            pl.BlockSpec((1, LANE), lambda n: (0, 0)),
            pl.BlockSpec((LANE, LANE), lambda n: (0, 0)),
            pl.BlockSpec((NG, WIN, WIN), lambda n: (0, 0, 0)),
            pl.BlockSpec((WIN, DG), lambda n: (0, 0)),
        ], pl.BlockSpec((2 * WIN, D), lambda n: (at(n), 0)), pl.BlockSpec((2 * WIN, D), lambda n: (late(n), 0))

    if fused:
        in_specs, rows, late_rows = specs(lambda n: jnp.minimum(n, ns - 1), lambda n: jnp.maximum(n - 1, 0))
        return pl.pallas_call(
            body,
            grid=(ns + 1,),
            in_specs=in_specs + [late_rows, pl.BlockSpec((D, D), lambda n: (0, 0), pipeline_mode=pl.Buffered(1))],
            out_specs=[rows, late_rows],
            out_shape=[SDS((T, D), MM), SDS((T, D), F32)],
            scratch_shapes=[pltpu.VMEM((2 * WIN, D), MM)],
            name="fwd_mix_out",
            compiler_params=_cp(("arbitrary",)),
        )(sinks, proj, proj, bias, wq2, wk2, b2, ws_tril, b_exp, x, w_out)
    in_specs, rows, _ = specs(lambda n: n, lambda n: n)
    return pl.pallas_call(
        body,
        grid=(ns,),
        in_specs=in_specs,
        out_specs=rows,
        out_shape=SDS((T, D), MM),
        name="fwd_mix",
        compiler_params=_cp(("arbitrary",)),
    )(sinks, proj, proj, bias, wq2, wk2, b2, ws_tril, b_exp)


def _bwd_mix_call(proj, dmix, bias, wq2, wk2, b2, sinks, ws_tril, ws_tril_t, b_exp):
    T = proj.shape[0]
    nb = T // WIN
    assert nb % 2 == 0
    ns = nb // 2

    def body(sink_ref, pm2_ref, kvp_ref, dm2_ref, bias_ref, wq_ref, wk_ref, b2_ref, ws_ref, wst_ref, be_ref,
             dp_ref, dwq_ref, dwk_ref, dsk_ref, dws_ref, dbs_ref, carry_ref, dbacc_ref):
        n = pl.program_id(0)

        @pl.when(n == 0)
        def _():
            carry_ref[...] = jnp.zeros_like(carry_ref)
            dbacc_ref[...] = jnp.zeros_like(dbacc_ref)
            dwq_ref[...] = jnp.zeros_like(dwq_ref)
            dwk_ref[...] = jnp.zeros_like(dwk_ref)
            dsk_ref[...] = jnp.zeros_like(dsk_ref)
            dws_ref[...] = jnp.zeros_like(dws_ref)
            dbs_ref[...] = jnp.zeros_like(dbs_ref)

        def one_block(pm_ref, dm_ref, prev_kv, table, crow):
            lo_half = lax.broadcasted_iota(jnp.int32, (1, LANE), 1) < HD
            u = pm_ref[:, C_U:C_U + DG]
            vs = pm_ref[:, C_VS:C_VS + DG]
            gb = pm_ref[:, C_GB:C_GB + DG]
            zu, dzu = _gelu_and_grad(u)
            zv, dzv = _gelu_and_grad(vs)
            zvb = zv.astype(MM)
            mixed = jnp.concatenate(
                [_sgu_mix(ws_ref, zvb[:, j * LANE:(j + 1) * LANE], lo_half, j) for j in range(NG // 2)], axis=1)
            mixed = mixed + be_ref[...]
            sgb = _sigmoid(gb)
            d_sgu = dm_ref[:, DA:DA + DG]
            crow[:, C_GB:DIN] = d_sgu * zu * mixed * (sgb * (1.0 + gb * (1.0 - sgb)))
            d_mixed = d_sgu * zu * (gb * sgb)
            crow[:, C_U:C_VS] = d_sgu * mixed * (gb * sgb) * dzu
            dbacc_ref[...] += d_mixed
            dmb = d_mixed.astype(MM)
            dzv_tiles = [_sgu_mix(wst_ref, dmb[:, j * LANE:(j + 1) * LANE], lo_half, j) for j in range(NG // 2)]
            crow[:, C_VS:C_GB] = jnp.concatenate(dzv_tiles, axis=1) * dzv
            for j in range(NG // 2):
                dt = dmb[:, j * LANE:(j + 1) * LANE]
                zt = zvb[:, j * LANE:(j + 1) * LANE]
                zero = jnp.zeros_like(dt)
                dws_ref[2 * j] += lax.dot_general(jnp.where(lo_half, dt, zero), zt, NT, preferred_element_type=F32)
                dws_ref[2 * j + 1] += lax.dot_general(jnp.where(lo_half, zero, dt), zt, NT, preferred_element_type=F32)

            wq2, wk2, b2 = wq_ref[...], wk_ref[...], b2_ref[...]
            a = _attn_fwd(pm_ref, prev_kv, table, wq2, wk2, b2, sink_ref)
            heads, km, vm, qs = a["heads"], a["km"], a["vm"], a["qs"]

            row_lo = lax.broadcasted_iota(jnp.int32, (LANE, LANE), 0) < HD
            pick = [jnp.where(row_lo, 1.0, 0.0).astype(MM), jnp.where(row_lo, 0.0, 1.0).astype(MM)]
            chan_lo = lax.broadcasted_iota(jnp.int32, (LANE, 1), 0) < HD
            tiles, hs = range(NQT), range(NQ)
            sel_t = [chan_lo, jnp.logical_not(chan_lo)]
            d_o, attn = [], []
            for j in tiles:
                cols = slice(C_GA + j * LANE, C_GA + (j + 1) * LANE)
                ga = pm_ref[:, cols]
                sga = _sigmoid(ga)
                d_gated = dm_ref[:, j * LANE:(j + 1) * LANE]
                attn.append(heads[2 * j]["o"] + heads[2 * j + 1]["o"])
                crow[:, cols] = d_gated * attn[j] * (sga * (1.0 + ga * (1.0 - sga)))
                d_o.append(d_gated * (ga * sga))
            d_ob = [d_o[j].astype(MM) for j in tiles]
            dlt = [(d_o[j] * attn[j]).astype(MM) for j in tiles]
            d_os_t = [(d_o[j] * jnp.where(lo_half, heads[2 * j]["inv"], heads[2 * j + 1]["inv"])).astype(MM).T
                      for j in tiles]
            qs_t = [qs[j * WIN:(j + 1) * WIN].T for j in tiles]
            zero_t = jnp.zeros_like(qs_t[0])
            dv_h = [jnp.dot(jnp.where(sel_t[h % 2], d_os_t[h // 2], zero_t), heads[h]["pb"], preferred_element_type=F32)
                    for h in hs]
            d_p = [lax.dot_general(d_ob[h // 2], vm[h // GRP, h % 2], NT, preferred_element_type=F32) for h in hs]
            delta = [jnp.dot(dlt[h // 2], pick[h % 2], preferred_element_type=F32) for h in hs]
            for h in hs:
                dsk_ref[h:h + 1, :] -= jnp.sum(heads[h]["esink"] * heads[h]["inv"] * delta[h], axis=0, keepdims=True)
            d_s = [(heads[h]["p"] * ((d_p[h] - jnp.concatenate([delta[h], delta[h]], axis=1))
                                     * jnp.concatenate([heads[h]["inv"], heads[h]["inv"]], axis=1))).astype(MM)
                   for h in hs]
            dqs_h = [jnp.dot(d_s[h], km[h // GRP, h % 2], preferred_element_type=F32) for h in hs]
            dqs_tiles = [dqs_h[2 * j] + dqs_h[2 * j + 1] for j in tiles]
            dk_h = [jnp.dot(jnp.where(sel_t[h % 2], qs_t[h // 2], zero_t), d_s[h], preferred_element_type=F32)
                    for h in hs]
            dk_acc, dv_acc = {}, {}
            for h in hs:
                key = (h // GRP, h % 2 == (h // GRP) % 2)
                dk_acc[key] = dk_h[h] if key not in dk_acc else dk_acc[key] + dk_h[h]
                dv_acc[key] = dv_h[h] if key not in dv_acc else dv_acc[key] + dv_h[h]

            dqs_ts = jnp.concatenate(dqs_tiles, axis=0)
            q_ts, rq = a["q_ts"], a["rq"]
            gq = dqs_ts * wq2
            d_q = rq * gq - q_ts * (rq * rq * rq) * (_half_sums(gq * q_ts, b2) * (1.0 / HD))
            dwq_ref[...] += SCALE * jnp.sum(dqs_ts * q_ts * rq, axis=0, keepdims=True)
            for j in range(NQT):
                crow[:, C_Q + j * LANE:C_Q + (j + 1) * LANE] = d_q[j * WIN:(j + 1) * WIN]

            def swap_halves(xt):
                return jnp.concatenate([xt[HD:], xt[:HD]], axis=0)

            dkn_tiles, dv_tiles = [], []
            for t in range(NKT):
                for acc, out in ((dk_acc, dkn_tiles), (dv_acc, dv_tiles)):
                    parts = [acc[hk, True] + swap_halves(acc[hk, False]) for hk in (2 * t, 2 * t + 1)]
                    out.append((parts[0] + parts[1]).T)
            dkn_ts = jnp.concatenate(dkn_tiles, axis=0)
            dv_ts = jnp.concatenate(dv_tiles, axis=0)
            k_ts, rk = a["k_ts"], a["rk"]
            gk = dkn_ts * wk2
            d_k = rk * gk - k_ts * (rk * rk * rk) * (_half_sums(gk * k_ts, b2) * (1.0 / HD))
            dwk_ref[...] += jnp.sum(dkn_ts * k_ts * rk, axis=0, keepdims=True)
            prev_share = []
            for base, val in ((C_K, d_k), (C_V, dv_ts)):
                for t in range(NKT):
                    r0 = t * 2 * WIN
                    crow[:, base + t * LANE:base + (t + 1) * LANE] = val[r0 + WIN:r0 + 2 * WIN]
                    prev_share.append(val[r0:r0 + WIN])
            return jnp.concatenate(prev_share, axis=1)

        @pl.when(n < ns)
        def _():
            dp_ref[0:WIN, :] = carry_ref[0:WIN, :].astype(MM)
            dp_ref[WIN:2 * WIN, C_Q:C_K] = carry_ref[WIN:2 * WIN, C_Q:C_K].astype(MM)
            dp_ref[WIN:2 * WIN, C_GA:DIN] = carry_ref[WIN:2 * WIN, C_GA:DIN].astype(MM)
            waiting_kv = carry_ref[WIN:2 * WIN, C_K:C_GA]
            first = pl.ds(0, WIN)
            second = pl.ds(WIN, WIN)
            share = one_block(pm2_ref.at[first, :], dm2_ref.at[first, :], kvp_ref,
                              bias_ref.at[pl.ds(jnp.minimum(n, 1), 1)], carry_ref.at[first, :])
            dp_ref[WIN:2 * WIN, C_K:C_GA] = (waiting_kv + share).astype(MM)
            share = one_block(pm2_ref.at[second, :], dm2_ref.at[second, :], _ColumnsFrom(pm2_ref.at[first, :], C_K),
                              bias_ref.at[pl.ds(1, 1)], carry_ref.at[second, :])
            carry_ref[0:WIN, C_K:C_GA] += share

        @pl.when(n == ns)
        def _():
            dp_ref[...] = carry_ref[...].astype(MM)
            lo_half = lax.broadcasted_iota(jnp.int32, (8, LANE), 1) < HD
            ones = [jnp.where(lo_half, 1.0, 0.0).astype(MM), jnp.where(lo_half, 0.0, 1.0).astype(MM)]
            hi, lo = _split(dbacc_ref[...])
            for h in range(NG):
                sl = slice((h // 2) * LANE, (h // 2 + 1) * LANE)
                r = (lax.dot_general(ones[h % 2], hi[:, sl], NT, preferred_element_type=F32)
                     + lax.dot_general(ones[h % 2], lo[:, sl], NT, preferred_element_type=F32))
                dbs_ref[h:h + 1, :] = r[0:1, :]
            row = lax.broadcasted_iota(jnp.int32, (WIN, WIN), 0)
            cl = lax.broadcasted_iota(jnp.int32, (WIN, WIN), 1)
            for h in range(NG):
                dws_ref[h] = jnp.where(row >= cl, dws_ref[h], 0.0)

    last = ns - 1
    return pl.pallas_call(
        body,
        grid_spec=pltpu.PrefetchScalarGridSpec(
            num_scalar_prefetch=0,
            grid=(ns + 1,),
            in_specs=[
                pl.BlockSpec(memory_space=pltpu.SMEM),
                pl.BlockSpec((2 * WIN, DIN), lambda n: (jnp.minimum(n, last), 0)),
                pl.BlockSpec((WIN, 2 * DKV),
                             lambda n: (jnp.maximum(2 * jnp.minimum(n, last) - 1, 0), C_K // (2 * DKV))),
                pl.BlockSpec((2 * WIN, D), lambda n: (jnp.minimum(n, last), 0)),
                pl.BlockSpec((2, NQ, WIN, 2 * WIN), lambda n: (0, 0, 0, 0)),
                pl.BlockSpec((1, LANE), lambda n: (0, 0)),
                pl.BlockSpec((1, LANE), lambda n: (0, 0)),
                pl.BlockSpec((LANE, LANE), lambda n: (0, 0)),
                pl.BlockSpec((NG, WIN, WIN), lambda n: (0, 0, 0)),
                pl.BlockSpec((NG, WIN, WIN), lambda n: (0, 0, 0)),
                pl.BlockSpec((WIN, DG), lambda n: (0, 0)),
            ],
            out_specs=[
                pl.BlockSpec((2 * WIN, DIN), lambda n: (jnp.maximum(n - 1, 0), 0)),
                pl.BlockSpec((1, LANE), lambda n: (0, 0)),
                pl.BlockSpec((1, LANE), lambda n: (0, 0)),
                pl.BlockSpec((NQ, WIN), lambda n: (0, 0)),
                pl.BlockSpec((NG, WIN, WIN), lambda n: (0, 0, 0)),
                pl.BlockSpec((NG, WIN), lambda n: (0, 0)),
            ],
            scratch_shapes=[pltpu.VMEM((2 * WIN, DIN), F32), pltpu.VMEM((WIN, DG), F32)],
        ),
        out_shape=[SDS((T, DIN), MM), SDS((1, LANE), F32), SDS((1, LANE), F32), SDS((NQ, WIN), F32),
                   SDS((NG, WIN, WIN), F32), SDS((NG, WIN), F32)],
        name="bwd_mix",
        compiler_params=_cp(("arbitrary",)),
    )(sinks, proj, proj, dmix, bias, wq2, wk2, b2, ws_tril, ws_tril_t, b_exp)


WEIGHT_RESIDENT_ROWS = 256
GRAD_TOKEN_TILE = 1024


def _row_tile(T):
    return min(512, T)


def _fwd_in_call(x, g_row, w_sh):
    T = x.shape[0]
    tm = min(WEIGHT_RESIDENT_ROWS, T)

    def body(x_ref, g_ref, w_hbm, proj_ref, h_ref, w_vmem, sem):
        @pl.when(pl.program_id(0) == 0)
        def _():
            cps = [pltpu.make_async_copy(w_hbm.at[j], w_vmem.at[:, pl.ds(j * SHW, SHW)], sem.at[j]) for j in range(NCHIP)]
            for cp in cps:
                cp.start()
            for cp in cps:
                cp.wait()

        xv = x_ref[...]
        r = lax.rsqrt(jnp.mean(xv * xv, axis=-1, keepdims=True) + EPS)
        h = (xv * r * g_ref[...]).astype(MM)
        h_ref[...] = h
        proj_ref[...] = jnp.dot(h, w_vmem[...], preferred_element_type=F32)

    return pl.pallas_call(
        body,
        grid=(T // tm,),
        in_specs=[pl.BlockSpec((tm, D), lambda i: (i, 0)),
                  pl.BlockSpec((1, D), lambda i: (0, 0)),
                  pl.BlockSpec(memory_space=pl.ANY)],
        out_specs=[pl.BlockSpec((tm, DIN), lambda i: (i, 0)),
                   pl.BlockSpec((tm, D), lambda i: (i, 0))],
        out_shape=[SDS((T, DIN), F32), SDS((T, D), MM)],
        scratch_shapes=[pltpu.VMEM((D, DIN), MM), pltpu.SemaphoreType.DMA((NCHIP,))],
        name="fwd_in",
        compiler_params=_cp(("arbitrary",)),
    )(x, g_row, w_sh)


def _fwd_out_call(x, mix, w_out):
    T = x.shape[0]
    tm = _row_tile(T)

    def body(x_ref, mix_ref, w_ref, y_ref):
        y_ref[...] = x_ref[...] + jnp.dot(mix_ref[...], w_ref[...], preferred_element_type=F32)

    return pl.pallas_call(
        body,
        grid=(T // tm,),
        in_specs=[pl.BlockSpec((tm, D), lambda i: (i, 0)),
                  pl.BlockSpec((tm, D), lambda i: (i, 0)),
                  pl.BlockSpec((D, D), lambda i: (0, 0))],
        out_specs=pl.BlockSpec((tm, D), lambda i: (i, 0)),
        out_shape=SDS((T, D), F32),
        name="fwd_out",
        compiler_params=_cp(("arbitrary",)),
    )(x, mix, w_out)


def _fwd_out_loss_call(x, mix, w_out, target):
    T = x.shape[0]
    tm = _row_tile(T)

    def body(x_ref, mix_ref, w_ref, t_ref, dy_ref, loss_ref):
        @pl.when(pl.program_id(0) == 0)
        def _():
            loss_ref[...] = jnp.zeros_like(loss_ref)

        e = x_ref[...] + jnp.dot(mix_ref[...], w_ref[...], preferred_element_type=F32) - t_ref[...]
        dy_ref[...] = e * (1.0 / D)
        loss_ref[...] += (0.5 / D) * jnp.sum(jnp.sum(e * e, axis=1, keepdims=True), axis=0, keepdims=True)

    return pl.pallas_call(
        body,
        grid=(T // tm,),
        in_specs=[pl.BlockSpec((tm, D), lambda i: (i, 0)),
                  pl.BlockSpec((tm, D), lambda i: (i, 0)),
                  pl.BlockSpec((D, D), lambda i: (0, 0)),
                  pl.BlockSpec((tm, D), lambda i: (i, 0))],
        out_specs=[pl.BlockSpec((tm, D), lambda i: (i, 0)),
                   pl.BlockSpec((1, 1), lambda i: (0, 0))],
        out_shape=[SDS((T, D), F32), SDS((1, 1), F32)],
        name="fwd_out_loss",
        compiler_params=_cp(("arbitrary",)),
    )(x, mix, w_out, target)


def _bwd_out_call(dy, w_out, token):
    T = dy.shape[0]
    tm = _row_tile(T)

    def body(dy_ref, w_ref, token_ref, o_ref):
        o_ref[...] = lax.dot_general(dy_ref[...].astype(MM), w_ref[...], NT, preferred_element_type=F32)

    return pl.pallas_call(
        body,
        grid=(T // tm,),
        in_specs=[pl.BlockSpec((tm, D), lambda i: (i, 0)),
                  pl.BlockSpec((D, D), lambda i: (0, 0)),
                  pl.BlockSpec(memory_space=pl.ANY)],
        out_specs=pl.BlockSpec((tm, D), lambda i: (i, 0)),
        out_shape=SDS((T, D), F32),
        name="bwd_out",
        compiler_params=_cp(("arbitrary",)),
    )(dy, w_out, token)


def _bwd_in_call(dproj, w_sh, x, dy, g_row, token):
    T = x.shape[0]
    tm = min(WEIGHT_RESIDENT_ROWS, T)

    def body(dp_ref, w_hbm, x_ref, dy_ref, g_ref, token_ref, dx_ref, dg_ref, w_vmem, sem):
        @pl.when(pl.program_id(0) == 0)
        def _():
            cps = [pltpu.make_async_copy(w_hbm.at[j], w_vmem.at[:, pl.ds(j * SHW, SHW)], sem.at[j]) for j in range(NCHIP)]
            for cp in cps:
                cp.start()
            dg_ref[...] = jnp.zeros_like(dg_ref)
            for cp in cps:
                cp.wait()

        dh = lax.dot_general(dp_ref[...], w_vmem[...], NT, preferred_element_type=F32)
        xv = x_ref[...]
        r = lax.rsqrt(jnp.mean(xv * xv, axis=-1, keepdims=True) + EPS)
        gd = dh * g_ref[...]
        dx_ref[...] = dy_ref[...] + r * gd - xv * ((r * r * r) * jnp.mean(gd * xv, axis=-1, keepdims=True))
        dg_ref[...] += jnp.sum(dh * xv * r, axis=0, keepdims=True)

    return pl.pallas_call(
        body,
        grid=(T // tm,),
        in_specs=[pl.BlockSpec((tm, DIN), lambda i: (i, 0)),
                  pl.BlockSpec(memory_space=pl.ANY),
                  pl.BlockSpec((tm, D), lambda i: (i, 0)),
                  pl.BlockSpec((tm, D), lambda i: (i, 0)),
                  pl.BlockSpec((1, D), lambda i: (0, 0)),
                  pl.BlockSpec(memory_space=pl.ANY)],
        out_specs=[pl.BlockSpec((tm, D), lambda i: (i, 0)),
                   pl.BlockSpec((1, D), lambda i: (0, 0))],
        out_shape=[SDS((T, D), F32), SDS((1, D), F32)],
        scratch_shapes=[pltpu.VMEM((D, DIN), MM), pltpu.SemaphoreType.DMA((NCHIP,))],
        name="bwd_in",
        compiler_params=_cp(("arbitrary",)),
    )(dproj, w_sh, x, dy, g_row, token)


def _grad_w_in_call(h, dproj, token):
    T = h.shape[0]
    tt = min(GRAD_TOKEN_TILE, T)
    nt = T // tt
    dh = D // 2

    def body(h_ref, dp_ref, token_ref, o_ref, acc_ref):
        t = pl.program_id(2)

        @pl.when(t == 0)
        def _():
            acc_ref[...] = jnp.zeros_like(acc_ref)

        acc_ref[...] += lax.dot_general(h_ref[...], dp_ref[...], TN, preferred_element_type=F32)

        @pl.when(t == nt - 1)
        def _():
            o_ref[0] = acc_ref[:, 0:SHW].astype(MM)
            o_ref[1] = acc_ref[:, SHW:2 * SHW].astype(MM)

    return pl.pallas_call(
        body,
        grid=(NCHIP // 2, 2, nt),
        in_specs=[pl.BlockSpec((tt, dh), lambda b, m, t: (t, m)),
                  pl.BlockSpec((tt, 2 * SHW), lambda b, m, t: (t, b)),
                  pl.BlockSpec(memory_space=pl.ANY)],
        out_specs=pl.BlockSpec((2, dh, SHW), lambda b, m, t: (b, m, 0)),
        out_shape=SDS((NCHIP, D, SHW), MM),
        scratch_shapes=[pltpu.VMEM((dh, 2 * SHW), F32)],
        name="grad_w_in",
        compiler_params=_cp(("arbitrary", "arbitrary", "arbitrary")),
    )(h, dproj, token)


def _grad_w_out_call(mix, dy):
    T = mix.shape[0]
    tt = min(GRAD_TOKEN_TILE, T)
    nt = T // tt
    tn = 1024

    def body(m_ref, dy_ref, o_ref, acc_ref):
        t = pl.program_id(1)

        @pl.when(t == 0)
        def _():
            acc_ref[...] = jnp.zeros_like(acc_ref)

        acc_ref[...] += lax.dot_general(m_ref[...], dy_ref[...].astype(MM), TN, preferred_element_type=F32)

        @pl.when(t == nt - 1)
        def _():
            o_ref[...] = acc_ref[...].astype(MM)

    return pl.pallas_call(
        body,
        grid=(D // tn, nt),
        in_specs=[pl.BlockSpec((tt, D), lambda j, t: (t, 0)),
                  pl.BlockSpec((tt, tn), lambda j, t: (t, j))],
        out_specs=pl.BlockSpec((D, tn), lambda j, t: (0, j)),
        out_shape=SDS((D, D), MM),
        scratch_shapes=[pltpu.VMEM((D, tn), F32)],
        name="grad_w_out",
        compiler_params=_cp(("arbitrary", "arbitrary")),
    )(mix, dy)


def _cast_to_slab_call(w, chip_idx, layers, after, name):
    _, R, C = w.shape
    n = len(layers)
    tr = 256

    def body(chip_ref, *refs):
        for k in range(n):
            refs[n + 1 + k][...] = refs[k][...].astype(MM)

    return pl.pallas_call(
        body,
        grid_spec=pltpu.PrefetchScalarGridSpec(
            num_scalar_prefetch=1,
            grid=(R // tr,),
            in_specs=[pl.BlockSpec((1, tr, C), functools.partial(lambda i, chip_ref, l: (l, i, 0), l=l))
                      for l in layers] + [ANY],
            out_specs=[pl.BlockSpec((1, tr, C), lambda i, chip_ref: (chip_ref[0], i, 0))] * n,
        ),
        out_shape=[SDS((NCHIP, R, C), MM)] * n,
        name=name,
        compiler_params=_cp(("arbitrary",)),
    )(chip_idx, *([w] * n), after)


def _adam_call(w, g_parts, m, v, name):
    R, C = w.shape
    tr = R
    for cand in (512, 256, 128, 64, 32, 16, 8):
        if R % cand == 0 and cand * C * 4 <= 1024 * 1024:
            tr = cand
            break
    c1 = 1.0 - B1 ** STEP
    c2 = 1.0 - B2 ** STEP
    ng = len(g_parts)

    def body(*refs):
        w_ref, m_ref, v_ref = refs[0], refs[1 + ng], refs[2 + ng]
        g_ref, d_ref, nm_ref, nv_ref = refs[3 + ng:]
        gv = refs[1][...].astype(F32)
        for k in range(1, ng):
            gv = gv + refs[1 + k][...].astype(F32)
        nm = B1 * m_ref[...] + (1.0 - B1) * gv
        nv = B2 * v_ref[...] + (1.0 - B2) * (gv * gv)
        g_ref[...] = gv
        nm_ref[...] = nm
        nv_ref[...] = nv
        d_ref[...] = -LR * ((nm / c1) / (jnp.sqrt(nv / c2) + ADAM_EPS) + WD * w_ref[...])

    spec = pl.BlockSpec((tr, C), lambda i: (i, 0))
    return pl.pallas_call(
        body,
        grid=(R // tr,),
        in_specs=[spec] * (3 + ng),
        out_specs=[spec] * 4,
        out_shape=[SDS((R, C), F32)] * 4,
        name=name,
        compiler_params=_cp(("arbitrary",)),
    )(w, *g_parts, m, v)


MESH = pl.DeviceIdType.MESH
ANY = pl.BlockSpec(memory_space=pl.ANY)
HBM = pl.BlockSpec(memory_space=pltpu.HBM)
SEMS = pl.BlockSpec(memory_space=pltpu.SEMAPHORE)
EFFECT = pltpu.SideEffectType.DATAFLOW_SIDE_EFFECTING
NDEV = 8


def _hbm(a):
    return pltpu.with_memory_space_constraint(a, pltpu.HBM)


def _place():
    x, y, c = lax.axis_index("x"), lax.axis_index("y"), lax.axis_index("c")
    others = [(1 - x, y), (x, 1 - y), (1 - x, 1 - y)]
    return x, y, c, 2 * x + y, others


def _flipped(x, y, c, r):
    return (1 - x if r & 4 else x, 1 - y if r & 2 else y, 1 - c if r & 1 else c)


def _rcopy(src, dst, ssem, rsem, dev):
    return pltpu.make_async_remote_copy(src_ref=src, dst_ref=dst, send_sem=ssem, recv_sem=rsem,
                                        device_id=dev, device_id_type=MESH)


def _slab(ref, chip, c, halved):
    if not halved:
        return ref.at[chip]
    h = ref.shape[1] // 2
    return ref.at[chip, pl.ds(c * h, h), :]


def _gather_start_call(fulls, n_halved, name):
    K = len(fulls)

    def body(*refs):
        full, ssem, rsem = refs[:K], refs[K:2 * K], refs[2 * K:3 * K]
        x, y, c, me, others = _place()
        for k in range(K):
            for j, (px, py) in enumerate(others):
                part = _slab(full[k], me, c, k < n_halved)
                _rcopy(part, part, ssem[k].at[j], rsem[k].at[j], (px, py, c)).start()

    outs = pl.pallas_call(
        body,
        in_specs=[HBM] * K,
        out_specs=[SEMS] * (2 * K) + [HBM] * K,
        out_shape=[pltpu.SemaphoreType.DMA((3,))] * (2 * K) + [pltpu.HBM(f.shape, f.dtype) for f in fulls],
        input_output_aliases={k: 2 * K + k for k in range(K)},
        name=name,
        compiler_params=pltpu.CompilerParams(has_side_effects=EFFECT),
    )(*[_hbm(f) for f in fulls])
    return list(outs[:K]), list(outs[K:2 * K]), list(outs[2 * K:])


def _gather_wait_call(fulls, ssems, rsems, after, halved, name):
    K = len(fulls)

    def body(*refs):
        full, ssem, rsem = refs[:K], refs[K:2 * K], refs[2 * K:3 * K]
        x, y, c, me, others = _place()
        for k in range(K):
            for j, (px, py) in enumerate(others):
                cp = _rcopy(_slab(full[k], me, c, halved), _slab(full[k], 2 * px + py, c, halved),
                            ssem[k].at[j], rsem[k].at[j], (px, py, c))
                cp.wait_send()
                cp.wait_recv()

    outs = pl.pallas_call(
        body,
        in_specs=[HBM] * K + [SEMS] * (2 * K) + [ANY],
        out_specs=[HBM] * K,
        out_shape=[pltpu.HBM(f.shape, f.dtype) for f in fulls],
        input_output_aliases={k: k for k in range(K)},
        name=name,
        compiler_params=pltpu.CompilerParams(has_side_effects=EFFECT),
    )(*fulls, *ssems, *rsems, after)
    return list(outs)


def _sibling_forward_call(fulls, name):
    K = len(fulls)

    def body(*refs):
        full = refs[:K]
        ssem, rsem = refs[2 * K:]
        x, y, c, me, others = _place()
        cps = []
        for k in range(K):
            for j, (px, py) in enumerate(others):
                mine = _slab(full[k], 2 * px + py, c, True)
                cps.append(_rcopy(mine, mine, ssem.at[3 * k + j], rsem.at[3 * k + j], (x, y, 1 - c)))
        for cp in cps:
            cp.start()
        for k in range(K):
            for j, (px, py) in enumerate(others):
                theirs = _slab(full[k], 2 * px + py, 1 - c, True)
                _rcopy(theirs, theirs, ssem.at[3 * k + j], rsem.at[3 * k + j], (x, y, 1 - c)).wait_recv()
        for cp in cps:
            cp.wait_send()

    outs = pl.pallas_call(
        body,
        in_specs=[ANY] * K,
        out_specs=[ANY] * K,
        out_shape=[SDS(f.shape, f.dtype) for f in fulls],
        input_output_aliases={k: k for k in range(K)},
        scratch_shapes=[pltpu.SemaphoreType.DMA((3 * K,)), pltpu.SemaphoreType.DMA((3 * K,))],
        name=name,
    )(*fulls)
    return list(outs)


def _grad_copies(srcs, lands, ssem, rsem, to_sibling):
    x, y, c, me, others = _place()
    if to_sibling:
        return [_rcopy(s, l, ssem.at[k], rsem.at[k], (x, y, 1 - c)) for k, (s, l) in enumerate(zip(srcs, lands))]
    cps, k = [], 0
    for src, land in zip(srcs, lands):
        if len(src.shape) == 3:
            for j, (px, py) in enumerate(others):
                cps.append(_rcopy(src.at[2 * px + py], land.at[j], ssem.at[k + j], rsem.at[k + j], (px, py, c)))
            k += 3
        else:
            for r in range(1, NDEV):
                cps.append(_rcopy(src, land.at[4 * x + 2 * y + c], ssem.at[k + r - 1], rsem.at[k + r - 1],
                                  _flipped(x, y, c, r)))
            k += NDEV - 1
    return cps


def _grad_start_call(srcs, name, to_sibling=False):
    srcs = list(srcs)
    K = len(srcs)
    if to_sibling:
        lands = [lax.empty(s.shape, s.dtype) for s in srcs]
        n = K
    else:
        lands = [lax.empty(((3,) + s.shape[1:]) if len(s.shape) == 3 else ((NDEV,) + s.shape), s.dtype) for s in srcs]
        n = sum(3 if len(s.shape) == 3 else NDEV - 1 for s in srcs)

    def body(*refs):
        ssem, rsem, token = refs[2 * K], refs[2 * K + 1], refs[-1]
        for cp in _grad_copies(refs[:K], refs[K:2 * K], ssem, rsem, to_sibling):
            cp.start()
        token[...] = jnp.zeros_like(token)

    outs = pl.pallas_call(
        body,
        in_specs=[HBM] * (2 * K),
        out_specs=[SEMS, SEMS] + [HBM] * (2 * K) + [pl.BlockSpec(memory_space=pltpu.VMEM)],
        out_shape=[pltpu.SemaphoreType.DMA((n,)), pltpu.SemaphoreType.DMA((n,))]
        + [pltpu.HBM(a.shape, a.dtype) for a in srcs + lands] + [SDS((8, 128), F32)],
        input_output_aliases={k: 2 + k for k in range(2 * K)},
        name=name,
        compiler_params=pltpu.CompilerParams(has_side_effects=EFFECT),
    )(*[_hbm(a) for a in srcs + lands])
    return list(outs[2:2 + K]), list(outs[2 + K:2 + 2 * K]), outs[0], outs[1], outs[-1]


def _grad_wait_call(srcs, lands, ssem, rsem, after, name, to_sibling=False):
    K = len(srcs)

    def body(*refs):
        for cp in _grad_copies(refs[:K], refs[K:2 * K], refs[2 * K], refs[2 * K + 1], to_sibling):
            cp.wait_send()
            cp.wait_recv()

    arrs = list(srcs) + list(lands)
    outs = pl.pallas_call(
        body,
        in_specs=[HBM] * (2 * K) + [SEMS, SEMS, ANY],
        out_specs=[HBM] * (2 * K),
        out_shape=[pltpu.HBM(a.shape, a.dtype) for a in arrs],
        input_output_aliases={k: k for k in range(2 * K)},
        name=name,
        compiler_params=pltpu.CompilerParams(has_side_effects=EFFECT),
    )(*arrs, ssem, rsem, after)
    return list(outs[:K]), list(outs[K:])


def _small_allreduce_call(a):
    R, C = a.shape

    def body(a_ref, o_ref, recv_ref, ssem, rsem):
        x, y, c, me, others = _place()
        dev = 4 * x + 2 * y + c
        recv_ref[pl.ds(dev, 1)] = a_ref[...][None]
        cps = [_rcopy(a_ref, recv_ref.at[dev], ssem.at[r - 1], rsem.at[r - 1], _flipped(x, y, c, r))
               for r in range(1, NDEV)]
        for cp in cps:
            cp.start()
        for cp in cps:
            cp.wait()
        acc = recv_ref[0]
        for s in range(1, NDEV):
            acc = acc + recv_ref[s]
        o_ref[...] = acc

    return pl.pallas_call(
        body,
        in_specs=[pl.BlockSpec(memory_space=pltpu.VMEM)],
        out_specs=pl.BlockSpec(memory_space=pltpu.VMEM),
        out_shape=SDS((R, C), F32),
        scratch_shapes=[pltpu.VMEM((NDEV, R, C), F32), pltpu.SemaphoreType.DMA((NDEV - 1,)),
                        pltpu.SemaphoreType.DMA((NDEV - 1,))],
        name="small_allreduce",
    )(a)


def _rows_tile(H, C):
    for cand in (512, 256, 128, 64, 32, 16, 8):
        if H % cand == 0 and cand * C * 4 <= 2 * 1024 * 1024:
            return cand
    raise ValueError((H, C))


def _sum_recv_call(own, recv, chip_idx, stack, l):
    _, R, C = own.shape
    tr = _rows_tile(R, C)

    def body(chip_ref, own_ref, r0, r1, r2, stack_ref, o_ref):
        o_ref[...] = (((own_ref[...].astype(F32) + r0[...].astype(F32)) + r1[...].astype(F32))
                      + r2[...].astype(F32)).astype(MM)

    return pl.pallas_call(
        body,
        grid_spec=pltpu.PrefetchScalarGridSpec(
            num_scalar_prefetch=1,
            grid=(R // tr,),
            in_specs=[pl.BlockSpec((1, tr, C), lambda i, chip_ref: (chip_ref[0], i, 0))]
            + [pl.BlockSpec((1, tr, C), functools.partial(lambda i, chip_ref, s: (s, i, 0), s=s)) for s in range(3)]
            + [ANY],
            out_specs=pl.BlockSpec((1, tr, C), lambda i, chip_ref: (l, i, 0)),
        ),
        out_shape=SDS(stack.shape, MM),
        input_output_aliases={5: 0},
        name="grad_sum_recv",
        compiler_params=_cp(("arbitrary",)),
    )(chip_idx, own, recv, recv, recv, stack)


def _sum_small_call(own, recv, dev_idx):
    RS, C = own.shape
    tr = _rows_tile(RS, C)

    def body(dev_ref, own_ref, *refs):
        o_ref = refs[NDEV]
        dev = dev_ref[0]
        acc = jnp.where(dev == 0, own_ref[...], refs[0][0]).astype(F32)
        for s in range(1, NDEV):
            acc = acc + jnp.where(dev == s, own_ref[...], refs[s][0]).astype(F32)
        o_ref[...] = acc

    return pl.pallas_call(
        body,
        grid_spec=pltpu.PrefetchScalarGridSpec(
            num_scalar_prefetch=1,
            grid=(RS // tr,),
            in_specs=[pl.BlockSpec((tr, C), lambda i, dev_ref: (i, 0))]
            + [pl.BlockSpec((1, tr, C), functools.partial(
                lambda i, dev_ref, s: (jnp.where(dev_ref[0] == s, (s + 1) % NDEV, s), i, 0), s=s)) for s in range(NDEV)],
            out_specs=pl.BlockSpec((tr, C), lambda i, dev_ref: (i, 0)),
        ),
        out_shape=SDS((RS, C), F32),
        name="grad_sum_small",
        compiler_params=_cp(("arbitrary",)),
    )(dev_idx, own, *([recv] * NDEV))


SMALL_ROWS_ALIGN = 128


def _pack_small(parts):
    flat = jnp.concatenate([p.reshape(-1) for p in parts])
    rows = -(-flat.shape[0] // (128 * SMALL_ROWS_ALIGN)) * SMALL_ROWS_ALIGN
    flat = jnp.pad(flat, (0, rows * 128 - flat.shape[0]))
    return flat.reshape(rows, 128)


def _unpack_small(packed, like):
    flat = packed.reshape(-1)
    out, off = [], 0
    for p in like:
        n = int(np.prod(p.shape))
        out.append(flat[off:off + n].reshape(p.shape))
        off += n
    return out


def kernel(x, norm_g, w_in, q_norm, k_norm, sinks, w_s, b_s, w_out, loss_target, m_norm_g, m_w_in, m_q_norm, m_k_norm, m_sinks, m_w_s, m_b_s, m_w_out, v_norm_g, v_w_in, v_q_norm, v_k_norm, v_sinks, v_w_s, v_b_s, v_w_out):
    L = norm_g.shape[0]
    xi, yi, ci = lax.axis_index("x"), lax.axis_index("y"), lax.axis_index("c")
    chip_idx = (2 * xi + yi).astype(jnp.int32).reshape(1)
    dev_idx = (4 * xi + 2 * yi + ci).astype(jnp.int32).reshape(1)
    bias = _alibi_bias()
    b2 = _half_sum_matrix()
    tri =jnp.tril(jnp.ones((WIN, WIN), F32))

    rest = list(range(1, L))
    fulls0 = _cast_to_slab_call(w_in, chip_idx, [0], chip_idx, "cast_w_in_0") \
        + _cast_to_slab_call(w_out, chip_idx, [0], chip_idx, "cast_w_out_0")
    ss0, rs0, fulls0 = _gather_start_call(fulls0, 2, "gather_start_0")
    fin = _cast_to_slab_call(w_in, chip_idx, rest, fulls0[0], "cast_w_in")
    fout = _cast_to_slab_call(w_out, chip_idx, rest, fulls0[1], "cast_w_out")
    g_ssems, g_rsems, fulls = _gather_start_call([a for pair in zip(fin, fout) for a in pair], 0, "gather_start")

    saved = []
    xs = x[0]
    dy = loss = None
    for l in range(L):
        if l == 0:
            (w_in_l,) = _gather_wait_call(fulls0[:1], ss0[:1], rs0[:1], fulls[0], True, "gather_wait_0")
            (w_in_l,) = _sibling_forward_call([w_in_l], "gather_forward_0")
        else:
            sl = slice(2 * (l - 1), 2 * l)
            w_in_l, w_out_l = _gather_wait_call(fulls[sl], g_ssems[sl], g_rsems[sl], xs, False, f"gather_wait_{l}")
        proj, h = _fwd_in_call(xs, norm_g[l:l + 1], w_in_l)
        ws_tril = (w_s[l] * tri).astype(MM)
        b_exp = jnp.repeat(b_s[l].T, HD, axis=1)
        wq2 = jnp.tile(q_norm[l:l + 1], (1, 2)) * SCALE
        wk2 = jnp.tile(k_norm[l:l + 1], (1, 2))
        x_l = xs
        if 0 < l < L - 1:
            w_out_l = w_out_l.reshape(D, D)
            mix, xs = _fwd_mix_call(proj, bias, wq2, wk2, b2, sinks[l], ws_tril, b_exp, xs, w_out_l)
        else:
            mix = _fwd_mix_call(proj, bias, wq2, wk2, b2, sinks[l], ws_tril, b_exp)
        if l == 0:
            (w_out_l,) = _gather_wait_call(fulls0[1:], ss0[1:], rs0[1:], mix, True, "gather_wait_0_out")
            (w_out_l,) = _sibling_forward_call([w_out_l], "gather_forward_0_out")
        w_out_l = w_out_l.reshape(D, D)
        saved.append((x_l, proj, h, mix, ws_tril, b_exp, w_in_l, w_out_l, wq2, wk2))
        if l == L - 1:
            dy, loss = _fwd_out_loss_call(xs, mix, w_out_l, loss_target[0])
        elif l == 0:
            xs = _fwd_out_call(xs, mix, w_out_l)

    s_in = lax.empty((L, D, SHW), MM)
    s_out = lax.empty((L, SHR, D), MM)
    ws_sums, tiny_sums = [None] * L, [None] * L

    def finish(pending, after):
        nonlocal s_in, s_out
        l, exchanges = pending
        for tag, kinds, srcs, lands, ssem, rsem in exchanges:
            srcs, lands = _grad_wait_call(srcs, lands, ssem, rsem, after, f"grad_wait_{l}{tag}")
            for kind, src, land in zip(kinds, srcs, lands):
                if kind == "in":
                    s_in = _sum_recv_call(src, land, chip_idx, s_in, l)
                elif kind == "out":
                    s_out = _sum_recv_call(src, land, chip_idx, s_out, l)
                elif kind == "ws":
                    ws_sums[l] = _sum_small_call(src, land, dev_idx)
                else:
                    tiny_sums[l] = _sum_small_call(src, land, dev_idx)

    def start(l, tag, kinds, srcs):
        srcs, lands, ssem, rsem, token = _grad_start_call(srcs, f"grad_start_{l}{tag}")
        return (tag, kinds, srcs, lands, ssem, rsem), token

    pending = None
    d_norm_g = [None] * L
    for l in reversed(range(L)):
        xs, proj, h, mix, ws_tril, b_exp, w_in_l, w_out_l, wq2, wk2 = saved[l]
        g_w_out = _grad_w_out_call(mix, dy).reshape(NCHIP, SHR, D)
        exchanges, token = [], jnp.zeros((8, 128), F32)
        if l == 0:
            ex, token = start(l, "_out", ["out"], [g_w_out])
            exchanges.append(ex)
        dmix = _bwd_out_call(dy, w_out_l, token)
        ws_tril_t = jnp.swapaxes(ws_tril, 1, 2)
        dproj, dwq, dwk, dsk, dws, dbs = _bwd_mix_call(
            proj, dmix, bias, wq2, wk2, b2, sinks[l], ws_tril, ws_tril_t, b_exp)
        dwq, dwk = dwq[:, :HD] + dwq[:, HD:], dwk[:, :HD] + dwk[:, HD:]
        g_ws = dws.reshape(NG * WIN, WIN).astype(MM)
        g_tiny = _pack_small([dwq, dwk, dsk[:, 0], dbs]).astype(MM)
        token = jnp.zeros((8, 128), F32)
        if l == 0:
            ex, token = start(l, "_small", ["ws", "tiny"], [g_ws, g_tiny])
            exchanges.append(ex)
        g_w_in = _grad_w_in_call(h, dproj, token)
        if l == 0:
            ex, token = start(l, "", ["in"], [g_w_in])
        else:
            ex, token = start(l, "", ["in", "out", "ws", "tiny"], [g_w_in, g_w_out, g_ws, g_tiny])
        exchanges.append(ex)
        dy, d_norm_g[l] = _bwd_in_call(dproj, w_in_l, xs, dy, norm_g[l:l + 1], token)
        if pending is not None:
            finish(pending, dy)
        pending = (l, exchanges)
    finish(pending, dy)
    grad_x = dy

    swap_srcs, swap_lands, sw_ssem, sw_rsem, _ = _grad_start_call([s_in, s_out], "grad_swap_start", to_sibling=True)
    g_norm_g = _small_allreduce_call(jnp.concatenate(d_norm_g, axis=0))

    def pack_layers(parts):
        return jnp.concatenate([_pack_small([p[l] for p in parts]) for l in range(L)], axis=0)

    ws_rows = (L * NG * WIN, WIN)
    ws_outs = _adam_call(w_s.reshape(ws_rows), [jnp.concatenate(ws_sums, axis=0)],
                         m_w_s.reshape(ws_rows), v_w_s.reshape(ws_rows), "adam_w_s")
    tiny_like = [q_norm, k_norm, sinks, b_s]
    tiny_outs = _adam_call(
        pack_layers(tiny_like), [jnp.concatenate(tiny_sums, axis=0)],
        pack_layers([m_q_norm, m_k_norm, m_sinks, m_b_s]),
        pack_layers([v_q_norm, v_k_norm, v_sinks, v_b_s]), "adam_tiny")
    norm_outs = _adam_call(norm_g, [g_norm_g], m_norm_g, v_norm_g, "adam_norm_g")

    small_done = ws_outs[1][:8] + tiny_outs[1][:8] + norm_outs[1][:1, :WIN]
    (s_in, s_out), (t_in, t_out) = _grad_wait_call(swap_srcs, swap_lands, sw_ssem, sw_rsem, small_done,
                                                   "grad_swap_wait", to_sibling=True)
    g_w_in, d_in, nm_in, nv_in = _adam_call(
        w_in.reshape(L * D, SHW), [s_in.reshape(L * D, SHW), t_in.reshape(L * D, SHW)],
        m_w_in.reshape(L * D, SHW), v_w_in.reshape(L * D, SHW), "adam_w_in")
    g_w_out, d_out, nm_out, nv_out = _adam_call(
        w_out.reshape(L * SHR, D), [s_out.reshape(L * SHR, D), t_out.reshape(L * SHR, D)],
        m_w_out.reshape(L * SHR, D), v_w_out.reshape(L * SHR, D), "adam_w_out")

    def full(i, win, wout):
        tiny = tiny_outs[i]
        rows = tiny.shape[0] // L
        per_layer = [_unpack_small(tiny[l * rows:(l + 1) * rows], [p[l] for p in tiny_like]) for l in range(L)]
        qn, kn, sk, bs = [jnp.stack([per_layer[l][k] for l in range(L)]) for k in range(4)]
        return [norm_outs[i], win.reshape(w_in.shape), qn, kn, sk, ws_outs[i].reshape(w_s.shape), bs,
                wout.reshape(w_out.shape)]

    loss_all = lax.psum(loss[0, 0], ("x", "y", "c"))
    return (loss_all, grad_x[None], *full(0, g_w_in, g_w_out), *full(1, d_in, d_out),
            *full(2, nm_in, nm_out), *full(3, nv_in, nv_out))
```

```python
import functools
import math

import numpy as np
import jax
import jax.numpy as jnp
from jax import lax
from jax.experimental import pallas as pl
from jax.experimental.pallas import tpu as pltpu

F32 = jnp.float32
MM = jnp.bfloat16

D = 2048
HD = 64
DA = 1024
DKV = 256
DG = 1024
NQ, NKV, GRP, NG = 16, 4, 4, 16
WIN = 128
DIN = 5632
C_Q, C_K, C_V, C_GA, C_U, C_VS, C_GB = 0, 1024, 1280, 1536, 2560, 3584, 4608
NCHIP = 4
SHW = DIN // NCHIP
SHR = D // NCHIP
EPS = 1e-6
NEG = -1e30
SCALE = HD ** -0.5
INV_SQRT2 = 1.0 / math.sqrt(2.0)
INV_SQRT_2PI = 1.0 / math.sqrt(2.0 * math.pi)
LR, B1, B2, ADAM_EPS, WD, STEP = 0.001, 0.9, 0.999, 1e-08, 0.01, 10
VMEM_LIMIT = 56 * 1024 * 1024

SDS = jax.ShapeDtypeStruct
NT = (((1,), (1,)), ((), ()))
TN = (((0,), (0,)), ((), ()))


def _cp(sem=None):
    return pltpu.CompilerParams(dimension_semantics=sem, vmem_limit_bytes=VMEM_LIMIT)


def _sigmoid(x):
    return 1.0 / (1.0 + jnp.exp(-x))


def _gelu(x):
    return 0.5 * x * (1.0 + lax.erf(x * INV_SQRT2))


def _gelu_and_grad(x):
    cdf = 0.5 * (1.0 + lax.erf(x * INV_SQRT2))
    return x * cdf, cdf + x * jnp.exp(-0.5 * x * x) * INV_SQRT_2PI


def _alibi_bias():
    slopes = 2.0 ** (-8.0 * np.arange(1, NQ + 1) / NQ)
    dist = (np.arange(WIN)[:, None] + WIN) - np.arange(2 * WIN)[None, :]
    ok = (dist >= 0) & (dist < WIN)
    first = ok & (np.arange(2 * WIN)[None, :] >= WIN)
    val = -slopes[:, None, None] * dist[None].astype(np.float64)
    return jnp.asarray(np.stack([np.where(first[None], val, NEG), np.where(ok[None], val, NEG)]), dtype=F32)


def _half_sum_matrix():
    half = np.arange(LANE) // HD
    return jnp.asarray(half[:, None] == half[None, :], dtype=MM)


LANE = 128
NQT = DA // LANE
NKT = DKV // LANE


class _ColumnsFrom:
    def __init__(self, ref, first):
        self.ref, self.first = ref, first

    def __getitem__(self, idx):
        rows, cols = idx
        return self.ref[rows, self.first + cols.start:self.first + cols.stop]


def _tiles(ref, c0, n):
    return jnp.concatenate([ref[:, c0 + j * LANE:c0 + (j + 1) * LANE] for j in range(n)], axis=0)


def _split(x):
    hi = x.astype(MM)
    return hi, (x - hi.astype(F32)).astype(MM)


def _half_sums(x, b2):
    hi, lo = _split(x)
    return jnp.dot(hi, b2, preferred_element_type=F32) + jnp.dot(lo, b2, preferred_element_type=F32)


def _attn_fwd(pm_ref, kvp_ref, bias_ref, wq2, wk2, b2, sink_ref, between=None):
    lo_half = lax.broadcasted_iota(jnp.int32, (1, LANE), 1) < HD
    q_ts = _tiles(pm_ref, C_Q, NQT)
    rq = lax.rsqrt(_half_sums(q_ts * q_ts, b2) * (1.0 / HD) + EPS)
    qs = (q_ts * rq * wq2).astype(MM)
    k_ts = jnp.concatenate([a[:, c0 + t * LANE:c0 + (t + 1) * LANE] for t in range(NKT)
                            for a, c0 in ((kvp_ref, 0), (pm_ref, C_K))], axis=0)
    rk = lax.rsqrt(_half_sums(k_ts * k_ts, b2) * (1.0 / HD) + EPS)
    kn = (k_ts * rk * wk2).astype(MM)
    v_ts = jnp.concatenate([a[:, c0 + t * LANE:c0 + (t + 1) * LANE] for t in range(NKT)
                            for a, c0 in ((kvp_ref, DKV), (pm_ref, C_V))], axis=0).astype(MM)
    ones = jnp.ones((2 * WIN, LANE), MM)
    km, vm = {}, {}
    for hk in range(NKV):
        t, eh = hk // 2, hk % 2
        sel = lo_half if eh == 0 else jnp.logical_not(lo_half)
        rows = slice(t * 2 * WIN, (t + 1) * 2 * WIN)
        k_same = jnp.where(sel, kn[rows], jnp.zeros_like(kn[rows]))
        v_same = jnp.where(sel, v_ts[rows], jnp.zeros_like(v_ts[rows]))
        km[hk, eh], km[hk, 1 - eh] = k_same, pltpu.roll(k_same, HD, axis=1)
        vm[hk, eh], vm[hk, 1 - eh] = v_same, pltpu.roll(v_same, HD, axis=1)
    hs = range(NQ)
    s = [lax.dot_general(qs[(h // 2) * WIN:(h // 2 + 1) * WIN], km[h // GRP, h % 2], NT, preferred_element_type=F32)
         + bias_ref[0, h] for h in hs]
    if between is not None:
        between()
    m = [jnp.maximum(jnp.max(s[h], axis=-1, keepdims=True), sink_ref[h]) for h in hs]
    p = [jnp.exp(s[h] - m[h]) for h in hs]
    pb = [p[h].astype(MM) for h in hs]
    res = [jnp.dot(pb[h], jnp.concatenate([vm[h // GRP, h % 2], ones], axis=1), preferred_element_type=F32) for h in hs]
    esink = [jnp.exp(sink_ref[h] - m[h]) for h in hs]
    inv = [1.0 / (res[h][:, LANE:] + esink[h]) for h in hs]
    heads = [dict(p=p[h], pb=pb[h], inv=inv[h], esink=esink[h], o=res[h][:, :LANE] * inv[h]) for h in hs]
    return dict(lo_half=lo_half, q_ts=q_ts, rq=rq, qs=qs, k_ts=k_ts, rk=rk, km=km, vm=vm, heads=heads)


def _sgu_mix(w_ref, zt, lo_half, j):
    zero = jnp.zeros_like(zt)
    return (jnp.dot(w_ref[2 * j], jnp.where(lo_half, zt, zero), preferred_element_type=F32)
            + jnp.dot(w_ref[2 * j + 1], jnp.where(lo_half, zero, zt), preferred_element_type=F32))


def _fwd_mix_call(proj, bias, wq2, wk2, b2, sinks, ws_tril, b_exp, x=None, w_out=None, target=None):
    T = proj.shape[0]
    nb = T // WIN
    assert nb % 2 == 0
    ns = nb // 2
    fused = x is not None
    last = target is not None

    def body(sink_ref, pm2_ref, kvp_ref, bias_ref, wq_ref, wk_ref, b2_ref, ws_ref, be_ref, *rest):
        n = pl.program_id(0)
        if fused:
            if last:
                x_ref, wo_ref, t_ref, mix2_ref, y_ref, loss_ref, late_ref = rest
            else:
                x_ref, wo_ref, mix2_ref, y_ref, late_ref = rest
            n = jnp.minimum(n, ns - 1)

            @pl.when(pl.program_id(0) == 0)
            def _():
                late_ref[...] = jnp.zeros_like(late_ref)
                if last:
                    loss_ref[...] = jnp.zeros_like(loss_ref)

        else:
            (mix2_ref,) = rest

        def project(part):
            if fused:
                cols = slice(part * (D // 4), (part + 1) * (D // 4))
                y = x_ref[:, cols] + jnp.dot(late_ref[...], wo_ref[:, cols], preferred_element_type=F32)
                if last:
                    e = y - t_ref[:, cols]
                    y_ref[:, cols] = e * (1.0 / D)
                    counts = jnp.where(pl.program_id(0) > 0, 0.5 / D, 0.0)
                    loss_ref[...] += counts * jnp.sum(jnp.sum(e * e, axis=1, keepdims=True), axis=0, keepdims=True)
                else:
                    y_ref[:, cols] = y

        lo_half = lax.broadcasted_iota(jnp.int32, (1, LANE), 1) < HD
        for sub in range(2):
            project(2 * sub)
            pm_ref = pm2_ref.at[pl.ds(sub * WIN, WIN), :]
            mix_ref = mix2_ref.at[pl.ds(sub * WIN, WIN), :]
            prev_kv = kvp_ref if sub == 0 else _ColumnsFrom(pm2_ref.at[pl.ds(0, WIN), :], C_K)
            table = bias_ref.at[pl.ds(jnp.minimum(n, 1), 1)] if sub == 0 else bias_ref.at[pl.ds(1, 1)]
            zu = _gelu(pm_ref[:, C_U:C_U + DG])
            zv = _gelu(pm_ref[:, C_VS:C_VS + DG]).astype(MM)
            mixed = jnp.concatenate(
                [_sgu_mix(ws_ref, zv[:, j * LANE:(j + 1) * LANE], lo_half, j) for j in range(NG // 2)], axis=1)
            mixed = mixed + be_ref[...]
            gb = pm_ref[:, C_GB:C_GB + DG]
            mix_ref[:, DA:DA + DG] = (zu * mixed * (gb * _sigmoid(gb))).astype(MM)
            a = _attn_fwd(pm_ref, prev_kv, table, wq_ref[...], wk_ref[...], b2_ref[...], sink_ref,
                          functools.partial(project, 2 * sub + 1))
            for j in range(NQT):
                cols = slice(j * LANE, (j + 1) * LANE)
                ga = pm_ref[:, C_GA + j * LANE:C_GA + (j + 1) * LANE]
                attn = a["heads"][2 * j]["o"] + a["heads"][2 * j + 1]["o"]
                mix_ref[:, cols] = (attn * (ga * _sigmoid(ga))).astype(MM)
        if fused:
            late_ref[...] = mix2_ref[...]

    def specs(at, late):
        return [
            pl.BlockSpec(memory_space=pltpu.SMEM),
            pl.BlockSpec((2 * WIN, DIN), lambda n: (at(n), 0)),
            pl.BlockSpec((WIN, 2 * DKV), lambda n: (jnp.maximum(2 * at(n) - 1, 0), C_K // (2 * DKV))),
            pl.BlockSpec((2, NQ, WIN, 2 * WIN), lambda n: (0, 0, 0, 0)),
            pl.BlockSpec((1, LANE), lambda n: (0, 0)),
            pl.BlockSpec((1, LANE), lambda n: (0, 0)),
            pl.BlockSpec((LANE, LANE), lambda n: (0, 0)),
            pl.BlockSpec((NG, WIN, WIN), lambda n: (0, 0, 0)),
            pl.BlockSpec((WIN, DG), lambda n: (0, 0)),
        ], pl.BlockSpec((2 * WIN, D), lambda n: (at(n), 0)), pl.BlockSpec((2 * WIN, D), lambda n: (late(n), 0))

    if fused:
        in_specs, rows, late_rows = specs(lambda n: jnp.minimum(n, ns - 1), lambda n: jnp.maximum(n - 1, 0))
        w_spec = pl.BlockSpec((D, D), lambda n: (0, 0), pipeline_mode=pl.Buffered(1))
        if last:
            return pl.pallas_call(
                body,
                grid=(ns + 1,),
                in_specs=in_specs + [late_rows, w_spec, late_rows],
                out_specs=[rows, late_rows, pl.BlockSpec((1, 1), lambda n: (0, 0))],
                out_shape=[SDS((T, D), MM), SDS((T, D), F32), SDS((1, 1), F32)],
                scratch_shapes=[pltpu.VMEM((2 * WIN, D), MM)],
                name="fwd_mix_out_loss",
                compiler_params=_cp(("arbitrary",)),
            )(sinks, proj, proj, bias, wq2, wk2, b2, ws_tril, b_exp, x, w_out, target)
        return pl.pallas_call(
            body,
            grid=(ns + 1,),
            in_specs=in_specs + [late_rows, w_spec],
            out_specs=[rows, late_rows],
            out_shape=[SDS((T, D), MM), SDS((T, D), F32)],
            scratch_shapes=[pltpu.VMEM((2 * WIN, D), MM)],
            name="fwd_mix_out",
            compiler_params=_cp(("arbitrary",)),
        )(sinks, proj, proj, bias, wq2, wk2, b2, ws_tril, b_exp, x, w_out)
    in_specs, rows, _ = specs(lambda n: n, lambda n: n)
    return pl.pallas_call(
        body,
        grid=(ns,),
        in_specs=in_specs,
        out_specs=rows,
        out_shape=SDS((T, D), MM),
        name="fwd_mix",
        compiler_params=_cp(("arbitrary",)),
    )(sinks, proj, proj, bias, wq2, wk2, b2, ws_tril, b_exp)


def _bwd_mix_call(proj, dmix, bias, wq2, wk2, b2, sinks, ws_tril, ws_tril_t, b_exp):
    T = proj.shape[0]
    nb = T // WIN
    assert nb % 2 == 0
    ns = nb // 2

    def body(sink_ref, pm2_ref, kvp_ref, dm2_ref, bias_ref, wq_ref, wk_ref, b2_ref, ws_ref, wst_ref, be_ref,
             dp_ref, dwq_ref, dwk_ref, dsk_ref, dws_ref, dbs_ref, carry_ref, dbacc_ref):
        n = pl.program_id(0)

        @pl.when(n == 0)
        def _():
            carry_ref[...] = jnp.zeros_like(carry_ref)
            dbacc_ref[...] = jnp.zeros_like(dbacc_ref)
            dwq_ref[...] = jnp.zeros_like(dwq_ref)
            dwk_ref[...] = jnp.zeros_like(dwk_ref)
            dsk_ref[...] = jnp.zeros_like(dsk_ref)
            dws_ref[...] = jnp.zeros_like(dws_ref)
            dbs_ref[...] = jnp.zeros_like(dbs_ref)

        def one_block(pm_ref, dm_ref, prev_kv, table, crow):
            lo_half = lax.broadcasted_iota(jnp.int32, (1, LANE), 1) < HD
            u = pm_ref[:, C_U:C_U + DG]
            vs = pm_ref[:, C_VS:C_VS + DG]
            gb = pm_ref[:, C_GB:C_GB + DG]
            zu, dzu = _gelu_and_grad(u)
            zv, dzv = _gelu_and_grad(vs)
            zvb = zv.astype(MM)
            mixed = jnp.concatenate(
                [_sgu_mix(ws_ref, zvb[:, j * LANE:(j + 1) * LANE], lo_half, j) for j in range(NG // 2)], axis=1)
            mixed = mixed + be_ref[...]
            sgb = _sigmoid(gb)
            d_sgu = dm_ref[:, DA:DA + DG]
            crow[:, C_GB:DIN] = d_sgu * zu * mixed * (sgb * (1.0 + gb * (1.0 - sgb)))
            d_mixed = d_sgu * zu * (gb * sgb)
            crow[:, C_U:C_VS] = d_sgu * mixed * (gb * sgb) * dzu
            dbacc_ref[...] += d_mixed
            dmb = d_mixed.astype(MM)
            dzv_tiles = [_sgu_mix(wst_ref, dmb[:, j * LANE:(j + 1) * LANE], lo_half, j) for j in range(NG // 2)]
            crow[:, C_VS:C_GB] = jnp.concatenate(dzv_tiles, axis=1) * dzv
            for j in range(NG // 2):
                dt = dmb[:, j * LANE:(j + 1) * LANE]
                zt = zvb[:, j * LANE:(j + 1) * LANE]
                zero = jnp.zeros_like(dt)
                dws_ref[2 * j] += lax.dot_general(jnp.where(lo_half, dt, zero), zt, NT, preferred_element_type=F32)
                dws_ref[2 * j + 1] += lax.dot_general(jnp.where(lo_half, zero, dt), zt, NT, preferred_element_type=F32)

            wq2, wk2, b2 = wq_ref[...], wk_ref[...], b2_ref[...]
            a = _attn_fwd(pm_ref, prev_kv, table, wq2, wk2, b2, sink_ref)
            heads, km, vm, qs = a["heads"], a["km"], a["vm"], a["qs"]

            row_lo = lax.broadcasted_iota(jnp.int32, (LANE, LANE), 0) < HD
            pick = [jnp.where(row_lo, 1.0, 0.0).astype(MM), jnp.where(row_lo, 0.0, 1.0).astype(MM)]
            chan_lo = lax.broadcasted_iota(jnp.int32, (LANE, 1), 0) < HD
            tiles, hs = range(NQT), range(NQ)
            sel_t = [chan_lo, jnp.logical_not(chan_lo)]
            d_o, attn = [], []
            for j in tiles:
                cols = slice(C_GA + j * LANE, C_GA + (j + 1) * LANE)
                ga = pm_ref[:, cols]
                sga = _sigmoid(ga)
                d_gated = dm_ref[:, j * LANE:(j + 1) * LANE]
                attn.append(heads[2 * j]["o"] + heads[2 * j + 1]["o"])
                crow[:, cols] = d_gated * attn[j] * (sga * (1.0 + ga * (1.0 - sga)))
                d_o.append(d_gated * (ga * sga))
            d_ob = [d_o[j].astype(MM) for j in tiles]
            dlt = [(d_o[j] * attn[j]).astype(MM) for j in tiles]
            d_os_t = [(d_o[j] * jnp.where(lo_half, heads[2 * j]["inv"], heads[2 * j + 1]["inv"])).astype(MM).T
                      for j in tiles]
            qs_t = [qs[j * WIN:(j + 1) * WIN].T for j in tiles]
            zero_t = jnp.zeros_like(qs_t[0])
            dv_h = [jnp.dot(jnp.where(sel_t[h % 2], d_os_t[h // 2], zero_t), heads[h]["pb"], preferred_element_type=F32)
                    for h in hs]
            d_p = [lax.dot_general(d_ob[h // 2], vm[h // GRP, h % 2], NT, preferred_element_type=F32) for h in hs]
            delta = [jnp.dot(dlt[h // 2], pick[h % 2], preferred_element_type=F32) for h in hs]
            for h in hs:
                dsk_ref[h:h + 1, :] -= jnp.sum(heads[h]["esink"] * heads[h]["inv"] * delta[h], axis=0, keepdims=True)
            d_s = [(heads[h]["p"] * ((d_p[h] - jnp.concatenate([delta[h], delta[h]], axis=1))
                                     * jnp.concatenate([heads[h]["inv"], heads[h]["inv"]], axis=1))).astype(MM)
                   for h in hs]
            dqs_h = [jnp.dot(d_s[h], km[h // GRP, h % 2], preferred_element_type=F32) for h in hs]
            dqs_tiles = [dqs_h[2 * j] + dqs_h[2 * j + 1] for j in tiles]
            dk_h = [jnp.dot(jnp.where(sel_t[h % 2], qs_t[h // 2], zero_t), d_s[h], preferred_element_type=F32)
                    for h in hs]
            dk_acc, dv_acc = {}, {}
            for h in hs:
                key = (h // GRP, h % 2 == (h // GRP) % 2)
                dk_acc[key] = dk_h[h] if key not in dk_acc else dk_acc[key] + dk_h[h]
                dv_acc[key] = dv_h[h] if key not in dv_acc else dv_acc[key] + dv_h[h]

            dqs_ts = jnp.concatenate(dqs_tiles, axis=0)
            q_ts, rq = a["q_ts"], a["rq"]
            gq = dqs_ts * wq2
            d_q = rq * gq - q_ts * (rq * rq * rq) * (_half_sums(gq * q_ts, b2) * (1.0 / HD))
            dwq_ref[...] += SCALE * jnp.sum(dqs_ts * q_ts * rq, axis=0, keepdims=True)
            for j in range(NQT):
                crow[:, C_Q + j * LANE:C_Q + (j + 1) * LANE] = d_q[j * WIN:(j + 1) * WIN]

            def swap_halves(xt):
                return jnp.concatenate([xt[HD:], xt[:HD]], axis=0)

            dkn_tiles, dv_tiles = [], []
            for t in range(NKT):
                for acc, out in ((dk_acc, dkn_tiles), (dv_acc, dv_tiles)):
                    parts = [acc[hk, True] + swap_halves(acc[hk, False]) for hk in (2 * t, 2 * t + 1)]
                    out.append((parts[0] + parts[1]).T)
            dkn_ts = jnp.concatenate(dkn_tiles, axis=0)
            dv_ts = jnp.concatenate(dv_tiles, axis=0)
            k_ts, rk = a["k_ts"], a["rk"]
            gk = dkn_ts * wk2
            d_k = rk * gk - k_ts * (rk * rk * rk) * (_half_sums(gk * k_ts, b2) * (1.0 / HD))
            dwk_ref[...] += jnp.sum(dkn_ts * k_ts * rk, axis=0, keepdims=True)
            prev_share = []
            for base, val in ((C_K, d_k), (C_V, dv_ts)):
                for t in range(NKT):
                    r0 = t * 2 * WIN
                    crow[:, base + t * LANE:base + (t + 1) * LANE] = val[r0 + WIN:r0 + 2 * WIN]
                    prev_share.append(val[r0:r0 + WIN])
            return jnp.concatenate(prev_share, axis=1)

        @pl.when(n < ns)
        def _():
            dp_ref[0:WIN, :] = carry_ref[0:WIN, :].astype(MM)
            dp_ref[WIN:2 * WIN, C_Q:C_K] = carry_ref[WIN:2 * WIN, C_Q:C_K].astype(MM)
            dp_ref[WIN:2 * WIN, C_GA:DIN] = carry_ref[WIN:2 * WIN, C_GA:DIN].astype(MM)
            waiting_kv = carry_ref[WIN:2 * WIN, C_K:C_GA]
            first = pl.ds(0, WIN)
            second = pl.ds(WIN, WIN)
            share = one_block(pm2_ref.at[first, :], dm2_ref.at[first, :], kvp_ref,
                              bias_ref.at[pl.ds(jnp.minimum(n, 1), 1)], carry_ref.at[first, :])
            dp_ref[WIN:2 * WIN, C_K:C_GA] = (waiting_kv + share).astype(MM)
            share = one_block(pm2_ref.at[second, :], dm2_ref.at[second, :], _ColumnsFrom(pm2_ref.at[first, :], C_K),
                              bias_ref.at[pl.ds(1, 1)], carry_ref.at[second, :])
            carry_ref[0:WIN, C_K:C_GA] += share

        @pl.when(n == ns)
        def _():
            dp_ref[...] = carry_ref[...].astype(MM)
            lo_half = lax.broadcasted_iota(jnp.int32, (8, LANE), 1) < HD
            ones = [jnp.where(lo_half, 1.0, 0.0).astype(MM), jnp.where(lo_half, 0.0, 1.0).astype(MM)]
            hi, lo = _split(dbacc_ref[...])
            for h in range(NG):
                sl = slice((h // 2) * LANE, (h // 2 + 1) * LANE)
                r = (lax.dot_general(ones[h % 2], hi[:, sl], NT, preferred_element_type=F32)
                     + lax.dot_general(ones[h % 2], lo[:, sl], NT, preferred_element_type=F32))
                dbs_ref[h:h + 1, :] = r[0:1, :]
            row = lax.broadcasted_iota(jnp.int32, (WIN, WIN), 0)
            cl = lax.broadcasted_iota(jnp.int32, (WIN, WIN), 1)
            for h in range(NG):
                dws_ref[h] = jnp.where(row >= cl, dws_ref[h], 0.0)

    last = ns - 1
    return pl.pallas_call(
        body,
        grid_spec=pltpu.PrefetchScalarGridSpec(
            num_scalar_prefetch=0,
            grid=(ns + 1,),
            in_specs=[
                pl.BlockSpec(memory_space=pltpu.SMEM),
                pl.BlockSpec((2 * WIN, DIN), lambda n: (jnp.minimum(n, last), 0)),
                pl.BlockSpec((WIN, 2 * DKV),
                             lambda n: (jnp.maximum(2 * jnp.minimum(n, last) - 1, 0), C_K // (2 * DKV))),
                pl.BlockSpec((2 * WIN, D), lambda n: (jnp.minimum(n, last), 0)),
                pl.BlockSpec((2, NQ, WIN, 2 * WIN), lambda n: (0, 0, 0, 0)),
                pl.BlockSpec((1, LANE), lambda n: (0, 0)),
                pl.BlockSpec((1, LANE), lambda n: (0, 0)),
                pl.BlockSpec((LANE, LANE), lambda n: (0, 0)),
                pl.BlockSpec((NG, WIN, WIN), lambda n: (0, 0, 0)),
                pl.BlockSpec((NG, WIN, WIN), lambda n: (0, 0, 0)),
                pl.BlockSpec((WIN, DG), lambda n: (0, 0)),
            ],
            out_specs=[
                pl.BlockSpec((2 * WIN, DIN), lambda n: (jnp.maximum(n - 1, 0), 0)),
                pl.BlockSpec((1, LANE), lambda n: (0, 0)),
                pl.BlockSpec((1, LANE), lambda n: (0, 0)),
                pl.BlockSpec((NQ, WIN), lambda n: (0, 0)),
                pl.BlockSpec((NG, WIN, WIN), lambda n: (0, 0, 0)),
                pl.BlockSpec((NG, WIN), lambda n: (0, 0)),
            ],
            scratch_shapes=[pltpu.VMEM((2 * WIN, DIN), F32), pltpu.VMEM((WIN, DG), F32)],
        ),
        out_shape=[SDS((T, DIN), MM), SDS((1, LANE), F32), SDS((1, LANE), F32), SDS((NQ, WIN), F32),
                   SDS((NG, WIN, WIN), F32), SDS((NG, WIN), F32)],
        name="bwd_mix",
        compiler_params=_cp(("arbitrary",)),
    )(sinks, proj, proj, dmix, bias, wq2, wk2, b2, ws_tril, ws_tril_t, b_exp)


WEIGHT_RESIDENT_ROWS = 256
GRAD_TOKEN_TILE = 1024


def _row_tile(T):
    return min(512, T)


def _fwd_in_call(x, g_row, w_sh):
    T = x.shape[0]
    tm = min(WEIGHT_RESIDENT_ROWS, T)

    def body(x_ref, g_ref, w_hbm, proj_ref, h_ref, w_vmem, sem):
        @pl.when(pl.program_id(0) == 0)
        def _():
            cps = [pltpu.make_async_copy(w_hbm.at[j], w_vmem.at[:, pl.ds(j * SHW, SHW)], sem.at[j]) for j in range(NCHIP)]
            for cp in cps:
                cp.start()
            for cp in cps:
                cp.wait()

        xv = x_ref[...]
        r = lax.rsqrt(jnp.mean(xv * xv, axis=-1, keepdims=True) + EPS)
        h = (xv * r * g_ref[...]).astype(MM)
        h_ref[...] = h
        proj_ref[...] = jnp.dot(h, w_vmem[...], preferred_element_type=F32)

    return pl.pallas_call(
        body,
        grid=(T // tm,),
        in_specs=[pl.BlockSpec((tm, D), lambda i: (i, 0)),
                  pl.BlockSpec((1, D), lambda i: (0, 0)),
                  pl.BlockSpec(memory_space=pl.ANY)],
        out_specs=[pl.BlockSpec((tm, DIN), lambda i: (i, 0)),
                   pl.BlockSpec((tm, D), lambda i: (i, 0))],
        out_shape=[SDS((T, DIN), F32), SDS((T, D), MM)],
        scratch_shapes=[pltpu.VMEM((D, DIN), MM), pltpu.SemaphoreType.DMA((NCHIP,))],
        name="fwd_in",
        compiler_params=_cp(("arbitrary",)),
    )(x, g_row, w_sh)


def _fwd_out_call(x, mix, w_out):
    T = x.shape[0]
    tm = _row_tile(T)

    def body(x_ref, mix_ref, w_ref, y_ref):
        y_ref[...] = x_ref[...] + jnp.dot(mix_ref[...], w_ref[...], preferred_element_type=F32)

    return pl.pallas_call(
        body,
        grid=(T // tm,),
        in_specs=[pl.BlockSpec((tm, D), lambda i: (i, 0)),
                  pl.BlockSpec((tm, D), lambda i: (i, 0)),
                  pl.BlockSpec((D, D), lambda i: (0, 0))],
        out_specs=pl.BlockSpec((tm, D), lambda i: (i, 0)),
        out_shape=SDS((T, D), F32),
        name="fwd_out",
        compiler_params=_cp(("arbitrary",)),
    )(x, mix, w_out)


def _fwd_out_loss_call(x, mix, w_out, target):
    T = x.shape[0]
    tm = _row_tile(T)

    def body(x_ref, mix_ref, w_ref, t_ref, dy_ref, loss_ref):
        @pl.when(pl.program_id(0) == 0)
        def _():
            loss_ref[...] = jnp.zeros_like(loss_ref)

        e = x_ref[...] + jnp.dot(mix_ref[...], w_ref[...], preferred_element_type=F32) - t_ref[...]
        dy_ref[...] = e * (1.0 / D)
        loss_ref[...] += (0.5 / D) * jnp.sum(jnp.sum(e * e, axis=1, keepdims=True), axis=0, keepdims=True)

    return pl.pallas_call(
        body,
        grid=(T // tm,),
        in_specs=[pl.BlockSpec((tm, D), lambda i: (i, 0)),
                  pl.BlockSpec((tm, D), lambda i: (i, 0)),
                  pl.BlockSpec((D, D), lambda i: (0, 0)),
                  pl.BlockSpec((tm, D), lambda i: (i, 0))],
        out_specs=[pl.BlockSpec((tm, D), lambda i: (i, 0)),
                   pl.BlockSpec((1, 1), lambda i: (0, 0))],
        out_shape=[SDS((T, D), F32), SDS((1, 1), F32)],
        name="fwd_out_loss",
        compiler_params=_cp(("arbitrary",)),
    )(x, mix, w_out, target)


def _bwd_out_call(dy, w_out, token):
    T = dy.shape[0]
    tm = _row_tile(T)

    def body(dy_ref, w_ref, token_ref, o_ref):
        o_ref[...] = lax.dot_general(dy_ref[...].astype(MM), w_ref[...], NT, preferred_element_type=F32)

    return pl.pallas_call(
        body,
        grid=(T // tm,),
        in_specs=[pl.BlockSpec((tm, D), lambda i: (i, 0)),
                  pl.BlockSpec((D, D), lambda i: (0, 0)),
                  pl.BlockSpec(memory_space=pl.ANY)],
        out_specs=pl.BlockSpec((tm, D), lambda i: (i, 0)),
        out_shape=SDS((T, D), F32),
        name="bwd_out",
        compiler_params=_cp(("arbitrary",)),
    )(dy, w_out, token)


def _bwd_in_call(dproj, w_sh, x, dy, g_row, token):
    T = x.shape[0]
    tm = min(WEIGHT_RESIDENT_ROWS, T)

    def body(dp_ref, w_hbm, x_ref, dy_ref, g_ref, token_ref, dx_ref, dg_ref, w_vmem, sem):
        @pl.when(pl.program_id(0) == 0)
        def _():
            cps = [pltpu.make_async_copy(w_hbm.at[j], w_vmem.at[:, pl.ds(j * SHW, SHW)], sem.at[j]) for j in range(NCHIP)]
            for cp in cps:
                cp.start()
            dg_ref[...] = jnp.zeros_like(dg_ref)
            for cp in cps:
                cp.wait()

        dh = lax.dot_general(dp_ref[...], w_vmem[...], NT, preferred_element_type=F32)
        xv = x_ref[...]
        r = lax.rsqrt(jnp.mean(xv * xv, axis=-1, keepdims=True) + EPS)
        gd = dh * g_ref[...]
        dx_ref[...] = dy_ref[...] + r * gd - xv * ((r * r * r) * jnp.mean(gd * xv, axis=-1, keepdims=True))
        dg_ref[...] += jnp.sum(dh * xv * r, axis=0, keepdims=True)

    return pl.pallas_call(
        body,
        grid=(T // tm,),
        in_specs=[pl.BlockSpec((tm, DIN), lambda i: (i, 0)),
                  pl.BlockSpec(memory_space=pl.ANY),
                  pl.BlockSpec((tm, D), lambda i: (i, 0)),
                  pl.BlockSpec((tm, D), lambda i: (i, 0)),
                  pl.BlockSpec((1, D), lambda i: (0, 0)),
                  pl.BlockSpec(memory_space=pl.ANY)],
        out_specs=[pl.BlockSpec((tm, D), lambda i: (i, 0)),
                   pl.BlockSpec((1, D), lambda i: (0, 0))],
        out_shape=[SDS((T, D), F32), SDS((1, D), F32)],
        scratch_shapes=[pltpu.VMEM((D, DIN), MM), pltpu.SemaphoreType.DMA((NCHIP,))],
        name="bwd_in",
        compiler_params=_cp(("arbitrary",)),
    )(dproj, w_sh, x, dy, g_row, token)


def _grad_w_in_call(h, dproj, token):
    T = h.shape[0]
    tt = min(GRAD_TOKEN_TILE, T)
    nt = T // tt
    dh = D // 2

    def body(h_ref, dp_ref, token_ref, o_ref, acc_ref):
        t = pl.program_id(2)

        @pl.when(t == 0)
        def _():
            acc_ref[...] = jnp.zeros_like(acc_ref)

        acc_ref[...] += lax.dot_general(h_ref[...], dp_ref[...], TN, preferred_element_type=F32)

        @pl.when(t == nt - 1)
        def _():
            o_ref[0] = acc_ref[:, 0:SHW].astype(MM)
            o_ref[1] = acc_ref[:, SHW:2 * SHW].astype(MM)

    return pl.pallas_call(
        body,
        grid=(NCHIP // 2, 2, nt),
        in_specs=[pl.BlockSpec((tt, dh), lambda b, m, t: (t, m)),
                  pl.BlockSpec((tt, 2 * SHW), lambda b, m, t: (t, b)),
                  pl.BlockSpec(memory_space=pl.ANY)],
        out_specs=pl.BlockSpec((2, dh, SHW), lambda b, m, t: (b, m, 0)),
        out_shape=SDS((NCHIP, D, SHW), MM),
        scratch_shapes=[pltpu.VMEM((dh, 2 * SHW), F32)],
        name="grad_w_in",
        compiler_params=_cp(("arbitrary", "arbitrary", "arbitrary")),
    )(h, dproj, token)


def _grad_w_out_call(mix, dy):
    T = mix.shape[0]
    tt = min(GRAD_TOKEN_TILE, T)
    nt = T // tt
    tn = 1024

    def body(m_ref, dy_ref, o_ref, acc_ref):
        t = pl.program_id(1)

        @pl.when(t == 0)
        def _():
            acc_ref[...] = jnp.zeros_like(acc_ref)

        acc_ref[...] += lax.dot_general(m_ref[...], dy_ref[...].astype(MM), TN, preferred_element_type=F32)

        @pl.when(t == nt - 1)
        def _():
            o_ref[...] = acc_ref[...].astype(MM)

    return pl.pallas_call(
        body,
        grid=(D // tn, nt),
        in_specs=[pl.BlockSpec((tt, D), lambda j, t: (t, 0)),
                  pl.BlockSpec((tt, tn), lambda j, t: (t, j))],
        out_specs=pl.BlockSpec((D, tn), lambda j, t: (0, j)),
        out_shape=SDS((D, D), MM),
        scratch_shapes=[pltpu.VMEM((D, tn), F32)],
        name="grad_w_out",
        compiler_params=_cp(("arbitrary", "arbitrary")),
    )(mix, dy)


def _cast_to_slab_call(w, chip_idx, layers, after, name):
    _, R, C = w.shape
    n = len(layers)
    tr = 256

    def body(chip_ref, *refs):
        for k in range(n):
            refs[n + 1 + k][...] = refs[k][...].astype(MM)

    return pl.pallas_call(
        body,
        grid_spec=pltpu.PrefetchScalarGridSpec(
            num_scalar_prefetch=1,
            grid=(R // tr,),
            in_specs=[pl.BlockSpec((1, tr, C), functools.partial(lambda i, chip_ref, l: (l, i, 0), l=l))
                      for l in layers] + [ANY],
            out_specs=[pl.BlockSpec((1, tr, C), lambda i, chip_ref: (chip_ref[0], i, 0))] * n,
        ),
        out_shape=[SDS((NCHIP, R, C), MM)] * n,
        name=name,
        compiler_params=_cp(("arbitrary",)),
    )(chip_idx, *([w] * n), after)


def _adam_call(w, g_parts, m, v, name):
    R, C = w.shape
    tr = R
    for cand in (512, 256, 128, 64, 32, 16, 8):
        if R % cand == 0 and cand * C * 4 <= 1024 * 1024:
            tr = cand
            break
    c1 = 1.0 - B1 ** STEP
    c2 = 1.0 - B2 ** STEP
    ng = len(g_parts)

    def body(*refs):
        w_ref, m_ref, v_ref = refs[0], refs[1 + ng], refs[2 + ng]
        g_ref, d_ref, nm_ref, nv_ref = refs[3 + ng:]
        gv = refs[1][...].astype(F32)
        for k in range(1, ng):
            gv = gv + refs[1 + k][...].astype(F32)
        nm = B1 * m_ref[...] + (1.0 - B1) * gv
        nv = B2 * v_ref[...] + (1.0 - B2) * (gv * gv)
        g_ref[...] = gv
        nm_ref[...] = nm
        nv_ref[...] = nv
        d_ref[...] = -LR * ((nm / c1) / (jnp.sqrt(nv / c2) + ADAM_EPS) + WD * w_ref[...])

    spec = pl.BlockSpec((tr, C), lambda i: (i, 0))
    return pl.pallas_call(
        body,
        grid=(R // tr,),
        in_specs=[spec] * (3 + ng),
        out_specs=[spec] * 4,
        out_shape=[SDS((R, C), F32)] * 4,
        name=name,
        compiler_params=_cp(("arbitrary",)),
    )(w, *g_parts, m, v)


MESH = pl.DeviceIdType.MESH
ANY = pl.BlockSpec(memory_space=pl.ANY)
HBM = pl.BlockSpec(memory_space=pltpu.HBM)
SEMS = pl.BlockSpec(memory_space=pltpu.SEMAPHORE)
EFFECT = pltpu.SideEffectType.DATAFLOW_SIDE_EFFECTING
NDEV = 8


def _hbm(a):
    return pltpu.with_memory_space_constraint(a, pltpu.HBM)


def _place():
    x, y, c = lax.axis_index("x"), lax.axis_index("y"), lax.axis_index("c")
    others = [(1 - x, y), (x, 1 - y), (1 - x, 1 - y)]
    return x, y, c, 2 * x + y, others


def _flipped(x, y, c, r):
    return (1 - x if r & 4 else x, 1 - y if r & 2 else y, 1 - c if r & 1 else c)


def _rcopy(src, dst, ssem, rsem, dev):
    return pltpu.make_async_remote_copy(src_ref=src, dst_ref=dst, send_sem=ssem, recv_sem=rsem,
                                        device_id=dev, device_id_type=MESH)


def _slab(ref, chip, c, halved):
    if not halved:
        return ref.at[chip]
    h = ref.shape[1] // 2
    return ref.at[chip, pl.ds(c * h, h), :]


def _gather_start_call(fulls, n_halved, name):
    K = len(fulls)

    def body(*refs):
        full, ssem, rsem = refs[:K], refs[K:2 * K], refs[2 * K:3 * K]
        x, y, c, me, others = _place()
        for k in range(K):
            for j, (px, py) in enumerate(others):
                part = _slab(full[k], me, c, k < n_halved)
                _rcopy(part, part, ssem[k].at[j], rsem[k].at[j], (px, py, c)).start()

    outs = pl.pallas_call(
        body,
        in_specs=[HBM] * K,
        out_specs=[SEMS] * (2 * K) + [HBM] * K,
        out_shape=[pltpu.SemaphoreType.DMA((3,))] * (2 * K) + [pltpu.HBM(f.shape, f.dtype) for f in fulls],
        input_output_aliases={k: 2 * K + k for k in range(K)},
        name=name,
        compiler_params=pltpu.CompilerParams(has_side_effects=EFFECT),
    )(*[_hbm(f) for f in fulls])
    return list(outs[:K]), list(outs[K:2 * K]), list(outs[2 * K:])


def _gather_wait_call(fulls, ssems, rsems, after, halved, name):
    K = len(fulls)

    def body(*refs):
        full, ssem, rsem = refs[:K], refs[K:2 * K], refs[2 * K:3 * K]
        x, y, c, me, others = _place()
        for k in range(K):
            for j, (px, py) in enumerate(others):
                cp = _rcopy(_slab(full[k], me, c, halved), _slab(full[k], 2 * px + py, c, halved),
                            ssem[k].at[j], rsem[k].at[j], (px, py, c))
                cp.wait_send()
                cp.wait_recv()

    outs = pl.pallas_call(
        body,
        in_specs=[HBM] * K + [SEMS] * (2 * K) + [ANY],
        out_specs=[HBM] * K,
        out_shape=[pltpu.HBM(f.shape, f.dtype) for f in fulls],
        input_output_aliases={k: k for k in range(K)},
        name=name,
        compiler_params=pltpu.CompilerParams(has_side_effects=EFFECT),
    )(*fulls, *ssems, *rsems, after)
    return list(outs)


def _sibling_forward_call(fulls, name):
    K = len(fulls)

    def body(*refs):
        full = refs[:K]
        ssem, rsem = refs[2 * K:]
        x, y, c, me, others = _place()
        cps = []
        for k in range(K):
            for j, (px, py) in enumerate(others):
                mine = _slab(full[k], 2 * px + py, c, True)
                cps.append(_rcopy(mine, mine, ssem.at[3 * k + j], rsem.at[3 * k + j], (x, y, 1 - c)))
        for cp in cps:
            cp.start()
        for k in range(K):
            for j, (px, py) in enumerate(others):
                theirs = _slab(full[k], 2 * px + py, 1 - c, True)
                _rcopy(theirs, theirs, ssem.at[3 * k + j], rsem.at[3 * k + j], (x, y, 1 - c)).wait_recv()
        for cp in cps:
            cp.wait_send()

    outs = pl.pallas_call(
        body,
        in_specs=[ANY] * K,
        out_specs=[ANY] * K,
        out_shape=[SDS(f.shape, f.dtype) for f in fulls],
        input_output_aliases={k: k for k in range(K)},
        scratch_shapes=[pltpu.SemaphoreType.DMA((3 * K,)), pltpu.SemaphoreType.DMA((3 * K,))],
        name=name,
    )(*fulls)
    return list(outs)


def _grad_copies(srcs, lands, ssem, rsem, to_sibling):
    x, y, c, me, others = _place()
    if to_sibling:
        return [_rcopy(s, l, ssem.at[k], rsem.at[k], (x, y, 1 - c)) for k, (s, l) in enumerate(zip(srcs, lands))]
    cps, k = [], 0
    for src, land in zip(srcs, lands):
        if len(src.shape) == 3:
            for j, (px, py) in enumerate(others):
                cps.append(_rcopy(src.at[2 * px + py], land.at[j], ssem.at[k + j], rsem.at[k + j], (px, py, c)))
            k += 3
        else:
            for r in range(1, NDEV):
                cps.append(_rcopy(src, land.at[4 * x + 2 * y + c], ssem.at[k + r - 1], rsem.at[k + r - 1],
                                  _flipped(x, y, c, r)))
            k += NDEV - 1
    return cps


def _grad_start_call(srcs, name, to_sibling=False):
    srcs = list(srcs)
    K = len(srcs)
    if to_sibling:
        lands = [lax.empty(s.shape, s.dtype) for s in srcs]
        n = K
    else:
        lands = [lax.empty(((3,) + s.shape[1:]) if len(s.shape) == 3 else ((NDEV,) + s.shape), s.dtype) for s in srcs]
        n = sum(3 if len(s.shape) == 3 else NDEV - 1 for s in srcs)

    def body(*refs):
        ssem, rsem, token = refs[2 * K], refs[2 * K + 1], refs[-1]
        for cp in _grad_copies(refs[:K], refs[K:2 * K], ssem, rsem, to_sibling):
            cp.start()
        token[...] = jnp.zeros_like(token)

    outs = pl.pallas_call(
        body,
        in_specs=[HBM] * (2 * K),
        out_specs=[SEMS, SEMS] + [HBM] * (2 * K) + [pl.BlockSpec(memory_space=pltpu.VMEM)],
        out_shape=[pltpu.SemaphoreType.DMA((n,)), pltpu.SemaphoreType.DMA((n,))]
        + [pltpu.HBM(a.shape, a.dtype) for a in srcs + lands] + [SDS((8, 128), F32)],
        input_output_aliases={k: 2 + k for k in range(2 * K)},
        name=name,
        compiler_params=pltpu.CompilerParams(has_side_effects=EFFECT),
    )(*[_hbm(a) for a in srcs + lands])
    return list(outs[2:2 + K]), list(outs[2 + K:2 + 2 * K]), outs[0], outs[1], outs[-1]


def _grad_wait_call(srcs, lands, ssem, rsem, after, name, to_sibling=False):
    K = len(srcs)

    def body(*refs):
        for cp in _grad_copies(refs[:K], refs[K:2 * K], refs[2 * K], refs[2 * K + 1], to_sibling):
            cp.wait_send()
            cp.wait_recv()

    arrs = list(srcs) + list(lands)
    outs = pl.pallas_call(
        body,
        in_specs=[HBM] * (2 * K) + [SEMS, SEMS, ANY],
        out_specs=[HBM] * (2 * K),
        out_shape=[pltpu.HBM(a.shape, a.dtype) for a in arrs],
        input_output_aliases={k: k for k in range(2 * K)},
        name=name,
        compiler_params=pltpu.CompilerParams(has_side_effects=EFFECT),
    )(*arrs, ssem, rsem, after)
    return list(outs[:K]), list(outs[K:])


def _small_allreduce_call(a):
    R, C = a.shape

    def body(a_ref, o_ref, recv_ref, ssem, rsem):
        x, y, c, me, others = _place()
        dev = 4 * x + 2 * y + c
        recv_ref[pl.ds(dev, 1)] = a_ref[...][None]
        cps = [_rcopy(a_ref, recv_ref.at[dev], ssem.at[r - 1], rsem.at[r - 1], _flipped(x, y, c, r))
               for r in range(1, NDEV)]
        for cp in cps:
            cp.start()
        for cp in cps:
            cp.wait()
        acc = recv_ref[0]
        for s in range(1, NDEV):
            acc = acc + recv_ref[s]
        o_ref[...] = acc

    return pl.pallas_call(
        body,
        in_specs=[pl.BlockSpec(memory_space=pltpu.VMEM)],
        out_specs=pl.BlockSpec(memory_space=pltpu.VMEM),
        out_shape=SDS((R, C), F32),
        scratch_shapes=[pltpu.VMEM((NDEV, R, C), F32), pltpu.SemaphoreType.DMA((NDEV - 1,)),
                        pltpu.SemaphoreType.DMA((NDEV - 1,))],
        name="small_allreduce",
    )(a)


def _rows_tile(H, C):
    for cand in (512, 256, 128, 64, 32, 16, 8):
        if H % cand == 0 and cand * C * 4 <= 2 * 1024 * 1024:
            return cand
    raise ValueError((H, C))


def _sum_recv_call(own, recv, chip_idx, stack, l):
    _, R, C = own.shape
    tr = _rows_tile(R, C)

    def body(chip_ref, own_ref, r0, r1, r2, stack_ref, o_ref):
        o_ref[...] = (((own_ref[...].astype(F32) + r0[...].astype(F32)) + r1[...].astype(F32))
                      + r2[...].astype(F32)).astype(MM)

    return pl.pallas_call(
        body,
        grid_spec=pltpu.PrefetchScalarGridSpec(
            num_scalar_prefetch=1,
            grid=(R // tr,),
            in_specs=[pl.BlockSpec((1, tr, C), lambda i, chip_ref: (chip_ref[0], i, 0))]
            + [pl.BlockSpec((1, tr, C), functools.partial(lambda i, chip_ref, s: (s, i, 0), s=s)) for s in range(3)]
            + [ANY],
            out_specs=pl.BlockSpec((1, tr, C), lambda i, chip_ref: (l, i, 0)),
        ),
        out_shape=SDS(stack.shape, MM),
        input_output_aliases={5: 0},
        name="grad_sum_recv",
        compiler_params=_cp(("arbitrary",)),
    )(chip_idx, own, recv, recv, recv, stack)


def _sum_small_call(own, recv, dev_idx):
    RS, C = own.shape
    tr = _rows_tile(RS, C)

    def body(dev_ref, own_ref, *refs):
        o_ref = refs[NDEV]
        dev = dev_ref[0]
        acc = jnp.where(dev == 0, own_ref[...], refs[0][0]).astype(F32)
        for s in range(1, NDEV):
            acc = acc + jnp.where(dev == s, own_ref[...], refs[s][0]).astype(F32)
        o_ref[...] = acc

    return pl.pallas_call(
        body,
        grid_spec=pltpu.PrefetchScalarGridSpec(
            num_scalar_prefetch=1,
            grid=(RS // tr,),
            in_specs=[pl.BlockSpec((tr, C), lambda i, dev_ref: (i, 0))]
            + [pl.BlockSpec((1, tr, C), functools.partial(
                lambda i, dev_ref, s: (jnp.where(dev_ref[0] == s, (s + 1) % NDEV, s), i, 0), s=s)) for s in range(NDEV)],
            out_specs=pl.BlockSpec((tr, C), lambda i, dev_ref: (i, 0)),
        ),
        out_shape=SDS((RS, C), F32),
        name="grad_sum_small",
        compiler_params=_cp(("arbitrary",)),
    )(dev_idx, own, *([recv] * NDEV))


SMALL_ROWS_ALIGN = 128


def _pack_small(parts):
    flat = jnp.concatenate([p.reshape(-1) for p in parts])
    rows = -(-flat.shape[0] // (128 * SMALL_ROWS_ALIGN)) * SMALL_ROWS_ALIGN
    flat = jnp.pad(flat, (0, rows * 128 - flat.shape[0]))
    return flat.reshape(rows, 128)


def _unpack_small(packed, like):
    flat = packed.reshape(-1)
    out, off = [], 0
    for p in like:
        n = int(np.prod(p.shape))
        out.append(flat[off:off + n].reshape(p.shape))
        off += n
    return out


def kernel(x, norm_g, w_in, q_norm, k_norm, sinks, w_s, b_s, w_out, loss_target, m_norm_g, m_w_in, m_q_norm, m_k_norm, m_sinks, m_w_s, m_b_s, m_w_out, v_norm_g, v_w_in, v_q_norm, v_k_norm, v_sinks, v_w_s, v_b_s, v_w_out):
    L = norm_g.shape[0]
    xi, yi, ci = lax.axis_index("x"), lax.axis_index("y"), lax.axis_index("c")
    chip_idx = (2 * xi + yi).astype(jnp.int32).reshape(1)
    dev_idx = (4 * xi + 2 * yi + ci).astype(jnp.int32).reshape(1)
    bias = _alibi_bias()
    b2 = _half_sum_matrix()
    tri =jnp.tril(jnp.ones((WIN, WIN), F32))

    rest = list(range(1, L))
    fulls0 = _cast_to_slab_call(w_in, chip_idx, [0], chip_idx, "cast_w_in_0") \
        + _cast_to_slab_call(w_out, chip_idx, [0], chip_idx, "cast_w_out_0")
    ss0, rs0, fulls0 = _gather_start_call(fulls0, 2, "gather_start_0")
    fin = _cast_to_slab_call(w_in, chip_idx, rest, fulls0[0], "cast_w_in")
    fout = _cast_to_slab_call(w_out, chip_idx, rest, fulls0[1], "cast_w_out")
    g_ssems, g_rsems, fulls = _gather_start_call([a for pair in zip(fin, fout) for a in pair], 0, "gather_start")

    saved = []
    xs = x[0]
    dy = loss = None
    for l in range(L):
        if l == 0:
            (w_in_l,) = _gather_wait_call(fulls0[:1], ss0[:1], rs0[:1], fulls[0], True, "gather_wait_0")
            (w_in_l,) = _sibling_forward_call([w_in_l], "gather_forward_0")
        else:
            sl = slice(2 * (l - 1), 2 * l)
            w_in_l, w_out_l = _gather_wait_call(fulls[sl], g_ssems[sl], g_rsems[sl], xs, False, f"gather_wait_{l}")
        proj, h = _fwd_in_call(xs, norm_g[l:l + 1], w_in_l)
        ws_tril = (w_s[l] * tri).astype(MM)
        b_exp = jnp.repeat(b_s[l].T, HD, axis=1)
        wq2 = jnp.tile(q_norm[l:l + 1], (1, 2)) * SCALE
        wk2 = jnp.tile(k_norm[l:l + 1], (1, 2))
        x_l = xs
        if 0 < l < L - 1:
            w_out_l = w_out_l.reshape(D, D)
            mix, xs = _fwd_mix_call(proj, bias, wq2, wk2, b2, sinks[l], ws_tril, b_exp, xs, w_out_l)
        elif 0 < l:
            w_out_l = w_out_l.reshape(D, D)
            mix, dy, loss = _fwd_mix_call(proj, bias, wq2, wk2, b2, sinks[l], ws_tril, b_exp, xs, w_out_l, loss_target[0])
        else:
            mix = _fwd_mix_call(proj, bias, wq2, wk2, b2, sinks[l], ws_tril, b_exp)
        if l == 0:
            (w_out_l,) = _gather_wait_call(fulls0[1:], ss0[1:], rs0[1:], mix, True, "gather_wait_0_out")
            (w_out_l,) = _sibling_forward_call([w_out_l], "gather_forward_0_out")
        w_out_l = w_out_l.reshape(D, D)
        saved.append((x_l, proj, h, mix, ws_tril, b_exp, w_in_l, w_out_l, wq2, wk2))
        if l == 0 and L == 1:
            dy, loss = _fwd_out_loss_call(xs, mix, w_out_l, loss_target[0])
        elif l == 0:
            xs = _fwd_out_call(xs, mix, w_out_l)

    s_in = lax.empty((L, D, SHW), MM)
    s_out = lax.empty((L, SHR, D), MM)
    ws_sums, tiny_sums = [None] * L, [None] * L

    def finish(pending, after):
        nonlocal s_in, s_out
        l, exchanges = pending
        for tag, kinds, srcs, lands, ssem, rsem in exchanges:
            srcs, lands = _grad_wait_call(srcs, lands, ssem, rsem, after, f"grad_wait_{l}{tag}")
            for kind, src, land in zip(kinds, srcs, lands):
                if kind == "in":
                    s_in = _sum_recv_call(src, land, chip_idx, s_in, l)
                elif kind == "out":
                    s_out = _sum_recv_call(src, land, chip_idx, s_out, l)
                elif kind == "ws":
                    ws_sums[l] = _sum_small_call(src, land, dev_idx)
                else:
                    tiny_sums[l] = _sum_small_call(src, land, dev_idx)

    def start(l, tag, kinds, srcs):
        srcs, lands, ssem, rsem, token = _grad_start_call(srcs, f"grad_start_{l}{tag}")
        return (tag, kinds, srcs, lands, ssem, rsem), token

    pending = None
    d_norm_g = [None] * L
    for l in reversed(range(L)):
        xs, proj, h, mix, ws_tril, b_exp, w_in_l, w_out_l, wq2, wk2 = saved[l]
        g_w_out = _grad_w_out_call(mix, dy).reshape(NCHIP, SHR, D)
        exchanges, token = [], jnp.zeros((8, 128), F32)
        if l == 0:
            ex, token = start(l, "_out", ["out"], [g_w_out])
            exchanges.append(ex)
        dmix = _bwd_out_call(dy, w_out_l, token)
        ws_tril_t = jnp.swapaxes(ws_tril, 1, 2)
        dproj, dwq, dwk, dsk, dws, dbs = _bwd_mix_call(
            proj, dmix, bias, wq2, wk2, b2, sinks[l], ws_tril, ws_tril_t, b_exp)
        dwq, dwk = dwq[:, :HD] + dwq[:, HD:], dwk[:, :HD] + dwk[:, HD:]
        g_ws = dws.reshape(NG * WIN, WIN).astype(MM)
        g_tiny = _pack_small([dwq, dwk, dsk[:, 0], dbs]).astype(MM)
        token = jnp.zeros((8, 128), F32)
        if l == 0:
            ex, token = start(l, "_small", ["ws", "tiny"], [g_ws, g_tiny])
            exchanges.append(ex)
        g_w_in = _grad_w_in_call(h, dproj, token)
        if l == 0:
            ex, token = start(l, "", ["in"], [g_w_in])
        else:
            ex, token = start(l, "", ["in", "out", "ws", "tiny"], [g_w_in, g_w_out, g_ws, g_tiny])
        exchanges.append(ex)
        dy, d_norm_g[l] = _bwd_in_call(dproj, w_in_l, xs, dy, norm_g[l:l + 1], token)
        if pending is not None:
            finish(pending, dy)
        pending = (l, exchanges)
    finish(pending, dy)
    grad_x = dy

    swap_srcs, swap_lands, sw_ssem, sw_rsem, _ = _grad_start_call([s_in, s_out], "grad_swap_start", to_sibling=True)
    g_norm_g = _small_allreduce_call(jnp.concatenate(d_norm_g, axis=0))

    def pack_layers(parts):
        return jnp.concatenate([_pack_small([p[l] for p in parts]) for l in range(L)], axis=0)

    ws_rows = (L * NG * WIN, WIN)
    ws_outs = _adam_call(w_s.reshape(ws_rows), [jnp.concatenate(ws_sums, axis=0)],
                         m_w_s.reshape(ws_rows), v_w_s.reshape(ws_rows), "adam_w_s")
    tiny_like = [q_norm, k_norm, sinks, b_s]
    tiny_outs = _adam_call(
        pack_layers(tiny_like), [jnp.concatenate(tiny_sums, axis=0)],
        pack_layers([m_q_norm, m_k_norm, m_sinks, m_b_s]),
        pack_layers([v_q_norm, v_k_norm, v_sinks, v_b_s]), "adam_tiny")
    norm_outs = _adam_call(norm_g, [g_norm_g], m_norm_g, v_norm_g, "adam_norm_g")

    small_done = ws_outs[1][:8] + tiny_outs[1][:8] + norm_outs[1][:1, :WIN]
    (s_in, s_out), (t_in, t_out) = _grad_wait_call(swap_srcs, swap_lands, sw_ssem, sw_rsem, small_done,
                                                   "grad_swap_wait", to_sibling=True)
    g_w_in, d_in, nm_in, nv_in = _adam_call(
        w_in.reshape(L * D, SHW), [s_in.reshape(L * D, SHW), t_in.reshape(L * D, SHW)],
        m_w_in.reshape(L * D, SHW), v_w_in.reshape(L * D, SHW), "adam_w_in")
    g_w_out, d_out, nm_out, nv_out = _adam_call(
        w_out.reshape(L * SHR, D), [s_out.reshape(L * SHR, D), t_out.reshape(L * SHR, D)],
        m_w_out.reshape(L * SHR, D), v_w_out.reshape(L * SHR, D), "adam_w_out")

    def full(i, win, wout):
        tiny = tiny_outs[i]
        rows = tiny.shape[0] // L
        per_layer = [_unpack_small(tiny[l * rows:(l + 1) * rows], [p[l] for p in tiny_like]) for l in range(L)]
        qn, kn, sk, bs = [jnp.stack([per_layer[l][k] for l in range(L)]) for k in range(4)]
        return [norm_outs[i], win.reshape(w_in.shape), qn, kn, sk, ws_outs[i].reshape(w_s.shape), bs,
                wout.reshape(w_out.shape)]

    loss_all = lax.psum(loss[0, 0], ("x", "y", "c"))
    return (loss_all, grad_x[None], *full(0, g_w_in, g_w_out), *full(1, d_in, d_out),
            *full(2, nm_in, nm_out), *full(3, nv_in, nv_out))
```
